```python
import functools
import numpy as np
import jax
import jax.numpy as jnp
from jax import lax

D_MODEL = 2048
BATCH = 32
SEQ = 256
DEPTH = 2
DEC_BATCH = 4
DEC_SEQ = 1024
PAST_LEN = 512

GRID_W = 64
MLA_HEADS = 16
QK_NOPE = 128
QK_ROPE = 64
V_HEAD = 128
Q_LORA = 512
KV_LORA = 512
ROPE_THETA = 10000.0
AXIS_HALF = QK_ROPE // 4
Q_BLOCK = 128
MLA_SCALE = (QK_NOPE + QK_ROPE) ** -0.5
GLA_HEADS = 4
GLA_DK = D_MODEL // 2 // GLA_HEADS
GLA_DV = D_MODEL // GLA_HEADS
GATE_RANK = 16
GATE_NORM = 16.0
GLA_CHUNK = 64
D_FF = 5632
N_EXPERTS = 8
TOP_K = 2
D_EXPERT = 2816
N_DENSE = (DEPTH + 1) // 2
N_MOE = DEPTH // 2
EPS = 1e-6
IN_SPLITS = (Q_LORA, KV_LORA, QK_ROPE,
             GLA_HEADS * GLA_DK, GLA_HEADS * GLA_DK, GLA_HEADS * GLA_DV,
             GATE_RANK, GATE_RANK, GLA_HEADS * GLA_DV,
             D_MODEL, D_MODEL)
IN_COLS = sum(IN_SPLITS)

kernel_name = "hybrid_mla_gla_diffusion_step"


def rmsnorm(x, g):
    xf = x.astype(jnp.float32)
    y = xf * lax.rsqrt(jnp.mean(xf * xf, axis=-1, keepdims=True) + EPS)
    return (y * g.astype(jnp.float32)).astype(x.dtype)


def axial_angles(n_tokens):
    rows = n_tokens // GRID_W
    row = jnp.repeat(jnp.arange(rows, dtype=jnp.float32), GRID_W)
    col = jnp.tile(jnp.arange(GRID_W, dtype=jnp.float32), rows)
    inv = ROPE_THETA ** (-jnp.arange(AXIS_HALF, dtype=jnp.float32) / AXIS_HALF)
    return jnp.stack([row[:, None] * inv, col[:, None] * inv], axis=1)


def apply_axial_rope(x, ang):
    shp = x.shape
    xr = x.astype(jnp.float32).reshape(shp[:-1] + (2, 2, AXIS_HALF))
    a = ang.reshape((1, shp[1]) + (1,) * (x.ndim - 3) + (2, AXIS_HALF))
    cos, sin = jnp.cos(a), jnp.sin(a)
    x1, x2 = xr[..., 0, :], xr[..., 1, :]
    out = jnp.stack([x1 * cos - x2 * sin, x2 * cos + x1 * sin], axis=-2)
    return out.reshape(shp).astype(x.dtype)


def expand_kv(c_kv, w_ukv):
    B, T, _ = c_kv.shape
    kv = (c_kv @ w_ukv).reshape(B, T, MLA_HEADS, QK_NOPE + V_HEAD)
    return kv[..., :QK_NOPE], kv[..., QK_NOPE:]


def mla_attention(q_nope, q_rope, k_nope, k_rope, v):
    B, T = q_nope.shape[:2]
    nb = T // Q_BLOCK

    def blocks(a):
        return jnp.moveaxis(a.reshape((B, nb, Q_BLOCK) + a.shape[2:]), 1, 0)

    def one_block(qs):
        qn, qr = qs
        s = (jnp.einsum("bqhd,bkhd->bhqk", qn, k_nope)
             + jnp.einsum("bqhr,bkr->bhqk", qr, k_rope))
        p = jax.nn.softmax(s.astype(jnp.float32) * MLA_SCALE, axis=-1).astype(v.dtype)
        return jnp.einsum("bhqk,bkhd->bqhd", p, v)

    o = lax.map(one_block, (blocks(q_nope), blocks(q_rope)))
    return jnp.moveaxis(o, 0, 1).reshape(B, T, MLA_HEADS * V_HEAD)


def gla_scan(q, k, v, log_a, s0):
    B, T, H, _ = q.shape
    DV = v.shape[-1]
    n = T // GLA_CHUNK

    def chunks(a):
        a = a.astype(jnp.float32).reshape(B, n, GLA_CHUNK, H, a.shape[-1])
        return jnp.transpose(a, (1, 0, 3, 2, 4))

    causal = jnp.tril(jnp.ones((GLA_CHUNK, GLA_CHUNK), dtype=bool))

    def step(S, inp):
        qc, kc, vc, lc = inp
        b = jnp.cumsum(lc, axis=-2)
        b_end = b[..., -1:, :]
        qd = qc * jnp.exp(b)
        kd = kc * jnp.exp(-b)
        A = jnp.where(causal, jnp.einsum("bhcd,bhsd->bhcs", qd, kd), 0.0)
        o = jnp.einsum("bhcd,bhde->bhce", qd, S) + jnp.einsum("bhcs,bhse->bhce", A, vc)
        S = (jnp.exp(b_end)[..., 0, :, None] * S
             + jnp.einsum("bhcd,bhce->bhde", kc * jnp.exp(b_end - b), vc))
        return S, o

    S, o = lax.scan(step, s0.astype(jnp.float32),
                    (chunks(q), chunks(k), chunks(v), chunks(log_a)))
    o = jnp.transpose(o, (1, 0, 3, 2, 4)).reshape(B, T, H, DV)
    return o, S


def bidirectional_gla(q, k, v, la_f, la_b, s0_f, s0_b):
    flip = lambda a: jnp.flip(a, axis=1)
    o_f, s_f = gla_scan(q, k, v, la_f, s0_f)
    o_b, s_b = gla_scan(flip(q), flip(k), flip(v), flip(la_b), s0_b)
    return o_f + flip(o_b), s_f, s_b


def swiglu(h, w_gate, w_up, w_down):
    return (jax.nn.silu(h @ w_gate) * (h @ w_up)) @ w_down


def moe_swiglu(h, w_router, w_gate, w_up, w_down):
    logits = (h @ w_router).astype(jnp.float32)
    top_v, top_i = lax.top_k(logits, TOP_K)
    weights = jax.nn.softmax(top_v, axis=-1)
    combine = jnp.einsum("btk,btke->bte", weights,
                         jax.nn.one_hot(top_i, N_EXPERTS, dtype=jnp.float32)).astype(h.dtype)
    out = jnp.zeros_like(h)
    for e in range(N_EXPERTS):
        out = out + combine[..., e:e + 1] * swiglu(h, w_gate[e], w_up[e], w_down[e])
    return out


def trunk_layer(x, mod, p, ffn, ang=None, ctx=None):
    B, T, _ = x.shape
    shift_m, scale_m, gate_m, shift_f, scale_f, gate_f = (mod[..., i, :] for i in range(6))
    h = rmsnorm(x, p["g_pre_mix"]) * (1 + scale_m) + shift_m
    offs = np.cumsum(IN_SPLITS)[:-1].tolist()
    (q_dn, kv_dn, k_rope, gq, gk, gv, gf_dn, gb_dn, g_out, br_a, br_b) = jnp.split(
        h @ p["w_in"], offs, axis=-1)
    q = (rmsnorm(q_dn, p["g_q"]) @ p["w_uq"]).reshape(B, T, MLA_HEADS, QK_NOPE + QK_ROPE)
    q_nope, q_rope = q[..., :QK_NOPE], q[..., QK_NOPE:]
    c_kv = rmsnorm(kv_dn, p["g_kv"])
    k_nope, v = expand_kv(c_kv, p["w_ukv"])
    gshape = (B, T, GLA_HEADS, GLA_DK)
    la_f = (jax.nn.log_sigmoid((gf_dn @ p["w_gate_f"] + p["b_gate_f"]).astype(jnp.float32))
            / GATE_NORM).reshape(gshape)
    la_b = (jax.nn.log_sigmoid((gb_dn @ p["w_gate_b"] + p["b_gate_b"]).astype(jnp.float32))
            / GATE_NORM).reshape(gshape)
    gq = gq.reshape(gshape) * GLA_DK ** -0.5
    gk = gk.reshape(gshape)
    gv = gv.reshape(B, T, GLA_HEADS, GLA_DV)
    if ctx is None:
        k_rope_all = k_rope
        s0 = jnp.zeros((B, GLA_HEADS, GLA_DK, GLA_DV), jnp.float32)
        s0_f, s0_b = s0, s0
    else:
        ckv_ctx, krope_ctx, s0_f, s0_b = ctx
        q_rope = apply_axial_rope(q_rope, ang)
        kn_c, v_c = expand_kv(ckv_ctx, p["w_ukv"])
        k_nope = jnp.concatenate([k_nope, kn_c], axis=1)
        v = jnp.concatenate([v, v_c], axis=1)
        k_rope_all = jnp.concatenate([apply_axial_rope(k_rope, ang), krope_ctx], axis=1)
    attn = mla_attention(q_nope, q_rope, k_nope, k_rope_all, v)
    o, s_f, s_b = bidirectional_gla(gq, gk, gv, la_f, la_b, s0_f, s0_b)
    gla = (rmsnorm(o.astype(x.dtype), p["g_gla"]).reshape(B, T, GLA_HEADS * GLA_DV)
           * jax.nn.silu(g_out))
    merged = (jax.nn.sigmoid(br_a) * (attn @ p["w_pa"])
              + jax.nn.sigmoid(br_b) * (gla @ p["w_pb"]))
    x = x + gate_m * rmsnorm(merged @ p["w_o"], p["g_post_mix"])
    h = rmsnorm(x, p["g_pre_ffn"]) * (1 + scale_f) + shift_f
    x = x + gate_f * rmsnorm(ffn(h), p["g_post_ffn"])
    return x, c_kv, k_rope, s_f, s_b


def setup_inputs(seed: int = 0) -> dict:
    key = jax.random.key(seed)
    keys = iter(jax.random.split(key, 48))
    D = D_MODEL

    def nrm(shape, scale=1.0):
        return jax.random.normal(next(keys), shape, jnp.float32) * scale

    def gain(shape):
        return 1.0 + nrm(shape, 0.05)

    return {
        "x_prompt": nrm((BATCH, SEQ, D)),
        "x_sample": nrm((DEC_BATCH, DEC_SEQ, D)),
        "cache_ckv": nrm((DEC_BATCH, DEPTH, PAST_LEN, KV_LORA)),
        "cache_krope": nrm((DEC_BATCH, DEPTH, PAST_LEN, QK_ROPE)),
        "state_gla_fwd": nrm((DEC_BATCH, DEPTH, GLA_HEADS, GLA_DK, GLA_DV)),
        "state_gla_bwd": nrm((DEC_BATCH, DEPTH, GLA_HEADS, GLA_DK, GLA_DV)),
        "c": nrm((DEC_BATCH, D)),
        "c_ctx": nrm((D,)),
        "w_ada": nrm((DEPTH, D, 6 * D), D ** -0.5),
        "b_ada": nrm((DEPTH, 6 * D), 0.02),
        "g_pre_mix": gain((DEPTH, D)),
        "g_post_mix": gain((DEPTH, D)),
        "g_pre_ffn": gain((DEPTH, D)),
        "g_post_ffn": gain((DEPTH, D)),
        "w_in": nrm((DEPTH, D, IN_COLS), D ** -0.5),
        "g_q": gain((DEPTH, Q_LORA)),
        "w_uq": nrm((DEPTH, Q_LORA, MLA_HEADS * (QK_NOPE + QK_ROPE)), Q_LORA ** -0.5),
        "g_kv": gain((DEPTH, KV_LORA)),
        "w_ukv": nrm((DEPTH, KV_LORA, MLA_HEADS * (QK_NOPE + V_HEAD)), KV_LORA ** -0.5),
        "w_gate_f": nrm((DEPTH, GATE_RANK, GLA_HEADS * GLA_DK), GATE_RANK ** -0.5),
        "b_gate_f": nrm((DEPTH, GLA_HEADS * GLA_DK), 0.05),
        "w_gate_b": nrm((DEPTH, GATE_RANK, GLA_HEADS * GLA_DK), GATE_RANK ** -0.5),
        "b_gate_b": nrm((DEPTH, GLA_HEADS * GLA_DK), 0.05),
        "g_gla": gain((DEPTH, GLA_DV)),
        "w_pa": nrm((DEPTH, MLA_HEADS * V_HEAD, D), (MLA_HEADS * V_HEAD) ** -0.5),
        "w_pb": nrm((DEPTH, GLA_HEADS * GLA_DV, D), (GLA_HEADS * GLA_DV) ** -0.5),
        "w_o": nrm((DEPTH, D, D), D ** -0.5),
        "w_ff_gate": nrm((N_DENSE, D, D_FF), D ** -0.5),
        "w_ff_up": nrm((N_DENSE, D, D_FF), D ** -0.5),
        "w_ff_down": nrm((N_DENSE, D_FF, D), D_FF ** -0.5),
        "w_router": nrm((N_MOE, D, N_EXPERTS), D ** -0.5),
        "w_ex_gate": nrm((N_MOE, N_EXPERTS, D, D_EXPERT), D ** -0.5),
        "w_ex_up": nrm((N_MOE, N_EXPERTS, D, D_EXPERT), D ** -0.5),
        "w_ex_down": nrm((N_MOE, N_EXPERTS, D_EXPERT, D), D_EXPERT ** -0.5),
    }


def reference(x_prompt, x_sample, cache_ckv, cache_krope, state_gla_fwd, state_gla_bwd,
              c, c_ctx, w_ada, b_ada, g_pre_mix, g_post_mix, g_pre_ffn, g_post_ffn,
              w_in, g_q, w_uq, g_kv, w_ukv, w_gate_f, b_gate_f, w_gate_b, b_gate_b,
              g_gla, w_pa, w_pb, w_o, w_ff_gate, w_ff_up, w_ff_down,
              w_router, w_ex_gate, w_ex_up, w_ex_down):
    ang = axial_angles(x_sample.shape[1])
    new_ckv, new_krope, new_sf, new_sb = [], [], [], []
    xp, xs = x_prompt, x_sample
    for l in range(DEPTH):
        p = {"g_pre_mix": g_pre_mix[l], "g_post_mix": g_post_mix[l],
             "g_pre_ffn": g_pre_ffn[l], "g_post_ffn": g_post_ffn[l],
             "w_in": w_in[l], "g_q": g_q[l], "w_uq": w_uq[l], "g_kv": g_kv[l],
             "w_ukv": w_ukv[l], "w_gate_f": w_gate_f[l], "b_gate_f": b_gate_f[l],
             "w_gate_b": w_gate_b[l], "b_gate_b": b_gate_b[l], "g_gla": g_gla[l],
             "w_pa": w_pa[l], "w_pb": w_pb[l], "w_o": w_o[l]}
        j = l // 2
        if l % 2 == 0:
            ffn = functools.partial(swiglu, w_gate=w_ff_gate[j], w_up=w_ff_up[j],
                                    w_down=w_ff_down[j])
        else:
            ffn = functools.partial(moe_swiglu, w_router=w_router[j], w_gate=w_ex_gate[j],
                                    w_up=w_ex_up[j], w_down=w_ex_down[j])
        mod_ctx = (jax.nn.silu(c_ctx) @ w_ada[l] + b_ada[l]).reshape(1, 1, 6, D_MODEL)
        mod_lat = (jax.nn.silu(c) @ w_ada[l] + b_ada[l]).reshape(c.shape[0], 1, 6, D_MODEL)
        xp, ckv, krope, sf, sb = trunk_layer(xp, mod_ctx, p, ffn)
        new_ckv.append(ckv)
        new_krope.append(krope)
        new_sf.append(sf)
        new_sb.append(sb)
        xs = trunk_layer(xs, mod_lat, p, ffn, ang=ang,
                         ctx=(cache_ckv[:, l], cache_krope[:, l],
                              state_gla_fwd[:, l], state_gla_bwd[:, l]))[0]
    return (xp, xs, jnp.stack(new_ckv, axis=1), jnp.stack(new_krope, axis=1),
            jnp.stack(new_sf, axis=1), jnp.stack(new_sb, axis=1))
```

```python
import functools
import math

import jax
import jax.numpy as jnp
from jax import lax
from jax.experimental import pallas as pl
from jax.experimental.pallas import tpu as pltpu

MLA_HEADS = 16
QK_NOPE = 128
V_HEAD = 128
GRID_W = 64
ROPE_THETA = 10000.0
GATE_NORM = 16.0
GLA_CHUNK = 64
TOP_K = 2
EPS = 1e-6
LANE = 128
VMEM_LIMIT = 52 * 1024 * 1024

BF16 = jnp.bfloat16
F32 = jnp.float32


def _params(n_grid):
    return pltpu.CompilerParams(
        dimension_semantics=("arbitrary",) * n_grid, vmem_limit_bytes=VMEM_LIMIT)


def _tile(n, target, quantum=LANE):
    if n <= target:
        return n
    t = (target // quantum) * quantum
    while t >= quantum:
        if n % t == 0:
            return t
        t -= quantum
    return n


def _rms(x, g):
    return x * lax.rsqrt(jnp.mean(x * x, axis=-1, keepdims=True) + EPS) * g


def _dot(a, b):
    return jnp.dot(a, b, preferred_element_type=F32)


def _dot_nt(a, b):
    return lax.dot_general(a, b, (((1,), (1,)), ((), ())), preferred_element_type=F32)


def _dot_tn(a, b):
    return lax.dot_general(a, b, (((0,), (0,)), ((), ())), preferred_element_type=F32)


def _mm_kernel(*refs, n_a, n_w, a_of_w, n_e, cast_w, prologue, epilogue):
    a_refs = refs[:n_a]
    w_refs = refs[n_a:n_a + n_w]
    e_refs = refs[n_a + n_w:n_a + n_w + n_e]
    o_ref = refs[n_a + n_w + n_e]
    wb_refs = refs[n_a + n_w + n_e + 1:]
    if cast_w:
        @pl.when(pl.program_id(1) == 0)
        def _():
            for w_ref, wb_ref in zip(w_refs, wb_refs):
                wb_ref[...] = w_ref[...].astype(BF16)
        w_use = wb_refs
    else:
        w_use = w_refs
    a_vals = []
    for k, a_ref in enumerate(a_refs):
        a = a_ref[...]
        if prologue is not None:
            a = prologue(k, a, e_refs)
        a_vals.append(a.astype(BF16))
    accs = [_dot(a_vals[a_of_w[k]], w_ref[...]) for k, w_ref in enumerate(w_use)]
    o_ref[...] = epilogue(accs, e_refs).astype(o_ref.dtype)


def _matmul(a_list, w_list, *, n_out, tm, tn, out_dtype, epilogue, a_of_w=None,
            prologue=None, extras=()):
    m = a_list[0][0].shape[0]
    a_of_w = a_of_w or list(range(len(w_list)))
    assert m % tm == 0 and n_out % tn == 0
    in_specs, args, scratch = [], [], []
    for arr, k, cb in a_list:
        in_specs.append(pl.BlockSpec((tm, k), lambda j, i, cb=cb: (i, cb)))
        args.append(arr)
    cast_w = w_list[0][0].dtype != BF16
    for arr, lead in w_list:
        k = arr.shape[-2]
        in_specs.append(pl.BlockSpec((None,) * len(lead) + (k, tn),
                                     lambda j, i, lead=lead: tuple(lead) + (0, j)))
        args.append(arr)
        if cast_w:
            scratch.append(pltpu.VMEM((k, tn), BF16))
    for arr, bs, im in extras:
        in_specs.append(pl.BlockSpec(bs, im))
        args.append(arr)
    kern = functools.partial(_mm_kernel, n_a=len(a_list), n_w=len(w_list), a_of_w=a_of_w,
                             n_e=len(extras), cast_w=cast_w, prologue=prologue,
                             epilogue=epilogue)
    return pl.pallas_call(
        kern,
        grid=(n_out // tn, m // tm),
        in_specs=in_specs,
        out_specs=pl.BlockSpec((tm, tn), lambda j, i: (i, j)),
        out_shape=jax.ShapeDtypeStruct((m, n_out), out_dtype),
        scratch_shapes=scratch,
        compiler_params=_params(2),
    )(*args)


def _first(accs, e_refs):
    return accs[0]


def _resid_norm_kernel(*refs, has_y, want_x, want_h, gate_idx, shift_idx, scale_idx):
    it = iter(refs)
    x_ref = next(it)
    y_ref = next(it) if has_y else None
    mod_res_ref = next(it) if has_y else None
    g_post_ref = next(it) if has_y else None
    mod_pre_ref = next(it) if want_h else None
    g_pre_ref = next(it) if want_h else None
    xo_ref = next(it) if want_x else None
    h_ref = next(it) if want_h else None
    x = x_ref[...]
    if has_y:
        gate = mod_res_ref[0, gate_idx:gate_idx + 1, :]
        x = x + gate * _rms(y_ref[...], g_post_ref[...])
    if want_x:
        xo_ref[...] = x
    if want_h:
        scale = mod_pre_ref[0, scale_idx:scale_idx + 1, :]
        shift = mod_pre_ref[0, shift_idx:shift_idx + 1, :]
        h_ref[...] = (_rms(x, g_pre_ref[...]) * (1.0 + scale) + shift).astype(h_ref.dtype)


def _resid_norm(x, sample_of_tile, tm, *, y=None, mod_res=None, g_post=None, gate_idx=0,
                mod_pre=None, g_pre=None, shift_idx=0, scale_idx=0):
    t, d = x.shape
    has_y, want_h = y is not None, mod_pre is not None
    want_x = has_y
    row = pl.BlockSpec((tm, d), lambda i: (i, 0))
    vec = pl.BlockSpec((1, d), lambda i: (0, 0))
    mod = pl.BlockSpec((1, 6, d), lambda i: (sample_of_tile(i), 0, 0))
    in_specs, args, out_specs, out_shape = [row], [x], [], []
    if has_y:
        in_specs += [row, mod, vec]
        args += [y, mod_res, g_post.reshape(1, d)]
    if want_h:
        in_specs += [mod, vec]
        args += [mod_pre, g_pre.reshape(1, d)]
    if want_x:
        out_specs.append(row)
        out_shape.append(jax.ShapeDtypeStruct((t, d), F32))
    if want_h:
        out_specs.append(row)
        out_shape.append(jax.ShapeDtypeStruct((t, d), BF16))
    kern = functools.partial(_resid_norm_kernel, has_y=has_y, want_x=want_x, want_h=want_h,
                             gate_idx=gate_idx, shift_idx=shift_idx, scale_idx=scale_idx)
    outs = pl.pallas_call(kern, grid=(t // tm,), in_specs=in_specs, out_specs=out_specs,
                          out_shape=out_shape, compiler_params=_params(1))(*args)
    x_new = outs[0] if want_x else x
    h = outs[-1] if want_h else None
    return x_new, h


def _norm_kernel(x_ref, g_ref, o_ref):
    o_ref[...] = _rms(x_ref[...], g_ref[...])


def _norm_cols(arr, width, col_block, g, tm):
    t = arr.shape[0]
    return pl.pallas_call(
        _norm_kernel, grid=(t // tm,),
        in_specs=[pl.BlockSpec((tm, width), lambda i: (i, col_block)),
                  pl.BlockSpec((1, width), lambda i: (0, 0))],
        out_specs=pl.BlockSpec((tm, width), lambda i: (i, 0)),
        out_shape=jax.ShapeDtypeStruct((t, width), F32),
        compiler_params=_params(1))(arr, g.reshape(1, width))


def _swap_halves(x, half):
    n = x.shape[-1]
    lane = lax.broadcasted_iota(jnp.int32, x.shape, x.ndim - 1)
    up = pltpu.roll(x, n - half, x.ndim - 1)
    down = pltpu.roll(x, half, x.ndim - 1)
    return jnp.where((lane & (2 * half - 1)) < half, up, down)


def _rope(x, cos, sin, half):
    return x * cos + _swap_halves(x, half) * sin


def _prep_kernel(misc_ref, cos_ref, sin_ref, wg_ref, bg_ref, kr_ref, la_ref, *, half):
    misc = misc_ref[...]
    kr_ref[...] = _rope(misc, cos_ref[...], sin_ref[...], half).astype(kr_ref.dtype)
    z = _dot(misc.astype(BF16), wg_ref[...]) + bg_ref[...]
    log_sig = jnp.minimum(z, 0.0) - jnp.log1p(jnp.exp(-jnp.abs(z)))
    la_ref[...] = log_sig * (1.0 / GATE_NORM)


def _attn_kernel(*refs, n_seg, scale):
    q_ref = refs[0]
    kv_refs = refs[1:1 + n_seg]
    kr_refs = refs[1 + n_seg:1 + 2 * n_seg]
    o_ref = refs[1 + 2 * n_seg]
    q = q_ref[...]
    qn, qr = q[:, :QK_NOPE], q[:, QK_NOPE:]
    scores, values = [], []
    for kv_ref, kr_ref in zip(kv_refs, kr_refs):
        kv = kv_ref[...]
        s = _dot_nt(qn, kv[:, :QK_NOPE]) + _dot_nt(qr, kr_ref[...])
        scores.append(s * scale)
        values.append(kv[:, QK_NOPE:])
    m = functools.reduce(jnp.maximum, [jnp.max(s, axis=-1, keepdims=True) for s in scores])
    ps = [jnp.exp(s - m) for s in scores]
    denom = functools.reduce(lambda a, b: a + b, [jnp.sum(p, axis=-1, keepdims=True) for p in ps])
    inv = 1.0 / denom
    o = functools.reduce(lambda a, b: a + b,
                         [_dot((p * inv).astype(BF16), v) for p, v in zip(ps, values)])
    o_ref[...] = o.astype(o_ref.dtype)


def _attention(q, kv, kr, *, n_batch, t_q, q_row0, segs, tq, scale):
    hw = QK_NOPE + V_HEAD
    nq = t_q // tq
    in_specs = [pl.BlockSpec((tq, hw), lambda b, h, i: (q_row0 // tq + b * nq + i, h))]
    args = [q]
    for (kv_arr, _), (tk, row0) in zip(kv, segs):
        in_specs.append(pl.BlockSpec((tk, hw), lambda b, h, i, tk=tk, row0=row0: (row0 // tk + b, h)))
        args.append(kv_arr)
    for (kr_arr, kr_row0), (tk, _) in zip(kr, segs):
        in_specs.append(pl.BlockSpec((tk, LANE), lambda b, h, i, tk=tk, r0=kr_row0: (r0 // tk + b, 0)))
        args.append(kr_arr)
    kern = functools.partial(_attn_kernel, n_seg=len(segs), scale=scale)
    return pl.pallas_call(
        kern, grid=(n_batch, MLA_HEADS, nq), in_specs=in_specs,
        out_specs=pl.BlockSpec((tq, V_HEAD), lambda b, h, i: (b * nq + i, h)),
        out_shape=jax.ShapeDtypeStruct((n_batch * t_q, MLA_HEADS * V_HEAD), BF16),
        compiler_params=_params(3))(*args)


def _split3(x):
    hi = x.astype(BF16)
    r1 = x - hi.astype(F32)
    mid = r1.astype(BF16)
    lo = (r1 - mid.astype(F32)).astype(BF16)
    return hi, mid, lo


def _gla_kernel(*refs, has_init, n_chunks, q_scale):
    if has_init:
        (q_ref, k_ref, v_ref, laf_ref, lab_ref, gout_ref, ggla_ref, s0f_ref, s0b_ref,
         o_ref, sf_ref, sb_ref, stf, stb, oacc) = refs
    else:
        (q_ref, k_ref, v_ref, laf_ref, lab_ref, gout_ref, ggla_ref,
         o_ref, sf_ref, sb_ref, stf, stb, oacc) = refs
    c = GLA_CHUNK
    if has_init:
        stf[...] = s0f_ref[0, 0].T
        stb[...] = s0b_ref[0, 0].T
    else:
        stf[...] = jnp.zeros_like(stf)
        stb[...] = jnp.zeros_like(stb)
    oacc[...] = jnp.zeros_like(oacc)
    row = lax.broadcasted_iota(jnp.int32, (c, c), 0)
    col = lax.broadcasted_iota(jnp.int32, (c, c), 1)
    lower = col <= row
    upper = col >= row

    def step(c0, la_ref, st, mask, end_row):
        tri = jnp.where(mask, 1.0, 0.0).astype(BF16)
        qc = q_ref[pl.ds(c0, c), :] * q_scale
        kc = k_ref[pl.ds(c0, c), :]
        vc = v_ref[pl.ds(c0, c), :].astype(BF16)
        hi, mid, lo = _split3(la_ref[pl.ds(c0, c), :])
        b = (_dot(tri, hi) + _dot(tri, mid)) + _dot(tri, lo)
        b_end = b[end_row:end_row + 1, :]
        qd = (qc * jnp.exp(b)).astype(BF16)
        kd = (kc * jnp.exp(-b)).astype(BF16)
        a = jnp.where(mask, _dot_nt(qd, kd), 0.0).astype(BF16)
        s_t = st[...]
        o = _dot_nt(qd, s_t.astype(BF16)) + _dot(a, vc)
        ks = (kc * jnp.exp(b_end - b)).astype(BF16)
        st[...] = s_t * jnp.exp(b_end) + _dot_tn(vc, ks)
        oacc[pl.ds(c0, c), :] += o

    def body(i, carry):
        step(pl.multiple_of(i * c, c), laf_ref, stf, lower, c - 1)
        step(pl.multiple_of((n_chunks - 1 - i) * c, c), lab_ref, stb, upper, 0)
        return carry

    lax.fori_loop(0, n_chunks, body, 0)
    g = gout_ref[...]
    o_ref[...] = (_rms(oacc[...], ggla_ref[...]) * (g * jax.nn.sigmoid(g))).astype(o_ref.dtype)
    sf_ref[0, 0] = stf[...].T
    sb_ref[0, 0] = stb[...].T


def _gla(proj, la, g_gla, *, n_batch, t_b, row0, heads, dk, dv, col_q, col_k, col_v, col_g,
         s0f=None, s0b=None, layer=0):
    has_init = s0f is not None
    rb = row0 // t_b
    in_specs = [
        pl.BlockSpec((t_b, dk), lambda b, h: (rb + b, col_q // dk + h)),
        pl.BlockSpec((t_b, dk), lambda b, h: (rb + b, col_k // dk + h)),
        pl.BlockSpec((t_b, dv), lambda b, h: (rb + b, col_v // dv + h)),
        pl.BlockSpec((t_b, dk), lambda b, h: (rb + b, h)),
        pl.BlockSpec((t_b, dk), lambda b, h: (rb + b, heads + h)),
        pl.BlockSpec((t_b, dv), lambda b, h: (rb + b, col_g // dv + h)),
        pl.BlockSpec((1, dv), lambda b, h: (0, 0)),
    ]
    args = [proj, proj, proj, la, la, proj, g_gla.reshape(1, dv)]
    if has_init:
        st_spec = pl.BlockSpec((1, None, 1, dk, dv), lambda b, h: (b, layer, h, 0, 0))
        in_specs += [st_spec, st_spec]
        args += [s0f, s0b]
    st_out = pl.BlockSpec((1, 1, dk, dv), lambda b, h: (b, h, 0, 0))
    kern = functools.partial(_gla_kernel, has_init=has_init, n_chunks=t_b // GLA_CHUNK,
                             q_scale=dk ** -0.5)
    return pl.pallas_call(
        kern, grid=(n_batch, heads), in_specs=in_specs,
        out_specs=[pl.BlockSpec((t_b, dv), lambda b, h: (b, h)), st_out, st_out],
        out_shape=[jax.ShapeDtypeStruct((n_batch * t_b, heads * dv), BF16),
                   jax.ShapeDtypeStruct((n_batch, heads, dk, dv), F32),
                   jax.ShapeDtypeStruct((n_batch, heads, dk, dv), F32)],
        scratch_shapes=[pltpu.VMEM((dv, dk), F32), pltpu.VMEM((dv, dk), F32),
                        pltpu.VMEM((t_b, dv), F32)],
        compiler_params=_params(2))(*args)


def _router_kernel(h_ref, w_ref, o_ref, *, n_experts):
    logits = _dot(h_ref[...], w_ref[...])
    lane = lax.broadcasted_iota(jnp.int32, logits.shape, 1).astype(F32)
    neg = jnp.float32(-jnp.inf)
    lg = jnp.where(lane < n_experts, logits, neg)
    m1 = jnp.max(lg, axis=-1, keepdims=True)
    i1 = jnp.min(jnp.where(lg == m1, lane, LANE), axis=-1, keepdims=True)
    lg2 = jnp.where(lane == i1, neg, lg)
    m2 = jnp.max(lg2, axis=-1, keepdims=True)
    i2 = jnp.min(jnp.where(lg2 == m2, lane, LANE), axis=-1, keepdims=True)
    e2 = jnp.exp(m2 - m1)
    inv = 1.0 / (1.0 + e2)
    o_ref[...] = jnp.where(lane == i1, inv, 0.0) + jnp.where(lane == i2, e2 * inv, 0.0)


def _silu(x):
    return x * jax.nn.sigmoid(x)


def _rope_tables(n_lat, rope, tm_id):
    axis_half = rope // 4
    rows = n_lat // GRID_W
    r = jnp.repeat(jnp.arange(rows, dtype=F32), GRID_W)
    c = jnp.tile(jnp.arange(GRID_W, dtype=F32), rows)
    inv = ROPE_THETA ** (-jnp.arange(axis_half, dtype=F32) / axis_half)
    ar, ac = r[:, None] * inv, c[:, None] * inv
    cos = jnp.concatenate([jnp.cos(ar), jnp.cos(ar), jnp.cos(ac), jnp.cos(ac)], axis=1)
    sin = jnp.concatenate([-jnp.sin(ar), jnp.sin(ar), -jnp.sin(ac), jnp.sin(ac)], axis=1)
    cos = jnp.concatenate([jnp.ones((tm_id, rope), F32), cos], axis=0)
    sin = jnp.concatenate([jnp.zeros((tm_id, rope), F32), sin], axis=0)
    return cos, sin


def kernel(x_prompt, x_sample, cache_ckv, cache_krope, state_gla_fwd, state_gla_bwd, c, c_ctx, w_ada, b_ada, g_pre_mix, g_post_mix, g_pre_ffn, g_post_ffn, w_in, g_q, w_uq, g_kv, w_ukv, w_gate_f, b_gate_f, w_gate_b, b_gate_b, g_gla, w_pa, w_pb, w_o, w_ff_gate, w_ff_up, w_ff_down, w_router, w_ex_gate, w_ex_up, w_ex_down):
    n_ctx_b, t_ctx, d = x_prompt.shape
    n_lat_b, t_lat, _ = x_sample.shape
    depth = w_in.shape[0]
    past = cache_ckv.shape[2]
    q_lora, kv_lora = g_q.shape[1], g_kv.shape[1]
    rope = cache_krope.shape[3]
    heads, dk, dv = state_gla_fwd.shape[2:]
    rank = w_gate_f.shape[1]
    n_experts = w_router.shape[2]
    hq = QK_NOPE + rope
    hw = QK_NOPE + V_HEAD
    n_ctx, n_lat = n_ctx_b * t_ctx, n_lat_b * t_lat
    t = n_ctx + n_lat
    half = rope // 4
    assert 2 * rope == LANE and 2 * rank <= LANE - rope

    tm = _tile(math.gcd(n_ctx, t_lat), 512)
    n_ctx_tiles = n_ctx // tm
    lat_tiles = t_lat // tm

    def sample_of_tile(i):
        return jnp.where(i < n_ctx_tiles, 0, 1 + (i - n_ctx_tiles) // lat_tiles)

    def pos_of_tile(i):
        return jnp.where(i < n_ctx_tiles, 0, 1 + (i - n_ctx_tiles) % lat_tiles)

    sizes = (q_lora, kv_lora, rope, heads * dk, heads * dk, heads * dv, rank, rank,
             heads * dv, d, d)
    offs = [0]
    for s in sizes:
        offs.append(offs[-1] + s)
    main_groups = (0, 1, 3, 4, 5, 8, 9, 10)
    col = {}
    acc = 0
    for gidx in main_groups:
        col[gidx] = acc
        acc += sizes[gidx]
    n_main = acc
    col_q, col_k, col_v, col_g, col_a, col_b = col[3], col[4], col[5], col[8], col[9], col[10]

    cos_r, sin_r = _rope_tables(t_lat, rope, tm)
    n_tab = cos_r.shape[0]
    cos_q = jnp.concatenate([jnp.ones((n_tab, QK_NOPE), F32), cos_r,
                             jnp.ones((n_tab, hw - hq), F32)], axis=1)
    sin_q = jnp.concatenate([jnp.zeros((n_tab, QK_NOPE), F32), sin_r,
                             jnp.zeros((n_tab, hw - hq), F32)], axis=1)
    cos_k = jnp.concatenate([cos_r, jnp.zeros((n_tab, LANE - rope), F32)], axis=1)
    sin_k = jnp.concatenate([sin_r, jnp.zeros((n_tab, LANE - rope), F32)], axis=1)

    x = jnp.concatenate([x_prompt.reshape(n_ctx, d), x_sample.reshape(n_lat, d)], axis=0)
    c_all = jnp.concatenate([c_ctx[None, :], c, jnp.zeros((8 - 1 - n_lat_b, d), F32)], axis=0)

    def modulation(l):
        def epi(accs, e_refs):
            return accs[0] + e_refs[0][...]
        tn = _tile(6 * d, 1024)
        m = _matmul([(c_all, d, 0)], [(w_ada, (l,))], n_out=6 * d, tm=8, tn=tn, out_dtype=F32,
                    epilogue=epi, prologue=lambda k, a, e: _silu(a),
                    extras=[(b_ada.reshape(depth, 1, 6 * d), (None, 1, tn),
                             lambda j, i, l=l: (l, 0, j))])
        return m.reshape(8, 6, d)

    mods = [modulation(l) for l in range(depth)]
    _, h = _resid_norm(x, sample_of_tile, tm, mod_pre=mods[0], g_pre=g_pre_mix[0],
                       shift_idx=0, scale_idx=1)

    new_ckv, new_krope, new_sf, new_sb = [], [], [], []
    for l in range(depth):
        w = w_in[l]
        w_main = jnp.concatenate([w[:, offs[gidx]:offs[gidx + 1]] for gidx in main_groups],
                                 axis=1).astype(BF16)
        w_misc = jnp.concatenate([w[:, offs[2]:offs[3]], w[:, offs[6]:offs[8]],
                                  jnp.zeros((d, LANE - rope - 2 * rank), F32)], axis=1).astype(BF16)
        proj = _matmul([(h, d, 0)], [(w_main, ())], n_out=n_main, tm=tm, tn=_tile(n_main, 1024),
                       out_dtype=F32, epilogue=_first)
        misc = _matmul([(h, d, 0)], [(w_misc, ())], n_out=LANE, tm=tm, tn=LANE,
                       out_dtype=F32, epilogue=_first)

        w_gate = jnp.zeros((LANE, 2 * heads * dk), F32)
        w_gate = w_gate.at[rope:rope + rank, :heads * dk].set(w_gate_f[l])
        w_gate = w_gate.at[rope + rank:rope + 2 * rank, heads * dk:].set(w_gate_b[l])
        b_gate = jnp.concatenate([b_gate_f[l], b_gate_b[l]]).reshape(1, 2 * heads * dk)
        kr_self, la = pl.pallas_call(
            functools.partial(_prep_kernel, half=half), grid=(t // tm,),
            in_specs=[pl.BlockSpec((tm, LANE), lambda i: (i, 0)),
                      pl.BlockSpec((tm, LANE), lambda i: (pos_of_tile(i), 0)),
                      pl.BlockSpec((tm, LANE), lambda i: (pos_of_tile(i), 0)),
                      pl.BlockSpec((LANE, 2 * heads * dk), lambda i: (0, 0)),
                      pl.BlockSpec((1, 2 * heads * dk), lambda i: (0, 0))],
            out_specs=[pl.BlockSpec((tm, LANE), lambda i: (i, 0)),
                       pl.BlockSpec((tm, 2 * heads * dk), lambda i: (i, 0))],
            out_shape=[jax.ShapeDtypeStruct((t, LANE), BF16),
                       jax.ShapeDtypeStruct((t, 2 * heads * dk), F32)],
            compiler_params=_params(1))(misc, cos_k, sin_k, w_gate.astype(BF16), b_gate)

        wq = w_uq[l].reshape(q_lora, MLA_HEADS, hq)
        wq = jnp.concatenate([wq, jnp.zeros((q_lora, MLA_HEADS, hw - hq), F32)], axis=2)
        wq = wq.reshape(q_lora, MLA_HEADS * hw).astype(BF16)
        tn_q = _tile(MLA_HEADS * hw, 1024, hw)

        def q_epi(accs, e_refs, tn_q=tn_q):
            cos, sin = e_refs[1][...], e_refs[2][...]
            outs = [_rope(accs[0][:, s:s + hw], cos, sin, half) for s in range(0, tn_q, hw)]
            return jnp.concatenate(outs, axis=1)

        q = _matmul([(proj, q_lora, col[0] // q_lora)], [(wq, ())], n_out=MLA_HEADS * hw, tm=tm,
                    tn=tn_q, out_dtype=BF16, epilogue=q_epi,
                    prologue=lambda k, a, e: _rms(a, e[0][...]),
                    extras=[(g_q[l].reshape(1, q_lora), (1, q_lora), lambda j, i: (0, 0)),
                            (cos_q, (tm, hw), lambda j, i: (pos_of_tile(i), 0)),
                            (sin_q, (tm, hw), lambda j, i: (pos_of_tile(i), 0))])

        c_kv = _norm_cols(proj, kv_lora, col[1] // kv_lora, g_kv[l], tm)
        tn_kv = _tile(MLA_HEADS * hw, 1024)
        kv_self = _matmul([(c_kv, kv_lora, 0)], [(w_ukv, (l,))], n_out=MLA_HEADS * hw, tm=tm,
                          tn=tn_kv, out_dtype=BF16, epilogue=_first)
        ckv_cache = cache_ckv[:, l].reshape(n_lat_b * past, kv_lora)
        kv_cache = _matmul([(ckv_cache, kv_lora, 0)], [(w_ukv, (l,))], n_out=MLA_HEADS * hw,
                           tm=_tile(n_lat_b * past, 512), tn=tn_kv, out_dtype=BF16,
                           epilogue=_first)
        kr_cache = jnp.pad(cache_krope[:, l].reshape(n_lat_b * past, rope),
                           ((0, 0), (0, LANE - rope))).astype(BF16)

        scale = hq ** -0.5
        tq = _tile(t_lat, 256)
        attn_ctx = _attention(q, [(kv_self, 0)], [(kr_self, 0)], n_batch=n_ctx_b, t_q=t_ctx,
                              q_row0=0, segs=[(t_ctx, 0)], tq=_tile(t_ctx, 256), scale=scale)
        attn_lat = _attention(q, [(kv_self, 0), (kv_cache, 0)], [(kr_self, n_ctx), (kr_cache, 0)],
                              n_batch=n_lat_b, t_q=t_lat, q_row0=n_ctx,
                              segs=[(t_lat, n_ctx), (past, 0)], tq=tq, scale=scale)
        attn = jnp.concatenate([attn_ctx, attn_lat], axis=0)

        gla_kw = dict(heads=heads, dk=dk, dv=dv, col_q=col_q, col_k=col_k, col_v=col_v,
                      col_g=col_g)
        gla_ctx, sf, sb = _gla(proj, la, g_gla[l], n_batch=n_ctx_b, t_b=t_ctx, row0=0, **gla_kw)
        gla_lat, _, _ = _gla(proj, la, g_gla[l], n_batch=n_lat_b, t_b=t_lat, row0=n_ctx,
                             s0f=state_gla_fwd, s0b=state_gla_bwd, layer=l, **gla_kw)
        gla = jnp.concatenate([gla_ctx, gla_lat], axis=0)

        new_ckv.append(c_kv[:n_ctx].reshape(n_ctx_b, t_ctx, kv_lora))
        new_krope.append(misc[:n_ctx, :rope].reshape(n_ctx_b, t_ctx, rope))
        new_sf.append(sf)
        new_sb.append(sb)

        tn_m = _tile(math.gcd(col_a, col_b, d), 512)

        def merge_epi(accs, e_refs):
            return (jax.nn.sigmoid(e_refs[0][...]) * accs[0]
                    + jax.nn.sigmoid(e_refs[1][...]) * accs[1])

        merged = _matmul([(attn, MLA_HEADS * V_HEAD, 0), (gla, heads * dv, 0)],
                         [(w_pa, (l,)), (w_pb, (l,))], n_out=d, tm=tm, tn=tn_m, out_dtype=BF16,
                         epilogue=merge_epi,
                         extras=[(proj, (tm, tn_m), lambda j, i: (i, col_a // tn_m + j)),
                                 (proj, (tm, tn_m), lambda j, i: (i, col_b // tn_m + j))])
        y = _matmul([(merged, d, 0)], [(w_o, (l,))], n_out=d, tm=tm, tn=_tile(d, 1024),
                    out_dtype=F32, epilogue=_first)
        x, h = _resid_norm(x, sample_of_tile, tm, y=y, mod_res=mods[l], g_post=g_post_mix[l],
                           gate_idx=2, mod_pre=mods[l], g_pre=g_pre_ffn[l], shift_idx=3,
                           scale_idx=4)

        jx = l // 2

        def swiglu_epi(accs, e_refs):
            return _silu(accs[0]) * accs[1]

        if l % 2 == 0:
            d_ff = w_ff_gate.shape[2]
            ff = _matmul([(h, d, 0)], [(w_ff_gate, (jx,)), (w_ff_up, (jx,))], a_of_w=[0, 0],
                         n_out=d_ff, tm=tm, tn=_tile(d_ff, 512), out_dtype=BF16,
                         epilogue=swiglu_epi)
            y = _matmul([(ff, d_ff, 0)], [(w_ff_down, (jx,))], n_out=d, tm=tm, tn=_tile(d, 256),
                        out_dtype=F32, epilogue=_first)
        else:
            d_ex = w_ex_gate.shape[3]
            w_r = jnp.pad(w_router[jx], ((0, 0), (0, LANE - n_experts))).astype(BF16)
            combine = pl.pallas_call(
                functools.partial(_router_kernel, n_experts=n_experts), grid=(t // tm,),
                in_specs=[pl.BlockSpec((tm, d), lambda i: (i, 0)),
                          pl.BlockSpec((d, LANE), lambda i: (0, 0))],
                out_specs=pl.BlockSpec((tm, LANE), lambda i: (i, 0)),
                out_shape=jax.ShapeDtypeStruct((t, LANE), F32),
                compiler_params=_params(1))(h, w_r)
            y = None
            tn_d = _tile(d, 512)
            for e in range(n_experts):
                ff = _matmul([(h, d, 0)], [(w_ex_gate, (jx, e)), (w_ex_up, (jx, e))],
                             a_of_w=[0, 0], n_out=d_ex, tm=tm, tn=_tile(d_ex, 256),
                             out_dtype=BF16, epilogue=swiglu_epi)
                if y is None:
                    def down_epi(accs, e_refs, e=e):
                        return e_refs[0][...][:, e:e + 1] * accs[0]
                    extras = [(combine, (tm, LANE), lambda j, i: (i, 0))]
                else:
                    def down_epi(accs, e_refs, e=e):
                        return e_refs[1][...] + e_refs[0][...][:, e:e + 1] * accs[0]
                    extras = [(combine, (tm, LANE), lambda j, i: (i, 0)),
                              (y, (tm, tn_d), lambda j, i: (i, j))]
                y = _matmul([(ff, d_ex, 0)], [(w_ex_down, (jx, e))], n_out=d, tm=tm, tn=tn_d,
                            out_dtype=F32, epilogue=down_epi, extras=extras)

        if l + 1 < depth:
            x, h = _resid_norm(x, sample_of_tile, tm, y=y, mod_res=mods[l], g_post=g_post_ffn[l],
                               gate_idx=5, mod_pre=mods[l + 1], g_pre=g_pre_mix[l + 1],
                               shift_idx=0, scale_idx=1)
        else:
            x, _ = _resid_norm(x, sample_of_tile, tm, y=y, mod_res=mods[l], g_post=g_post_ffn[l],
                               gate_idx=5)

    return (x[:n_ctx].reshape(n_ctx_b, t_ctx, d), x[n_ctx:].reshape(n_lat_b, t_lat, d),
            jnp.stack(new_ckv, axis=1), jnp.stack(new_krope, axis=1),
            jnp.stack(new_sf, axis=1), jnp.stack(new_sb, axis=1))
```

```python
import functools
import math

import jax
import jax.numpy as jnp
from jax import lax
from jax.experimental import pallas as pl
from jax.experimental.pallas import tpu as pltpu

MLA_HEADS = 16
QK_NOPE = 128
V_HEAD = 128
GRID_W = 64
ROPE_THETA = 10000.0
GATE_NORM = 16.0
GLA_CHUNK = 64
TOP_K = 2
EPS = 1e-6
LANE = 128
VMEM_LIMIT = 52 * 1024 * 1024
EXPERT_TILE = 512

BF16 = jnp.bfloat16
F32 = jnp.float32


def _params(n_grid, **kw):
    return pltpu.CompilerParams(
        dimension_semantics=("arbitrary",) * n_grid, vmem_limit_bytes=VMEM_LIMIT, **kw)


def _tile(n, target, quantum=LANE):
    if n <= target:
        return n
    t = (target // quantum) * quantum
    while t >= quantum:
        if n % t == 0:
            return t
        t -= quantum
    return n


def _rms(x, g):
    return x * lax.rsqrt(jnp.mean(x * x, axis=-1, keepdims=True) + EPS) * g


def _silu(x):
    return x * jax.nn.sigmoid(x)


def _dot(a, b):
    return jnp.dot(a, b, preferred_element_type=F32)


def _dot_nt(a, b):
    return lax.dot_general(a, b, (((1,), (1,)), ((), ())), preferred_element_type=F32)


def _dot_tn(a, b):
    return lax.dot_general(a, b, (((0,), (0,)), ((), ())), preferred_element_type=F32)


def _mm_kernel(*refs, n_a, n_w, a_of_w, n_e, cast_w, grouped, prologue, epilogue):
    if grouped:
        te_ref, refs = refs[0], refs[1:]
    a_refs = refs[:n_a]
    w_refs = refs[n_a:n_a + n_w]
    e_refs = refs[n_a + n_w:n_a + n_w + n_e]
    o_ref = refs[n_a + n_w + n_e]
    wb_refs = refs[n_a + n_w + n_e + 1:]
    if cast_w:
        i = pl.program_id(1)
        fresh = i == 0
        if grouped:
            fresh = fresh | (te_ref[i] != te_ref[jnp.maximum(i - 1, 0)])

        @pl.when(fresh)
        def _():
            for w_ref, wb_ref in zip(w_refs, wb_refs):
                wb_ref[...] = w_ref[...].astype(BF16)
        w_use = wb_refs
    else:
        w_use = w_refs
    a_vals = []
    for k, a_ref in enumerate(a_refs):
        a = a_ref[...]
        if prologue is not None:
            a = prologue(k, a, e_refs)
        a_vals.append(a.astype(BF16))
    accs = [_dot(a_vals[a_of_w[k]], w_ref[...]) for k, w_ref in enumerate(w_use)]
    o_ref[...] = epilogue(accs, e_refs).astype(o_ref.dtype)


def _matmul(a_list, w_list, *, n_out, tm, tn, out_dtype, epilogue, name, a_of_w=None,
            prologue=None, extras=(), tile_expert=None):
    m = a_list[0][0].shape[0]
    a_of_w = a_of_w or list(range(len(w_list)))
    grouped = tile_expert is not None
    assert m % tm == 0 and n_out % tn == 0
    in_specs, args, scratch = [], [], []
    for arr, k, cb in a_list:
        in_specs.append(pl.BlockSpec((tm, k), lambda j, i, *_, cb=cb: (i, cb)))
        args.append(arr)
    cast_w = w_list[0][0].dtype != BF16
    for arr, lead in w_list:
        k = arr.shape[-2]
        if grouped:
            im = lambda j, i, te, lead=lead: tuple(lead[:-1]) + (te[i], 0, j)
        else:
            im = lambda j, i, lead=lead: tuple(lead) + (0, j)
        in_specs.append(pl.BlockSpec((None,) * len(lead) + (k, tn), im))
        args.append(arr)
        if cast_w:
            scratch.append(pltpu.VMEM((k, tn), BF16))
    for arr, bs, im in extras:
        in_specs.append(pl.BlockSpec(bs, lambda j, i, *_, im=im: im(j, i)))
        args.append(arr)
    kern = functools.partial(_mm_kernel, n_a=len(a_list), n_w=len(w_list), a_of_w=a_of_w,
                             n_e=len(extras), cast_w=cast_w, grouped=grouped,
                             prologue=prologue, epilogue=epilogue)
    grid_spec = pltpu.PrefetchScalarGridSpec(
        num_scalar_prefetch=1 if grouped else 0,
        grid=(n_out // tn, m // tm),
        in_specs=in_specs,
        out_specs=pl.BlockSpec((tm, tn), lambda j, i, *_: (i, j)),
        scratch_shapes=scratch)
    if grouped:
        args = [tile_expert] + args
    return pl.pallas_call(
        kern, grid_spec=grid_spec,
        out_shape=jax.ShapeDtypeStruct((m, n_out), out_dtype),
        compiler_params=_params(2), name=name,
    )(*args)


def _first(accs, e_refs):
    return accs[0]


def _row_copy(src_hbm, row, dst, r, sem):
    return pltpu.make_async_copy(src_hbm.at[pl.ds(row, 1), :], dst.at[pl.ds(r, 1), :], sem)


def _gather_rows(idx_ref, base, stride, n, src_hbm, dst, sem):
    def issue(r, carry):
        _row_copy(src_hbm, idx_ref[base + r * stride], dst, r, sem).start()
        return carry

    def wait(r, carry):
        _row_copy(src_hbm, 0, dst, r, sem).wait()
        return carry

    lax.fori_loop(0, n, issue, 0, unroll=8)
    lax.fori_loop(0, n, wait, 0, unroll=8)


def _gather_cast_kernel(idx_ref, src_hbm, o_ref, buf, sem):
    tg = o_ref.shape[0]
    _gather_rows(idx_ref, pl.program_id(0) * tg, 1, tg, src_hbm, buf, sem)
    o_ref[...] = buf[...].astype(o_ref.dtype)


def _gather_cast(src, idx, tg, out_dtype, name):
    n, d = idx.shape[0], src.shape[1]
    grid_spec = pltpu.PrefetchScalarGridSpec(
        num_scalar_prefetch=1, grid=(n // tg,),
        in_specs=[pl.BlockSpec(memory_space=pl.ANY)],
        out_specs=pl.BlockSpec((tg, d), lambda i, idx: (i, 0)),
        scratch_shapes=[pltpu.VMEM((tg, d), src.dtype), pltpu.SemaphoreType.DMA(())])
    return pl.pallas_call(
        _gather_cast_kernel, grid_spec=grid_spec,
        out_shape=jax.ShapeDtypeStruct((n, d), out_dtype),
        compiler_params=_params(1), name=name)(idx, src)


def _resid_norm_kernel(*refs, mode, want_x, want_h, gate_idx, shift_idx, scale_idx):
    it = iter(refs)
    dest_ref = next(it) if mode == "routed" else None
    x_ref = next(it)
    if mode == "dense":
        y_ref = next(it)
    elif mode == "routed":
        ys_hbm, route_ref = next(it), next(it)
    if mode is not None:
        mod_res_ref, g_post_ref = next(it), next(it)
    if want_h:
        mod_pre_ref, g_pre_ref = next(it), next(it)
    xo_ref = next(it) if want_x else None
    h_ref = next(it) if want_h else None
    x = x_ref[...]
    if mode is not None:
        if mode == "routed":
            buf1, buf2, sem = next(it), next(it), next(it)
            tm = x_ref.shape[0]
            base = pl.program_id(0) * tm * TOP_K
            _gather_rows(dest_ref, base, TOP_K, tm, ys_hbm, buf1, sem)
            _gather_rows(dest_ref, base + 1, TOP_K, tm, ys_hbm, buf2, sem)
            route = route_ref[...]
            y = route[:, 2:3] * buf1[...] + route[:, 3:4] * buf2[...]
        else:
            y = y_ref[...]
        gate = mod_res_ref[0, gate_idx:gate_idx + 1, :]
        x = x + gate * _rms(y, g_post_ref[...])
    if want_x:
        xo_ref[...] = x
    if want_h:
        scale = mod_pre_ref[0, scale_idx:scale_idx + 1, :]
        shift = mod_pre_ref[0, shift_idx:shift_idx + 1, :]
        h_ref[...] = (_rms(x, g_pre_ref[...]) * (1.0 + scale) + shift).astype(h_ref.dtype)


def _resid_norm(x, sample_of_tile, tm, name, *, y=None, routed=None, mod_res=None, g_post=None,
                gate_idx=0, mod_pre=None, g_pre=None, shift_idx=0, scale_idx=0, h_dtype=BF16):
    t, d = x.shape
    mode = "dense" if y is not None else ("routed" if routed is not None else None)
    want_h = mod_pre is not None
    want_x = mode is not None
    row = pl.BlockSpec((tm, d), lambda i, *_: (i, 0))
    vec = pl.BlockSpec((1, d), lambda i, *_: (0, 0))
    mod = pl.BlockSpec((1, 6, d), lambda i, *_: (sample_of_tile(i), 0, 0))
    in_specs, args, out_specs, out_shape, scratch = [row], [x], [], [], []
    if mode == "dense":
        in_specs.append(row)
        args.append(y)
    elif mode == "routed":
        ys, dest, route = routed
        in_specs += [pl.BlockSpec(memory_space=pl.ANY),
                     pl.BlockSpec((tm, LANE), lambda i, *_: (i, 0))]
        args += [ys, route]
        scratch = [pltpu.VMEM((tm, d), F32), pltpu.VMEM((tm, d), F32),
                   pltpu.SemaphoreType.DMA(())]
    if mode is not None:
        in_specs += [mod, vec]
        args += [mod_res, g_post.reshape(1, d)]
    if want_h:
        in_specs += [mod, vec]
        args += [mod_pre, g_pre.reshape(1, d)]
    if want_x:
        out_specs.append(row)
        out_shape.append(jax.ShapeDtypeStruct((t, d), F32))
    if want_h:
        out_specs.append(row)
        out_shape.append(jax.ShapeDtypeStruct((t, d), h_dtype))
    kern = functools.partial(_resid_norm_kernel, mode=mode, want_x=want_x, want_h=want_h,
                             gate_idx=gate_idx, shift_idx=shift_idx, scale_idx=scale_idx)
    grid_spec = pltpu.PrefetchScalarGridSpec(
        num_scalar_prefetch=1 if mode == "routed" else 0, grid=(t // tm,),
        in_specs=in_specs, out_specs=out_specs, scratch_shapes=scratch)
    if mode == "routed":
        args = [dest] + args
    outs = pl.pallas_call(kern, grid_spec=grid_spec, out_shape=out_shape,
                          compiler_params=_params(1), name=name)(*args)
    x_new = outs[0] if want_x else x
    h = outs[-1] if want_h else None
    return x_new, h


def _norm_kernel(x_ref, g_ref, o_ref):
    o_ref[...] = _rms(x_ref[...], g_ref[...])


def _norm_cols(arr, width, col_block, g, tm, name):
    t = arr.shape[0]
    return pl.pallas_call(
        _norm_kernel, grid=(t // tm,),
        in_specs=[pl.BlockSpec((tm, width), lambda i: (i, col_block)),
                  pl.BlockSpec((1, width), lambda i: (0, 0))],
        out_specs=pl.BlockSpec((tm, width), lambda i: (i, 0)),
        out_shape=jax.ShapeDtypeStruct((t, width), F32),
        compiler_params=_params(1), name=name)(arr, g.reshape(1, width))


def _swap_halves(x, half):
    n = x.shape[-1]
    lane = lax.broadcasted_iota(jnp.int32, x.shape, x.ndim - 1)
    up = pltpu.roll(x, n - half, x.ndim - 1)
    down = pltpu.roll(x, half, x.ndim - 1)
    return jnp.where((lane & (2 * half - 1)) < half, up, down)


def _rope(x, cos, sin, half):
    return x * cos + _swap_halves(x, half) * sin


def _prep_kernel(misc_ref, cos_ref, sin_ref, wg_ref, bg_ref, kr_ref, la_ref, *, half):
    misc = misc_ref[...]
    kr_ref[...] = _rope(misc, cos_ref[...], sin_ref[...], half).astype(kr_ref.dtype)
    z = _dot(misc.astype(BF16), wg_ref[...]) + bg_ref[...]
    log_sig = jnp.minimum(z, 0.0) - jnp.log1p(jnp.exp(-jnp.abs(z)))
    la_ref[...] = log_sig * (1.0 / GATE_NORM)


def _attn_kernel(*refs, n_seg, n_heads, scale):
    q_ref = refs[0]
    kv_refs = refs[1:1 + n_seg]
    kr_refs = refs[1 + n_seg:1 + 2 * n_seg]
    o_ref = refs[1 + 2 * n_seg]
    hw = QK_NOPE + V_HEAD
    krs = [kr_ref[...] for kr_ref in kr_refs]
    for hd in range(n_heads):
        qn = q_ref[:, hd * hw:hd * hw + QK_NOPE]
        qr = q_ref[:, hd * hw + QK_NOPE:(hd + 1) * hw]
        scores = []
        for kv_ref, kr in zip(kv_refs, krs):
            s = _dot_nt(qn, kv_ref[:, hd * hw:hd * hw + QK_NOPE]) + _dot_nt(qr, kr)
            scores.append(s * scale)
        m = functools.reduce(jnp.maximum, [jnp.max(s, axis=-1, keepdims=True) for s in scores])
        ps = [jnp.exp(s - m) for s in scores]
        denom = functools.reduce(lambda a, b: a + b,
                                 [jnp.sum(p, axis=-1, keepdims=True) for p in ps])
        inv = 1.0 / denom
        o = functools.reduce(
            lambda a, b: a + b,
            [_dot((p * inv).astype(BF16), kv_ref[:, hd * hw + QK_NOPE:(hd + 1) * hw])
             for p, kv_ref in zip(ps, kv_refs)])
        o_ref[:, hd * V_HEAD:(hd + 1) * V_HEAD] = o.astype(o_ref.dtype)


def _attention(q, kvs, krs, *, n_batch, t_q, q_row0, tq, n_heads, scale, name):
    hw = QK_NOPE + V_HEAD
    nq = t_q // tq
    in_specs = [pl.BlockSpec((tq, n_heads * hw),
                             lambda b, g, i: (q_row0 // tq + b * nq + i, g))]
    args = [q]
    for arr, tk, row0 in kvs:
        in_specs.append(pl.BlockSpec((tk, n_heads * hw),
                                     lambda b, g, i, tk=tk, row0=row0: (row0 // tk + b, g)))
        args.append(arr)
    for arr, tk, row0 in krs:
        in_specs.append(pl.BlockSpec((tk, LANE),
                                     lambda b, g, i, tk=tk, row0=row0: (row0 // tk + b, 0)))
        args.append(arr)
    kern = functools.partial(_attn_kernel, n_seg=len(kvs), n_heads=n_heads, scale=scale)
    return pl.pallas_call(
        kern, grid=(n_batch, MLA_HEADS // n_heads, nq), in_specs=in_specs,
        out_specs=pl.BlockSpec((tq, n_heads * V_HEAD), lambda b, g, i: (b * nq + i, g)),
        out_shape=jax.ShapeDtypeStruct((n_batch * t_q, MLA_HEADS * V_HEAD), BF16),
        compiler_params=_params(3), name=name)(*args)


def _split3(x):
    hi = x.astype(BF16)
    r1 = x - hi.astype(F32)
    mid = r1.astype(BF16)
    lo = (r1 - mid.astype(F32)).astype(BF16)
    return hi, mid, lo


def _gla_kernel(*refs, has_init, n_chunks, q_scale):
    if has_init:
        (q_ref, k_ref, v_ref, laf_ref, lab_ref, gout_ref, ggla_ref, s0f_ref, s0b_ref,
         o_ref, sf_ref, sb_ref, stf, stb, oacc) = refs
    else:
        (q_ref, k_ref, v_ref, laf_ref, lab_ref, gout_ref, ggla_ref,
         o_ref, sf_ref, sb_ref, stf, stb, oacc) = refs
    c = GLA_CHUNK
    if has_init:
        stf[...] = s0f_ref[0, 0].T
        stb[...] = s0b_ref[0, 0].T
    else:
        stf[...] = jnp.zeros_like(stf)
        stb[...] = jnp.zeros_like(stb)
    oacc[...] = jnp.zeros_like(oacc)
    row = lax.broadcasted_iota(jnp.int32, (c, c), 0)
    col = lax.broadcasted_iota(jnp.int32, (c, c), 1)
    lower = col <= row
    upper = col >= row

    def step(c0, la_ref, st, mask, end_row):
        tri = jnp.where(mask, 1.0, 0.0).astype(BF16)
        qc = q_ref[pl.ds(c0, c), :] * q_scale
        kc = k_ref[pl.ds(c0, c), :]
        vc = v_ref[pl.ds(c0, c), :].astype(BF16)
        hi, mid, lo = _split3(la_ref[pl.ds(c0, c), :])
        b = (_dot(tri, hi) + _dot(tri, mid)) + _dot(tri, lo)
        b_end = b[end_row:end_row + 1, :]
        qd = (qc * jnp.exp(b)).astype(BF16)
        kd = (kc * jnp.exp(-b)).astype(BF16)
        a = jnp.where(mask, _dot_nt(qd, kd), 0.0).astype(BF16)
        s_t = st[...]
        o = _dot_nt(qd, s_t.astype(BF16)) + _dot(a, vc)
        ks = (kc * jnp.exp(b_end - b)).astype(BF16)
        st[...] = s_t * jnp.exp(b_end) + _dot_tn(vc, ks)
        oacc[pl.ds(c0, c), :] += o

    def body(i, carry):
        step(pl.multiple_of(i * c, c), laf_ref, stf, lower, c - 1)
        step(pl.multiple_of((n_chunks - 1 - i) * c, c), lab_ref, stb, upper, 0)
        return carry

    lax.fori_loop(0, n_chunks, body, 0)
    o_ref[...] = (_rms(oacc[...], ggla_ref[...]) * _silu(gout_ref[...])).astype(o_ref.dtype)
    sf_ref[0, 0] = stf[...].T
    sb_ref[0, 0] = stb[...].T


def _gla(proj, la, g_gla, *, n_batch, t_b, row0, heads, dk, dv, col_q, col_k, col_v, col_g, name,
         s0f=None, s0b=None, layer=0):
    has_init = s0f is not None
    rb = row0 // t_b
    in_specs = [
        pl.BlockSpec((t_b, dk), lambda b, h: (rb + b, col_q // dk + h)),
        pl.BlockSpec((t_b, dk), lambda b, h: (rb + b, col_k // dk + h)),
        pl.BlockSpec((t_b, dv), lambda b, h: (rb + b, col_v // dv + h)),
        pl.BlockSpec((t_b, dk), lambda b, h: (rb + b, h)),
        pl.BlockSpec((t_b, dk), lambda b, h: (rb + b, heads + h)),
        pl.BlockSpec((t_b, dv), lambda b, h: (rb + b, col_g // dv + h)),
        pl.BlockSpec((1, dv), lambda b, h: (0, 0)),
    ]
    args = [proj, proj, proj, la, la, proj, g_gla.reshape(1, dv)]
    if has_init:
        st_spec = pl.BlockSpec((1, None, 1, dk, dv), lambda b, h: (b, layer, h, 0, 0))
        in_specs += [st_spec, st_spec]
        args += [s0f, s0b]
    st_out = pl.BlockSpec((1, 1, dk, dv), lambda b, h: (b, h, 0, 0))
    kern = functools.partial(_gla_kernel, has_init=has_init, n_chunks=t_b // GLA_CHUNK,
                             q_scale=dk ** -0.5)
    return pl.pallas_call(
        kern, grid=(n_batch, heads), in_specs=in_specs,
        out_specs=[pl.BlockSpec((t_b, dv), lambda b, h: (b, h)), st_out, st_out],
        out_shape=[jax.ShapeDtypeStruct((n_batch * t_b, heads * dv), BF16),
                   jax.ShapeDtypeStruct((n_batch, heads, dk, dv), F32),
                   jax.ShapeDtypeStruct((n_batch, heads, dk, dv), F32)],
        scratch_shapes=[pltpu.VMEM((dv, dk), F32), pltpu.VMEM((dv, dk), F32),
                        pltpu.VMEM((t_b, dv), F32)],
        compiler_params=_params(2), name=name)(*args)


def _router_kernel(h_ref, w_ref, o_ref, *, n_experts):
    logits = _dot(h_ref[...].astype(BF16), w_ref[...])
    lane = lax.broadcasted_iota(jnp.int32, logits.shape, 1).astype(F32)
    neg = jnp.float32(-jnp.inf)
    lg = jnp.where(lane < n_experts, logits, neg)
    m1 = jnp.max(lg, axis=-1, keepdims=True)
    i1 = jnp.min(jnp.where(lg == m1, lane, LANE), axis=-1, keepdims=True)
    lg2 = jnp.where(lane == i1, neg, lg)
    m2 = jnp.max(lg2, axis=-1, keepdims=True)
    i2 = jnp.min(jnp.where(lg2 == m2, lane, LANE), axis=-1, keepdims=True)
    e2 = jnp.exp(m2 - m1)
    inv = 1.0 / (1.0 + e2)
    o_ref[...] = (jnp.where(lane == 0, i1, 0.0) + jnp.where(lane == 1, i2, 0.0)
                  + jnp.where(lane == 2, inv, 0.0) + jnp.where(lane == 3, e2 * inv, 0.0))


def _routing_tables(route, n_experts, tile):
    t = route.shape[0]
    n_assign = t * TOP_K
    n_rows = n_assign + n_experts * tile
    e_flat = route[:, :TOP_K].astype(jnp.int32).reshape(n_assign)
    onehot = (e_flat[:, None] == jnp.arange(n_experts, dtype=jnp.int32)[None, :]).astype(jnp.int32)
    csum = jnp.cumsum(onehot, axis=0)
    rank = jnp.take_along_axis(csum, e_flat[:, None], axis=1)[:, 0] - 1
    counts = csum[-1]
    padded = ((counts + tile - 1) // tile) * tile
    ends = jnp.cumsum(padded)
    starts = ends - padded
    dest = starts[e_flat] + rank
    tok_of = jnp.zeros((n_rows,), jnp.int32).at[dest].set(
        jnp.arange(n_assign, dtype=jnp.int32) // TOP_K, unique_indices=True)
    tile_start = jnp.arange(n_rows // tile, dtype=jnp.int32) * tile
    tile_expert = jnp.minimum(jnp.searchsorted(ends, tile_start, side="right"),
                              n_experts - 1).astype(jnp.int32)
    return tok_of, dest.astype(jnp.int32), tile_expert


def _rope_tables(n_lat, rope, tm_id):
    axis_half = rope // 4
    rows = n_lat // GRID_W
    r = jnp.repeat(jnp.arange(rows, dtype=F32), GRID_W)
    c = jnp.tile(jnp.arange(GRID_W, dtype=F32), rows)
    inv = ROPE_THETA ** (-jnp.arange(axis_half, dtype=F32) / axis_half)
    ar, ac = r[:, None] * inv, c[:, None] * inv
    cos = jnp.concatenate([jnp.cos(ar), jnp.cos(ar), jnp.cos(ac), jnp.cos(ac)], axis=1)
    sin = jnp.concatenate([-jnp.sin(ar), jnp.sin(ar), -jnp.sin(ac), jnp.sin(ac)], axis=1)
    cos = jnp.concatenate([jnp.ones((tm_id, rope), F32), cos], axis=0)
    sin = jnp.concatenate([jnp.zeros((tm_id, rope), F32), sin], axis=0)
    return cos, sin


def kernel(x_prompt, x_sample, cache_ckv, cache_krope, state_gla_fwd, state_gla_bwd, c, c_ctx, w_ada, b_ada, g_pre_mix, g_post_mix, g_pre_ffn, g_post_ffn, w_in, g_q, w_uq, g_kv, w_ukv, w_gate_f, b_gate_f, w_gate_b, b_gate_b, g_gla, w_pa, w_pb, w_o, w_ff_gate, w_ff_up, w_ff_down, w_router, w_ex_gate, w_ex_up, w_ex_down):
    n_ctx_b, t_ctx, d = x_prompt.shape
    n_lat_b, t_lat, _ = x_sample.shape
    depth = w_in.shape[0]
    past = cache_ckv.shape[2]
    q_lora, kv_lora = g_q.shape[1], g_kv.shape[1]
    rope = cache_krope.shape[3]
    heads, dk, dv = state_gla_fwd.shape[2:]
    rank = w_gate_f.shape[1]
    n_experts = w_router.shape[2]
    hq = QK_NOPE + rope
    hw = QK_NOPE + V_HEAD
    n_ctx, n_lat = n_ctx_b * t_ctx, n_lat_b * t_lat
    t = n_ctx + n_lat
    half = rope // 4
    assert 2 * rope == LANE and 2 * rank <= LANE - rope

    tm = _tile(math.gcd(n_ctx, t_lat), 512)
    n_ctx_tiles = n_ctx // tm
    lat_tiles = t_lat // tm

    def sample_of_tile(i):
        return jnp.where(i < n_ctx_tiles, 0, 1 + (i - n_ctx_tiles) // lat_tiles)

    def pos_of_tile(i):
        return jnp.where(i < n_ctx_tiles, 0, 1 + (i - n_ctx_tiles) % lat_tiles)

    sizes = (q_lora, kv_lora, rope, heads * dk, heads * dk, heads * dv, rank, rank,
             heads * dv, d, d)
    offs = [0]
    for s in sizes:
        offs.append(offs[-1] + s)
    main_groups = (0, 1, 3, 4, 5, 8, 9, 10)
    col = {}
    acc = 0
    for gidx in main_groups:
        col[gidx] = acc
        acc += sizes[gidx]
    n_main = acc
    col_q, col_k, col_v, col_g, col_a, col_b = col[3], col[4], col[5], col[8], col[9], col[10]

    cos_r, sin_r = _rope_tables(t_lat, rope, tm)
    n_tab = cos_r.shape[0]
    cos_q = jnp.concatenate([jnp.ones((n_tab, QK_NOPE), F32), cos_r,
                             jnp.ones((n_tab, hw - hq), F32)], axis=1)
    sin_q = jnp.concatenate([jnp.zeros((n_tab, QK_NOPE), F32), sin_r,
                             jnp.zeros((n_tab, hw - hq), F32)], axis=1)
    cos_k = jnp.concatenate([cos_r, jnp.zeros((n_tab, LANE - rope), F32)], axis=1)
    sin_k = jnp.concatenate([sin_r, jnp.zeros((n_tab, LANE - rope), F32)], axis=1)

    x = jnp.concatenate([x_prompt.reshape(n_ctx, d), x_sample.reshape(n_lat, d)], axis=0)
    c_all = jnp.concatenate([c_ctx[None, :], c, jnp.zeros((8 - 1 - n_lat_b, d), F32)], axis=0)

    def modulation(l):
        def epi(accs, e_refs):
            return accs[0] + e_refs[0][...]
        tn = _tile(6 * d, 1024)
        m = _matmul([(c_all, d, 0)], [(w_ada, (l,))], n_out=6 * d, tm=8, tn=tn, out_dtype=F32,
                    epilogue=epi, prologue=lambda k, a, e: _silu(a), name=f"ada{l}",
                    extras=[(b_ada.reshape(depth, 1, 6 * d), (None, 1, tn),
                             lambda j, i, l=l: (l, 0, j))])
        return m.reshape(8, 6, d)

    mods = [modulation(l) for l in range(depth)]
    _, h = _resid_norm(x, sample_of_tile, tm, "prenorm0", mod_pre=mods[0], g_pre=g_pre_mix[0],
                       shift_idx=0, scale_idx=1)

    new_ckv, new_krope, new_sf, new_sb = [], [], [], []
    for l in range(depth):
        w = w_in[l]
        w_main = jnp.concatenate([w[:, offs[gidx]:offs[gidx + 1]] for gidx in main_groups],
                                 axis=1).astype(BF16)
        w_misc = jnp.concatenate([w[:, offs[2]:offs[3]], w[:, offs[6]:offs[8]],
                                  jnp.zeros((d, LANE - rope - 2 * rank), F32)], axis=1).astype(BF16)
        proj = _matmul([(h, d, 0)], [(w_main, ())], n_out=n_main, tm=tm, tn=_tile(n_main, 1024),
                       out_dtype=F32, epilogue=_first, name=f"w_in{l}")
        misc = _matmul([(h, d, 0)], [(w_misc, ())], n_out=LANE, tm=tm, tn=LANE,
                       out_dtype=F32, epilogue=_first, name=f"w_in_misc{l}")

        w_gate = jnp.zeros((LANE, 2 * heads * dk), F32)
        w_gate = w_gate.at[rope:rope + rank, :heads * dk].set(w_gate_f[l])
        w_gate = w_gate.at[rope + rank:rope + 2 * rank, heads * dk:].set(w_gate_b[l])
        b_gate = jnp.concatenate([b_gate_f[l], b_gate_b[l]]).reshape(1, 2 * heads * dk)
        kr_self, la = pl.pallas_call(
            functools.partial(_prep_kernel, half=half), grid=(t // tm,),
            in_specs=[pl.BlockSpec((tm, LANE), lambda i: (i, 0)),
                      pl.BlockSpec((tm, LANE), lambda i: (pos_of_tile(i), 0)),
                      pl.BlockSpec((tm, LANE), lambda i: (pos_of_tile(i), 0)),
                      pl.BlockSpec((LANE, 2 * heads * dk), lambda i: (0, 0)),
                      pl.BlockSpec((1, 2 * heads * dk), lambda i: (0, 0))],
            out_specs=[pl.BlockSpec((tm, LANE), lambda i: (i, 0)),
                       pl.BlockSpec((tm, 2 * heads * dk), lambda i: (i, 0))],
            out_shape=[jax.ShapeDtypeStruct((t, LANE), BF16),
                       jax.ShapeDtypeStruct((t, 2 * heads * dk), F32)],
            compiler_params=_params(1), name=f"prep{l}")(misc, cos_k, sin_k, w_gate.astype(BF16), b_gate)

        wq = w_uq[l].reshape(q_lora, MLA_HEADS, hq)
        wq = jnp.concatenate([wq, jnp.zeros((q_lora, MLA_HEADS, hw - hq), F32)], axis=2)
        wq = wq.reshape(q_lora, MLA_HEADS * hw).astype(BF16)
        tn_q = _tile(MLA_HEADS * hw, 1024, hw)

        def q_epi(accs, e_refs, tn_q=tn_q):
            cos, sin = e_refs[1][...], e_refs[2][...]
            outs = [_rope(accs[0][:, s:s + hw], cos, sin, half) for s in range(0, tn_q, hw)]
            return jnp.concatenate(outs, axis=1)

        q = _matmul([(proj, q_lora, col[0] // q_lora)], [(wq, ())], n_out=MLA_HEADS * hw, tm=tm,
                    tn=tn_q, out_dtype=BF16, epilogue=q_epi, name=f"w_uq{l}",
                    prologue=lambda k, a, e: _rms(a, e[0][...]),
                    extras=[(g_q[l].reshape(1, q_lora), (1, q_lora), lambda j, i: (0, 0)),
                            (cos_q, (tm, hw), lambda j, i: (pos_of_tile(i), 0)),
                            (sin_q, (tm, hw), lambda j, i: (pos_of_tile(i), 0))])

        c_kv = _norm_cols(proj, kv_lora, col[1] // kv_lora, g_kv[l], tm, f"ckv_norm{l}")
        tn_kv = _tile(MLA_HEADS * hw, 1024)
        kv_self = _matmul([(c_kv, kv_lora, 0)], [(w_ukv, (l,))], n_out=MLA_HEADS * hw, tm=tm,
                          tn=tn_kv, out_dtype=BF16, epilogue=_first, name=f"w_ukv{l}")
        ckv_cache = cache_ckv[:, l].reshape(n_lat_b * past, kv_lora)
        kv_cache = _matmul([(ckv_cache, kv_lora, 0)], [(w_ukv, (l,))], n_out=MLA_HEADS * hw,
                           tm=_tile(n_lat_b * past, 512), tn=tn_kv, out_dtype=BF16,
                           epilogue=_first, name=f"w_ukv_cache{l}")
        kr_cache = jnp.pad(cache_krope[:, l].reshape(n_lat_b * past, rope),
                           ((0, 0), (0, LANE - rope))).astype(BF16)

        scale = hq ** -0.5
        attn_ctx = _attention(q, [(kv_self, t_ctx, 0)], [(kr_self, t_ctx, 0)], n_batch=n_ctx_b,
                              t_q=t_ctx, q_row0=0, tq=_tile(t_ctx, 256),
                              n_heads=math.gcd(MLA_HEADS, 8), scale=scale, name=f"attn_ctx{l}")
        attn_lat = _attention(q, [(kv_self, t_lat, n_ctx), (kv_cache, past, 0)],
                              [(kr_self, t_lat, n_ctx), (kr_cache, past, 0)], n_batch=n_lat_b,
                              t_q=t_lat, q_row0=n_ctx, tq=_tile(t_lat, 512),
                              n_heads=math.gcd(MLA_HEADS, 4), scale=scale, name=f"attn_lat{l}")
        attn = jnp.concatenate([attn_ctx, attn_lat], axis=0)

        gla_kw = dict(heads=heads, dk=dk, dv=dv, col_q=col_q, col_k=col_k, col_v=col_v,
                      col_g=col_g)
        gla_ctx, sf, sb = _gla(proj, la, g_gla[l], n_batch=n_ctx_b, t_b=t_ctx, row0=0,
                               name=f"gla_ctx{l}", **gla_kw)
        gla_lat, _, _ = _gla(proj, la, g_gla[l], n_batch=n_lat_b, t_b=t_lat, row0=n_ctx,
                             s0f=state_gla_fwd, s0b=state_gla_bwd, layer=l, name=f"gla_lat{l}",
                             **gla_kw)
        gla = jnp.concatenate([gla_ctx, gla_lat], axis=0)

        new_ckv.append(c_kv[:n_ctx].reshape(n_ctx_b, t_ctx, kv_lora))
        new_krope.append(misc[:n_ctx, :rope].reshape(n_ctx_b, t_ctx, rope))
        new_sf.append(sf)
        new_sb.append(sb)

        tn_m = _tile(math.gcd(col_a, col_b, d), 512)

        def merge_epi(accs, e_refs):
            return (jax.nn.sigmoid(e_refs[0][...]) * accs[0]
                    + jax.nn.sigmoid(e_refs[1][...]) * accs[1])

        merged = _matmul([(attn, MLA_HEADS * V_HEAD, 0), (gla, heads * dv, 0)],
                         [(w_pa, (l,)), (w_pb, (l,))], n_out=d, tm=tm, tn=tn_m, out_dtype=BF16,
                         epilogue=merge_epi, name=f"merge{l}",
                         extras=[(proj, (tm, tn_m), lambda j, i: (i, col_a // tn_m + j)),
                                 (proj, (tm, tn_m), lambda j, i: (i, col_b // tn_m + j))])
        y = _matmul([(merged, d, 0)], [(w_o, (l,))], n_out=d, tm=tm, tn=_tile(d, 1024),
                    out_dtype=F32, epilogue=_first, name=f"w_o{l}")
        moe = l % 2 == 1
        x, h = _resid_norm(x, sample_of_tile, tm, f"mix_resid{l}", y=y, mod_res=mods[l],
                           g_post=g_post_mix[l], gate_idx=2, mod_pre=mods[l], g_pre=g_pre_ffn[l],
                           shift_idx=3, scale_idx=4, h_dtype=F32 if moe else BF16)

        jx = l // 2

        def swiglu_epi(accs, e_refs):
            return _silu(accs[0]) * accs[1]

        ffn_out = {}
        if not moe:
            d_ff = w_ff_gate.shape[2]
            ff = _matmul([(h, d, 0)], [(w_ff_gate, (jx,)), (w_ff_up, (jx,))], a_of_w=[0, 0],
                         n_out=d_ff, tm=tm, tn=_tile(d_ff, 512), out_dtype=BF16,
                         epilogue=swiglu_epi, name=f"ffn_up{l}")
            ffn_out["y"] = _matmul([(ff, d_ff, 0)], [(w_ff_down, (jx,))], n_out=d, tm=tm,
                                   tn=_tile(d, 256), out_dtype=F32, epilogue=_first,
                                   name=f"ffn_down{l}")
        else:
            d_ex = w_ex_gate.shape[3]
            w_r = jnp.pad(w_router[jx], ((0, 0), (0, LANE - n_experts))).astype(BF16)
            route = pl.pallas_call(
                functools.partial(_router_kernel, n_experts=n_experts), grid=(t // tm,),
                in_specs=[pl.BlockSpec((tm, d), lambda i: (i, 0)),
                          pl.BlockSpec((d, LANE), lambda i: (0, 0))],
                out_specs=pl.BlockSpec((tm, LANE), lambda i: (i, 0)),
                out_shape=jax.ShapeDtypeStruct((t, LANE), F32),
                compiler_params=_params(1), name=f"router{l}")(h, w_r)
            tile_e = _tile(t * TOP_K, EXPERT_TILE, 8)
            tok_of, dest, tile_expert = _routing_tables(route, n_experts, tile_e)
            xs = _gather_cast(h, tok_of, tile_e, BF16, f"moe_gather{l}")
            ff = _matmul([(xs, d, 0)], [(w_ex_gate, (jx, 0)), (w_ex_up, (jx, 0))], a_of_w=[0, 0],
                         n_out=d_ex, tm=tile_e, tn=_tile(d_ex, 256), out_dtype=BF16,
                         epilogue=swiglu_epi, name=f"moe_up{l}", tile_expert=tile_expert)
            ys = _matmul([(ff, d_ex, 0)], [(w_ex_down, (jx, 0))], n_out=d, tm=tile_e,
                         tn=_tile(d, 512), out_dtype=F32, epilogue=_first, name=f"moe_down{l}",
                         tile_expert=tile_expert)
            ffn_out["routed"] = (ys, dest, route)

        if l + 1 < depth:
            x, h = _resid_norm(x, sample_of_tile, tm, f"ffn_resid{l}", mod_res=mods[l],
                               g_post=g_post_ffn[l], gate_idx=5, mod_pre=mods[l + 1],
                               g_pre=g_pre_mix[l + 1], shift_idx=0, scale_idx=1, **ffn_out)
        else:
            x, _ = _resid_norm(x, sample_of_tile, tm, f"ffn_resid{l}", mod_res=mods[l],
                               g_post=g_post_ffn[l], gate_idx=5, **ffn_out)

    return (x[:n_ctx].reshape(n_ctx_b, t_ctx, d), x[n_ctx:].reshape(n_lat_b, t_lat, d),
            jnp.stack(new_ckv, axis=1), jnp.stack(new_krope, axis=1),
            jnp.stack(new_sf, axis=1), jnp.stack(new_sb, axis=1))
```

```python
import functools
import math

import jax
import jax.numpy as jnp
from jax import lax
from jax.experimental import pallas as pl
from jax.experimental.pallas import tpu as pltpu

MLA_HEADS = 16
QK_NOPE = 128
V_HEAD = 128
GRID_W = 64
ROPE_THETA = 10000.0
GATE_NORM = 16.0
GLA_CHUNK = 64
TOP_K = 2
EPS = 1e-6
LANE = 128
VMEM_LIMIT = 52 * 1024 * 1024
EXPERT_TILE = 512

BF16 = jnp.bfloat16
F32 = jnp.float32


def _params(n_grid, **kw):
    return pltpu.CompilerParams(
        dimension_semantics=("arbitrary",) * n_grid, vmem_limit_bytes=VMEM_LIMIT, **kw)


def _tile(n, target, quantum=LANE):
    if n <= target:
        return n
    t = (target // quantum) * quantum
    while t >= quantum:
        if n % t == 0:
            return t
        t -= quantum
    return n


def _rms(x, g):
    return x * lax.rsqrt(jnp.mean(x * x, axis=-1, keepdims=True) + EPS) * g


def _silu(x):
    return x * jax.nn.sigmoid(x)


def _dot(a, b):
    return jnp.dot(a, b, preferred_element_type=F32)


def _dot_nt(a, b):
    return lax.dot_general(a, b, (((1,), (1,)), ((), ())), preferred_element_type=F32)


def _dot_tn(a, b):
    return lax.dot_general(a, b, (((0,), (0,)), ((), ())), preferred_element_type=F32)


def _mm_kernel(*refs, n_a, n_w, a_of_w, n_e, cast_w, grouped, prologue, epilogue):
    if grouped:
        te_ref, refs = refs[0], refs[1:]
    a_refs = refs[:n_a]
    w_refs = refs[n_a:n_a + n_w]
    e_refs = refs[n_a + n_w:n_a + n_w + n_e]
    o_ref = refs[n_a + n_w + n_e]
    wb_refs = refs[n_a + n_w + n_e + 1:]
    i = pl.program_id(1)

    def compute():
        if cast_w:
            fresh = i == 0
            if grouped:
                fresh = fresh | (te_ref[i] != te_ref[jnp.maximum(i - 1, 0)])

            @pl.when(fresh)
            def _():
                for w_ref, wb_ref in zip(w_refs, wb_refs):
                    wb_ref[...] = w_ref[...].astype(BF16)
            w_use = wb_refs
        else:
            w_use = w_refs
        a_vals = []
        for k, a_ref in enumerate(a_refs):
            a = a_ref[...]
            if prologue is not None:
                a = prologue(k, a, e_refs)
            a_vals.append(a.astype(BF16))
        accs = [_dot(a_vals[a_of_w[k]], w_ref[...]) for k, w_ref in enumerate(w_use)]
        o_ref[...] = epilogue(accs, e_refs).astype(o_ref.dtype)

    if grouped:
        has_rows = i < te_ref[pl.num_programs(1)]
        pl.when(has_rows)(compute)

        @pl.when(jnp.logical_not(has_rows))
        def _():
            o_ref[...] = jnp.zeros_like(o_ref)
    else:
        compute()


def _matmul(a_list, w_list, *, n_out, tm, tn, out_dtype, epilogue, name, a_of_w=None,
            prologue=None, extras=(), tile_expert=None):
    m = a_list[0][0].shape[0]
    a_of_w = a_of_w or list(range(len(w_list)))
    grouped = tile_expert is not None
    assert m % tm == 0 and n_out % tn == 0
    in_specs, args, scratch = [], [], []
    for arr, k, cb in a_list:
        in_specs.append(pl.BlockSpec((tm, k), lambda j, i, *_, cb=cb: (i, cb)))
        args.append(arr)
    cast_w = w_list[0][0].dtype != BF16
    for arr, lead in w_list:
        k = arr.shape[-2]
        if grouped:
            im = lambda j, i, te, lead=lead: tuple(lead[:-1]) + (te[i], 0, j)
        else:
            im = lambda j, i, lead=lead: tuple(lead) + (0, j)
        in_specs.append(pl.BlockSpec((None,) * len(lead) + (k, tn), im))
        args.append(arr)
        if cast_w:
            scratch.append(pltpu.VMEM((k, tn), BF16))
    for arr, bs, im in extras:
        in_specs.append(pl.BlockSpec(bs, lambda j, i, *_, im=im: im(j, i)))
        args.append(arr)
    kern = functools.partial(_mm_kernel, n_a=len(a_list), n_w=len(w_list), a_of_w=a_of_w,
                             n_e=len(extras), cast_w=cast_w, grouped=grouped,
                             prologue=prologue, epilogue=epilogue)
    grid_spec = pltpu.PrefetchScalarGridSpec(
        num_scalar_prefetch=1 if grouped else 0,
        grid=(n_out // tn, m // tm),
        in_specs=in_specs,
        out_specs=pl.BlockSpec((tm, tn), lambda j, i, *_: (i, j)),
        scratch_shapes=scratch)
    if grouped:
        args = [tile_expert] + args
    return pl.pallas_call(
        kern, grid_spec=grid_spec,
        out_shape=jax.ShapeDtypeStruct((m, n_out), out_dtype),
        compiler_params=_params(2), name=name,
    )(*args)


def _first(accs, e_refs):
    return accs[0]


def _row_copy(src, row, dst, r, sem):
    return pltpu.make_async_copy(src.at[pl.ds(row, 1)], dst.at[pl.ds(r, 1)], sem)


def _gather_rows(idx_ref, base, stride, n, src, dst, dst_base, sem):
    def issue(r, carry):
        _row_copy(src, idx_ref[base + r * stride], dst, dst_base + r, sem).start()
        return carry

    def wait(r, carry):
        _row_copy(src, 0, dst, dst_base + r, sem).wait()
        return carry

    lax.fori_loop(0, n, issue, 0, unroll=8)
    lax.fori_loop(0, n, wait, 0, unroll=8)


def _gather_kernel(idx_ref, src_hbm, dst_hbm, sem, *, tg):
    base = pl.program_id(0) * tg
    _gather_rows(idx_ref, base, 1, tg, src_hbm, dst_hbm, base, sem)


def _gather(src, idx, tg, name):
    n = idx.shape[0]
    grid_spec = pltpu.PrefetchScalarGridSpec(
        num_scalar_prefetch=1, grid=(n // tg,),
        in_specs=[pl.BlockSpec(memory_space=pl.ANY)],
        out_specs=pl.BlockSpec(memory_space=pl.ANY),
        scratch_shapes=[pltpu.SemaphoreType.DMA(())])
    return pl.pallas_call(
        functools.partial(_gather_kernel, tg=tg), grid_spec=grid_spec,
        out_shape=jax.ShapeDtypeStruct((n,) + src.shape[1:], src.dtype),
        compiler_params=_params(1, has_side_effects=True), name=name)(idx, src)


def _resid_norm_kernel(*refs, mode, want_x, want_h, gate_idx, shift_idx, scale_idx):
    it = iter(refs)
    dest_ref = next(it) if mode == "routed" else None
    x_ref = next(it)
    if mode == "dense":
        y_ref = next(it)
    elif mode == "routed":
        ys_hbm, route_ref = next(it), next(it)
    if mode is not None:
        mod_res_ref, g_post_ref = next(it), next(it)
    if want_h:
        mod_pre_ref, g_pre_ref = next(it), next(it)
    xo_ref = next(it) if want_x else None
    h_ref = next(it) if want_h else None
    x = x_ref[...]
    if mode is not None:
        if mode == "routed":
            buf1, buf2, sem = next(it), next(it), next(it)
            tm = x_ref.shape[0]
            base = pl.program_id(0) * tm * TOP_K
            _gather_rows(dest_ref, base, TOP_K, tm, ys_hbm, buf1, 0, sem)
            _gather_rows(dest_ref, base + 1, TOP_K, tm, ys_hbm, buf2, 0, sem)
            route = route_ref[...]
            y = route[:, 2:3] * buf1[...] + route[:, 3:4] * buf2[...]
        else:
            y = y_ref[...]
        gate = mod_res_ref[0, gate_idx:gate_idx + 1, :]
        x = x + gate * _rms(y, g_post_ref[...])
    if want_x:
        xo_ref[...] = x
    if want_h:
        scale = mod_pre_ref[0, scale_idx:scale_idx + 1, :]
        shift = mod_pre_ref[0, shift_idx:shift_idx + 1, :]
        h_ref[...] = (_rms(x, g_pre_ref[...]) * (1.0 + scale) + shift).astype(h_ref.dtype)


def _resid_norm(x, sample_of_tile, tm, name, *, y=None, routed=None, mod_res=None, g_post=None,
                gate_idx=0, mod_pre=None, g_pre=None, shift_idx=0, scale_idx=0, h_dtype=BF16):
    t, d = x.shape
    mode = "dense" if y is not None else ("routed" if routed is not None else None)
    want_h = mod_pre is not None
    want_x = mode is not None
    row = pl.BlockSpec((tm, d), lambda i, *_: (i, 0))
    vec = pl.BlockSpec((1, d), lambda i, *_: (0, 0))
    mod = pl.BlockSpec((1, 6, d), lambda i, *_: (sample_of_tile(i), 0, 0))
    in_specs, args, out_specs, out_shape, scratch = [row], [x], [], [], []
    if mode == "dense":
        in_specs.append(row)
        args.append(y)
    elif mode == "routed":
        ys, dest, route = routed
        in_specs += [pl.BlockSpec(memory_space=pl.ANY),
                     pl.BlockSpec((tm, LANE), lambda i, *_: (i, 0))]
        args += [ys, route]
        scratch = [pltpu.VMEM((tm, d), F32), pltpu.VMEM((tm, d), F32),
                   pltpu.SemaphoreType.DMA(())]
    if mode is not None:
        in_specs += [mod, vec]
        args += [mod_res, g_post.reshape(1, d)]
    if want_h:
        in_specs += [mod, vec]
        args += [mod_pre, g_pre.reshape(1, d)]
    if want_x:
        out_specs.append(row)
        out_shape.append(jax.ShapeDtypeStruct((t, d), F32))
    if want_h:
        out_specs.append(row)
        out_shape.append(jax.ShapeDtypeStruct((t, d), h_dtype))
    kern = functools.partial(_resid_norm_kernel, mode=mode, want_x=want_x, want_h=want_h,
                             gate_idx=gate_idx, shift_idx=shift_idx, scale_idx=scale_idx)
    grid_spec = pltpu.PrefetchScalarGridSpec(
        num_scalar_prefetch=1 if mode == "routed" else 0, grid=(t // tm,),
        in_specs=in_specs, out_specs=out_specs, scratch_shapes=scratch)
    if mode == "routed":
        args = [dest] + args
    outs = pl.pallas_call(kern, grid_spec=grid_spec, out_shape=out_shape,
                          compiler_params=_params(1), name=name)(*args)
    x_new = outs[0] if want_x else x
    h = outs[-1] if want_h else None
    return x_new, h


def _norm_kernel(x_ref, g_ref, o_ref):
    o_ref[...] = _rms(x_ref[...], g_ref[...])


def _norm_cols(arr, width, col_block, g, tm, name):
    t = arr.shape[0]
    return pl.pallas_call(
        _norm_kernel, grid=(t // tm,),
        in_specs=[pl.BlockSpec((tm, width), lambda i: (i, col_block)),
                  pl.BlockSpec((1, width), lambda i: (0, 0))],
        out_specs=pl.BlockSpec((tm, width), lambda i: (i, 0)),
        out_shape=jax.ShapeDtypeStruct((t, width), F32),
        compiler_params=_params(1), name=name)(arr, g.reshape(1, width))


def _swap_halves(x, half):
    n = x.shape[-1]
    lane = lax.broadcasted_iota(jnp.int32, x.shape, x.ndim - 1)
    up = pltpu.roll(x, n - half, x.ndim - 1)
    down = pltpu.roll(x, half, x.ndim - 1)
    return jnp.where((lane & (2 * half - 1)) < half, up, down)


def _rope(x, cos, sin, half):
    return x * cos + _swap_halves(x, half) * sin


def _prep_kernel(misc_ref, cos_ref, sin_ref, wg_ref, bg_ref, kr_ref, la_ref, *, half):
    misc = misc_ref[...]
    kr_ref[...] = _rope(misc, cos_ref[...], sin_ref[...], half).astype(kr_ref.dtype)
    z = _dot(misc.astype(BF16), wg_ref[...]) + bg_ref[...]
    log_sig = jnp.minimum(z, 0.0) - jnp.log1p(jnp.exp(-jnp.abs(z)))
    la_ref[...] = log_sig * (1.0 / GATE_NORM)


def _attn_kernel(*refs, n_seg, n_heads, scale):
    q_ref = refs[0]
    kv_refs = refs[1:1 + n_seg]
    kr_refs = refs[1 + n_seg:1 + 2 * n_seg]
    o_ref = refs[-1]
    hw = QK_NOPE + V_HEAD
    krs = [kr_ref[...] for kr_ref in kr_refs]
    for hd in range(n_heads):
        qn = q_ref[:, hd * hw:hd * hw + QK_NOPE]
        qr = q_ref[:, hd * hw + QK_NOPE:(hd + 1) * hw]
        scores = []
        for kv_ref, kr in zip(kv_refs, krs):
            s = _dot_nt(qn, kv_ref[:, hd * hw:hd * hw + QK_NOPE]) + _dot_nt(qr, kr)
            scores.append(s * scale)
        m = functools.reduce(jnp.maximum, [jnp.max(s, axis=-1, keepdims=True) for s in scores])
        ps = [jnp.exp(s - m) for s in scores]
        denom = functools.reduce(lambda a, b: a + b,
                                 [jnp.sum(p, axis=-1, keepdims=True) for p in ps])
        inv = 1.0 / denom
        o = functools.reduce(
            lambda a, b: a + b,
            [_dot((p * inv).astype(BF16), kv_ref[:, hd * hw + QK_NOPE:(hd + 1) * hw])
             for p, kv_ref in zip(ps, kv_refs)])
        o_ref[:, hd * V_HEAD:(hd + 1) * V_HEAD] = o.astype(o_ref.dtype)


def _attention(q, kvs, krs, *, n_batch, t_q, q_row0, tq, n_heads, scale, name, o_prev):
    hw = QK_NOPE + V_HEAD
    nq = t_q // tq
    in_specs = [pl.BlockSpec((tq, n_heads * hw),
                             lambda b, g, i: (q_row0 // tq + b * nq + i, g))]
    args = [q]
    for arr, tk, row0 in kvs:
        in_specs.append(pl.BlockSpec((tk, n_heads * hw),
                                     lambda b, g, i, tk=tk, row0=row0: (row0 // tk + b, g)))
        args.append(arr)
    for arr, tk, row0 in krs:
        in_specs.append(pl.BlockSpec((tk, LANE),
                                     lambda b, g, i, tk=tk, row0=row0: (row0 // tk + b, 0)))
        args.append(arr)
    aliases = {len(args): 0}
    in_specs.append(pl.BlockSpec(memory_space=pl.ANY))
    args.append(o_prev)
    kern = functools.partial(_attn_kernel, n_seg=len(kvs), n_heads=n_heads, scale=scale)
    return pl.pallas_call(
        kern, grid=(n_batch, MLA_HEADS // n_heads, nq), in_specs=in_specs,
        out_specs=pl.BlockSpec((tq, n_heads * V_HEAD),
                               lambda b, g, i: (q_row0 // tq + b * nq + i, g)),
        out_shape=jax.ShapeDtypeStruct(o_prev.shape, o_prev.dtype),
        input_output_aliases=aliases,
        compiler_params=_params(3), name=name)(*args)


def _split3(x):
    hi = x.astype(BF16)
    r1 = x - hi.astype(F32)
    mid = r1.astype(BF16)
    lo = (r1 - mid.astype(F32)).astype(BF16)
    return hi, mid, lo


def _gla_kernel(*refs, has_init, want_states, n_alias, q_scale):
    it = iter(refs)
    q_ref, k_ref, v_ref, laf_ref, lab_ref, gout_ref, ggla_ref = (next(it) for _ in range(7))
    s0_refs = (next(it), next(it)) if has_init else None
    for _ in range(n_alias):
        next(it)
    o_ref = next(it)
    s_out_refs = (next(it), next(it)) if want_states else None
    st_refs = (next(it), next(it))
    vb, qd_s, ks_s, dec_s, oacc = (next(it) for _ in range(5))
    c = GLA_CHUNK
    t_b, dk = q_ref.shape
    n_chunks = t_b // c
    rb = min(t_b, 4 * c)
    row = lax.broadcasted_iota(jnp.int32, (rb, rb), 0)
    col = lax.broadcasted_iota(jnp.int32, (rb, rb), 1)
    same = (row // c) == (col // c)
    masks = (same & (col <= row), same & (col >= row))
    ones_bd = jnp.where(same, 1.0, 0.0).astype(BF16)
    la_refs = (laf_ref, lab_ref)

    vb[...] = v_ref[...].astype(BF16)
    for d in range(2):
        if has_init:
            st_refs[d][...] = s0_refs[d][0, 0].T
        else:
            st_refs[d][...] = jnp.zeros_like(st_refs[d])

    for d in range(2):
        tri = jnp.where(masks[d], 1.0, 0.0).astype(BF16)
        for r0 in range(0, t_b, rb):
            rows = pl.ds(r0, rb)
            hi, mid, lo = _split3(la_refs[d][rows, :])
            b = (_dot(tri, hi) + _dot(tri, mid)) + _dot(tri, lo)
            b_end = (_dot(ones_bd, hi) + _dot(ones_bd, mid)) + _dot(ones_bd, lo)
            kc = k_ref[rows, :]
            qd = (q_ref[rows, :] * q_scale * jnp.exp(b)).astype(BF16)
            kd = (kc * jnp.exp(-b)).astype(BF16)
            qd_s[d, rows, :] = qd
            ks_s[d, rows, :] = (kc * jnp.exp(b_end - b)).astype(BF16)
            dec_s[d, rows, :] = jnp.exp(b_end)
            a = jnp.where(masks[d], _dot_nt(qd, kd), 0.0).astype(BF16)
            o_in = _dot(a, vb[rows, :])
            if d == 0:
                oacc[rows, :] = o_in
            else:
                oacc[rows, :] += o_in

    for i in range(n_chunks):
        for d in range(2):
            c0 = (i if d == 0 else n_chunks - 1 - i) * c
            rows = pl.ds(c0, c)
            st = st_refs[d]
            u = _dot_tn(vb[rows, :], ks_s[d, rows, :])
            s_t = st[...]
            oacc[rows, :] += _dot_nt(qd_s[d, rows, :], s_t.astype(BF16))
            st[...] = s_t * dec_s[d, pl.ds(c0, 1), :] + u

    o_ref[...] = (_rms(oacc[...], ggla_ref[...]) * _silu(gout_ref[...])).astype(o_ref.dtype)
    if want_states:
        for d in range(2):
            s_out_refs[d][0, 0] = st_refs[d][...].T


def _gla(proj, la, g_gla, *, n_batch, t_b, row0, heads, dk, dv, col_q, col_k, col_v, col_g,
         name, o_prev, s0=None, layer=0, st_prev=None):
    has_init = s0 is not None
    rb = row0 // t_b
    in_specs = [
        pl.BlockSpec((t_b, dk), lambda b, h: (rb + b, col_q // dk + h)),
        pl.BlockSpec((t_b, dk), lambda b, h: (rb + b, col_k // dk + h)),
        pl.BlockSpec((t_b, dv), lambda b, h: (rb + b, col_v // dv + h)),
        pl.BlockSpec((t_b, dk), lambda b, h: (rb + b, h)),
        pl.BlockSpec((t_b, dk), lambda b, h: (rb + b, heads + h)),
        pl.BlockSpec((t_b, dv), lambda b, h: (rb + b, col_g // dv + h)),
        pl.BlockSpec((1, dv), lambda b, h: (0, 0)),
    ]
    args = [proj, proj, proj, la, la, proj, g_gla.reshape(1, dv)]
    st_spec = pl.BlockSpec((1, None, 1, dk, dv), lambda b, h: (b, layer, h, 0, 0))
    if has_init:
        in_specs += [st_spec, st_spec]
        args += list(s0)
    out_specs = [pl.BlockSpec((t_b, dv), lambda b, h: (rb + b, h))]
    out_shape = [jax.ShapeDtypeStruct(o_prev.shape, o_prev.dtype)]
    aliases = {len(args): 0}
    in_specs.append(pl.BlockSpec(memory_space=pl.ANY))
    args.append(o_prev)
    want_states = st_prev is not None
    if want_states:
        out_specs += [st_spec, st_spec]
        for k, arr in enumerate(st_prev):
            out_shape.append(jax.ShapeDtypeStruct(arr.shape, arr.dtype))
            aliases[len(args)] = 1 + k
            in_specs.append(pl.BlockSpec(memory_space=pl.ANY))
            args.append(arr)
    kern = functools.partial(_gla_kernel, has_init=has_init, want_states=want_states,
                             n_alias=len(aliases), q_scale=dk ** -0.5)
    return pl.pallas_call(
        kern, grid=(n_batch, heads), in_specs=in_specs, out_specs=out_specs, out_shape=out_shape,
        input_output_aliases=aliases,
        scratch_shapes=[pltpu.VMEM((dv, dk), F32), pltpu.VMEM((dv, dk), F32),
                        pltpu.VMEM((t_b, dv), BF16), pltpu.VMEM((2, t_b, dk), BF16),
                        pltpu.VMEM((2, t_b, dk), BF16), pltpu.VMEM((2, t_b, dk), F32),
                        pltpu.VMEM((t_b, dv), F32)],
        compiler_params=_params(2), name=name)(*args)


def _router_kernel(h_ref, w_ref, o_ref, *, n_experts):
    logits = _dot(h_ref[...].astype(BF16), w_ref[...])
    lane = lax.broadcasted_iota(jnp.int32, logits.shape, 1).astype(F32)
    neg = jnp.float32(-jnp.inf)
    lg = jnp.where(lane < n_experts, logits, neg)
    m1 = jnp.max(lg, axis=-1, keepdims=True)
    i1 = jnp.min(jnp.where(lg == m1, lane, LANE), axis=-1, keepdims=True)
    lg2 = jnp.where(lane == i1, neg, lg)
    m2 = jnp.max(lg2, axis=-1, keepdims=True)
    i2 = jnp.min(jnp.where(lg2 == m2, lane, LANE), axis=-1, keepdims=True)
    e2 = jnp.exp(m2 - m1)
    inv = 1.0 / (1.0 + e2)
    o_ref[...] = (jnp.where(lane == 0, i1, 0.0) + jnp.where(lane == 1, i2, 0.0)
                  + jnp.where(lane == 2, inv, 0.0) + jnp.where(lane == 3, e2 * inv, 0.0))


def _routing_tables(route, n_experts, tile):
    t = route.shape[0]
    n_assign = t * TOP_K
    n_rows = n_assign + n_experts * tile
    e_flat = route[:, :TOP_K].astype(jnp.int32).reshape(n_assign)
    onehot = (e_flat[:, None] == jnp.arange(n_experts, dtype=jnp.int32)[None, :]).astype(jnp.int32)
    csum = jnp.cumsum(onehot, axis=0)
    rank = jnp.take_along_axis(csum, e_flat[:, None], axis=1)[:, 0] - 1
    counts = csum[-1]
    padded = ((counts + tile - 1) // tile) * tile
    ends = jnp.cumsum(padded)
    starts = ends - padded
    dest = starts[e_flat] + rank
    tok_of = jnp.zeros((n_rows,), jnp.int32).at[dest].set(
        jnp.arange(n_assign, dtype=jnp.int32) // TOP_K, unique_indices=True)
    n_valid = ends[-1] // tile
    tile_idx = jnp.minimum(jnp.arange(n_rows // tile, dtype=jnp.int32), n_valid - 1)
    tile_expert = jnp.sum((tile_idx * tile)[:, None] >= ends[None, :], axis=1, dtype=jnp.int32)
    tile_info = jnp.concatenate([tile_expert, n_valid[None]]).astype(jnp.int32)
    return tok_of, dest.astype(jnp.int32), tile_info


def _rope_tables(n_lat, rope, tm_id):
    axis_half = rope // 4
    rows = n_lat // GRID_W
    r = jnp.repeat(jnp.arange(rows, dtype=F32), GRID_W)
    c = jnp.tile(jnp.arange(GRID_W, dtype=F32), rows)
    inv = ROPE_THETA ** (-jnp.arange(axis_half, dtype=F32) / axis_half)
    ar, ac = r[:, None] * inv, c[:, None] * inv
    cos = jnp.concatenate([jnp.cos(ar), jnp.cos(ar), jnp.cos(ac), jnp.cos(ac)], axis=1)
    sin = jnp.concatenate([-jnp.sin(ar), jnp.sin(ar), -jnp.sin(ac), jnp.sin(ac)], axis=1)
    cos = jnp.concatenate([jnp.ones((tm_id, rope), F32), cos], axis=0)
    sin = jnp.concatenate([jnp.zeros((tm_id, rope), F32), sin], axis=0)
    return cos, sin


def kernel(x_prompt, x_sample, cache_ckv, cache_krope, state_gla_fwd, state_gla_bwd, c, c_ctx, w_ada, b_ada, g_pre_mix, g_post_mix, g_pre_ffn, g_post_ffn, w_in, g_q, w_uq, g_kv, w_ukv, w_gate_f, b_gate_f, w_gate_b, b_gate_b, g_gla, w_pa, w_pb, w_o, w_ff_gate, w_ff_up, w_ff_down, w_router, w_ex_gate, w_ex_up, w_ex_down):
    n_ctx_b, t_ctx, d = x_prompt.shape
    n_lat_b, t_lat, _ = x_sample.shape
    depth = w_in.shape[0]
    past = cache_ckv.shape[2]
    q_lora, kv_lora = g_q.shape[1], g_kv.shape[1]
    rope = cache_krope.shape[3]
    heads, dk, dv = state_gla_fwd.shape[2:]
    rank = w_gate_f.shape[1]
    n_experts = w_router.shape[2]
    hq = QK_NOPE + rope
    hw = QK_NOPE + V_HEAD
    n_ctx, n_lat = n_ctx_b * t_ctx, n_lat_b * t_lat
    t = n_ctx + n_lat
    half = rope // 4
    assert 2 * rope == LANE and 2 * rank <= LANE - rope

    tm = _tile(math.gcd(n_ctx, t_lat), 512)
    tm_s = _tile(math.gcd(n_ctx, t_lat), 1024)
    n_ctx_tiles = n_ctx // tm
    lat_tiles = t_lat // tm

    def sample_of_tile(i):
        return jnp.where(i < n_ctx_tiles, 0, 1 + (i - n_ctx_tiles) // lat_tiles)

    def pos_of_tile(i, tile=tm):
        first = n_ctx // tile
        return jnp.where(i < first, 0, tm_s // tile + (i - first) % (t_lat // tile))

    sizes = (q_lora, kv_lora, rope, heads * dk, heads * dk, heads * dv, rank, rank,
             heads * dv, d, d)
    offs = [0]
    for s in sizes:
        offs.append(offs[-1] + s)
    main_groups = (0, 1, 3, 4, 5, 8, 9, 10)
    col = {}
    acc = 0
    for gidx in main_groups:
        col[gidx] = acc
        acc += sizes[gidx]
    n_main = acc
    col_q, col_k, col_v, col_g, col_a, col_b = col[3], col[4], col[5], col[8], col[9], col[10]

    cos_r, sin_r = _rope_tables(t_lat, rope, tm_s)
    n_tab = cos_r.shape[0]
    cos_q = jnp.concatenate([jnp.ones((n_tab, QK_NOPE), F32), cos_r,
                             jnp.ones((n_tab, hw - hq), F32)], axis=1)
    sin_q = jnp.concatenate([jnp.zeros((n_tab, QK_NOPE), F32), sin_r,
                             jnp.zeros((n_tab, hw - hq), F32)], axis=1)
    cos_k = jnp.concatenate([cos_r, jnp.zeros((n_tab, LANE - rope), F32)], axis=1)
    sin_k = jnp.concatenate([sin_r, jnp.zeros((n_tab, LANE - rope), F32)], axis=1)

    x = jnp.concatenate([x_prompt.reshape(n_ctx, d), x_sample.reshape(n_lat, d)], axis=0)
    c_all = jnp.concatenate([c_ctx[None, :], c, jnp.zeros((8 - 1 - n_lat_b, d), F32)], axis=0)

    def modulation(l):
        def epi(accs, e_refs):
            return accs[0] + e_refs[0][...]
        tn = _tile(6 * d, 1024)
        m = _matmul([(c_all, d, 0)], [(w_ada, (l,))], n_out=6 * d, tm=8, tn=tn, out_dtype=F32,
                    epilogue=epi, prologue=lambda k, a, e: _silu(a), name=f"ada{l}",
                    extras=[(b_ada.reshape(depth, 1, 6 * d), (None, 1, tn),
                             lambda j, i, l=l: (l, 0, j))])
        return m.reshape(8, 6, d)

    mods = [modulation(l) for l in range(depth)]
    _, h = _resid_norm(x, sample_of_tile, tm, "prenorm0", mod_pre=mods[0], g_pre=g_pre_mix[0],
                       shift_idx=0, scale_idx=1)

    new_ckv, new_krope = [], []
    states = [jnp.zeros((n_ctx_b, depth, heads, dk, dv), F32) for _ in range(2)]
    for l in range(depth):
        w = w_in[l]
        w_main = jnp.concatenate([w[:, offs[gidx]:offs[gidx + 1]] for gidx in main_groups],
                                 axis=1).astype(BF16)
        w_misc = jnp.concatenate([w[:, offs[2]:offs[3]], w[:, offs[6]:offs[8]],
                                  jnp.zeros((d, LANE - rope - 2 * rank), F32)], axis=1).astype(BF16)
        proj = _matmul([(h, d, 0)], [(w_main, ())], n_out=n_main, tm=tm, tn=_tile(n_main, 1024),
                       out_dtype=F32, epilogue=_first, name=f"w_in{l}")
        misc = _matmul([(h, d, 0)], [(w_misc, ())], n_out=LANE, tm=tm, tn=LANE,
                       out_dtype=F32, epilogue=_first, name=f"w_in_misc{l}")

        w_gate = jnp.zeros((LANE, 2 * heads * dk), F32)
        w_gate = w_gate.at[rope:rope + rank, :heads * dk].set(w_gate_f[l])
        w_gate = w_gate.at[rope + rank:rope + 2 * rank, heads * dk:].set(w_gate_b[l])
        b_gate = jnp.concatenate([b_gate_f[l], b_gate_b[l]]).reshape(1, 2 * heads * dk)
        kr_self, la = pl.pallas_call(
            functools.partial(_prep_kernel, half=half), grid=(t // tm,),
            in_specs=[pl.BlockSpec((tm, LANE), lambda i: (i, 0)),
                      pl.BlockSpec((tm, LANE), lambda i: (pos_of_tile(i), 0)),
                      pl.BlockSpec((tm, LANE), lambda i: (pos_of_tile(i), 0)),
                      pl.BlockSpec((LANE, 2 * heads * dk), lambda i: (0, 0)),
                      pl.BlockSpec((1, 2 * heads * dk), lambda i: (0, 0))],
            out_specs=[pl.BlockSpec((tm, LANE), lambda i: (i, 0)),
                       pl.BlockSpec((tm, 2 * heads * dk), lambda i: (i, 0))],
            out_shape=[jax.ShapeDtypeStruct((t, LANE), BF16),
                       jax.ShapeDtypeStruct((t, 2 * heads * dk), F32)],
            compiler_params=_params(1), name=f"prep{l}")(misc, cos_k, sin_k, w_gate.astype(BF16), b_gate)

        wq = w_uq[l].reshape(q_lora, MLA_HEADS, hq)
        wq = jnp.concatenate([wq, jnp.zeros((q_lora, MLA_HEADS, hw - hq), F32)], axis=2)
        wq = wq.reshape(q_lora, MLA_HEADS * hw).astype(BF16)
        tn_q = _tile(MLA_HEADS * hw, 1024, hw)

        def q_epi(accs, e_refs, tn_q=tn_q):
            cos, sin = e_refs[1][...], e_refs[2][...]
            outs = [_rope(accs[0][:, s:s + hw], cos, sin, half) for s in range(0, tn_q, hw)]
            return jnp.concatenate(outs, axis=1)

        q = _matmul([(proj, q_lora, col[0] // q_lora)], [(wq, ())], n_out=MLA_HEADS * hw, tm=tm_s,
                    tn=tn_q, out_dtype=BF16, epilogue=q_epi, name=f"w_uq{l}",
                    prologue=lambda k, a, e: _rms(a, e[0][...]),
                    extras=[(g_q[l].reshape(1, q_lora), (1, q_lora), lambda j, i: (0, 0)),
                            (cos_q, (tm_s, hw), lambda j, i: (pos_of_tile(i, tm_s), 0)),
                            (sin_q, (tm_s, hw), lambda j, i: (pos_of_tile(i, tm_s), 0))])

        c_kv = _norm_cols(proj, kv_lora, col[1] // kv_lora, g_kv[l], tm, f"ckv_norm{l}")
        tn_kv = _tile(MLA_HEADS * hw, 1024)
        kv_self = _matmul([(c_kv, kv_lora, 0)], [(w_ukv, (l,))], n_out=MLA_HEADS * hw, tm=tm_s,
                          tn=tn_kv, out_dtype=BF16, epilogue=_first, name=f"w_ukv{l}")
        ckv_cache = cache_ckv[:, l].reshape(n_lat_b * past, kv_lora)
        kv_cache = _matmul([(ckv_cache, kv_lora, 0)], [(w_ukv, (l,))], n_out=MLA_HEADS * hw,
                           tm=_tile(n_lat_b * past, 512), tn=tn_kv, out_dtype=BF16,
                           epilogue=_first, name=f"w_ukv_cache{l}")
        kr_cache = jnp.pad(cache_krope[:, l].reshape(n_lat_b * past, rope),
                           ((0, 0), (0, LANE - rope))).astype(BF16)

        scale = hq ** -0.5
        attn = _attention(q, [(kv_self, t_ctx, 0)], [(kr_self, t_ctx, 0)], n_batch=n_ctx_b,
                          t_q=t_ctx, q_row0=0, tq=_tile(t_ctx, 256),
                          n_heads=math.gcd(MLA_HEADS, 8), scale=scale, name=f"attn_ctx{l}",
                          o_prev=jnp.zeros((t, MLA_HEADS * V_HEAD), BF16))
        attn = _attention(q, [(kv_self, t_lat, n_ctx), (kv_cache, past, 0)],
                          [(kr_self, t_lat, n_ctx), (kr_cache, past, 0)], n_batch=n_lat_b,
                          t_q=t_lat, q_row0=n_ctx, tq=_tile(t_lat, 512),
                          n_heads=math.gcd(MLA_HEADS, 4), scale=scale, name=f"attn_lat{l}",
                          o_prev=attn)

        gla_kw = dict(heads=heads, dk=dk, dv=dv, col_q=col_q, col_k=col_k, col_v=col_v,
                      col_g=col_g, layer=l)
        gla, *states = _gla(proj, la, g_gla[l], n_batch=n_ctx_b, t_b=t_ctx, row0=0,
                            name=f"gla_ctx{l}", o_prev=jnp.zeros((t, heads * dv), BF16),
                            st_prev=states, **gla_kw)
        gla, = _gla(proj, la, g_gla[l], n_batch=n_lat_b, t_b=t_lat, row0=n_ctx,
                    s0=(state_gla_fwd, state_gla_bwd), name=f"gla_lat{l}", o_prev=gla, **gla_kw)

        new_ckv.append(c_kv[:n_ctx].reshape(n_ctx_b, t_ctx, kv_lora))
        new_krope.append(misc[:n_ctx, :rope].reshape(n_ctx_b, t_ctx, rope))

        tn_m = _tile(math.gcd(col_a, col_b, d), 512)

        def merge_epi(accs, e_refs):
            return (jax.nn.sigmoid(e_refs[0][...]) * accs[0]
                    + jax.nn.sigmoid(e_refs[1][...]) * accs[1])

        merged = _matmul([(attn, MLA_HEADS * V_HEAD, 0), (gla, heads * dv, 0)],
                         [(w_pa, (l,)), (w_pb, (l,))], n_out=d, tm=tm, tn=tn_m, out_dtype=BF16,
                         epilogue=merge_epi, name=f"merge{l}",
                         extras=[(proj, (tm, tn_m), lambda j, i: (i, col_a // tn_m + j)),
                                 (proj, (tm, tn_m), lambda j, i: (i, col_b // tn_m + j))])
        y = _matmul([(merged, d, 0)], [(w_o, (l,))], n_out=d, tm=tm, tn=_tile(d, 1024),
                    out_dtype=F32, epilogue=_first, name=f"w_o{l}")
        moe = l % 2 == 1
        x, h = _resid_norm(x, sample_of_tile, tm, f"mix_resid{l}", y=y, mod_res=mods[l],
                           g_post=g_post_mix[l], gate_idx=2, mod_pre=mods[l], g_pre=g_pre_ffn[l],
                           shift_idx=3, scale_idx=4)

        jx = l // 2

        def swiglu_epi(accs, e_refs):
            return _silu(accs[0]) * accs[1]

        ffn_out = {}
        if not moe:
            d_ff = w_ff_gate.shape[2]
            ff = _matmul([(h, d, 0)], [(w_ff_gate, (jx,)), (w_ff_up, (jx,))], a_of_w=[0, 0],
                         n_out=d_ff, tm=tm, tn=_tile(d_ff, 512), out_dtype=BF16,
                         epilogue=swiglu_epi, name=f"ffn_up{l}")
            ffn_out["y"] = _matmul([(ff, d_ff, 0)], [(w_ff_down, (jx,))], n_out=d, tm=tm,
                                   tn=_tile(d, 512), out_dtype=F32, epilogue=_first,
                                   name=f"ffn_down{l}")
        else:
            d_ex = w_ex_gate.shape[3]
            w_r = jnp.pad(w_router[jx], ((0, 0), (0, LANE - n_experts))).astype(BF16)
            route = pl.pallas_call(
                functools.partial(_router_kernel, n_experts=n_experts), grid=(t // tm,),
                in_specs=[pl.BlockSpec((tm, d), lambda i: (i, 0)),
                          pl.BlockSpec((d, LANE), lambda i: (0, 0))],
                out_specs=pl.BlockSpec((tm, LANE), lambda i: (i, 0)),
                out_shape=jax.ShapeDtypeStruct((t, LANE), F32),
                compiler_params=_params(1), name=f"router{l}")(h, w_r)
            tile_e = _tile(t * TOP_K, EXPERT_TILE, 8)
            tok_of, dest, tile_expert = _routing_tables(route, n_experts, tile_e)
            xs = _gather(h.reshape(t, d // LANE, LANE), tok_of, tile_e, f"moe_gather{l}")
            xs = xs.reshape(xs.shape[0], d)
            ff = _matmul([(xs, d, 0)], [(w_ex_gate, (jx, 0)), (w_ex_up, (jx, 0))], a_of_w=[0, 0],
                         n_out=d_ex, tm=tile_e, tn=_tile(d_ex, 256), out_dtype=BF16,
                         epilogue=swiglu_epi, name=f"moe_up{l}", tile_expert=tile_expert)
            ys = _matmul([(ff, d_ex, 0)], [(w_ex_down, (jx, 0))], n_out=d, tm=tile_e,
                         tn=_tile(d, 512), out_dtype=F32, epilogue=_first, name=f"moe_down{l}",
                         tile_expert=tile_expert)
            ffn_out["routed"] = (ys, dest, route)

        if l + 1 < depth:
            x, h = _resid_norm(x, sample_of_tile, tm, f"ffn_resid{l}", mod_res=mods[l],
                               g_post=g_post_ffn[l], gate_idx=5, mod_pre=mods[l + 1],
                               g_pre=g_pre_mix[l + 1], shift_idx=0, scale_idx=1, **ffn_out)
        else:
            x, _ = _resid_norm(x, sample_of_tile, tm, f"ffn_resid{l}", mod_res=mods[l],
                               g_post=g_post_ffn[l], gate_idx=5, **ffn_out)

    return (x[:n_ctx].reshape(n_ctx_b, t_ctx, d), x[n_ctx:].reshape(n_lat_b, t_lat, d),
            jnp.stack(new_ckv, axis=1), jnp.stack(new_krope, axis=1),
            states[0], states[1])
```

```python
import functools
import math

import jax
import jax.numpy as jnp
from jax import lax
from jax.experimental import pallas as pl
from jax.experimental.pallas import tpu as pltpu

MLA_HEADS = 16
QK_NOPE = 128
V_HEAD = 128
GRID_W = 64
ROPE_THETA = 10000.0
GATE_NORM = 16.0
GLA_CHUNK = 64
TOP_K = 2
EPS = 1e-6
LANE = 128
VMEM_LIMIT = 52 * 1024 * 1024
EXPERT_TILE = 512

BF16 = jnp.bfloat16
F32 = jnp.float32


def _params(n_grid, **kw):
    return pltpu.CompilerParams(
        dimension_semantics=("arbitrary",) * n_grid, vmem_limit_bytes=VMEM_LIMIT, **kw)


def _tile(n, target, quantum=LANE):
    if n <= target:
        return n
    t = (target // quantum) * quantum
    while t >= quantum:
        if n % t == 0:
            return t
        t -= quantum
    return n


def _rms(x, g):
    return x * lax.rsqrt(jnp.mean(x * x, axis=-1, keepdims=True) + EPS) * g


def _silu(x):
    return x * jax.nn.sigmoid(x)


def _dot(a, b):
    return jnp.dot(a, b, preferred_element_type=F32)


def _dot_nt(a, b):
    return lax.dot_general(a, b, (((1,), (1,)), ((), ())), preferred_element_type=F32)


def _dot_tn(a, b):
    return lax.dot_general(a, b, (((0,), (0,)), ((), ())), preferred_element_type=F32)


def _mm_kernel(*refs, n_a, n_w, a_of_w, n_e, cast_w, grouped, prologue, epilogue):
    if grouped:
        te_ref, refs = refs[0], refs[1:]
    a_refs = refs[:n_a]
    w_refs = refs[n_a:n_a + n_w]
    e_refs = refs[n_a + n_w:n_a + n_w + n_e]
    o_ref = refs[n_a + n_w + n_e]
    wb_refs = refs[n_a + n_w + n_e + 1:]
    i = pl.program_id(1)

    def compute():
        if cast_w:
            fresh = i == 0
            if grouped:
                fresh = fresh | (te_ref[i] != te_ref[jnp.maximum(i - 1, 0)])

            @pl.when(fresh)
            def _():
                for w_ref, wb_ref in zip(w_refs, wb_refs):
                    wb_ref[...] = w_ref[...].astype(BF16)
            w_use = wb_refs
        else:
            w_use = w_refs
        a_vals = []
        for k, a_ref in enumerate(a_refs):
            a = a_ref[...]
            if prologue is not None:
                a = prologue(k, a, e_refs)
            a_vals.append(a.astype(BF16))
        accs = [_dot(a_vals[a_of_w[k]], w_ref[...]) for k, w_ref in enumerate(w_use)]
        o_ref[...] = epilogue(accs, e_refs).astype(o_ref.dtype)

    if grouped:
        has_rows = i < te_ref[pl.num_programs(1)]
        pl.when(has_rows)(compute)

        @pl.when(jnp.logical_not(has_rows))
        def _():
            o_ref[...] = jnp.zeros_like(o_ref)
    else:
        compute()


def _matmul(a_list, w_list, *, n_out, tm, tn, out_dtype, epilogue, name, a_of_w=None,
            prologue=None, extras=(), tile_expert=None):
    m = a_list[0][0].shape[0]
    a_of_w = a_of_w or list(range(len(w_list)))
    grouped = tile_expert is not None
    assert m % tm == 0 and n_out % tn == 0
    in_specs, args, scratch = [], [], []
    for arr, k, cb in a_list:
        in_specs.append(pl.BlockSpec((tm, k), lambda j, i, *_, cb=cb: (i, cb)))
        args.append(arr)
    cast_w = w_list[0][0].dtype != BF16
    for arr, lead in w_list:
        k = arr.shape[-2]
        if grouped:
            im = lambda j, i, te, lead=lead: tuple(lead[:-1]) + (te[i], 0, j)
        else:
            im = lambda j, i, lead=lead: tuple(lead) + (0, j)
        in_specs.append(pl.BlockSpec((None,) * len(lead) + (k, tn), im))
        args.append(arr)
        if cast_w:
            scratch.append(pltpu.VMEM((k, tn), BF16))
    for arr, bs, im in extras:
        in_specs.append(pl.BlockSpec(bs, lambda j, i, *_, im=im: im(j, i)))
        args.append(arr)
    kern = functools.partial(_mm_kernel, n_a=len(a_list), n_w=len(w_list), a_of_w=a_of_w,
                             n_e=len(extras), cast_w=cast_w, grouped=grouped,
                             prologue=prologue, epilogue=epilogue)
    grid_spec = pltpu.PrefetchScalarGridSpec(
        num_scalar_prefetch=1 if grouped else 0,
        grid=(n_out // tn, m // tm),
        in_specs=in_specs,
        out_specs=pl.BlockSpec((tm, tn), lambda j, i, *_: (i, j)),
        scratch_shapes=scratch)
    if grouped:
        args = [tile_expert] + args
    return pl.pallas_call(
        kern, grid_spec=grid_spec,
        out_shape=jax.ShapeDtypeStruct((m, n_out), out_dtype),
        compiler_params=_params(2), name=name,
    )(*args)


def _first(accs, e_refs):
    return accs[0]


def _row_copy(src, row, dst, r, sem):
    return pltpu.make_async_copy(src.at[pl.ds(row, 1)], dst.at[pl.ds(r, 1)], sem)


def _gather_rows(idx_ref, base, stride, n, src, dst, dst_base, sem):
    def issue(r, carry):
        _row_copy(src, idx_ref[base + r * stride], dst, dst_base + r, sem).start()
        return carry

    def wait(r, carry):
        _row_copy(src, 0, dst, dst_base + r, sem).wait()
        return carry

    lax.fori_loop(0, n, issue, 0, unroll=8)
    lax.fori_loop(0, n, wait, 0, unroll=8)


def _gather_kernel(idx_ref, src_hbm, o_ref, sem):
    tg = o_ref.shape[0]
    _gather_rows(idx_ref, pl.program_id(0) * tg, 1, tg, src_hbm, o_ref, 0, sem)


def _gather(src, idx, tg, name):
    n = idx.shape[0]
    blk = (tg,) + src.shape[1:]
    grid_spec = pltpu.PrefetchScalarGridSpec(
        num_scalar_prefetch=1, grid=(n // tg,),
        in_specs=[pl.BlockSpec(memory_space=pl.ANY)],
        out_specs=pl.BlockSpec(blk, lambda i, idx: (i,) + (0,) * (len(blk) - 1)),
        scratch_shapes=[pltpu.SemaphoreType.DMA(())])
    return pl.pallas_call(
        _gather_kernel, grid_spec=grid_spec,
        out_shape=jax.ShapeDtypeStruct((n,) + src.shape[1:], src.dtype),
        compiler_params=_params(1), name=name)(idx, src)


def _resid_norm_kernel(*refs, mode, want_x, want_h, gate_idx, shift_idx, scale_idx):
    it = iter(refs)
    dest_ref = next(it) if mode == "routed" else None
    x_ref = next(it)
    if mode == "dense":
        y_ref = next(it)
    elif mode == "routed":
        ys_hbm, route_ref = next(it), next(it)
    if mode is not None:
        mod_res_ref, g_post_ref = next(it), next(it)
    if want_h:
        mod_pre_ref, g_pre_ref = next(it), next(it)
    xo_ref = next(it) if want_x else None
    h_ref = next(it) if want_h else None
    x = x_ref[...]
    if mode is not None:
        if mode == "routed":
            buf1, buf2, sem = next(it), next(it), next(it)
            tm = x_ref.shape[0]
            base = pl.program_id(0) * tm * TOP_K
            _gather_rows(dest_ref, base, TOP_K, tm, ys_hbm, buf1, 0, sem)
            _gather_rows(dest_ref, base + 1, TOP_K, tm, ys_hbm, buf2, 0, sem)
            route = route_ref[...]
            y = route[:, 2:3] * buf1[...] + route[:, 3:4] * buf2[...]
        else:
            y = y_ref[...]
        gate = mod_res_ref[0, gate_idx:gate_idx + 1, :]
        x = x + gate * _rms(y, g_post_ref[...])
    if want_x:
        xo_ref[...] = x
    if want_h:
        scale = mod_pre_ref[0, scale_idx:scale_idx + 1, :]
        shift = mod_pre_ref[0, shift_idx:shift_idx + 1, :]
        h_ref[...] = (_rms(x, g_pre_ref[...]) * (1.0 + scale) + shift).astype(h_ref.dtype)


def _resid_norm(x, sample_of_tile, tm, name, *, y=None, routed=None, mod_res=None, g_post=None,
                gate_idx=0, mod_pre=None, g_pre=None, shift_idx=0, scale_idx=0, h_dtype=BF16):
    t, d = x.shape
    mode = "dense" if y is not None else ("routed" if routed is not None else None)
    want_h = mod_pre is not None
    want_x = mode is not None
    row = pl.BlockSpec((tm, d), lambda i, *_: (i, 0))
    vec = pl.BlockSpec((1, d), lambda i, *_: (0, 0))
    mod = pl.BlockSpec((1, 6, d), lambda i, *_: (sample_of_tile(i), 0, 0))
    in_specs, args, out_specs, out_shape, scratch = [row], [x], [], [], []
    if mode == "dense":
        in_specs.append(row)
        args.append(y)
    elif mode == "routed":
        ys, dest, route = routed
        in_specs += [pl.BlockSpec(memory_space=pl.ANY),
                     pl.BlockSpec((tm, LANE), lambda i, *_: (i, 0))]
        args += [ys, route]
        scratch = [pltpu.VMEM((tm, d), F32), pltpu.VMEM((tm, d), F32),
                   pltpu.SemaphoreType.DMA(())]
    if mode is not None:
        in_specs += [mod, vec]
        args += [mod_res, g_post.reshape(1, d)]
    if want_h:
        in_specs += [mod, vec]
        args += [mod_pre, g_pre.reshape(1, d)]
    if want_x:
        out_specs.append(row)
        out_shape.append(jax.ShapeDtypeStruct((t, d), F32))
    if want_h:
        out_specs.append(row)
        out_shape.append(jax.ShapeDtypeStruct((t, d), h_dtype))
    kern = functools.partial(_resid_norm_kernel, mode=mode, want_x=want_x, want_h=want_h,
                             gate_idx=gate_idx, shift_idx=shift_idx, scale_idx=scale_idx)
    grid_spec = pltpu.PrefetchScalarGridSpec(
        num_scalar_prefetch=1 if mode == "routed" else 0, grid=(t // tm,),
        in_specs=in_specs, out_specs=out_specs, scratch_shapes=scratch)
    if mode == "routed":
        args = [dest] + args
    outs = pl.pallas_call(kern, grid_spec=grid_spec, out_shape=out_shape,
                          compiler_params=_params(1), name=name)(*args)
    x_new = outs[0] if want_x else x
    h = outs[-1] if want_h else None
    return x_new, h


def _norm_kernel(x_ref, g_ref, o_ref):
    o_ref[...] = _rms(x_ref[...], g_ref[...])


def _norm_cols(arr, width, col_block, g, tm, name):
    t = arr.shape[0]
    return pl.pallas_call(
        _norm_kernel, grid=(t // tm,),
        in_specs=[pl.BlockSpec((tm, width), lambda i: (i, col_block)),
                  pl.BlockSpec((1, width), lambda i: (0, 0))],
        out_specs=pl.BlockSpec((tm, width), lambda i: (i, 0)),
        out_shape=jax.ShapeDtypeStruct((t, width), F32),
        compiler_params=_params(1), name=name)(arr, g.reshape(1, width))


def _swap_halves(x, half):
    n = x.shape[-1]
    lane = lax.broadcasted_iota(jnp.int32, x.shape, x.ndim - 1)
    up = pltpu.roll(x, n - half, x.ndim - 1)
    down = pltpu.roll(x, half, x.ndim - 1)
    return jnp.where((lane & (2 * half - 1)) < half, up, down)


def _rope(x, cos, sin, half):
    return x * cos + _swap_halves(x, half) * sin


def _prep_kernel(misc_ref, cos_ref, sin_ref, wg_ref, bg_ref, kr_ref, la_ref, *, half):
    misc = misc_ref[...]
    kr_ref[...] = _rope(misc, cos_ref[...], sin_ref[...], half).astype(kr_ref.dtype)
    z = _dot(misc.astype(BF16), wg_ref[...]) + bg_ref[...]
    log_sig = jnp.minimum(z, 0.0) - jnp.log1p(jnp.exp(-jnp.abs(z)))
    la_ref[...] = log_sig * (1.0 / GATE_NORM)


def _attn_kernel(*refs, n_seg, n_heads, group, scale):
    q_ref = refs[0]
    kv_refs = refs[1:1 + n_seg]
    kr_refs = refs[1 + n_seg:1 + 2 * n_seg]
    o_ref = refs[-1]
    hw = QK_NOPE + V_HEAD
    krs = [kr_ref[...] for kr_ref in kr_refs]
    add = lambda a, b: a + b
    for h0 in range(0, n_heads, group):
        hds = range(h0, min(h0 + group, n_heads))
        scores = []
        for hd in hds:
            qn = q_ref[:, hd * hw:hd * hw + QK_NOPE]
            qr = q_ref[:, hd * hw + QK_NOPE:(hd + 1) * hw]
            scores.append([(_dot_nt(qn, kv_ref[:, hd * hw:hd * hw + QK_NOPE]) + _dot_nt(qr, kr))
                           * scale for kv_ref, kr in zip(kv_refs, krs)])
        ms = [functools.reduce(jnp.maximum, [jnp.max(s, axis=-1, keepdims=True) for s in ss])
              for ss in scores]
        ps = [[jnp.exp(s - m) for s in ss] for ss, m in zip(scores, ms)]
        invs = [1.0 / functools.reduce(add, [jnp.sum(p, axis=-1, keepdims=True) for p in pp])
                for pp in ps]
        for hd, pp, inv in zip(hds, ps, invs):
            o = functools.reduce(add, [
                _dot((p * inv).astype(BF16), kv_ref[:, hd * hw + QK_NOPE:(hd + 1) * hw])
                for p, kv_ref in zip(pp, kv_refs)])
            o_ref[:, hd * V_HEAD:(hd + 1) * V_HEAD] = o.astype(o_ref.dtype)


def _attention(q, kvs, krs, *, n_batch, t_q, q_row0, tq, n_heads, group, scale, name, o_prev):
    hw = QK_NOPE + V_HEAD
    nq = t_q // tq
    in_specs = [pl.BlockSpec((tq, n_heads * hw),
                             lambda b, g, i: (q_row0 // tq + b * nq + i, g))]
    args = [q]
    for arr, tk, row0 in kvs:
        in_specs.append(pl.BlockSpec((tk, n_heads * hw),
                                     lambda b, g, i, tk=tk, row0=row0: (row0 // tk + b, g)))
        args.append(arr)
    for arr, tk, row0 in krs:
        in_specs.append(pl.BlockSpec((tk, LANE),
                                     lambda b, g, i, tk=tk, row0=row0: (row0 // tk + b, 0)))
        args.append(arr)
    aliases = {len(args): 0}
    in_specs.append(pl.BlockSpec(memory_space=pl.ANY))
    args.append(o_prev)
    kern = functools.partial(_attn_kernel, n_seg=len(kvs), n_heads=n_heads, group=group,
                             scale=scale)
    return pl.pallas_call(
        kern, grid=(n_batch, MLA_HEADS // n_heads, nq), in_specs=in_specs,
        out_specs=pl.BlockSpec((tq, n_heads * V_HEAD),
                               lambda b, g, i: (q_row0 // tq + b * nq + i, g)),
        out_shape=jax.ShapeDtypeStruct(o_prev.shape, o_prev.dtype),
        input_output_aliases=aliases,
        compiler_params=_params(3), name=name)(*args)


def _split3(x):
    hi = x.astype(BF16)
    r1 = x - hi.astype(F32)
    mid = r1.astype(BF16)
    lo = (r1 - mid.astype(F32)).astype(BF16)
    return hi, mid, lo


def _gla_kernel(*refs, has_init, want_states, n_alias, q_scale, layer, all_layers):
    it = iter(refs)
    q_ref, k_ref, v_ref, laf_ref, lab_ref, gout_ref, ggla_ref = (next(it) for _ in range(7))
    s0_refs = (next(it), next(it)) if has_init else None
    for _ in range(n_alias):
        next(it)
    o_ref = next(it)
    s_out_refs = (next(it), next(it)) if want_states else None
    st_refs = (next(it), next(it))
    vb, qd_s, ks_s, dec_s, oacc = (next(it) for _ in range(5))
    c = GLA_CHUNK
    t_b, dk = q_ref.shape
    n_chunks = t_b // c
    rb = min(t_b, 4 * c)
    row = lax.broadcasted_iota(jnp.int32, (rb, rb), 0)
    col = lax.broadcasted_iota(jnp.int32, (rb, rb), 1)
    same = (row // c) == (col // c)
    masks = (same & (col <= row), same & (col >= row))
    ones_bd = jnp.where(same, 1.0, 0.0).astype(BF16)
    la_refs = (laf_ref, lab_ref)

    vb[...] = v_ref[...].astype(BF16)
    for d in range(2):
        if has_init:
            st_refs[d][...] = s0_refs[d][0, 0].T
        else:
            st_refs[d][...] = jnp.zeros_like(st_refs[d])

    for d in range(2):
        tri = jnp.where(masks[d], 1.0, 0.0).astype(BF16)
        for r0 in range(0, t_b, rb):
            rows = pl.ds(r0, rb)
            hi, mid, lo = _split3(la_refs[d][rows, :])
            b = (_dot(tri, hi) + _dot(tri, mid)) + _dot(tri, lo)
            b_end = (_dot(ones_bd, hi) + _dot(ones_bd, mid)) + _dot(ones_bd, lo)
            kc = k_ref[rows, :]
            qd = (q_ref[rows, :] * q_scale * jnp.exp(b)).astype(BF16)
            kd = (kc * jnp.exp(-b)).astype(BF16)
            qd_s[d, rows, :] = qd
            ks_s[d, rows, :] = (kc * jnp.exp(b_end - b)).astype(BF16)
            dec_s[d, rows, :] = jnp.exp(b_end)
            a = jnp.where(masks[d], _dot_nt(qd, kd), 0.0).astype(BF16)
            o_in = _dot(a, vb[rows, :])
            if d == 0:
                oacc[rows, :] = o_in
            else:
                oacc[rows, :] += o_in

    for i in range(n_chunks):
        for d in range(2):
            c0 = (i if d == 0 else n_chunks - 1 - i) * c
            rows = pl.ds(c0, c)
            st = st_refs[d]
            u = _dot_tn(vb[rows, :], ks_s[d, rows, :])
            s_t = st[...]
            oacc[rows, :] += _dot_nt(qd_s[d, rows, :], s_t.astype(BF16))
            st[...] = s_t * dec_s[d, pl.ds(c0, 1), :] + u

    o_ref[...] = (_rms(oacc[...], ggla_ref[...]) * _silu(gout_ref[...])).astype(o_ref.dtype)
    if want_states:
        for d in range(2):
            s_fin = st_refs[d][...].T
            if all_layers:
                for ll in range(s_out_refs[d].shape[1]):
                    s_out_refs[d][0, ll, 0] = s_fin if ll == layer else jnp.zeros_like(s_fin)
            else:
                s_out_refs[d][0, 0] = s_fin


def _gla(proj, la, g_gla, *, n_batch, t_b, row0, heads, dk, dv, col_q, col_k, col_v, col_g,
         name, o_prev, s0=None, layer=0, st_prev=None, st_depth=0):
    has_init = s0 is not None
    rb = row0 // t_b
    in_specs = [
        pl.BlockSpec((t_b, dk), lambda b, h: (rb + b, col_q // dk + h)),
        pl.BlockSpec((t_b, dk), lambda b, h: (rb + b, col_k // dk + h)),
        pl.BlockSpec((t_b, dv), lambda b, h: (rb + b, col_v // dv + h)),
        pl.BlockSpec((t_b, dk), lambda b, h: (rb + b, h)),
        pl.BlockSpec((t_b, dk), lambda b, h: (rb + b, heads + h)),
        pl.BlockSpec((t_b, dv), lambda b, h: (rb + b, col_g // dv + h)),
        pl.BlockSpec((1, dv), lambda b, h: (0, 0)),
    ]
    args = [proj, proj, proj, la, la, proj, g_gla.reshape(1, dv)]
    st_spec = pl.BlockSpec((1, None, 1, dk, dv), lambda b, h: (b, layer, h, 0, 0))
    if has_init:
        in_specs += [st_spec, st_spec]
        args += list(s0)
    out_specs = [pl.BlockSpec((t_b, dv), lambda b, h: (rb + b, h))]
    out_shape = [jax.ShapeDtypeStruct(o_prev.shape, o_prev.dtype)]
    aliases = {len(args): 0}
    in_specs.append(pl.BlockSpec(memory_space=pl.ANY))
    args.append(o_prev)
    want_states = st_prev is not None or st_depth > 0
    if st_prev is not None:
        out_specs += [st_spec, st_spec]
        for k, arr in enumerate(st_prev):
            out_shape.append(jax.ShapeDtypeStruct(arr.shape, arr.dtype))
            aliases[len(args)] = 1 + k
            in_specs.append(pl.BlockSpec(memory_space=pl.ANY))
            args.append(arr)
    elif want_states:
        out_specs += [pl.BlockSpec((1, st_depth, 1, dk, dv), lambda b, h: (b, 0, h, 0, 0))] * 2
        out_shape += [jax.ShapeDtypeStruct((n_batch, st_depth, heads, dk, dv), F32)] * 2
    kern = functools.partial(_gla_kernel, has_init=has_init, want_states=want_states,
                             n_alias=len(aliases), q_scale=dk ** -0.5, layer=layer,
                             all_layers=st_prev is None)
    return pl.pallas_call(
        kern, grid=(n_batch, heads), in_specs=in_specs, out_specs=out_specs, out_shape=out_shape,
        input_output_aliases=aliases,
        scratch_shapes=[pltpu.VMEM((dv, dk), F32), pltpu.VMEM((dv, dk), F32),
                        pltpu.VMEM((t_b, dv), BF16), pltpu.VMEM((2, t_b, dk), BF16),
                        pltpu.VMEM((2, t_b, dk), BF16), pltpu.VMEM((2, t_b, dk), F32),
                        pltpu.VMEM((t_b, dv), F32)],
        compiler_params=_params(2), name=name)(*args)


def _router_kernel(h_ref, w_ref, o_ref, *, n_experts):
    logits = _dot(h_ref[...].astype(BF16), w_ref[...])
    lane = lax.broadcasted_iota(jnp.int32, logits.shape, 1).astype(F32)
    neg = jnp.float32(-jnp.inf)
    lg = jnp.where(lane < n_experts, logits, neg)
    m1 = jnp.max(lg, axis=-1, keepdims=True)
    i1 = jnp.min(jnp.where(lg == m1, lane, LANE), axis=-1, keepdims=True)
    lg2 = jnp.where(lane == i1, neg, lg)
    m2 = jnp.max(lg2, axis=-1, keepdims=True)
    i2 = jnp.min(jnp.where(lg2 == m2, lane, LANE), axis=-1, keepdims=True)
    e2 = jnp.exp(m2 - m1)
    inv = 1.0 / (1.0 + e2)
    o_ref[...] = (jnp.where(lane == 0, i1, 0.0) + jnp.where(lane == 1, i2, 0.0)
                  + jnp.where(lane == 2, inv, 0.0) + jnp.where(lane == 3, e2 * inv, 0.0))


def _routing_tables(route, n_experts, tile):
    t = route.shape[0]
    n_assign = t * TOP_K
    n_rows = n_assign + n_experts * tile
    e_flat = route[:, :TOP_K].astype(jnp.int32).reshape(n_assign)
    onehot = (e_flat[:, None] == jnp.arange(n_experts, dtype=jnp.int32)[None, :]).astype(jnp.int32)
    csum = jnp.cumsum(onehot, axis=0)
    rank = jnp.take_along_axis(csum, e_flat[:, None], axis=1)[:, 0] - 1
    counts = csum[-1]
    padded = ((counts + tile - 1) // tile) * tile
    ends = jnp.cumsum(padded)
    starts = ends - padded
    dest = starts[e_flat] + rank
    tok_of = jnp.zeros((n_rows,), jnp.int32).at[dest].set(
        jnp.arange(n_assign, dtype=jnp.int32) // TOP_K, unique_indices=True)
    n_valid = ends[-1] // tile
    tile_idx = jnp.minimum(jnp.arange(n_rows // tile, dtype=jnp.int32), n_valid - 1)
    tile_expert = jnp.sum((tile_idx * tile)[:, None] >= ends[None, :], axis=1, dtype=jnp.int32)
    tile_info = jnp.concatenate([tile_expert, n_valid[None]]).astype(jnp.int32)
    return tok_of, dest.astype(jnp.int32), tile_info


def _rope_tables(n_lat, rope, tm_id):
    axis_half = rope // 4
    rows = n_lat // GRID_W
    r = jnp.repeat(jnp.arange(rows, dtype=F32), GRID_W)
    c = jnp.tile(jnp.arange(GRID_W, dtype=F32), rows)
    inv = ROPE_THETA ** (-jnp.arange(axis_half, dtype=F32) / axis_half)
    ar, ac = r[:, None] * inv, c[:, None] * inv
    cos = jnp.concatenate([jnp.cos(ar), jnp.cos(ar), jnp.cos(ac), jnp.cos(ac)], axis=1)
    sin = jnp.concatenate([-jnp.sin(ar), jnp.sin(ar), -jnp.sin(ac), jnp.sin(ac)], axis=1)
    cos = jnp.concatenate([jnp.ones((tm_id, rope), F32), cos], axis=0)
    sin = jnp.concatenate([jnp.zeros((tm_id, rope), F32), sin], axis=0)
    return cos, sin


def kernel(x_prompt, x_sample, cache_ckv, cache_krope, state_gla_fwd, state_gla_bwd, c, c_ctx, w_ada, b_ada, g_pre_mix, g_post_mix, g_pre_ffn, g_post_ffn, w_in, g_q, w_uq, g_kv, w_ukv, w_gate_f, b_gate_f, w_gate_b, b_gate_b, g_gla, w_pa, w_pb, w_o, w_ff_gate, w_ff_up, w_ff_down, w_router, w_ex_gate, w_ex_up, w_ex_down):
    n_ctx_b, t_ctx, d = x_prompt.shape
    n_lat_b, t_lat, _ = x_sample.shape
    depth = w_in.shape[0]
    past = cache_ckv.shape[2]
    q_lora, kv_lora = g_q.shape[1], g_kv.shape[1]
    rope = cache_krope.shape[3]
    heads, dk, dv = state_gla_fwd.shape[2:]
    rank = w_gate_f.shape[1]
    n_experts = w_router.shape[2]
    hq = QK_NOPE + rope
    hw = QK_NOPE + V_HEAD
    n_ctx, n_lat = n_ctx_b * t_ctx, n_lat_b * t_lat
    t = n_ctx + n_lat
    half = rope // 4
    assert 2 * rope == LANE and 2 * rank <= LANE - rope

    tm = _tile(math.gcd(n_ctx, t_lat), 512)
    tm_s = _tile(math.gcd(n_ctx, t_lat), 1024)
    n_ctx_tiles = n_ctx // tm
    lat_tiles = t_lat // tm

    def sample_of_tile(i):
        return jnp.where(i < n_ctx_tiles, 0, 1 + (i - n_ctx_tiles) // lat_tiles)

    def pos_of_tile(i, tile=tm):
        first = n_ctx // tile
        return jnp.where(i < first, 0, tm_s // tile + (i - first) % (t_lat // tile))

    sizes = (q_lora, kv_lora, rope, heads * dk, heads * dk, heads * dv, rank, rank,
             heads * dv, d, d)
    offs = [0]
    for s in sizes:
        offs.append(offs[-1] + s)
    main_groups = (0, 1, 3, 4, 5, 8, 9, 10)
    col = {}
    acc = 0
    for gidx in main_groups:
        col[gidx] = acc
        acc += sizes[gidx]
    n_main = acc
    col_q, col_k, col_v, col_g, col_a, col_b = col[3], col[4], col[5], col[8], col[9], col[10]

    cos_r, sin_r = _rope_tables(t_lat, rope, tm_s)
    n_tab = cos_r.shape[0]
    cos_q = jnp.concatenate([jnp.ones((n_tab, QK_NOPE), F32), cos_r,
                             jnp.ones((n_tab, hw - hq), F32)], axis=1)
    sin_q = jnp.concatenate([jnp.zeros((n_tab, QK_NOPE), F32), sin_r,
                             jnp.zeros((n_tab, hw - hq), F32)], axis=1)
    cos_k = jnp.concatenate([cos_r, jnp.zeros((n_tab, LANE - rope), F32)], axis=1)
    sin_k = jnp.concatenate([sin_r, jnp.zeros((n_tab, LANE - rope), F32)], axis=1)

    x = jnp.concatenate([x_prompt.reshape(n_ctx, d), x_sample.reshape(n_lat, d)], axis=0)
    c_all = jnp.concatenate([c_ctx[None, :], c, jnp.zeros((8 - 1 - n_lat_b, d), F32)], axis=0)

    def modulation(l):
        def epi(accs, e_refs):
            return accs[0] + e_refs[0][...]
        tn = _tile(6 * d, 1024)
        m = _matmul([(c_all, d, 0)], [(w_ada, (l,))], n_out=6 * d, tm=8, tn=tn, out_dtype=F32,
                    epilogue=epi, prologue=lambda k, a, e: _silu(a), name=f"ada{l}",
                    extras=[(b_ada.reshape(depth, 1, 6 * d), (None, 1, tn),
                             lambda j, i, l=l: (l, 0, j))])
        return m.reshape(8, 6, d)

    mods = [modulation(l) for l in range(depth)]
    _, h = _resid_norm(x, sample_of_tile, tm, "prenorm0", mod_pre=mods[0], g_pre=g_pre_mix[0],
                       shift_idx=0, scale_idx=1)

    new_ckv, new_krope = [], []
    states = None
    for l in range(depth):
        w = w_in[l]
        w_main = jnp.concatenate([w[:, offs[gidx]:offs[gidx + 1]] for gidx in main_groups],
                                 axis=1).astype(BF16)
        w_misc = jnp.concatenate([w[:, offs[2]:offs[3]], w[:, offs[6]:offs[8]],
                                  jnp.zeros((d, LANE - rope - 2 * rank), F32)], axis=1).astype(BF16)
        proj = _matmul([(h, d, 0)], [(w_main, ())], n_out=n_main, tm=tm, tn=_tile(n_main, 1024),
                       out_dtype=F32, epilogue=_first, name=f"w_in{l}")
        misc = _matmul([(h, d, 0)], [(w_misc, ())], n_out=LANE, tm=tm, tn=LANE,
                       out_dtype=F32, epilogue=_first, name=f"w_in_misc{l}")

        w_gate = jnp.zeros((LANE, 2 * heads * dk), F32)
        w_gate = w_gate.at[rope:rope + rank, :heads * dk].set(w_gate_f[l])
        w_gate = w_gate.at[rope + rank:rope + 2 * rank, heads * dk:].set(w_gate_b[l])
        b_gate = jnp.concatenate([b_gate_f[l], b_gate_b[l]]).reshape(1, 2 * heads * dk)
        kr_self, la = pl.pallas_call(
            functools.partial(_prep_kernel, half=half), grid=(t // tm,),
            in_specs=[pl.BlockSpec((tm, LANE), lambda i: (i, 0)),
                      pl.BlockSpec((tm, LANE), lambda i: (pos_of_tile(i), 0)),
                      pl.BlockSpec((tm, LANE), lambda i: (pos_of_tile(i), 0)),
                      pl.BlockSpec((LANE, 2 * heads * dk), lambda i: (0, 0)),
                      pl.BlockSpec((1, 2 * heads * dk), lambda i: (0, 0))],
            out_specs=[pl.BlockSpec((tm, LANE), lambda i: (i, 0)),
                       pl.BlockSpec((tm, 2 * heads * dk), lambda i: (i, 0))],
            out_shape=[jax.ShapeDtypeStruct((t, LANE), BF16),
                       jax.ShapeDtypeStruct((t, 2 * heads * dk), F32)],
            compiler_params=_params(1), name=f"prep{l}")(misc, cos_k, sin_k, w_gate.astype(BF16), b_gate)

        wq = w_uq[l].reshape(q_lora, MLA_HEADS, hq)
        wq = jnp.concatenate([wq, jnp.zeros((q_lora, MLA_HEADS, hw - hq), F32)], axis=2)
        wq = wq.reshape(q_lora, MLA_HEADS * hw).astype(BF16)
        tn_q = _tile(MLA_HEADS * hw, 1024, hw)

        def q_epi(accs, e_refs, tn_q=tn_q):
            cos, sin = e_refs[1][...], e_refs[2][...]
            outs = [_rope(accs[0][:, s:s + hw], cos, sin, half) for s in range(0, tn_q, hw)]
            return jnp.concatenate(outs, axis=1)

        q = _matmul([(proj, q_lora, col[0] // q_lora)], [(wq, ())], n_out=MLA_HEADS * hw, tm=tm_s,
                    tn=tn_q, out_dtype=BF16, epilogue=q_epi, name=f"w_uq{l}",
                    prologue=lambda k, a, e: _rms(a, e[0][...]),
                    extras=[(g_q[l].reshape(1, q_lora), (1, q_lora), lambda j, i: (0, 0)),
                            (cos_q, (tm_s, hw), lambda j, i: (pos_of_tile(i, tm_s), 0)),
                            (sin_q, (tm_s, hw), lambda j, i: (pos_of_tile(i, tm_s), 0))])

        c_kv = _norm_cols(proj, kv_lora, col[1] // kv_lora, g_kv[l], tm, f"ckv_norm{l}")
        tn_kv = _tile(MLA_HEADS * hw, 1024)
        kv_self = _matmul([(c_kv, kv_lora, 0)], [(w_ukv, (l,))], n_out=MLA_HEADS * hw, tm=tm_s,
                          tn=tn_kv, out_dtype=BF16, epilogue=_first, name=f"w_ukv{l}")
        ckv_cache = cache_ckv[:, l].reshape(n_lat_b * past, kv_lora)
        kv_cache = _matmul([(ckv_cache, kv_lora, 0)], [(w_ukv, (l,))], n_out=MLA_HEADS * hw,
                           tm=_tile(n_lat_b * past, 512), tn=tn_kv, out_dtype=BF16,
                           epilogue=_first, name=f"w_ukv_cache{l}")
        kr_cache = jnp.pad(cache_krope[:, l].reshape(n_lat_b * past, rope),
                           ((0, 0), (0, LANE - rope))).astype(BF16)

        scale = hq ** -0.5
        attn = _attention(q, [(kv_self, t_ctx, 0)], [(kr_self, t_ctx, 0)], n_batch=n_ctx_b,
                          t_q=t_ctx, q_row0=0, tq=_tile(t_ctx, 256),
                          n_heads=math.gcd(MLA_HEADS, 8), group=4, scale=scale, name=f"attn_ctx{l}",
                          o_prev=jnp.zeros((t, MLA_HEADS * V_HEAD), BF16))
        attn = _attention(q, [(kv_self, t_lat, n_ctx), (kv_cache, past, 0)],
                          [(kr_self, t_lat, n_ctx), (kr_cache, past, 0)], n_batch=n_lat_b,
                          t_q=t_lat, q_row0=n_ctx, tq=_tile(t_lat, 512),
                          n_heads=math.gcd(MLA_HEADS, 4), group=2, scale=scale, name=f"attn_lat{l}",
                          o_prev=attn)

        gla_kw = dict(heads=heads, dk=dk, dv=dv, col_q=col_q, col_k=col_k, col_v=col_v,
                      col_g=col_g, layer=l)
        gla, *states = _gla(proj, la, g_gla[l], n_batch=n_ctx_b, t_b=t_ctx, row0=0,
                            name=f"gla_ctx{l}", o_prev=jnp.zeros((t, heads * dv), BF16),
                            st_prev=states, st_depth=depth, **gla_kw)
        gla, = _gla(proj, la, g_gla[l], n_batch=n_lat_b, t_b=t_lat, row0=n_ctx,
                    s0=(state_gla_fwd, state_gla_bwd), name=f"gla_lat{l}", o_prev=gla, **gla_kw)

        new_ckv.append(c_kv[:n_ctx].reshape(n_ctx_b, t_ctx, kv_lora))
        new_krope.append(misc[:n_ctx, :rope].reshape(n_ctx_b, t_ctx, rope))

        tn_m = _tile(math.gcd(col_a, col_b, d), 512)

        def merge_epi(accs, e_refs):
            return (jax.nn.sigmoid(e_refs[0][...]) * accs[0]
                    + jax.nn.sigmoid(e_refs[1][...]) * accs[1])

        merged = _matmul([(attn, MLA_HEADS * V_HEAD, 0), (gla, heads * dv, 0)],
                         [(w_pa, (l,)), (w_pb, (l,))], n_out=d, tm=tm, tn=tn_m, out_dtype=BF16,
                         epilogue=merge_epi, name=f"merge{l}",
                         extras=[(proj, (tm, tn_m), lambda j, i: (i, col_a // tn_m + j)),
                                 (proj, (tm, tn_m), lambda j, i: (i, col_b // tn_m + j))])
        y = _matmul([(merged, d, 0)], [(w_o, (l,))], n_out=d, tm=tm, tn=_tile(d, 1024),
                    out_dtype=F32, epilogue=_first, name=f"w_o{l}")
        moe = l % 2 == 1
        x, h = _resid_norm(x, sample_of_tile, tm, f"mix_resid{l}", y=y, mod_res=mods[l],
                           g_post=g_post_mix[l], gate_idx=2, mod_pre=mods[l], g_pre=g_pre_ffn[l],
                           shift_idx=3, scale_idx=4)

        jx = l // 2

        def swiglu_epi(accs, e_refs):
            return _silu(accs[0]) * accs[1]

        ffn_out = {}
        if not moe:
            d_ff = w_ff_gate.shape[2]
            ff = _matmul([(h, d, 0)], [(w_ff_gate, (jx,)), (w_ff_up, (jx,))], a_of_w=[0, 0],
                         n_out=d_ff, tm=tm, tn=_tile(d_ff, 512), out_dtype=BF16,
                         epilogue=swiglu_epi, name=f"ffn_up{l}")
            ffn_out["y"] = _matmul([(ff, d_ff, 0)], [(w_ff_down, (jx,))], n_out=d, tm=tm,
                                   tn=_tile(d, 512), out_dtype=F32, epilogue=_first,
                                   name=f"ffn_down{l}")
        else:
            d_ex = w_ex_gate.shape[3]
            w_r = jnp.pad(w_router[jx], ((0, 0), (0, LANE - n_experts))).astype(BF16)
            route = pl.pallas_call(
                functools.partial(_router_kernel, n_experts=n_experts), grid=(t // tm,),
                in_specs=[pl.BlockSpec((tm, d), lambda i: (i, 0)),
                          pl.BlockSpec((d, LANE), lambda i: (0, 0))],
                out_specs=pl.BlockSpec((tm, LANE), lambda i: (i, 0)),
                out_shape=jax.ShapeDtypeStruct((t, LANE), F32),
                compiler_params=_params(1), name=f"router{l}")(h, w_r)
            tile_e = _tile(t * TOP_K, EXPERT_TILE, 8)
            tok_of, dest, tile_expert = _routing_tables(route, n_experts, tile_e)
            xs = _gather(h.reshape(t, d // LANE, LANE), tok_of, tile_e, f"moe_gather{l}")
            xs = xs.reshape(xs.shape[0], d)
            ff = _matmul([(xs, d, 0)], [(w_ex_gate, (jx, 0)), (w_ex_up, (jx, 0))], a_of_w=[0, 0],
                         n_out=d_ex, tm=tile_e, tn=_tile(d_ex, 256), out_dtype=BF16,
                         epilogue=swiglu_epi, name=f"moe_up{l}", tile_expert=tile_expert)
            ys = _matmul([(ff, d_ex, 0)], [(w_ex_down, (jx, 0))], n_out=d, tm=tile_e,
                         tn=_tile(d, 1024), out_dtype=F32, epilogue=_first, name=f"moe_down{l}",
                         tile_expert=tile_expert)
            ffn_out["routed"] = (ys, dest, route)

        if l + 1 < depth:
            x, h = _resid_norm(x, sample_of_tile, tm, f"ffn_resid{l}", mod_res=mods[l],
                               g_post=g_post_ffn[l], gate_idx=5, mod_pre=mods[l + 1],
                               g_pre=g_pre_mix[l + 1], shift_idx=0, scale_idx=1, **ffn_out)
        else:
            x, _ = _resid_norm(x, sample_of_tile, tm, f"ffn_resid{l}", mod_res=mods[l],
                               g_post=g_post_ffn[l], gate_idx=5, **ffn_out)

    return (x[:n_ctx].reshape(n_ctx_b, t_ctx, d), x[n_ctx:].reshape(n_lat_b, t_lat, d),
            jnp.stack(new_ckv, axis=1), jnp.stack(new_krope, axis=1),
            states[0], states[1])
```

```python
import functools
import math

import jax
import jax.numpy as jnp
from jax import lax
from jax.experimental import pallas as pl
from jax.experimental.pallas import tpu as pltpu

MLA_HEADS = 16
QK_NOPE = 128
V_HEAD = 128
GRID_W = 64
ROPE_THETA = 10000.0
GATE_NORM = 16.0
GLA_CHUNK = 64
TOP_K = 2
EPS = 1e-6
LANE = 128
VMEM_LIMIT = 52 * 1024 * 1024
EXPERT_TILE = 512

BF16 = jnp.bfloat16
F32 = jnp.float32


def _params(n_grid, **kw):
    return pltpu.CompilerParams(
        dimension_semantics=("arbitrary",) * n_grid, vmem_limit_bytes=VMEM_LIMIT, **kw)


def _tile(n, target, quantum=LANE):
    if n <= target:
        return n
    t = (target // quantum) * quantum
    while t >= quantum:
        if n % t == 0:
            return t
        t -= quantum
    return n


def _rms(x, g):
    return x * lax.rsqrt(jnp.mean(x * x, axis=-1, keepdims=True) + EPS) * g


def _silu(x):
    return x * jax.nn.sigmoid(x)


def _dot(a, b):
    return jnp.dot(a, b, preferred_element_type=F32)


def _dot_nt(a, b):
    return lax.dot_general(a, b, (((1,), (1,)), ((), ())), preferred_element_type=F32)


def _dot_tn(a, b):
    return lax.dot_general(a, b, (((0,), (0,)), ((), ())), preferred_element_type=F32)


def _mm_kernel(*refs, n_a, n_w, a_of_w, n_e, cast_w, grouped, prologue, epilogue, late):
    if grouped:
        te_ref, refs = refs[0], refs[1:]
    a_refs = refs[:n_a]
    w_refs = refs[n_a:n_a + n_w]
    e_refs = refs[n_a + n_w:n_a + n_w + n_e]
    o_ref = refs[n_a + n_w + n_e]
    wb_refs = refs[n_a + n_w + n_e + 1:]
    i = pl.program_id(1)

    def compute():
        if cast_w:
            fresh = i == 0
            if grouped:
                fresh = fresh | (te_ref[i] != te_ref[jnp.maximum(i - 1, 0)])

            @pl.when(fresh)
            def _():
                for w_ref, wb_ref in zip(w_refs, wb_refs):
                    wb_ref[...] = w_ref[...].astype(BF16)
            w_use = wb_refs
        else:
            w_use = w_refs
        a_vals = []
        for k, a_ref in enumerate(a_refs):
            a = a_ref[...]
            if prologue is not None:
                a = prologue(k, a, e_refs)
            a_vals.append(a.astype(BF16))
        accs = [_dot(a_vals[a_of_w[k]], w_ref[...]) for k, w_ref in enumerate(w_use)]
        if late is None:
            o_ref[...] = epilogue(accs, e_refs).astype(o_ref.dtype)
        else:
            j = pl.program_id(0)

            @pl.when(j < late[0])
            def _():
                o_ref[...] = epilogue(accs, e_refs).astype(o_ref.dtype)

            @pl.when(j >= late[0])
            def _():
                o_ref[...] = late[1](accs, e_refs).astype(o_ref.dtype)

    if grouped:
        has_rows = i < te_ref[pl.num_programs(1)]
        pl.when(has_rows)(compute)

        @pl.when(jnp.logical_not(has_rows))
        def _():
            o_ref[...] = jnp.zeros_like(o_ref)
    else:
        compute()


def _matmul(a_list, w_list, *, n_out, tm, tn, out_dtype, epilogue, name, a_of_w=None,
            prologue=None, extras=(), tile_expert=None, late=None):
    m = a_list[0][0].shape[0]
    a_of_w = a_of_w or list(range(len(w_list)))
    grouped = tile_expert is not None
    assert m % tm == 0 and n_out % tn == 0
    in_specs, args, scratch = [], [], []
    for arr, k, cb in a_list:
        in_specs.append(pl.BlockSpec((tm, k), lambda j, i, *_, cb=cb: (i, cb)))
        args.append(arr)
    cast_w = w_list[0][0].dtype != BF16
    for arr, lead in w_list:
        k = arr.shape[-2]
        if grouped:
            im = lambda j, i, te, lead=lead: tuple(lead[:-1]) + (te[i], 0, j)
        else:
            im = lambda j, i, lead=lead: tuple(lead) + (0, j)
        in_specs.append(pl.BlockSpec((None,) * len(lead) + (k, tn), im))
        args.append(arr)
        if cast_w:
            scratch.append(pltpu.VMEM((k, tn), BF16))
    for arr, bs, im in extras:
        in_specs.append(pl.BlockSpec(bs, lambda j, i, *_, im=im: im(j, i)))
        args.append(arr)
    kern = functools.partial(_mm_kernel, n_a=len(a_list), n_w=len(w_list), a_of_w=a_of_w,
                             n_e=len(extras), cast_w=cast_w, grouped=grouped,
                             prologue=prologue, epilogue=epilogue, late=late)
    grid_spec = pltpu.PrefetchScalarGridSpec(
        num_scalar_prefetch=1 if grouped else 0,
        grid=(n_out // tn, m // tm),
        in_specs=in_specs,
        out_specs=pl.BlockSpec((tm, tn), lambda j, i, *_: (i, j)),
        scratch_shapes=scratch)
    if grouped:
        args = [tile_expert] + args
    return pl.pallas_call(
        kern, grid_spec=grid_spec,
        out_shape=jax.ShapeDtypeStruct((m, n_out), out_dtype),
        compiler_params=_params(2), name=name,
    )(*args)


def _first(accs, e_refs):
    return accs[0]


def _row_copy(src, row, dst, r, sem):
    return pltpu.make_async_copy(src.at[pl.ds(row, 1)], dst.at[pl.ds(r, 1)], sem)


def _gather_rows(idx_ref, base, stride, n, src, dst, dst_base, sem):
    def issue(r, carry):
        _row_copy(src, idx_ref[base + r * stride], dst, dst_base + r, sem).start()
        return carry

    def wait(r, carry):
        _row_copy(src, 0, dst, dst_base + r, sem).wait()
        return carry

    lax.fori_loop(0, n, issue, 0, unroll=8)
    lax.fori_loop(0, n, wait, 0, unroll=8)


def _gather_kernel(idx_ref, src_hbm, o_ref, sem):
    tg = o_ref.shape[0]
    _gather_rows(idx_ref, pl.program_id(0) * tg, 1, tg, src_hbm, o_ref, 0, sem)


def _gather(src, idx, tg, name):
    n = idx.shape[0]
    blk = (tg,) + src.shape[1:]
    grid_spec = pltpu.PrefetchScalarGridSpec(
        num_scalar_prefetch=1, grid=(n // tg,),
        in_specs=[pl.BlockSpec(memory_space=pl.ANY)],
        out_specs=pl.BlockSpec(blk, lambda i, idx: (i,) + (0,) * (len(blk) - 1)),
        scratch_shapes=[pltpu.SemaphoreType.DMA(())])
    return pl.pallas_call(
        _gather_kernel, grid_spec=grid_spec,
        out_shape=jax.ShapeDtypeStruct((n,) + src.shape[1:], src.dtype),
        compiler_params=_params(1), name=name)(idx, src)


def _resid_norm_kernel(*refs, mode, want_x, want_h, gate_idx, shift_idx, scale_idx, split_in,
                       split_out, n_first):
    it = iter(refs)
    dest_ref = next(it) if mode == "routed" else None
    x_refs = [next(it) for _ in range(2 if split_in else 1)]
    if mode == "dense":
        y_ref = next(it)
    elif mode == "routed":
        ys_hbm, route_ref = next(it), next(it)
    if mode is not None:
        mod_res_ref, g_post_ref = next(it), next(it)
    if want_h:
        mod_pre_ref, g_pre_ref = next(it), next(it)
    xo_refs = [next(it) for _ in range((2 if split_out else 1) if want_x else 0)]
    h_ref = next(it) if want_h else None
    i = pl.program_id(0)
    x = x_refs[0][...]
    if split_in:
        x = jnp.where(i < n_first, x, x_refs[1][...])
    if mode is not None:
        if mode == "routed":
            buf1, buf2, sem = next(it), next(it), next(it)
            tm = x.shape[0]
            base = i * tm * TOP_K
            _gather_rows(dest_ref, base, TOP_K, tm, ys_hbm, buf1, 0, sem)
            _gather_rows(dest_ref, base + 1, TOP_K, tm, ys_hbm, buf2, 0, sem)
            route = route_ref[...]
            y = route[:, 2:3] * buf1[...] + route[:, 3:4] * buf2[...]
        else:
            y = y_ref[...]
        gate = mod_res_ref[0, gate_idx:gate_idx + 1, :]
        x = x + gate * _rms(y, g_post_ref[...])
    if want_x and split_out:
        @pl.when(i < n_first)
        def _():
            xo_refs[0][...] = x

        @pl.when(i >= n_first)
        def _():
            xo_refs[1][...] = x
    elif want_x:
        xo_refs[0][...] = x
    if want_h:
        scale = mod_pre_ref[0, scale_idx:scale_idx + 1, :]
        shift = mod_pre_ref[0, shift_idx:shift_idx + 1, :]
        h_ref[...] = (_rms(x, g_pre_ref[...]) * (1.0 + scale) + shift).astype(h_ref.dtype)


def _resid_norm(x, sample_of_tile, tm, name, *, y=None, routed=None, mod_res=None, g_post=None,
                gate_idx=0, mod_pre=None, g_pre=None, shift_idx=0, scale_idx=0, split_out=False,
                n_first=0):
    split_in = isinstance(x, (tuple, list))
    xs_in = list(x) if split_in else [x]
    t, d = sum(a.shape[0] for a in xs_in), xs_in[0].shape[1]
    mode = "dense" if y is not None else ("routed" if routed is not None else None)
    want_h = mod_pre is not None
    want_x = mode is not None
    row = pl.BlockSpec((tm, d), lambda i, *_: (i, 0))
    first = pl.BlockSpec((tm, d), lambda i, *_: (jnp.minimum(i, n_first - 1), 0))
    rest = pl.BlockSpec((tm, d), lambda i, *_: (jnp.maximum(i - n_first, 0), 0))
    vec = pl.BlockSpec((1, d), lambda i, *_: (0, 0))
    mod = pl.BlockSpec((1, 6, d), lambda i, *_: (sample_of_tile(i), 0, 0))
    in_specs, args = ([first, rest] if split_in else [row]), xs_in
    out_specs, out_shape, scratch = [], [], []
    if mode == "dense":
        in_specs.append(row)
        args.append(y)
    elif mode == "routed":
        ys, dest, route = routed
        in_specs += [pl.BlockSpec(memory_space=pl.ANY),
                     pl.BlockSpec((tm, LANE), lambda i, *_: (i, 0))]
        args += [ys, route]
        scratch = [pltpu.VMEM((tm, d), F32), pltpu.VMEM((tm, d), F32),
                   pltpu.SemaphoreType.DMA(())]
    if mode is not None:
        in_specs += [mod, vec]
        args += [mod_res, g_post.reshape(1, d)]
    if want_h:
        in_specs += [mod, vec]
        args += [mod_pre, g_pre.reshape(1, d)]
    if want_x and split_out:
        out_specs += [first, rest]
        out_shape += [jax.ShapeDtypeStruct((n_first * tm, d), F32),
                      jax.ShapeDtypeStruct((t - n_first * tm, d), F32)]
    elif want_x:
        out_specs.append(row)
        out_shape.append(jax.ShapeDtypeStruct((t, d), F32))
    if want_h:
        out_specs.append(row)
        out_shape.append(jax.ShapeDtypeStruct((t, d), BF16))
    kern = functools.partial(_resid_norm_kernel, mode=mode, want_x=want_x, want_h=want_h,
                             gate_idx=gate_idx, shift_idx=shift_idx, scale_idx=scale_idx,
                             split_in=split_in, split_out=split_out, n_first=n_first)
    grid_spec = pltpu.PrefetchScalarGridSpec(
        num_scalar_prefetch=1 if mode == "routed" else 0, grid=(t // tm,),
        in_specs=in_specs, out_specs=out_specs, scratch_shapes=scratch)
    if mode == "routed":
        args = [dest] + args
    outs = pl.pallas_call(kern, grid_spec=grid_spec, out_shape=out_shape,
                          compiler_params=_params(1), name=name)(*args)
    n_x = (2 if split_out else 1) if want_x else 0
    x_new = (tuple(outs[:2]) if split_out else outs[0]) if want_x else x
    h = outs[n_x] if want_h else None
    return x_new, h


def _norm_kernel(x_ref, g_ref, o_ref):
    o_ref[...] = _rms(x_ref[...], g_ref[...])


def _norm_cols(arr, width, col_block, g, tm, name):
    t = arr.shape[0]
    return pl.pallas_call(
        _norm_kernel, grid=(t // tm,),
        in_specs=[pl.BlockSpec((tm, width), lambda i: (i, col_block)),
                  pl.BlockSpec((1, width), lambda i: (0, 0))],
        out_specs=pl.BlockSpec((tm, width), lambda i: (i, 0)),
        out_shape=jax.ShapeDtypeStruct((t, width), F32),
        compiler_params=_params(1), name=name)(arr, g.reshape(1, width))


def _swap_pairs(w, rope):
    q = rope // 4
    return jnp.concatenate([w[..., q:2 * q], w[..., :q], w[..., 3 * q:], w[..., 2 * q:3 * q]],
                           axis=-1)


def _prep_kernel(misc_ref, cos_ref, sin_ref, wg_ref, bg_ref, kr_ref, la_ref):
    misc = misc_ref[:, :LANE]
    kr = misc * cos_ref[...] + misc_ref[:, LANE:] * sin_ref[...]
    kr_ref[...] = kr.astype(kr_ref.dtype)
    z = _dot(misc.astype(BF16), wg_ref[...]) + bg_ref[...]
    log_sig = jnp.minimum(z, 0.0) - jnp.log1p(jnp.exp(-jnp.abs(z)))
    la_ref[...] = log_sig * (1.0 / GATE_NORM)


def _attn_kernel(*refs, n_seg, n_heads, group, scale):
    qn_ref, qr_ref = refs[:2]
    kv_refs = refs[2:2 + n_seg]
    kr_refs = refs[2 + n_seg:2 + 2 * n_seg]
    o_ref = refs[-1]
    hw = QK_NOPE + V_HEAD
    krs = [kr_ref[...] for kr_ref in kr_refs]
    add = lambda a, b: a + b
    c = scale * math.log2(math.e)
    for h0 in range(0, n_heads, group):
        hds = range(h0, min(h0 + group, n_heads))
        scores = []
        for hd in hds:
            qn = qn_ref[:, hd * QK_NOPE:(hd + 1) * QK_NOPE]
            qr = qr_ref[:, hd * LANE:(hd + 1) * LANE]
            scores.append([_dot_nt(qn, kv_ref[:, hd * hw:hd * hw + QK_NOPE]) + _dot_nt(qr, kr)
                           for kv_ref, kr in zip(kv_refs, krs)])
        ms = [functools.reduce(jnp.maximum, [jnp.max(s, axis=-1, keepdims=True) for s in ss])
              for ss in scores]
        ps = [[jnp.exp2((s - m) * c) for s in ss] for ss, m in zip(scores, ms)]
        invs = [1.0 / functools.reduce(add, [jnp.sum(p, axis=-1, keepdims=True) for p in pp])
                for pp in ps]
        for hd, pp, inv in zip(hds, ps, invs):
            o = functools.reduce(add, [
                _dot((p * inv).astype(BF16), kv_ref[:, hd * hw + QK_NOPE:(hd + 1) * hw])
                for p, kv_ref in zip(pp, kv_refs)])
            o_ref[:, hd * V_HEAD:(hd + 1) * V_HEAD] = o.astype(o_ref.dtype)


def _attention(q_nope, q_rope, kvs, krs, *, n_batch, t_q, q_row0, tq, n_heads, group, scale, name,
               o_prev):
    hw = QK_NOPE + V_HEAD
    nq = t_q // tq
    q_map = lambda b, g, i: (q_row0 // tq + b * nq + i, g)
    in_specs = [pl.BlockSpec((tq, n_heads * QK_NOPE), q_map),
                pl.BlockSpec((tq, n_heads * LANE), q_map)]
    args = [q_nope, q_rope]
    for arr, tk, row0 in kvs:
        in_specs.append(pl.BlockSpec((tk, n_heads * hw),
                                     lambda b, g, i, tk=tk, row0=row0: (row0 // tk + b, g)))
        args.append(arr)
    for arr, tk, row0 in krs:
        in_specs.append(pl.BlockSpec((tk, LANE),
                                     lambda b, g, i, tk=tk, row0=row0: (row0 // tk + b, 0)))
        args.append(arr)
    aliases = {len(args): 0}
    in_specs.append(pl.BlockSpec(memory_space=pl.ANY))
    args.append(o_prev)
    kern = functools.partial(_attn_kernel, n_seg=len(kvs), n_heads=n_heads, group=group,
                             scale=scale)
    return pl.pallas_call(
        kern, grid=(n_batch, MLA_HEADS // n_heads, nq), in_specs=in_specs,
        out_specs=pl.BlockSpec((tq, n_heads * V_HEAD), q_map),
        out_shape=jax.ShapeDtypeStruct(o_prev.shape, o_prev.dtype),
        input_output_aliases=aliases,
        compiler_params=_params(3), name=name)(*args)


def _split3(x):
    hi = x.astype(BF16)
    r1 = x - hi.astype(F32)
    mid = r1.astype(BF16)
    lo = (r1 - mid.astype(F32)).astype(BF16)
    return hi, mid, lo


def _gla_kernel(*refs, has_init, want_states, n_alias, q_scale, layer, all_layers):
    it = iter(refs)
    q_ref, k_ref, v_ref, laf_ref, lab_ref, gout_ref, ggla_ref = (next(it) for _ in range(7))
    s0_refs = (next(it), next(it)) if has_init else None
    for _ in range(n_alias):
        next(it)
    o_ref = next(it)
    s_out_refs = (next(it), next(it)) if want_states else None
    st_refs = (next(it), next(it))
    vb, qd_s, ks_s, dec_s, oacc = (next(it) for _ in range(5))
    c = GLA_CHUNK
    t_b, dk = q_ref.shape
    n_chunks = t_b // c
    rb = min(t_b, 4 * c)
    row = lax.broadcasted_iota(jnp.int32, (rb, rb), 0)
    col = lax.broadcasted_iota(jnp.int32, (rb, rb), 1)
    same = (row // c) == (col // c)
    masks = (same & (col <= row), same & (col >= row))
    ones_bd = jnp.where(same, 1.0, 0.0).astype(BF16)
    la_refs = (laf_ref, lab_ref)

    vb[...] = v_ref[...].astype(BF16)
    for d in range(2):
        if has_init:
            st_refs[d][...] = s0_refs[d][0, 0].T
        else:
            st_refs[d][...] = jnp.zeros_like(st_refs[d])

    for d in range(2):
        tri = jnp.where(masks[d], 1.0, 0.0).astype(BF16)
        for r0 in range(0, t_b, rb):
            rows = pl.ds(r0, rb)
            hi, mid, lo = _split3(la_refs[d][rows, :])
            b = (_dot(tri, hi) + _dot(tri, mid)) + _dot(tri, lo)
            b_end = (_dot(ones_bd, hi) + _dot(ones_bd, mid)) + _dot(ones_bd, lo)
            kc = k_ref[rows, :]
            qd = (q_ref[rows, :] * q_scale * jnp.exp(b)).astype(BF16)
            kd = (kc * jnp.exp(-b)).astype(BF16)
            qd_s[d, rows, :] = qd
            ks_s[d, rows, :] = (kc * jnp.exp(b_end - b)).astype(BF16)
            dec_s[d, rows, :] = jnp.exp(b_end)
            a = jnp.where(masks[d], _dot_nt(qd, kd), 0.0).astype(BF16)
            o_in = _dot(a, vb[rows, :])
            if d == 0:
                oacc[rows, :] = o_in
            else:
                oacc[rows, :] += o_in

    for i in range(n_chunks):
        for d in range(2):
            c0 = (i if d == 0 else n_chunks - 1 - i) * c
            rows = pl.ds(c0, c)
            st = st_refs[d]
            u = _dot_tn(vb[rows, :], ks_s[d, rows, :])
            s_t = st[...]
            oacc[rows, :] += _dot_nt(qd_s[d, rows, :], s_t.astype(BF16))
            st[...] = s_t * dec_s[d, pl.ds(c0, 1), :] + u

    o_ref[...] = (_rms(oacc[...], ggla_ref[...]) * _silu(gout_ref[...])).astype(o_ref.dtype)
    if want_states:
        for d in range(2):
            s_fin = st_refs[d][...].T
            if all_layers:
                for ll in range(s_out_refs[d].shape[1]):
                    s_out_refs[d][0, ll, 0] = s_fin if ll == layer else jnp.zeros_like(s_fin)
            else:
                s_out_refs[d][0, 0] = s_fin


def _gla(proj, la, g_gla, *, n_batch, t_b, row0, heads, dk, dv, col_q, col_k, col_v, col_g,
         name, o_prev, s0=None, layer=0, st_prev=None, st_depth=0):
    has_init = s0 is not None
    rb = row0 // t_b
    in_specs = [
        pl.BlockSpec((t_b, dk), lambda b, h: (rb + b, col_q // dk + h)),
        pl.BlockSpec((t_b, dk), lambda b, h: (rb + b, col_k // dk + h)),
        pl.BlockSpec((t_b, dv), lambda b, h: (rb + b, col_v // dv + h)),
        pl.BlockSpec((t_b, dk), lambda b, h: (rb + b, h)),
        pl.BlockSpec((t_b, dk), lambda b, h: (rb + b, heads + h)),
        pl.BlockSpec((t_b, dv), lambda b, h: (rb + b, col_g // dv + h)),
        pl.BlockSpec((1, dv), lambda b, h: (0, 0)),
    ]
    args = [proj, proj, proj, la, la, proj, g_gla.reshape(1, dv)]
    st_spec = pl.BlockSpec((1, None, 1, dk, dv), lambda b, h: (b, layer, h, 0, 0))
    if has_init:
        in_specs += [st_spec, st_spec]
        args += list(s0)
    out_specs = [pl.BlockSpec((t_b, dv), lambda b, h: (rb + b, h))]
    out_shape = [jax.ShapeDtypeStruct(o_prev.shape, o_prev.dtype)]
    aliases = {len(args): 0}
    in_specs.append(pl.BlockSpec(memory_space=pl.ANY))
    args.append(o_prev)
    want_states = st_prev is not None or st_depth > 0
    if st_prev is not None:
        out_specs += [st_spec, st_spec]
        for k, arr in enumerate(st_prev):
            out_shape.append(jax.ShapeDtypeStruct(arr.shape, arr.dtype))
            aliases[len(args)] = 1 + k
            in_specs.append(pl.BlockSpec(memory_space=pl.ANY))
            args.append(arr)
    elif want_states:
        out_specs += [pl.BlockSpec((1, st_depth, 1, dk, dv), lambda b, h: (b, 0, h, 0, 0))] * 2
        out_shape += [jax.ShapeDtypeStruct((n_batch, st_depth, heads, dk, dv), F32)] * 2
    kern = functools.partial(_gla_kernel, has_init=has_init, want_states=want_states,
                             n_alias=len(aliases), q_scale=dk ** -0.5, layer=layer,
                             all_layers=st_prev is None)
    return pl.pallas_call(
        kern, grid=(n_batch, heads), in_specs=in_specs, out_specs=out_specs, out_shape=out_shape,
        input_output_aliases=aliases,
        scratch_shapes=[pltpu.VMEM((dv, dk), F32), pltpu.VMEM((dv, dk), F32),
                        pltpu.VMEM((t_b, dv), BF16), pltpu.VMEM((2, t_b, dk), BF16),
                        pltpu.VMEM((2, t_b, dk), BF16), pltpu.VMEM((2, t_b, dk), F32),
                        pltpu.VMEM((t_b, dv), F32)],
        compiler_params=_params(2), name=name)(*args)


def _router_kernel(h_ref, w_ref, o_ref, *, n_experts):
    logits = _dot(h_ref[...].astype(BF16), w_ref[...])
    lane = lax.broadcasted_iota(jnp.int32, logits.shape, 1).astype(F32)
    neg = jnp.float32(-jnp.inf)
    lg = jnp.where(lane < n_experts, logits, neg)
    m1 = jnp.max(lg, axis=-1, keepdims=True)
    i1 = jnp.min(jnp.where(lg == m1, lane, LANE), axis=-1, keepdims=True)
    lg2 = jnp.where(lane == i1, neg, lg)
    m2 = jnp.max(lg2, axis=-1, keepdims=True)
    i2 = jnp.min(jnp.where(lg2 == m2, lane, LANE), axis=-1, keepdims=True)
    e2 = jnp.exp(m2 - m1)
    inv = 1.0 / (1.0 + e2)
    o_ref[...] = (jnp.where(lane == 0, i1, 0.0) + jnp.where(lane == 1, i2, 0.0)
                  + jnp.where(lane == 2, inv, 0.0) + jnp.where(lane == 3, e2 * inv, 0.0))


def _routing_tables(route, n_experts, tile):
    t = route.shape[0]
    n_assign = t * TOP_K
    n_rows = n_assign + n_experts * tile
    e_flat = route[:, :TOP_K].astype(jnp.int32).reshape(n_assign)
    onehot = (e_flat[:, None] == jnp.arange(n_experts, dtype=jnp.int32)[None, :]).astype(jnp.int32)
    csum = jnp.cumsum(onehot, axis=0)
    rank = jnp.take_along_axis(csum, e_flat[:, None], axis=1)[:, 0] - 1
    counts = csum[-1]
    padded = ((counts + tile - 1) // tile) * tile
    ends = jnp.cumsum(padded)
    starts = ends - padded
    dest = starts[e_flat] + rank
    tok_of = jnp.zeros((n_rows,), jnp.int32).at[dest].set(
        jnp.arange(n_assign, dtype=jnp.int32) // TOP_K, unique_indices=True)
    n_valid = ends[-1] // tile
    tile_idx = jnp.minimum(jnp.arange(n_rows // tile, dtype=jnp.int32), n_valid - 1)
    tile_expert = jnp.sum((tile_idx * tile)[:, None] >= ends[None, :], axis=1, dtype=jnp.int32)
    tile_info = jnp.concatenate([tile_expert, n_valid[None]]).astype(jnp.int32)
    return tok_of, dest.astype(jnp.int32), tile_info


def _rope_tables(n_lat, rope, tm_id):
    axis_half = rope // 4
    rows = n_lat // GRID_W
    r = jnp.repeat(jnp.arange(rows, dtype=F32), GRID_W)
    c = jnp.tile(jnp.arange(GRID_W, dtype=F32), rows)
    inv = ROPE_THETA ** (-jnp.arange(axis_half, dtype=F32) / axis_half)
    ar, ac = r[:, None] * inv, c[:, None] * inv
    cos = jnp.concatenate([jnp.cos(ar), jnp.cos(ar), jnp.cos(ac), jnp.cos(ac)], axis=1)
    sin = jnp.concatenate([-jnp.sin(ar), jnp.sin(ar), -jnp.sin(ac), jnp.sin(ac)], axis=1)
    cos = jnp.concatenate([jnp.ones((tm_id, rope), F32), cos], axis=0)
    sin = jnp.concatenate([jnp.zeros((tm_id, rope), F32), sin], axis=0)
    return cos, sin


def kernel(x_prompt, x_sample, cache_ckv, cache_krope, state_gla_fwd, state_gla_bwd, c, c_ctx, w_ada, b_ada, g_pre_mix, g_post_mix, g_pre_ffn, g_post_ffn, w_in, g_q, w_uq, g_kv, w_ukv, w_gate_f, b_gate_f, w_gate_b, b_gate_b, g_gla, w_pa, w_pb, w_o, w_ff_gate, w_ff_up, w_ff_down, w_router, w_ex_gate, w_ex_up, w_ex_down):
    n_ctx_b, t_ctx, d = x_prompt.shape
    n_lat_b, t_lat, _ = x_sample.shape
    depth = w_in.shape[0]
    past = cache_ckv.shape[2]
    q_lora, kv_lora = g_q.shape[1], g_kv.shape[1]
    rope = cache_krope.shape[3]
    heads, dk, dv = state_gla_fwd.shape[2:]
    rank = w_gate_f.shape[1]
    n_experts = w_router.shape[2]
    hq = QK_NOPE + rope
    hw = QK_NOPE + V_HEAD
    n_ctx, n_lat = n_ctx_b * t_ctx, n_lat_b * t_lat
    t = n_ctx + n_lat
    assert 2 * rope == LANE and 2 * rank <= LANE - rope

    tm = _tile(math.gcd(n_ctx, t_lat), 512)
    tm_s = _tile(math.gcd(n_ctx, t_lat), 1024)
    n_ctx_tiles = n_ctx // tm
    lat_tiles = t_lat // tm

    def sample_of_tile(i):
        return jnp.where(i < n_ctx_tiles, 0, 1 + (i - n_ctx_tiles) // lat_tiles)

    def pos_of_tile(i, tile=tm):
        first = n_ctx // tile
        return jnp.where(i < first, 0, tm_s // tile + (i - first) % (t_lat // tile))

    sizes = (q_lora, kv_lora, rope, heads * dk, heads * dk, heads * dv, rank, rank,
             heads * dv, d, d)
    offs = [0]
    for s in sizes:
        offs.append(offs[-1] + s)
    main_groups = (0, 1, 3, 4, 5, 8, 9, 10)
    col = {}
    acc = 0
    for gidx in main_groups:
        col[gidx] = acc
        acc += sizes[gidx]
    n_main = acc
    col_q, col_k, col_v, col_g, col_a, col_b = col[3], col[4], col[5], col[8], col[9], col[10]

    cos_r, sin_r = _rope_tables(t_lat, rope, tm_s)
    n_tab = cos_r.shape[0]
    cos_k = jnp.concatenate([cos_r, jnp.zeros((n_tab, LANE - rope), F32)], axis=1)
    sin_k = jnp.concatenate([sin_r, jnp.zeros((n_tab, LANE - rope), F32)], axis=1)

    x = (x_prompt.reshape(n_ctx, d), x_sample.reshape(n_lat, d))
    c_all = jnp.concatenate([c_ctx[None, :], c, jnp.zeros((8 - 1 - n_lat_b, d), F32)], axis=0)

    def modulation(l):
        def epi(accs, e_refs):
            return accs[0] + e_refs[0][...]
        tn = _tile(6 * d, 1024)
        m = _matmul([(c_all, d, 0)], [(w_ada, (l,))], n_out=6 * d, tm=8, tn=tn, out_dtype=F32,
                    epilogue=epi, prologue=lambda k, a, e: _silu(a), name=f"ada{l}",
                    extras=[(b_ada.reshape(depth, 1, 6 * d), (None, 1, tn),
                             lambda j, i, l=l: (l, 0, j))])
        return m.reshape(8, 6, d)

    mods = [modulation(l) for l in range(depth)]
    _, h = _resid_norm(x, sample_of_tile, tm, "prenorm0", mod_pre=mods[0], g_pre=g_pre_mix[0],
                       shift_idx=0, scale_idx=1, n_first=n_ctx_tiles)

    new_ckv, new_krope = [], []
    states = None
    for l in range(depth):
        w = w_in[l]
        w_main = jnp.concatenate([w[:, offs[gidx]:offs[gidx + 1]] for gidx in main_groups],
                                 axis=1).astype(BF16)
        w_kr = w[:, offs[2]:offs[3]]
        w_misc = jnp.concatenate([w_kr, w[:, offs[6]:offs[8]],
                                  jnp.zeros((d, LANE - rope - 2 * rank), F32),
                                  _swap_pairs(w_kr, rope), jnp.zeros((d, LANE - rope), F32)],
                                 axis=1).astype(BF16)
        tn_in = _tile(math.gcd(n_main, col_a), 1024)
        proj = _matmul([(h, d, 0)], [(w_main, ())], n_out=n_main, tm=tm, tn=tn_in,
                       out_dtype=F32, epilogue=_first, name=f"w_in{l}",
                       late=(col_a // tn_in, lambda accs, e: jax.nn.sigmoid(accs[0])))
        misc = _matmul([(h, d, 0)], [(w_misc, ())], n_out=2 * LANE, tm=tm, tn=2 * LANE,
                       out_dtype=F32, epilogue=_first, name=f"w_in_misc{l}")

        w_gate = jnp.zeros((LANE, 2 * heads * dk), F32)
        w_gate = w_gate.at[rope:rope + rank, :heads * dk].set(w_gate_f[l])
        w_gate = w_gate.at[rope + rank:rope + 2 * rank, heads * dk:].set(w_gate_b[l])
        b_gate = jnp.concatenate([b_gate_f[l], b_gate_b[l]]).reshape(1, 2 * heads * dk)
        kr_self, la = pl.pallas_call(
            _prep_kernel, grid=(t // tm,),
            in_specs=[pl.BlockSpec((tm, 2 * LANE), lambda i: (i, 0)),
                      pl.BlockSpec((tm, LANE), lambda i: (pos_of_tile(i), 0)),
                      pl.BlockSpec((tm, LANE), lambda i: (pos_of_tile(i), 0)),
                      pl.BlockSpec((LANE, 2 * heads * dk), lambda i: (0, 0)),
                      pl.BlockSpec((1, 2 * heads * dk), lambda i: (0, 0))],
            out_specs=[pl.BlockSpec((tm, LANE), lambda i: (i, 0)),
                       pl.BlockSpec((tm, 2 * heads * dk), lambda i: (i, 0))],
            out_shape=[jax.ShapeDtypeStruct((t, LANE), BF16),
                       jax.ShapeDtypeStruct((t, 2 * heads * dk), F32)],
            compiler_params=_params(1), name=f"prep{l}")(misc, cos_k, sin_k, w_gate.astype(BF16), b_gate)

        wq = w_uq[l].reshape(q_lora, MLA_HEADS, hq)
        wq_r = wq[:, :, QK_NOPE:]
        pad = jnp.zeros((q_lora, MLA_HEADS, LANE - rope), F32)
        as_cols = lambda a: a.reshape(q_lora, -1).astype(BF16)
        wq_n = as_cols(wq[:, :, :QK_NOPE])
        wq_s = as_cols(jnp.concatenate([_swap_pairs(wq_r, rope), pad], axis=2))
        wq_r = as_cols(jnp.concatenate([wq_r, pad], axis=2))
        q_in = [(proj, q_lora, col[0] // q_lora)]
        q_norm = lambda k, a, e: _rms(a, e[0][...])
        g_q_extra = (g_q[l].reshape(1, q_lora), (1, q_lora), lambda j, i: (0, 0))
        tn_q = _tile(MLA_HEADS * LANE, 1024)

        def rope_epi(accs, e_refs, tn_q=tn_q):
            cos, sin = e_refs[1][...], e_refs[2][...]
            return jnp.concatenate([accs[0][:, s:s + LANE] * cos + accs[1][:, s:s + LANE] * sin
                                    for s in range(0, tn_q, LANE)], axis=1)

        q_nope = _matmul(q_in, [(wq_n, ())], n_out=MLA_HEADS * QK_NOPE, tm=tm_s,
                         tn=_tile(MLA_HEADS * QK_NOPE, 1024), out_dtype=BF16, epilogue=_first,
                         name=f"w_uq_nope{l}", prologue=q_norm, extras=[g_q_extra])
        q_rope = _matmul(q_in, [(wq_r, ()), (wq_s, ())], a_of_w=[0, 0], n_out=MLA_HEADS * LANE,
                         tm=tm_s, tn=tn_q, out_dtype=BF16, epilogue=rope_epi,
                         name=f"w_uq_rope{l}", prologue=q_norm,
                         extras=[g_q_extra,
                                 (cos_k, (tm_s, LANE), lambda j, i: (pos_of_tile(i, tm_s), 0)),
                                 (sin_k, (tm_s, LANE), lambda j, i: (pos_of_tile(i, tm_s), 0))])

        c_kv = _norm_cols(proj, kv_lora, col[1] // kv_lora, g_kv[l], tm, f"ckv_norm{l}")
        tn_kv = _tile(MLA_HEADS * hw, 1024)
        kv_self = _matmul([(c_kv, kv_lora, 0)], [(w_ukv, (l,))], n_out=MLA_HEADS * hw, tm=tm_s,
                          tn=tn_kv, out_dtype=BF16, epilogue=_first, name=f"w_ukv{l}")
        ckv_cache = cache_ckv[:, l].reshape(n_lat_b * past, kv_lora)
        kv_cache = _matmul([(ckv_cache, kv_lora, 0)], [(w_ukv, (l,))], n_out=MLA_HEADS * hw,
                           tm=_tile(n_lat_b * past, 512), tn=tn_kv, out_dtype=BF16,
                           epilogue=_first, name=f"w_ukv_cache{l}")
        kr_cache = jnp.pad(cache_krope[:, l].reshape(n_lat_b * past, rope),
                           ((0, 0), (0, LANE - rope))).astype(BF16)

        scale = hq ** -0.5
        attn = _attention(q_nope, q_rope, [(kv_self, t_ctx, 0)], [(kr_self, t_ctx, 0)],
                          n_batch=n_ctx_b, t_q=t_ctx, q_row0=0, tq=_tile(t_ctx, 256),
                          n_heads=math.gcd(MLA_HEADS, 8), group=4, scale=scale, name=f"attn_ctx{l}",
                          o_prev=jnp.zeros((t, MLA_HEADS * V_HEAD), BF16))
        attn = _attention(q_nope, q_rope, [(kv_self, t_lat, n_ctx), (kv_cache, past, 0)],
                          [(kr_self, t_lat, n_ctx), (kr_cache, past, 0)], n_batch=n_lat_b,
                          t_q=t_lat, q_row0=n_ctx, tq=_tile(t_lat, 512),
                          n_heads=math.gcd(MLA_HEADS, 4), group=2, scale=scale, name=f"attn_lat{l}",
                          o_prev=attn)

        gla_kw = dict(heads=heads, dk=dk, dv=dv, col_q=col_q, col_k=col_k, col_v=col_v,
                      col_g=col_g, layer=l)
        gla, *states = _gla(proj, la, g_gla[l], n_batch=n_ctx_b, t_b=t_ctx, row0=0,
                            name=f"gla_ctx{l}", o_prev=jnp.zeros((t, heads * dv), BF16),
                            st_prev=states, st_depth=depth, **gla_kw)
        gla, = _gla(proj, la, g_gla[l], n_batch=n_lat_b, t_b=t_lat, row0=n_ctx,
                    s0=(state_gla_fwd, state_gla_bwd), name=f"gla_lat{l}", o_prev=gla, **gla_kw)

        new_ckv.append(c_kv[:n_ctx].reshape(n_ctx_b, t_ctx, kv_lora))
        new_krope.append(misc[:n_ctx, :rope].reshape(n_ctx_b, t_ctx, rope))

        tn_m = _tile(math.gcd(col_a, col_b, d), 512)

        def merge_epi(accs, e_refs):
            return e_refs[0][...] * accs[0] + e_refs[1][...] * accs[1]

        merged = _matmul([(attn, MLA_HEADS * V_HEAD, 0), (gla, heads * dv, 0)],
                         [(w_pa, (l,)), (w_pb, (l,))], n_out=d, tm=tm, tn=tn_m, out_dtype=BF16,
                         epilogue=merge_epi, name=f"merge{l}",
                         extras=[(proj, (tm, tn_m), lambda j, i: (i, col_a // tn_m + j)),
                                 (proj, (tm, tn_m), lambda j, i: (i, col_b // tn_m + j))])
        y = _matmul([(merged, d, 0)], [(w_o, (l,))], n_out=d, tm=tm, tn=_tile(d, 1024),
                    out_dtype=F32, epilogue=_first, name=f"w_o{l}")
        moe = l % 2 == 1
        x, h = _resid_norm(x, sample_of_tile, tm, f"mix_resid{l}", y=y, mod_res=mods[l],
                           g_post=g_post_mix[l], gate_idx=2, mod_pre=mods[l], g_pre=g_pre_ffn[l],
                           shift_idx=3, scale_idx=4, n_first=n_ctx_tiles)

        jx = l // 2

        def swiglu_epi(accs, e_refs):
            return _silu(accs[0]) * accs[1]

        ffn_out = {}
        if not moe:
            d_ff = w_ff_gate.shape[2]
            ff = _matmul([(h, d, 0)], [(w_ff_gate, (jx,)), (w_ff_up, (jx,))], a_of_w=[0, 0],
                         n_out=d_ff, tm=tm, tn=_tile(d_ff, 512), out_dtype=BF16,
                         epilogue=swiglu_epi, name=f"ffn_up{l}")
            ffn_out["y"] = _matmul([(ff, d_ff, 0)], [(w_ff_down, (jx,))], n_out=d, tm=tm,
                                   tn=_tile(d, 512), out_dtype=F32, epilogue=_first,
                                   name=f"ffn_down{l}")
        else:
            d_ex = w_ex_gate.shape[3]
            w_r = jnp.pad(w_router[jx], ((0, 0), (0, LANE - n_experts))).astype(BF16)
            route = pl.pallas_call(
                functools.partial(_router_kernel, n_experts=n_experts), grid=(t // tm,),
                in_specs=[pl.BlockSpec((tm, d), lambda i: (i, 0)),
                          pl.BlockSpec((d, LANE), lambda i: (0, 0))],
                out_specs=pl.BlockSpec((tm, LANE), lambda i: (i, 0)),
                out_shape=jax.ShapeDtypeStruct((t, LANE), F32),
                compiler_params=_params(1), name=f"router{l}")(h, w_r)
            tile_e = _tile(t * TOP_K, EXPERT_TILE, 8)
            tok_of, dest, tile_expert = _routing_tables(route, n_experts, tile_e)
            xs = _gather(h.reshape(t, d // LANE, LANE), tok_of, tile_e, f"moe_gather{l}")
            xs = xs.reshape(xs.shape[0], d)
            ff = _matmul([(xs, d, 0)], [(w_ex_gate, (jx, 0)), (w_ex_up, (jx, 0))], a_of_w=[0, 0],
                         n_out=d_ex, tm=tile_e, tn=_tile(d_ex, 256), out_dtype=BF16,
                         epilogue=swiglu_epi, name=f"moe_up{l}", tile_expert=tile_expert)
            ys = _matmul([(ff, d_ex, 0)], [(w_ex_down, (jx, 0))], n_out=d, tm=tile_e,
                         tn=_tile(d, 1024), out_dtype=F32, epilogue=_first, name=f"moe_down{l}",
                         tile_expert=tile_expert)
            ffn_out["routed"] = (ys, dest, route)

        if l + 1 < depth:
            x, h = _resid_norm(x, sample_of_tile, tm, f"ffn_resid{l}", mod_res=mods[l],
                               g_post=g_post_ffn[l], gate_idx=5, mod_pre=mods[l + 1],
                               g_pre=g_pre_mix[l + 1], shift_idx=0, scale_idx=1, **ffn_out)
        else:
            x, _ = _resid_norm(x, sample_of_tile, tm, f"ffn_resid{l}", mod_res=mods[l],
                               g_post=g_post_ffn[l], gate_idx=5, split_out=True,
                               n_first=n_ctx_tiles, **ffn_out)

    return (x[0].reshape(n_ctx_b, t_ctx, d), x[1].reshape(n_lat_b, t_lat, d),
            jnp.stack(new_ckv, axis=1), jnp.stack(new_krope, axis=1),
            states[0], states[1])
```

```python
import functools
import math

import jax
import jax.numpy as jnp
from jax import lax
from jax.experimental import pallas as pl
from jax.experimental.pallas import tpu as pltpu

MLA_HEADS = 16
QK_NOPE = 128
V_HEAD = 128
GRID_W = 64
ROPE_THETA = 10000.0
GATE_NORM = 16.0
GLA_CHUNK = 64
TOP_K = 2
EPS = 1e-6
LANE = 128
VMEM_LIMIT = 52 * 1024 * 1024
EXPERT_TILE = 512
EXPERT_RUN_TILES = 4

BF16 = jnp.bfloat16
F32 = jnp.float32


def _params(n_grid, **kw):
    return pltpu.CompilerParams(
        dimension_semantics=("arbitrary",) * n_grid, vmem_limit_bytes=VMEM_LIMIT, **kw)


def _tile(n, target, quantum=LANE):
    if n <= target:
        return n
    t = (target // quantum) * quantum
    while t >= quantum:
        if n % t == 0:
            return t
        t -= quantum
    return n


def _rms(x, g):
    return x * lax.rsqrt(jnp.mean(x * x, axis=-1, keepdims=True) + EPS) * g


def _silu(x):
    return x * jax.nn.sigmoid(x)


def _dot(a, b):
    return jnp.dot(a, b, preferred_element_type=F32)


def _dot_nt(a, b):
    return lax.dot_general(a, b, (((1,), (1,)), ((), ())), preferred_element_type=F32)


def _dot_tn(a, b):
    return lax.dot_general(a, b, (((0,), (0,)), ((), ())), preferred_element_type=F32)


def _mm_kernel(*refs, n_a, n_w, a_of_w, n_e, cast_w, grouped, prologue, epilogue):
    if grouped:
        te_ref, refs = refs[0], refs[1:]
    a_refs = refs[:n_a]
    w_refs = refs[n_a:n_a + n_w]
    e_refs = refs[n_a + n_w:n_a + n_w + n_e]
    o_ref = refs[n_a + n_w + n_e]
    wb_refs = refs[n_a + n_w + n_e + 1:]
    i = pl.program_id(1)

    def compute():
        if cast_w:
            fresh = i == 0
            if grouped:
                fresh = fresh | (te_ref[i] != te_ref[jnp.maximum(i - 1, 0)])

            @pl.when(fresh)
            def _():
                for w_ref, wb_ref in zip(w_refs, wb_refs):
                    wb_ref[...] = w_ref[...].astype(BF16)
            w_use = wb_refs
        else:
            w_use = w_refs
        e = [e_ref[...] for e_ref in e_refs]
        a_vals = []
        for k, a_ref in enumerate(a_refs):
            a = a_ref[...]
            if prologue is not None:
                a = prologue(k, a, e)
            a_vals.append(a.astype(BF16))
        accs = [_dot(a_vals[a_of_w[k]], w_ref[...]) for k, w_ref in enumerate(w_use)]
        o_ref[...] = epilogue(accs, e).astype(o_ref.dtype)

    if grouped:
        has_rows = te_ref[pl.num_programs(1) + i] != 0
        pl.when(has_rows)(compute)

        @pl.when(jnp.logical_not(has_rows))
        def _():
            o_ref[...] = jnp.zeros_like(o_ref)
    else:
        compute()


def _matmul(a_list, w_list, *, n_out, tm, tn, out_dtype, epilogue, name, a_of_w=None,
            prologue=None, extras=(), tile_expert=None):
    m = a_list[0][0].shape[0]
    a_of_w = a_of_w or list(range(len(w_list)))
    grouped = tile_expert is not None
    assert m % tm == 0 and n_out % tn == 0
    in_specs, args, scratch = [], [], []
    for arr, k, cb in a_list:
        in_specs.append(pl.BlockSpec((tm, k), lambda j, i, *_, cb=cb: (i, cb)))
        args.append(arr)
    cast_w = w_list[0][0].dtype != BF16
    for arr, lead in w_list:
        k = arr.shape[-2]
        if grouped:
            im = lambda j, i, te, lead=lead: tuple(lead[:-1]) + (te[i], 0, j)
        else:
            im = lambda j, i, lead=lead: tuple(lead) + (0, j)
        in_specs.append(pl.BlockSpec((None,) * len(lead) + (k, tn), im))
        args.append(arr)
        if cast_w:
            scratch.append(pltpu.VMEM((k, tn), BF16))
    for arr, bs, im in extras:
        in_specs.append(pl.BlockSpec(bs, lambda j, i, *_, im=im: im(j, i)))
        args.append(arr)
    kern = functools.partial(_mm_kernel, n_a=len(a_list), n_w=len(w_list), a_of_w=a_of_w,
                             n_e=len(extras), cast_w=cast_w, grouped=grouped,
                             prologue=prologue, epilogue=epilogue)
    grid_spec = pltpu.PrefetchScalarGridSpec(
        num_scalar_prefetch=1 if grouped else 0,
        grid=(n_out // tn, m // tm),
        in_specs=in_specs,
        out_specs=pl.BlockSpec((tm, tn), lambda j, i, *_: (i, j)),
        scratch_shapes=scratch)
    if grouped:
        args = [tile_expert] + args
    return pl.pallas_call(
        kern, grid_spec=grid_spec,
        out_shape=jax.ShapeDtypeStruct((m, n_out), out_dtype),
        compiler_params=_params(2), name=name,
    )(*args)


def _first(accs, e):
    return accs[0]


def _row_copy(src, row, dst, r, sem):
    return pltpu.make_async_copy(src.at[pl.ds(row, 1)], dst.at[pl.ds(r, 1)], sem)


def _gather_rows(idx_ref, base, stride, n, src, dst, dst_base, sem):
    def issue(r, carry):
        _row_copy(src, idx_ref[base + r * stride], dst, dst_base + r, sem).start()
        return carry

    def wait(r, carry):
        _row_copy(src, 0, dst, dst_base + r, sem).wait()
        return carry

    lax.fori_loop(0, n, issue, 0, unroll=8)
    lax.fori_loop(0, n, wait, 0, unroll=8)


def _gather_kernel(idx_ref, flag_ref, src_hbm, o_ref, sem):
    tg = o_ref.shape[0]
    i = pl.program_id(0)

    @pl.when(flag_ref[i] != 0)
    def _():
        _gather_rows(idx_ref, i * tg, 1, tg, src_hbm, o_ref, 0, sem)

    @pl.when(flag_ref[i] == 0)
    def _():
        o_ref[...] = jnp.zeros_like(o_ref)


def _gather(src, idx, flags, tg, name):
    n = idx.shape[0]
    blk = (tg,) + src.shape[1:]
    grid_spec = pltpu.PrefetchScalarGridSpec(
        num_scalar_prefetch=2, grid=(n // tg,),
        in_specs=[pl.BlockSpec(memory_space=pl.ANY)],
        out_specs=pl.BlockSpec(blk, lambda i, *_: (i,) + (0,) * (len(blk) - 1)),
        scratch_shapes=[pltpu.SemaphoreType.DMA(())])
    return pl.pallas_call(
        _gather_kernel, grid_spec=grid_spec,
        out_shape=jax.ShapeDtypeStruct((n,) + src.shape[1:], src.dtype),
        compiler_params=_params(1), name=name)(idx, flags, src)


def _resid_norm_kernel(*refs, mode, want_x, want_h, gate_idx, shift_idx, scale_idx, split_in,
                       split_out, n_first):
    it = iter(refs)
    dest_ref = next(it) if mode == "routed" else None
    x_refs = [next(it) for _ in range(2 if split_in else 1)]
    if mode == "dense":
        y_ref = next(it)
    elif mode == "routed":
        ys_hbm, route_ref = next(it), next(it)
    if mode is not None:
        mod_res_ref, g_post_ref = next(it), next(it)
    if want_h:
        mod_pre_ref, g_pre_ref = next(it), next(it)
    xo_refs = [next(it) for _ in range((2 if split_out else 1) if want_x else 0)]
    h_ref = next(it) if want_h else None
    i = pl.program_id(0)
    x = x_refs[0][...]
    if split_in:
        x = jnp.where(i < n_first, x, x_refs[1][...])
    if mode is not None:
        if mode == "routed":
            buf1, buf2, sem = next(it), next(it), next(it)
            tm = x.shape[0]
            base = i * tm * TOP_K
            _gather_rows(dest_ref, base, TOP_K, tm, ys_hbm, buf1, 0, sem)
            _gather_rows(dest_ref, base + 1, TOP_K, tm, ys_hbm, buf2, 0, sem)
            route = route_ref[...]
            y = route[:, 2:3] * buf1[...] + route[:, 3:4] * buf2[...]
        else:
            y = y_ref[...]
        gate = mod_res_ref[0, gate_idx:gate_idx + 1, :]
        x = x + gate * _rms(y, g_post_ref[...])
    if want_x and split_out:
        @pl.when(i < n_first)
        def _():
            xo_refs[0][...] = x

        @pl.when(i >= n_first)
        def _():
            xo_refs[1][...] = x
    elif want_x:
        xo_refs[0][...] = x
    if want_h:
        scale = mod_pre_ref[0, scale_idx:scale_idx + 1, :]
        shift = mod_pre_ref[0, shift_idx:shift_idx + 1, :]
        h_ref[...] = (_rms(x, g_pre_ref[...]) * (1.0 + scale) + shift).astype(h_ref.dtype)


def _resid_norm(x, sample_of_tile, tm, name, *, y=None, routed=None, mod_res=None, g_post=None,
                gate_idx=0, mod_pre=None, g_pre=None, shift_idx=0, scale_idx=0, split_out=False,
                n_first=0):
    split_in = isinstance(x, (tuple, list))
    xs_in = list(x) if split_in else [x]
    t, d = sum(a.shape[0] for a in xs_in), xs_in[0].shape[1]
    mode = "dense" if y is not None else ("routed" if routed is not None else None)
    want_h = mod_pre is not None
    want_x = mode is not None
    row = pl.BlockSpec((tm, d), lambda i, *_: (i, 0))
    first = pl.BlockSpec((tm, d), lambda i, *_: (jnp.minimum(i, n_first - 1), 0))
    rest = pl.BlockSpec((tm, d), lambda i, *_: (jnp.maximum(i - n_first, 0), 0))
    vec = pl.BlockSpec((1, d), lambda i, *_: (0, 0))
    mod = pl.BlockSpec((1, 6, d), lambda i, *_: (sample_of_tile(i), 0, 0))
    in_specs, args = ([first, rest] if split_in else [row]), xs_in
    out_specs, out_shape, scratch = [], [], []
    if mode == "dense":
        in_specs.append(row)
        args.append(y)
    elif mode == "routed":
        ys, dest, route = routed
        in_specs += [pl.BlockSpec(memory_space=pl.ANY),
                     pl.BlockSpec((tm, LANE), lambda i, *_: (i, 0))]
        args += [ys, route]
        scratch = [pltpu.VMEM((tm, d), F32), pltpu.VMEM((tm, d), F32),
                   pltpu.SemaphoreType.DMA(())]
    if mode is not None:
        in_specs += [mod, vec]
        args += [mod_res, g_post.reshape(1, d)]
    if want_h:
        in_specs += [mod, vec]
        args += [mod_pre, g_pre.reshape(1, d)]
    if want_x and split_out:
        out_specs += [first, rest]
        out_shape += [jax.ShapeDtypeStruct((n_first * tm, d), F32),
                      jax.ShapeDtypeStruct((t - n_first * tm, d), F32)]
    elif want_x:
        out_specs.append(row)
        out_shape.append(jax.ShapeDtypeStruct((t, d), F32))
    if want_h:
        out_specs.append(row)
        out_shape.append(jax.ShapeDtypeStruct((t, d), BF16))
    kern = functools.partial(_resid_norm_kernel, mode=mode, want_x=want_x, want_h=want_h,
                             gate_idx=gate_idx, shift_idx=shift_idx, scale_idx=scale_idx,
                             split_in=split_in, split_out=split_out, n_first=n_first)
    grid_spec = pltpu.PrefetchScalarGridSpec(
        num_scalar_prefetch=1 if mode == "routed" else 0, grid=(t // tm,),
        in_specs=in_specs, out_specs=out_specs, scratch_shapes=scratch)
    if mode == "routed":
        args = [dest] + args
    outs = pl.pallas_call(kern, grid_spec=grid_spec, out_shape=out_shape,
                          compiler_params=_params(1), name=name)(*args)
    n_x = (2 if split_out else 1) if want_x else 0
    x_new = (tuple(outs[:2]) if split_out else outs[0]) if want_x else x
    h = outs[n_x] if want_h else None
    return x_new, h


def _norm_kernel(x_ref, g_ref, o_ref):
    o_ref[...] = _rms(x_ref[...], g_ref[...])


def _norm_cols(arr, width, col_block, g, tm, name):
    t = arr.shape[0]
    return pl.pallas_call(
        _norm_kernel, grid=(t // tm,),
        in_specs=[pl.BlockSpec((tm, width), lambda i: (i, col_block)),
                  pl.BlockSpec((1, width), lambda i: (0, 0))],
        out_specs=pl.BlockSpec((tm, width), lambda i: (i, 0)),
        out_shape=jax.ShapeDtypeStruct((t, width), F32),
        compiler_params=_params(1), name=name)(arr, g.reshape(1, width))


def _swap_pairs(w, rope):
    q = rope // 4
    return jnp.concatenate([w[..., q:2 * q], w[..., :q], w[..., 3 * q:], w[..., 2 * q:3 * q]],
                           axis=-1)


def _prep_kernel(misc_ref, cos_ref, sin_ref, wg_ref, bg_ref, kr_ref, la_ref):
    misc = misc_ref[:, :LANE]
    kr = misc * cos_ref[...] + misc_ref[:, LANE:] * sin_ref[...]
    kr_ref[...] = kr.astype(kr_ref.dtype)
    z = _dot(misc.astype(BF16), wg_ref[...]) + bg_ref[...]
    log_sig = jnp.minimum(z, 0.0) - jnp.log1p(jnp.exp(-jnp.abs(z)))
    la_ref[...] = log_sig * (1.0 / GATE_NORM)


def _attn_kernel(*refs, n_seg, n_heads, group, scale):
    qn_ref, qr_ref = refs[:2]
    kv_refs = refs[2:2 + n_seg]
    kr_refs = refs[2 + n_seg:2 + 2 * n_seg]
    o_ref = refs[-1]
    hw = QK_NOPE + V_HEAD
    krs = [kr_ref[...] for kr_ref in kr_refs]
    add = lambda a, b: a + b
    c = scale * math.log2(math.e)
    for h0 in range(0, n_heads, group):
        hds = range(h0, min(h0 + group, n_heads))
        scores = []
        for hd in hds:
            qn = qn_ref[:, hd * QK_NOPE:(hd + 1) * QK_NOPE]
            qr = qr_ref[:, hd * LANE:(hd + 1) * LANE]
            scores.append([_dot_nt(qn, kv_ref[:, hd * hw:hd * hw + QK_NOPE]) + _dot_nt(qr, kr)
                           for kv_ref, kr in zip(kv_refs, krs)])
        ms = [functools.reduce(jnp.maximum, [jnp.max(s, axis=-1, keepdims=True) for s in ss])
              for ss in scores]
        ps = [[jnp.exp2((s - m) * c) for s in ss] for ss, m in zip(scores, ms)]
        invs = [1.0 / functools.reduce(add, [jnp.sum(p, axis=-1, keepdims=True) for p in pp])
                for pp in ps]
        for hd, pp, inv in zip(hds, ps, invs):
            o = functools.reduce(add, [
                _dot((p * inv).astype(BF16), kv_ref[:, hd * hw + QK_NOPE:(hd + 1) * hw])
                for p, kv_ref in zip(pp, kv_refs)])
            o_ref[:, hd * V_HEAD:(hd + 1) * V_HEAD] = o.astype(o_ref.dtype)


def _attention(q_nope, q_rope, kvs, krs, *, n_batch, t_q, q_row0, tq, n_heads, group, scale, name,
               o_prev):
    hw = QK_NOPE + V_HEAD
    nq = t_q // tq
    q_map = lambda b, g, i: (q_row0 // tq + b * nq + i, g)
    in_specs = [pl.BlockSpec((tq, n_heads * QK_NOPE), q_map),
                pl.BlockSpec((tq, n_heads * LANE), q_map)]
    args = [q_nope, q_rope]
    for arr, tk, row0 in kvs:
        in_specs.append(pl.BlockSpec((tk, n_heads * hw),
                                     lambda b, g, i, tk=tk, row0=row0: (row0 // tk + b, g)))
        args.append(arr)
    for arr, tk, row0 in krs:
        in_specs.append(pl.BlockSpec((tk, LANE),
                                     lambda b, g, i, tk=tk, row0=row0: (row0 // tk + b, 0)))
        args.append(arr)
    aliases = {len(args): 0}
    in_specs.append(pl.BlockSpec(memory_space=pl.ANY))
    args.append(o_prev)
    kern = functools.partial(_attn_kernel, n_seg=len(kvs), n_heads=n_heads, group=group,
                             scale=scale)
    return pl.pallas_call(
        kern, grid=(n_batch, MLA_HEADS // n_heads, nq), in_specs=in_specs,
        out_specs=pl.BlockSpec((tq, n_heads * V_HEAD), q_map),
        out_shape=jax.ShapeDtypeStruct(o_prev.shape, o_prev.dtype),
        input_output_aliases=aliases,
        compiler_params=_params(3), name=name)(*args)


def _split3(x):
    hi = x.astype(BF16)
    r1 = x - hi.astype(F32)
    mid = r1.astype(BF16)
    lo = (r1 - mid.astype(F32)).astype(BF16)
    return hi, mid, lo


def _gla_kernel(*refs, has_init, want_states, n_alias, q_scale, layer, all_layers):
    it = iter(refs)
    q_ref, k_ref, v_ref, laf_ref, lab_ref, gout_ref, ggla_ref = (next(it) for _ in range(7))
    s0_refs = (next(it), next(it)) if has_init else None
    for _ in range(n_alias):
        next(it)
    o_ref = next(it)
    s_out_refs = (next(it), next(it)) if want_states else None
    st_refs = (next(it), next(it))
    vb, qd_s, ks_s, dec_s, oacc = (next(it) for _ in range(5))
    c = GLA_CHUNK
    t_b, dk = q_ref.shape
    n_chunks = t_b // c
    rb = min(t_b, 4 * c)
    row = lax.broadcasted_iota(jnp.int32, (rb, rb), 0)
    col = lax.broadcasted_iota(jnp.int32, (rb, rb), 1)
    same = (row // c) == (col // c)
    masks = (same & (col <= row), same & (col >= row))
    ones_bd = jnp.where(same, 1.0, 0.0).astype(BF16)
    la_refs = (laf_ref, lab_ref)

    vb[...] = v_ref[...].astype(BF16)
    for d in range(2):
        if has_init:
            st_refs[d][...] = s0_refs[d][0, 0].T
        else:
            st_refs[d][...] = jnp.zeros_like(st_refs[d])

    for d in range(2):
        tri = jnp.where(masks[d], 1.0, 0.0).astype(BF16)
        for r0 in range(0, t_b, rb):
            rows = pl.ds(r0, rb)
            hi, mid, lo = _split3(la_refs[d][rows, :])
            b = (_dot(tri, hi) + _dot(tri, mid)) + _dot(tri, lo)
            b_end = (_dot(ones_bd, hi) + _dot(ones_bd, mid)) + _dot(ones_bd, lo)
            kc = k_ref[rows, :]
            qd = (q_ref[rows, :] * q_scale * jnp.exp(b)).astype(BF16)
            kd = (kc * jnp.exp(-b)).astype(BF16)
            qd_s[d, rows, :] = qd
            ks_s[d, rows, :] = (kc * jnp.exp(b_end - b)).astype(BF16)
            dec_s[d, rows, :] = jnp.exp(b_end)
            a = jnp.where(masks[d], _dot_nt(qd, kd), 0.0).astype(BF16)
            o_in = _dot(a, vb[rows, :])
            if d == 0:
                oacc[rows, :] = o_in
            else:
                oacc[rows, :] += o_in

    for i in range(n_chunks):
        for d in range(2):
            c0 = (i if d == 0 else n_chunks - 1 - i) * c
            rows = pl.ds(c0, c)
            st = st_refs[d]
            u = _dot_tn(vb[rows, :], ks_s[d, rows, :])
            s_t = st[...]
            oacc[rows, :] += _dot_nt(qd_s[d, rows, :], s_t.astype(BF16))
            st[...] = s_t * dec_s[d, pl.ds(c0, 1), :] + u

    o_ref[...] = (_rms(oacc[...], ggla_ref[...]) * _silu(gout_ref[...])).astype(o_ref.dtype)
    if want_states:
        for d in range(2):
            s_fin = st_refs[d][...].T
            if all_layers:
                for ll in range(s_out_refs[d].shape[1]):
                    s_out_refs[d][0, ll, 0] = s_fin if ll == layer else jnp.zeros_like(s_fin)
            else:
                s_out_refs[d][0, 0] = s_fin


def _gla(proj, la, g_gla, *, n_batch, t_b, row0, heads, dk, dv, col_q, col_k, col_v, col_g,
         name, o_prev, s0=None, layer=0, st_prev=None, st_depth=0):
    has_init = s0 is not None
    rb = row0 // t_b
    in_specs = [
        pl.BlockSpec((t_b, dk), lambda b, h: (rb + b, col_q // dk + h)),
        pl.BlockSpec((t_b, dk), lambda b, h: (rb + b, col_k // dk + h)),
        pl.BlockSpec((t_b, dv), lambda b, h: (rb + b, col_v // dv + h)),
        pl.BlockSpec((t_b, dk), lambda b, h: (rb + b, h)),
        pl.BlockSpec((t_b, dk), lambda b, h: (rb + b, heads + h)),
        pl.BlockSpec((t_b, dv), lambda b, h: (rb + b, col_g // dv + h)),
        pl.BlockSpec((1, dv), lambda b, h: (0, 0)),
    ]
    args = [proj, proj, proj, la, la, proj, g_gla.reshape(1, dv)]
    st_spec = pl.BlockSpec((1, None, 1, dk, dv), lambda b, h: (b, layer, h, 0, 0))
    if has_init:
        in_specs += [st_spec, st_spec]
        args += list(s0)
    out_specs = [pl.BlockSpec((t_b, dv), lambda b, h: (rb + b, h))]
    out_shape = [jax.ShapeDtypeStruct(o_prev.shape, o_prev.dtype)]
    aliases = {len(args): 0}
    in_specs.append(pl.BlockSpec(memory_space=pl.ANY))
    args.append(o_prev)
    want_states = st_prev is not None or st_depth > 0
    if st_prev is not None:
        out_specs += [st_spec, st_spec]
        for k, arr in enumerate(st_prev):
            out_shape.append(jax.ShapeDtypeStruct(arr.shape, arr.dtype))
            aliases[len(args)] = 1 + k
            in_specs.append(pl.BlockSpec(memory_space=pl.ANY))
            args.append(arr)
    elif want_states:
        out_specs += [pl.BlockSpec((1, st_depth, 1, dk, dv), lambda b, h: (b, 0, h, 0, 0))] * 2
        out_shape += [jax.ShapeDtypeStruct((n_batch, st_depth, heads, dk, dv), F32)] * 2
    kern = functools.partial(_gla_kernel, has_init=has_init, want_states=want_states,
                             n_alias=len(aliases), q_scale=dk ** -0.5, layer=layer,
                             all_layers=st_prev is None)
    return pl.pallas_call(
        kern, grid=(n_batch, heads), in_specs=in_specs, out_specs=out_specs, out_shape=out_shape,
        input_output_aliases=aliases,
        scratch_shapes=[pltpu.VMEM((dv, dk), F32), pltpu.VMEM((dv, dk), F32),
                        pltpu.VMEM((t_b, dv), BF16), pltpu.VMEM((2, t_b, dk), BF16),
                        pltpu.VMEM((2, t_b, dk), BF16), pltpu.VMEM((2, t_b, dk), F32),
                        pltpu.VMEM((t_b, dv), F32)],
        compiler_params=_params(2), name=name)(*args)


def _router_kernel(h_ref, w_ref, o_ref, *, n_experts):
    logits = _dot(h_ref[...].astype(BF16), w_ref[...])
    lane = lax.broadcasted_iota(jnp.int32, logits.shape, 1).astype(F32)
    neg = jnp.float32(-jnp.inf)
    lg = jnp.where(lane < n_experts, logits, neg)
    m1 = jnp.max(lg, axis=-1, keepdims=True)
    i1 = jnp.min(jnp.where(lg == m1, lane, LANE), axis=-1, keepdims=True)
    lg2 = jnp.where(lane == i1, neg, lg)
    m2 = jnp.max(lg2, axis=-1, keepdims=True)
    i2 = jnp.min(jnp.where(lg2 == m2, lane, LANE), axis=-1, keepdims=True)
    e2 = jnp.exp(m2 - m1)
    inv = 1.0 / (1.0 + e2)
    o_ref[...] = (jnp.where(lane == 0, i1, 0.0) + jnp.where(lane == 1, i2, 0.0)
                  + jnp.where(lane == 2, inv, 0.0) + jnp.where(lane == 3, e2 * inv, 0.0))


def _moe_up_kernel(info_ref, x_ref, wg_ref, wu_ref, o_ref, *, n_sub):
    s = pl.program_id(0)
    flag0 = pl.num_programs(0) + s * n_sub
    sub = x_ref.shape[0] // n_sub

    @pl.when(info_ref[flag0] != 0)
    def _():
        wg = wg_ref[...].astype(BF16)
        wu = wu_ref[...].astype(BF16)
        for k in range(n_sub):
            rows = pl.ds(k * sub, sub)

            @pl.when(info_ref[flag0 + k] != 0)
            def _():
                a = x_ref[rows, :]
                o_ref[rows, :] = (_silu(_dot(a, wg)) * _dot(a, wu)).astype(o_ref.dtype)

            @pl.when(info_ref[flag0 + k] == 0)
            def _():
                o_ref[rows, :] = jnp.zeros((sub, o_ref.shape[1]), o_ref.dtype)

    @pl.when(info_ref[flag0] == 0)
    def _():
        o_ref[...] = jnp.zeros_like(o_ref)


def _moe_up(xs, w_gate, w_up, lead, info, *, run, sub, tn, name):
    n_rows, d = xs.shape
    n_out = w_gate.shape[-1]
    n_run = n_rows // run
    n_sub = run // sub

    def w_map(s, j, info):
        return tuple(lead) + (info[s], 0, jnp.where(info[n_run + s * n_sub] != 0, j, 0))

    w_spec = pl.BlockSpec((None,) * (len(lead) + 1) + (d, tn), w_map)
    grid_spec = pltpu.PrefetchScalarGridSpec(
        num_scalar_prefetch=1, grid=(n_run, n_out // tn),
        in_specs=[pl.BlockSpec((run, d), lambda s, j, info: (s, 0)), w_spec, w_spec],
        out_specs=pl.BlockSpec((run, tn), lambda s, j, info: (s, j)))
    return pl.pallas_call(
        functools.partial(_moe_up_kernel, n_sub=n_sub), grid_spec=grid_spec,
        out_shape=jax.ShapeDtypeStruct((n_rows, n_out), BF16),
        compiler_params=_params(2), name=name)(info, xs, w_gate, w_up)


def _routing_tables(route, n_experts, tile, run):
    t = route.shape[0]
    n_assign = t * TOP_K
    n_rows = -(-(n_assign + n_experts * (run - 1)) // run) * run
    e_flat = route[:, :TOP_K].astype(jnp.int32).reshape(n_assign)
    onehot = (e_flat[:, None] == jnp.arange(n_experts, dtype=jnp.int32)[None, :]).astype(jnp.int32)
    csum = jnp.cumsum(onehot, axis=0)
    rank = jnp.take_along_axis(csum, e_flat[:, None], axis=1)[:, 0] - 1
    counts = csum[-1]
    padded = ((counts + run - 1) // run) * run
    ends = jnp.cumsum(padded)
    starts = ends - padded
    dest = starts[e_flat] + rank
    tok_of = jnp.zeros((n_rows,), jnp.int32).at[dest].set(
        jnp.arange(n_assign, dtype=jnp.int32) // TOP_K, unique_indices=True)
    tile_start = jnp.arange(n_rows // tile, dtype=jnp.int32) * tile
    tile_expert = jnp.minimum(
        jnp.sum(tile_start[:, None] >= ends[None, :], axis=1, dtype=jnp.int32), n_experts - 1)
    flags = (tile_start < (starts + counts)[tile_expert]).astype(jnp.int32)
    run_info = jnp.concatenate([tile_expert[::run // tile], flags])
    tile_info = jnp.concatenate([tile_expert, flags])
    return tok_of, dest.astype(jnp.int32), flags, run_info, tile_info


def _rope_tables(n_lat, rope, tm_id):
    axis_half = rope // 4
    rows = n_lat // GRID_W
    r = jnp.repeat(jnp.arange(rows, dtype=F32), GRID_W)
    c = jnp.tile(jnp.arange(GRID_W, dtype=F32), rows)
    inv = ROPE_THETA ** (-jnp.arange(axis_half, dtype=F32) / axis_half)
    ar, ac = r[:, None] * inv, c[:, None] * inv
    cos = jnp.concatenate([jnp.cos(ar), jnp.cos(ar), jnp.cos(ac), jnp.cos(ac)], axis=1)
    sin = jnp.concatenate([-jnp.sin(ar), jnp.sin(ar), -jnp.sin(ac), jnp.sin(ac)], axis=1)
    cos = jnp.concatenate([jnp.ones((tm_id, rope), F32), cos], axis=0)
    sin = jnp.concatenate([jnp.zeros((tm_id, rope), F32), sin], axis=0)
    return cos, sin


def kernel(x_prompt, x_sample, cache_ckv, cache_krope, state_gla_fwd, state_gla_bwd, c, c_ctx, w_ada, b_ada, g_pre_mix, g_post_mix, g_pre_ffn, g_post_ffn, w_in, g_q, w_uq, g_kv, w_ukv, w_gate_f, b_gate_f, w_gate_b, b_gate_b, g_gla, w_pa, w_pb, w_o, w_ff_gate, w_ff_up, w_ff_down, w_router, w_ex_gate, w_ex_up, w_ex_down):
    n_ctx_b, t_ctx, d = x_prompt.shape
    n_lat_b, t_lat, _ = x_sample.shape
    depth = w_in.shape[0]
    past = cache_ckv.shape[2]
    q_lora, kv_lora = g_q.shape[1], g_kv.shape[1]
    rope = cache_krope.shape[3]
    heads, dk, dv = state_gla_fwd.shape[2:]
    rank = w_gate_f.shape[1]
    n_experts = w_router.shape[2]
    hq = QK_NOPE + rope
    hw = QK_NOPE + V_HEAD
    n_ctx, n_lat = n_ctx_b * t_ctx, n_lat_b * t_lat
    t = n_ctx + n_lat
    assert 2 * rope == LANE and 2 * rank <= LANE - rope

    tm = _tile(math.gcd(n_ctx, t_lat), 512)
    tm_s = _tile(math.gcd(n_ctx, t_lat), 1024)
    n_ctx_tiles = n_ctx // tm
    lat_tiles = t_lat // tm

    def sample_of_tile(i):
        return jnp.where(i < n_ctx_tiles, 0, 1 + (i - n_ctx_tiles) // lat_tiles)

    def pos_of_tile(i, tile=tm):
        first = n_ctx // tile
        return jnp.where(i < first, 0, tm_s // tile + (i - first) % (t_lat // tile))

    sizes = (q_lora, kv_lora, rope, heads * dk, heads * dk, heads * dv, rank, rank,
             heads * dv, d, d)
    offs = [0]
    for s in sizes:
        offs.append(offs[-1] + s)
    main_groups = (0, 1, 3, 4, 5, 8, 9, 10)
    col = {}
    acc = 0
    for gidx in main_groups:
        col[gidx] = acc
        acc += sizes[gidx]
    n_main = acc
    col_q, col_k, col_v, col_g, col_a, col_b = col[3], col[4], col[5], col[8], col[9], col[10]

    cos_r, sin_r = _rope_tables(t_lat, rope, tm_s)
    n_tab = cos_r.shape[0]
    cos_k = jnp.concatenate([cos_r, jnp.zeros((n_tab, LANE - rope), F32)], axis=1)
    sin_k = jnp.concatenate([sin_r, jnp.zeros((n_tab, LANE - rope), F32)], axis=1)

    x = (x_prompt.reshape(n_ctx, d), x_sample.reshape(n_lat, d))
    c_all = jnp.concatenate([c_ctx[None, :], c, jnp.zeros((8 - 1 - n_lat_b, d), F32)], axis=0)

    def modulation(l):
        def epi(accs, e):
            return accs[0] + e[0]
        tn = _tile(6 * d, 1024)
        m = _matmul([(c_all, d, 0)], [(w_ada, (l,))], n_out=6 * d, tm=8, tn=tn, out_dtype=F32,
                    epilogue=epi, prologue=lambda k, a, e: _silu(a), name=f"ada{l}",
                    extras=[(b_ada.reshape(depth, 1, 6 * d), (None, 1, tn),
                             lambda j, i, l=l: (l, 0, j))])
        return m.reshape(8, 6, d)

    mods = [modulation(l) for l in range(depth)]
    _, h = _resid_norm(x, sample_of_tile, tm, "prenorm0", mod_pre=mods[0], g_pre=g_pre_mix[0],
                       shift_idx=0, scale_idx=1, n_first=n_ctx_tiles)

    new_ckv, new_krope = [], []
    states = None
    for l in range(depth):
        w = w_in[l]
        w_main = jnp.concatenate([w[:, offs[gidx]:offs[gidx + 1]] for gidx in main_groups],
                                 axis=1).astype(BF16)
        w_kr = w[:, offs[2]:offs[3]]
        w_misc = jnp.concatenate([w_kr, w[:, offs[6]:offs[8]],
                                  jnp.zeros((d, LANE - rope - 2 * rank), F32),
                                  _swap_pairs(w_kr, rope), jnp.zeros((d, LANE - rope), F32)],
                                 axis=1).astype(BF16)
        proj = _matmul([(h, d, 0)], [(w_main, ())], n_out=n_main, tm=tm, tn=_tile(n_main, 1024),
                       out_dtype=F32, epilogue=_first, name=f"w_in{l}")
        misc = _matmul([(h, d, 0)], [(w_misc, ())], n_out=2 * LANE, tm=tm, tn=2 * LANE,
                       out_dtype=F32, epilogue=_first, name=f"w_in_misc{l}")

        w_gate = jnp.zeros((LANE, 2 * heads * dk), F32)
        w_gate = w_gate.at[rope:rope + rank, :heads * dk].set(w_gate_f[l])
        w_gate = w_gate.at[rope + rank:rope + 2 * rank, heads * dk:].set(w_gate_b[l])
        b_gate = jnp.concatenate([b_gate_f[l], b_gate_b[l]]).reshape(1, 2 * heads * dk)
        kr_self, la = pl.pallas_call(
            _prep_kernel, grid=(t // tm,),
            in_specs=[pl.BlockSpec((tm, 2 * LANE), lambda i: (i, 0)),
                      pl.BlockSpec((tm, LANE), lambda i: (pos_of_tile(i), 0)),
                      pl.BlockSpec((tm, LANE), lambda i: (pos_of_tile(i), 0)),
                      pl.BlockSpec((LANE, 2 * heads * dk), lambda i: (0, 0)),
                      pl.BlockSpec((1, 2 * heads * dk), lambda i: (0, 0))],
            out_specs=[pl.BlockSpec((tm, LANE), lambda i: (i, 0)),
                       pl.BlockSpec((tm, 2 * heads * dk), lambda i: (i, 0))],
            out_shape=[jax.ShapeDtypeStruct((t, LANE), BF16),
                       jax.ShapeDtypeStruct((t, 2 * heads * dk), F32)],
            compiler_params=_params(1), name=f"prep{l}")(misc, cos_k, sin_k, w_gate.astype(BF16), b_gate)

        wq = w_uq[l].reshape(q_lora, MLA_HEADS, hq)
        wq_r = wq[:, :, QK_NOPE:]
        pad = jnp.zeros((q_lora, MLA_HEADS, LANE - rope), F32)
        as_cols = lambda a: a.reshape(q_lora, -1).astype(BF16)
        wq_n = as_cols(wq[:, :, :QK_NOPE])
        wq_s = as_cols(jnp.concatenate([_swap_pairs(wq_r, rope), pad], axis=2))
        wq_r = as_cols(jnp.concatenate([wq_r, pad], axis=2))
        q_in = [(proj, q_lora, col[0] // q_lora)]
        q_norm = lambda k, a, e: _rms(a, e[0])
        g_q_extra = (g_q[l].reshape(1, q_lora), (1, q_lora), lambda j, i: (0, 0))
        tn_q = _tile(MLA_HEADS * LANE, 1024)

        def rope_epi(accs, e, tn_q=tn_q):
            cos, sin = e[1], e[2]
            return jnp.concatenate([accs[0][:, s:s + LANE] * cos + accs[1][:, s:s + LANE] * sin
                                    for s in range(0, tn_q, LANE)], axis=1)

        q_nope = _matmul(q_in, [(wq_n, ())], n_out=MLA_HEADS * QK_NOPE, tm=tm_s,
                         tn=_tile(MLA_HEADS * QK_NOPE, 1024), out_dtype=BF16, epilogue=_first,
                         name=f"w_uq_nope{l}", prologue=q_norm, extras=[g_q_extra])
        q_rope = _matmul(q_in, [(wq_r, ()), (wq_s, ())], a_of_w=[0, 0], n_out=MLA_HEADS * LANE,
                         tm=tm_s, tn=tn_q, out_dtype=BF16, epilogue=rope_epi,
                         name=f"w_uq_rope{l}", prologue=q_norm,
                         extras=[g_q_extra,
                                 (cos_k, (tm_s, LANE), lambda j, i: (pos_of_tile(i, tm_s), 0)),
                                 (sin_k, (tm_s, LANE), lambda j, i: (pos_of_tile(i, tm_s), 0))])

        c_kv = _norm_cols(proj, kv_lora, col[1] // kv_lora, g_kv[l], tm, f"ckv_norm{l}")
        tn_kv = _tile(MLA_HEADS * hw, 1024)
        kv_self = _matmul([(c_kv, kv_lora, 0)], [(w_ukv, (l,))], n_out=MLA_HEADS * hw, tm=tm_s,
                          tn=tn_kv, out_dtype=BF16, epilogue=_first, name=f"w_ukv{l}")
        ckv_cache = cache_ckv[:, l].reshape(n_lat_b * past, kv_lora)
        kv_cache = _matmul([(ckv_cache, kv_lora, 0)], [(w_ukv, (l,))], n_out=MLA_HEADS * hw,
                           tm=_tile(n_lat_b * past, 512), tn=tn_kv, out_dtype=BF16,
                           epilogue=_first, name=f"w_ukv_cache{l}")
        kr_cache = jnp.pad(cache_krope[:, l].reshape(n_lat_b * past, rope),
                           ((0, 0), (0, LANE - rope))).astype(BF16)

        scale = hq ** -0.5
        attn = _attention(q_nope, q_rope, [(kv_self, t_ctx, 0)], [(kr_self, t_ctx, 0)],
                          n_batch=n_ctx_b, t_q=t_ctx, q_row0=0, tq=_tile(t_ctx, 256),
                          n_heads=math.gcd(MLA_HEADS, 8), group=4, scale=scale, name=f"attn_ctx{l}",
                          o_prev=jnp.zeros((t, MLA_HEADS * V_HEAD), BF16))
        attn = _attention(q_nope, q_rope, [(kv_self, t_lat, n_ctx), (kv_cache, past, 0)],
                          [(kr_self, t_lat, n_ctx), (kr_cache, past, 0)], n_batch=n_lat_b,
                          t_q=t_lat, q_row0=n_ctx, tq=_tile(t_lat, 512),
                          n_heads=math.gcd(MLA_HEADS, 4), group=2, scale=scale, name=f"attn_lat{l}",
                          o_prev=attn)

        gla_kw = dict(heads=heads, dk=dk, dv=dv, col_q=col_q, col_k=col_k, col_v=col_v,
                      col_g=col_g, layer=l)
        gla, *states = _gla(proj, la, g_gla[l], n_batch=n_ctx_b, t_b=t_ctx, row0=0,
                            name=f"gla_ctx{l}", o_prev=jnp.zeros((t, heads * dv), BF16),
                            st_prev=states, st_depth=depth, **gla_kw)
        gla, = _gla(proj, la, g_gla[l], n_batch=n_lat_b, t_b=t_lat, row0=n_ctx,
                    s0=(state_gla_fwd, state_gla_bwd), name=f"gla_lat{l}", o_prev=gla, **gla_kw)

        new_ckv.append(c_kv[:n_ctx].reshape(n_ctx_b, t_ctx, kv_lora))
        new_krope.append(misc[:n_ctx, :rope].reshape(n_ctx_b, t_ctx, rope))

        tn_m = _tile(math.gcd(col_a, col_b, d), 1024)
        tm_m = _tile(tm, 256, 8)

        def merge_epi(accs, e):
            return jax.nn.sigmoid(e[0]) * accs[0] + jax.nn.sigmoid(e[1]) * accs[1]

        merged = _matmul([(attn, MLA_HEADS * V_HEAD, 0), (gla, heads * dv, 0)],
                         [(w_pa, (l,)), (w_pb, (l,))], n_out=d, tm=tm_m, tn=tn_m, out_dtype=BF16,
                         epilogue=merge_epi, name=f"merge{l}",
                         extras=[(proj, (tm_m, tn_m), lambda j, i: (i, col_a // tn_m + j)),
                                 (proj, (tm_m, tn_m), lambda j, i: (i, col_b // tn_m + j))])
        y = _matmul([(merged, d, 0)], [(w_o, (l,))], n_out=d, tm=tm, tn=_tile(d, 1024),
                    out_dtype=F32, epilogue=_first, name=f"w_o{l}")
        moe = l % 2 == 1
        x, h = _resid_norm(x, sample_of_tile, tm, f"mix_resid{l}", y=y, mod_res=mods[l],
                           g_post=g_post_mix[l], gate_idx=2, mod_pre=mods[l], g_pre=g_pre_ffn[l],
                           shift_idx=3, scale_idx=4, n_first=n_ctx_tiles)

        jx = l // 2

        def swiglu_epi(accs, e):
            return _silu(accs[0]) * accs[1]

        ffn_out = {}
        if not moe:
            d_ff = w_ff_gate.shape[2]
            ff = _matmul([(h, d, 0)], [(w_ff_gate, (jx,)), (w_ff_up, (jx,))], a_of_w=[0, 0],
                         n_out=d_ff, tm=tm, tn=_tile(d_ff, 512), out_dtype=BF16,
                         epilogue=swiglu_epi, name=f"ffn_up{l}")
            ffn_out["y"] = _matmul([(ff, d_ff, 0)], [(w_ff_down, (jx,))], n_out=d, tm=tm,
                                   tn=_tile(d, 512), out_dtype=F32, epilogue=_first,
                                   name=f"ffn_down{l}")
        else:
            d_ex = w_ex_gate.shape[3]
            w_r = jnp.pad(w_router[jx], ((0, 0), (0, LANE - n_experts))).astype(BF16)
            route = pl.pallas_call(
                functools.partial(_router_kernel, n_experts=n_experts), grid=(t // tm,),
                in_specs=[pl.BlockSpec((tm, d), lambda i: (i, 0)),
                          pl.BlockSpec((d, LANE), lambda i: (0, 0))],
                out_specs=pl.BlockSpec((tm, LANE), lambda i: (i, 0)),
                out_shape=jax.ShapeDtypeStruct((t, LANE), F32),
                compiler_params=_params(1), name=f"router{l}")(h, w_r)
            tile_e = _tile(t * TOP_K, EXPERT_TILE, 8)
            run_e = tile_e * EXPERT_RUN_TILES
            tok_of, dest, flags, run_info, tile_info = _routing_tables(route, n_experts, tile_e,
                                                                       run_e)
            xs = _gather(h.reshape(t, d // LANE, LANE), tok_of, flags, tile_e, f"moe_gather{l}")
            xs = xs.reshape(xs.shape[0], d)
            ff = _moe_up(xs, w_ex_gate, w_ex_up, (jx,), run_info, run=run_e, sub=tile_e,
                         tn=_tile(d_ex, 256), name=f"moe_up{l}")
            ys = _matmul([(ff, d_ex, 0)], [(w_ex_down, (jx, 0))], n_out=d, tm=tile_e,
                         tn=_tile(d, 1024), out_dtype=F32, epilogue=_first, name=f"moe_down{l}",
                         tile_expert=tile_info)
            ffn_out["routed"] = (ys, dest, route)

        if l + 1 < depth:
            x, h = _resid_norm(x, sample_of_tile, tm, f"ffn_resid{l}", mod_res=mods[l],
                               g_post=g_post_ffn[l], gate_idx=5, mod_pre=mods[l + 1],
                               g_pre=g_pre_mix[l + 1], shift_idx=0, scale_idx=1, **ffn_out)
        else:
            x, _ = _resid_norm(x, sample_of_tile, tm, f"ffn_resid{l}", mod_res=mods[l],
                               g_post=g_post_ffn[l], gate_idx=5, split_out=True,
                               n_first=n_ctx_tiles, **ffn_out)

    return (x[0].reshape(n_ctx_b, t_ctx, d), x[1].reshape(n_lat_b, t_lat, d),
            jnp.stack(new_ckv, axis=1), jnp.stack(new_krope, axis=1),
            states[0], states[1])
```

```python
import functools
import math

import jax
import jax.numpy as jnp
from jax import lax
from jax.experimental import pallas as pl
from jax.experimental.pallas import tpu as pltpu

MLA_HEADS = 16
QK_NOPE = 128
V_HEAD = 128
GRID_W = 64
ROPE_THETA = 10000.0
GATE_NORM = 16.0
GLA_CHUNK = 64
TOP_K = 2
EPS = 1e-6
LANE = 128
VMEM_LIMIT = 52 * 1024 * 1024
EXPERT_TILE = 512
EXPERT_RUN_TILES = 4

BF16 = jnp.bfloat16
F32 = jnp.float32


def _params(n_grid, **kw):
    return pltpu.CompilerParams(
        dimension_semantics=("arbitrary",) * n_grid, vmem_limit_bytes=VMEM_LIMIT, **kw)


def _tile(n, target, quantum=LANE):
    if n <= target:
        return n
    t = (target // quantum) * quantum
    while t >= quantum:
        if n % t == 0:
            return t
        t -= quantum
    return n


def _rms(x, g):
    return x * lax.rsqrt(jnp.mean(x * x, axis=-1, keepdims=True) + EPS) * g


def _silu(x):
    return x * jax.nn.sigmoid(x)


def _dot(a, b):
    return jnp.dot(a, b, preferred_element_type=F32)


def _dot_nt(a, b):
    return lax.dot_general(a, b, (((1,), (1,)), ((), ())), preferred_element_type=F32)


def _dot_tn(a, b):
    return lax.dot_general(a, b, (((0,), (0,)), ((), ())), preferred_element_type=F32)


def _mm_kernel(*refs, n_a, n_w, a_of_w, n_e, cast_w, grouped, prologue, epilogue):
    if grouped:
        te_ref, refs = refs[0], refs[1:]
    a_refs = refs[:n_a]
    w_refs = refs[n_a:n_a + n_w]
    e_refs = refs[n_a + n_w:n_a + n_w + n_e]
    o_ref = refs[n_a + n_w + n_e]
    wb_refs = refs[n_a + n_w + n_e + 1:]
    i = pl.program_id(1)

    def compute():
        if cast_w:
            fresh = i == 0
            if grouped:
                fresh = fresh | (te_ref[i] != te_ref[jnp.maximum(i - 1, 0)])

            @pl.when(fresh)
            def _():
                for w_ref, wb_ref in zip(w_refs, wb_refs):
                    wb_ref[...] = w_ref[...].astype(BF16)
            w_use = wb_refs
        else:
            w_use = w_refs
        e = [e_ref[...] for e_ref in e_refs]
        a_vals = []
        for k, a_ref in enumerate(a_refs):
            a = a_ref[...]
            if prologue is not None:
                a = prologue(k, a, e)
            a_vals.append(a.astype(BF16))
        accs = [_dot(a_vals[a_of_w[k]], w_ref[...]) for k, w_ref in enumerate(w_use)]
        o_ref[...] = epilogue(accs, e).astype(o_ref.dtype)

    if grouped:
        has_rows = te_ref[pl.num_programs(1) + i] != 0
        pl.when(has_rows)(compute)

        @pl.when(jnp.logical_not(has_rows))
        def _():
            o_ref[...] = jnp.zeros_like(o_ref)
    else:
        compute()


def _matmul(a_list, w_list, *, n_out, tm, tn, out_dtype, epilogue, name, a_of_w=None,
            prologue=None, extras=(), tile_expert=None):
    m = a_list[0][0].shape[0]
    a_of_w = a_of_w or list(range(len(w_list)))
    grouped = tile_expert is not None
    assert m % tm == 0 and n_out % tn == 0
    in_specs, args, scratch = [], [], []
    for arr, k, cb in a_list:
        in_specs.append(pl.BlockSpec((tm, k), lambda j, i, *_, cb=cb: (i, cb)))
        args.append(arr)
    cast_w = w_list[0][0].dtype != BF16
    for arr, lead in w_list:
        k = arr.shape[-2]
        if grouped:
            im = lambda j, i, te, lead=lead: tuple(lead[:-1]) + (te[i], 0, j)
        else:
            im = lambda j, i, lead=lead: tuple(lead) + (0, j)
        in_specs.append(pl.BlockSpec((None,) * len(lead) + (k, tn), im))
        args.append(arr)
        if cast_w:
            scratch.append(pltpu.VMEM((k, tn), BF16))
    for arr, bs, im in extras:
        in_specs.append(pl.BlockSpec(bs, lambda j, i, *_, im=im: im(j, i)))
        args.append(arr)
    kern = functools.partial(_mm_kernel, n_a=len(a_list), n_w=len(w_list), a_of_w=a_of_w,
                             n_e=len(extras), cast_w=cast_w, grouped=grouped,
                             prologue=prologue, epilogue=epilogue)
    grid_spec = pltpu.PrefetchScalarGridSpec(
        num_scalar_prefetch=1 if grouped else 0,
        grid=(n_out // tn, m // tm),
        in_specs=in_specs,
        out_specs=pl.BlockSpec((tm, tn), lambda j, i, *_: (i, j)),
        scratch_shapes=scratch)
    if grouped:
        args = [tile_expert] + args
    return pl.pallas_call(
        kern, grid_spec=grid_spec,
        out_shape=jax.ShapeDtypeStruct((m, n_out), out_dtype),
        compiler_params=_params(2), name=name,
    )(*args)


def _first(accs, e):
    return accs[0]


def _row_copy(src, row, dst, r, sem):
    return pltpu.make_async_copy(src.at[pl.ds(row, 1)], dst.at[pl.ds(r, 1)], sem)


def _gather_rows(idx_ref, base, stride, n, src, dst, dst_base, sem):
    def issue(r, carry):
        _row_copy(src, idx_ref[base + r * stride], dst, dst_base + r, sem).start()
        return carry

    def wait(r, carry):
        _row_copy(src, 0, dst, dst_base + r, sem).wait()
        return carry

    lax.fori_loop(0, n, issue, 0, unroll=8)
    lax.fori_loop(0, n, wait, 0, unroll=8)


def _gather_kernel(idx_ref, flag_ref, src_hbm, o_ref, sem):
    tg = o_ref.shape[0]
    i = pl.program_id(0)

    @pl.when(flag_ref[i] != 0)
    def _():
        _gather_rows(idx_ref, i * tg, 1, tg, src_hbm, o_ref, 0, sem)

    @pl.when(flag_ref[i] == 0)
    def _():
        o_ref[...] = jnp.zeros_like(o_ref)


def _gather(src, idx, flags, tg, name):
    n = idx.shape[0]
    blk = (tg,) + src.shape[1:]
    grid_spec = pltpu.PrefetchScalarGridSpec(
        num_scalar_prefetch=2, grid=(n // tg,),
        in_specs=[pl.BlockSpec(memory_space=pl.ANY)],
        out_specs=pl.BlockSpec(blk, lambda i, *_: (i,) + (0,) * (len(blk) - 1)),
        scratch_shapes=[pltpu.SemaphoreType.DMA(())])
    return pl.pallas_call(
        _gather_kernel, grid_spec=grid_spec,
        out_shape=jax.ShapeDtypeStruct((n,) + src.shape[1:], src.dtype),
        compiler_params=_params(1), name=name)(idx, flags, src)


def _resid_norm_kernel(*refs, mode, want_x, want_h, gate_idx, shift_idx, scale_idx, split_in,
                       split_out, n_first):
    it = iter(refs)
    dest_ref = next(it) if mode == "routed" else None
    x_refs = [next(it) for _ in range(2 if split_in else 1)]
    if mode == "dense":
        y_ref = next(it)
    elif mode == "routed":
        ys_hbm, route_ref = next(it), next(it)
    if mode is not None:
        mod_res_ref, g_post_ref = next(it), next(it)
    if want_h:
        mod_pre_ref, g_pre_ref = next(it), next(it)
    xo_refs = [next(it) for _ in range((2 if split_out else 1) if want_x else 0)]
    h_ref = next(it) if want_h else None
    i = pl.program_id(0)
    x = x_refs[0][...]
    if split_in:
        x = jnp.where(i < n_first, x, x_refs[1][...])
    if mode is not None:
        if mode == "routed":
            buf1, buf2, sem = next(it), next(it), next(it)
            tm = x.shape[0]
            base = i * tm * TOP_K
            _gather_rows(dest_ref, base, TOP_K, tm, ys_hbm, buf1, 0, sem)
            _gather_rows(dest_ref, base + 1, TOP_K, tm, ys_hbm, buf2, 0, sem)
            route = route_ref[...]
            y = route[:, 2:3] * buf1[...] + route[:, 3:4] * buf2[...]
        else:
            y = y_ref[...]
        gate = mod_res_ref[0, gate_idx:gate_idx + 1, :]
        x = x + gate * _rms(y, g_post_ref[...])
    if want_x and split_out:
        @pl.when(i < n_first)
        def _():
            xo_refs[0][...] = x

        @pl.when(i >= n_first)
        def _():
            xo_refs[1][...] = x
    elif want_x:
        xo_refs[0][...] = x
    if want_h:
        scale = mod_pre_ref[0, scale_idx:scale_idx + 1, :]
        shift = mod_pre_ref[0, shift_idx:shift_idx + 1, :]
        h_ref[...] = (_rms(x, g_pre_ref[...]) * (1.0 + scale) + shift).astype(h_ref.dtype)


def _resid_norm(x, sample_of_tile, tm, name, *, y=None, routed=None, mod_res=None, g_post=None,
                gate_idx=0, mod_pre=None, g_pre=None, shift_idx=0, scale_idx=0, split_out=False,
                n_first=0):
    split_in = isinstance(x, (tuple, list))
    xs_in = list(x) if split_in else [x]
    t, d = sum(a.shape[0] for a in xs_in), xs_in[0].shape[1]
    mode = "dense" if y is not None else ("routed" if routed is not None else None)
    want_h = mod_pre is not None
    want_x = mode is not None
    row = pl.BlockSpec((tm, d), lambda i, *_: (i, 0))
    first = pl.BlockSpec((tm, d), lambda i, *_: (jnp.minimum(i, n_first - 1), 0))
    rest = pl.BlockSpec((tm, d), lambda i, *_: (jnp.maximum(i - n_first, 0), 0))
    vec = pl.BlockSpec((1, d), lambda i, *_: (0, 0))
    mod = pl.BlockSpec((1, 6, d), lambda i, *_: (sample_of_tile(i), 0, 0))
    in_specs, args = ([first, rest] if split_in else [row]), xs_in
    out_specs, out_shape, scratch = [], [], []
    if mode == "dense":
        in_specs.append(row)
        args.append(y)
    elif mode == "routed":
        ys, dest, route = routed
        in_specs += [pl.BlockSpec(memory_space=pl.ANY),
                     pl.BlockSpec((tm, LANE), lambda i, *_: (i, 0))]
        args += [ys, route]
        scratch = [pltpu.VMEM((tm, d), F32), pltpu.VMEM((tm, d), F32),
                   pltpu.SemaphoreType.DMA(())]
    if mode is not None:
        in_specs += [mod, vec]
        args += [mod_res, g_post.reshape(1, d)]
    if want_h:
        in_specs += [mod, vec]
        args += [mod_pre, g_pre.reshape(1, d)]
    if want_x and split_out:
        out_specs += [first, rest]
        out_shape += [jax.ShapeDtypeStruct((n_first * tm, d), F32),
                      jax.ShapeDtypeStruct((t - n_first * tm, d), F32)]
    elif want_x:
        out_specs.append(row)
        out_shape.append(jax.ShapeDtypeStruct((t, d), F32))
    if want_h:
        out_specs.append(row)
        out_shape.append(jax.ShapeDtypeStruct((t, d), BF16))
    kern = functools.partial(_resid_norm_kernel, mode=mode, want_x=want_x, want_h=want_h,
                             gate_idx=gate_idx, shift_idx=shift_idx, scale_idx=scale_idx,
                             split_in=split_in, split_out=split_out, n_first=n_first)
    grid_spec = pltpu.PrefetchScalarGridSpec(
        num_scalar_prefetch=1 if mode == "routed" else 0, grid=(t // tm,),
        in_specs=in_specs, out_specs=out_specs, scratch_shapes=scratch)
    if mode == "routed":
        args = [dest] + args
    outs = pl.pallas_call(kern, grid_spec=grid_spec, out_shape=out_shape,
                          compiler_params=_params(1), name=name)(*args)
    n_x = (2 if split_out else 1) if want_x else 0
    x_new = (tuple(outs[:2]) if split_out else outs[0]) if want_x else x
    h = outs[n_x] if want_h else None
    return x_new, h


def _norm_kernel(x_ref, g_ref, o_ref):
    o_ref[...] = _rms(x_ref[...], g_ref[...])


def _norm_cols(arr, width, col_block, g, tm, name):
    t = arr.shape[0]
    return pl.pallas_call(
        _norm_kernel, grid=(t // tm,),
        in_specs=[pl.BlockSpec((tm, width), lambda i: (i, col_block)),
                  pl.BlockSpec((1, width), lambda i: (0, 0))],
        out_specs=pl.BlockSpec((tm, width), lambda i: (i, 0)),
        out_shape=jax.ShapeDtypeStruct((t, width), F32),
        compiler_params=_params(1), name=name)(arr, g.reshape(1, width))


def _swap_pairs(w, rope):
    q = rope // 4
    return jnp.concatenate([w[..., q:2 * q], w[..., :q], w[..., 3 * q:], w[..., 2 * q:3 * q]],
                           axis=-1)


def _prep_kernel(misc_ref, cos_ref, sin_ref, wg_ref, bg_ref, kr_ref, la_ref):
    misc = misc_ref[:, :LANE]
    kr = misc * cos_ref[...] + misc_ref[:, LANE:] * sin_ref[...]
    kr_ref[...] = kr.astype(kr_ref.dtype)
    z = _dot(misc.astype(BF16), wg_ref[...]) + bg_ref[...]
    log_sig = jnp.minimum(z, 0.0) - jnp.log1p(jnp.exp(-jnp.abs(z)))
    la_ref[...] = log_sig * (1.0 / GATE_NORM)


def _attn_kernel(*refs, n_seg, n_heads, group, scale):
    qn_ref, qr_ref = refs[:2]
    kv_refs = refs[2:2 + n_seg]
    kr_refs = refs[2 + n_seg:2 + 2 * n_seg]
    o_ref = refs[-1]
    hw = QK_NOPE + V_HEAD
    krs = [kr_ref[...] for kr_ref in kr_refs]
    add = lambda a, b: a + b
    c = scale * math.log2(math.e)
    for h0 in range(0, n_heads, group):
        hds = range(h0, min(h0 + group, n_heads))
        scores = []
        for hd in hds:
            qn = qn_ref[:, hd * QK_NOPE:(hd + 1) * QK_NOPE]
            qr = qr_ref[:, hd * LANE:(hd + 1) * LANE]
            scores.append([_dot_nt(qn, kv_ref[:, hd * hw:hd * hw + QK_NOPE]) + _dot_nt(qr, kr)
                           for kv_ref, kr in zip(kv_refs, krs)])
        ms = [functools.reduce(jnp.maximum, [jnp.max(s, axis=-1, keepdims=True) for s in ss])
              for ss in scores]
        ps = [[jnp.exp2((s - m) * c) for s in ss] for ss, m in zip(scores, ms)]
        invs = [1.0 / functools.reduce(add, [jnp.sum(p, axis=-1, keepdims=True) for p in pp])
                for pp in ps]
        for hd, pp, inv in zip(hds, ps, invs):
            o = functools.reduce(add, [
                _dot((p * inv).astype(BF16), kv_ref[:, hd * hw + QK_NOPE:(hd + 1) * hw])
                for p, kv_ref in zip(pp, kv_refs)])
            o_ref[:, hd * V_HEAD:(hd + 1) * V_HEAD] = o.astype(o_ref.dtype)


def _attention(q_nope, q_rope, kvs, krs, *, n_batch, t_q, q_row0, tq, n_heads, group, scale, name,
               o_prev):
    hw = QK_NOPE + V_HEAD
    nq = t_q // tq
    q_map = lambda b, g, i: (q_row0 // tq + b * nq + i, g)
    in_specs = [pl.BlockSpec((tq, n_heads * QK_NOPE), q_map),
                pl.BlockSpec((tq, n_heads * LANE), q_map)]
    args = [q_nope, q_rope]
    for arr, tk, row0 in kvs:
        in_specs.append(pl.BlockSpec((tk, n_heads * hw),
                                     lambda b, g, i, tk=tk, row0=row0: (row0 // tk + b, g)))
        args.append(arr)
    for arr, tk, row0 in krs:
        in_specs.append(pl.BlockSpec((tk, LANE),
                                     lambda b, g, i, tk=tk, row0=row0: (row0 // tk + b, 0)))
        args.append(arr)
    aliases = {len(args): 0}
    in_specs.append(pl.BlockSpec(memory_space=pl.ANY))
    args.append(o_prev)
    kern = functools.partial(_attn_kernel, n_seg=len(kvs), n_heads=n_heads, group=group,
                             scale=scale)
    return pl.pallas_call(
        kern, grid=(n_batch, MLA_HEADS // n_heads, nq), in_specs=in_specs,
        out_specs=pl.BlockSpec((tq, n_heads * V_HEAD), q_map),
        out_shape=jax.ShapeDtypeStruct(o_prev.shape, o_prev.dtype),
        input_output_aliases=aliases,
        compiler_params=_params(3), name=name)(*args)


def _split3(x):
    hi = x.astype(BF16)
    r1 = x - hi.astype(F32)
    mid = r1.astype(BF16)
    lo = (r1 - mid.astype(F32)).astype(BF16)
    return hi, mid, lo


def _gla_kernel(*refs, has_init, want_states, n_alias, n_heads, q_scale, layer, all_layers):
    it = iter(refs)
    q_ref, k_ref, v_ref, laf_ref, lab_ref, gout_ref, ggla_ref = (next(it) for _ in range(7))
    s0_refs = (next(it), next(it)) if has_init else None
    for _ in range(n_alias):
        next(it)
    o_ref = next(it)
    s_out_refs = (next(it), next(it)) if want_states else None
    st, vb, qd_s, ks_s, dec_s, oacc = (next(it) for _ in range(6))
    c = GLA_CHUNK
    t_b = q_ref.shape[0]
    dk, dv = q_ref.shape[1] // n_heads, v_ref.shape[1] // n_heads
    n_chunks = t_b // c
    rb = min(t_b, 4 * c)
    row = lax.broadcasted_iota(jnp.int32, (rb, rb), 0)
    col = lax.broadcasted_iota(jnp.int32, (rb, rb), 1)
    same = (row // c) == (col // c)
    masks = (same & (col <= row), same & (col >= row))
    tris = [jnp.where(m, 1.0, 0.0).astype(BF16) for m in masks]
    ones_bd = jnp.where(same, 1.0, 0.0).astype(BF16)
    la_refs = (laf_ref, lab_ref)
    streams = [(g, d) for g in range(n_heads) for d in range(2)]
    kcols = [pl.ds(g * dk, dk) for g in range(n_heads)]
    vcols = [pl.ds(g * dv, dv) for g in range(n_heads)]

    vb[...] = v_ref[...].astype(BF16)
    for s, (g, d) in enumerate(streams):
        st[s] = s0_refs[d][0, g].T if has_init else jnp.zeros(st.shape[1:], F32)

    for r0 in range(0, t_b, rb):
        rows = pl.ds(r0, rb)
        for s, (g, d) in enumerate(streams):
            hi, mid, lo = _split3(la_refs[d][rows, kcols[g]])
            b = (_dot(tris[d], hi) + _dot(tris[d], mid)) + _dot(tris[d], lo)
            b_end = (_dot(ones_bd, hi) + _dot(ones_bd, mid)) + _dot(ones_bd, lo)
            kc = k_ref[rows, kcols[g]]
            qd = (q_ref[rows, kcols[g]] * q_scale * jnp.exp(b)).astype(BF16)
            kd = (kc * jnp.exp(-b)).astype(BF16)
            qd_s[s, rows, :] = qd
            ks_s[s, rows, :] = (kc * jnp.exp(b_end - b)).astype(BF16)
            dec_s[s, rows, :] = jnp.exp(b_end)
            a = jnp.where(masks[d], _dot_nt(qd, kd), 0.0).astype(BF16)
            o_in = _dot(a, vb[rows, vcols[g]])
            if d == 0:
                oacc[rows, vcols[g]] = o_in
            else:
                oacc[rows, vcols[g]] += o_in

    for i in range(n_chunks):
        for s, (g, d) in enumerate(streams):
            c0 = (i if d == 0 else n_chunks - 1 - i) * c
            rows = pl.ds(c0, c)
            u = _dot_tn(vb[rows, vcols[g]], ks_s[s, rows, :])
            s_t = st[s]
            oacc[rows, vcols[g]] += _dot_nt(qd_s[s, rows, :], s_t.astype(BF16))
            st[s] = s_t * dec_s[s, pl.ds(c0, 1), :] + u

    for g in range(n_heads):
        o_ref[:, vcols[g]] = (_rms(oacc[:, vcols[g]], ggla_ref[...])
                              * _silu(gout_ref[:, vcols[g]])).astype(o_ref.dtype)
    if want_states:
        for s, (g, d) in enumerate(streams):
            s_fin = st[s].T
            if all_layers:
                for ll in range(s_out_refs[d].shape[1]):
                    s_out_refs[d][0, ll, g] = s_fin if ll == layer else jnp.zeros_like(s_fin)
            else:
                s_out_refs[d][0, g] = s_fin


def _gla(proj, la, g_gla, *, n_batch, t_b, row0, heads, n_heads, dk, dv, col_q, col_k, col_v,
         col_g, name, o_prev, s0=None, layer=0, st_prev=None, st_depth=0):
    has_init = s0 is not None
    rb = row0 // t_b
    wk, wv = n_heads * dk, n_heads * dv
    assert col_q % wk == 0 and col_k % wk == 0 and col_v % wv == 0 and col_g % wv == 0
    in_specs = [
        pl.BlockSpec((t_b, wk), lambda b, h: (rb + b, col_q // wk + h)),
        pl.BlockSpec((t_b, wk), lambda b, h: (rb + b, col_k // wk + h)),
        pl.BlockSpec((t_b, wv), lambda b, h: (rb + b, col_v // wv + h)),
        pl.BlockSpec((t_b, wk), lambda b, h: (rb + b, h)),
        pl.BlockSpec((t_b, wk), lambda b, h: (rb + b, heads // n_heads + h)),
        pl.BlockSpec((t_b, wv), lambda b, h: (rb + b, col_g // wv + h)),
        pl.BlockSpec((1, dv), lambda b, h: (0, 0)),
    ]
    args = [proj, proj, proj, la, la, proj, g_gla.reshape(1, dv)]
    st_spec = pl.BlockSpec((1, None, n_heads, dk, dv), lambda b, h: (b, layer, h, 0, 0))
    if has_init:
        in_specs += [st_spec, st_spec]
        args += list(s0)
    out_specs = [pl.BlockSpec((t_b, wv), lambda b, h: (rb + b, h))]
    out_shape = [jax.ShapeDtypeStruct(o_prev.shape, o_prev.dtype)]
    aliases = {len(args): 0}
    in_specs.append(pl.BlockSpec(memory_space=pl.ANY))
    args.append(o_prev)
    want_states = st_prev is not None or st_depth > 0
    if st_prev is not None:
        out_specs += [st_spec, st_spec]
        for k, arr in enumerate(st_prev):
            out_shape.append(jax.ShapeDtypeStruct(arr.shape, arr.dtype))
            aliases[len(args)] = 1 + k
            in_specs.append(pl.BlockSpec(memory_space=pl.ANY))
            args.append(arr)
    elif want_states:
        out_specs += [pl.BlockSpec((1, st_depth, n_heads, dk, dv),
                                   lambda b, h: (b, 0, h, 0, 0))] * 2
        out_shape += [jax.ShapeDtypeStruct((n_batch, st_depth, heads, dk, dv), F32)] * 2
    kern = functools.partial(_gla_kernel, has_init=has_init, want_states=want_states,
                             n_alias=len(aliases), n_heads=n_heads, q_scale=dk ** -0.5,
                             layer=layer, all_layers=st_prev is None)
    n_str = 2 * n_heads
    return pl.pallas_call(
        kern, grid=(n_batch, heads // n_heads), in_specs=in_specs, out_specs=out_specs,
        out_shape=out_shape, input_output_aliases=aliases,
        scratch_shapes=[pltpu.VMEM((n_str, dv, dk), F32), pltpu.VMEM((t_b, wv), BF16),
                        pltpu.VMEM((n_str, t_b, dk), BF16), pltpu.VMEM((n_str, t_b, dk), BF16),
                        pltpu.VMEM((n_str, t_b, dk), F32), pltpu.VMEM((t_b, wv), F32)],
        compiler_params=_params(2), name=name)(*args)


def _router_kernel(h_ref, w_ref, o_ref, *, n_experts):
    logits = _dot(h_ref[...].astype(BF16), w_ref[...])
    lane = lax.broadcasted_iota(jnp.int32, logits.shape, 1).astype(F32)
    neg = jnp.float32(-jnp.inf)
    lg = jnp.where(lane < n_experts, logits, neg)
    m1 = jnp.max(lg, axis=-1, keepdims=True)
    i1 = jnp.min(jnp.where(lg == m1, lane, LANE), axis=-1, keepdims=True)
    lg2 = jnp.where(lane == i1, neg, lg)
    m2 = jnp.max(lg2, axis=-1, keepdims=True)
    i2 = jnp.min(jnp.where(lg2 == m2, lane, LANE), axis=-1, keepdims=True)
    e2 = jnp.exp(m2 - m1)
    inv = 1.0 / (1.0 + e2)
    o_ref[...] = (jnp.where(lane == 0, i1, 0.0) + jnp.where(lane == 1, i2, 0.0)
                  + jnp.where(lane == 2, inv, 0.0) + jnp.where(lane == 3, e2 * inv, 0.0))


def _moe_up_kernel(info_ref, x_ref, wg_ref, wu_ref, o_ref, *, n_sub):
    s = pl.program_id(0)
    flag0 = pl.num_programs(0) + s * n_sub
    sub = x_ref.shape[0] // n_sub

    @pl.when(info_ref[flag0] != 0)
    def _():
        wg = wg_ref[...].astype(BF16)
        wu = wu_ref[...].astype(BF16)
        for k in range(n_sub):
            rows = pl.ds(k * sub, sub)

            @pl.when(info_ref[flag0 + k] != 0)
            def _():
                a = x_ref[rows, :]
                o_ref[rows, :] = (_silu(_dot(a, wg)) * _dot(a, wu)).astype(o_ref.dtype)

            @pl.when(info_ref[flag0 + k] == 0)
            def _():
                o_ref[rows, :] = jnp.zeros((sub, o_ref.shape[1]), o_ref.dtype)

    @pl.when(info_ref[flag0] == 0)
    def _():
        o_ref[...] = jnp.zeros_like(o_ref)


def _moe_up(xs, w_gate, w_up, lead, info, *, run, sub, tn, name):
    n_rows, d = xs.shape
    n_out = w_gate.shape[-1]
    n_run = n_rows // run
    n_sub = run // sub

    def w_map(s, j, info):
        return tuple(lead) + (info[s], 0, jnp.where(info[n_run + s * n_sub] != 0, j, 0))

    w_spec = pl.BlockSpec((None,) * (len(lead) + 1) + (d, tn), w_map)
    grid_spec = pltpu.PrefetchScalarGridSpec(
        num_scalar_prefetch=1, grid=(n_run, n_out // tn),
        in_specs=[pl.BlockSpec((run, d), lambda s, j, info: (s, 0)), w_spec, w_spec],
        out_specs=pl.BlockSpec((run, tn), lambda s, j, info: (s, j)))
    return pl.pallas_call(
        functools.partial(_moe_up_kernel, n_sub=n_sub), grid_spec=grid_spec,
        out_shape=jax.ShapeDtypeStruct((n_rows, n_out), BF16),
        compiler_params=_params(2), name=name)(info, xs, w_gate, w_up)


def _routing_tables(route, n_experts, tile, run):
    t = route.shape[0]
    n_assign = t * TOP_K
    n_rows = -(-(n_assign + n_experts * (run - 1)) // run) * run
    e_flat = route[:, :TOP_K].astype(jnp.int32).reshape(n_assign)
    onehot = (e_flat[:, None] == jnp.arange(n_experts, dtype=jnp.int32)[None, :]).astype(jnp.int32)
    csum = jnp.cumsum(onehot, axis=0)
    rank = jnp.take_along_axis(csum, e_flat[:, None], axis=1)[:, 0] - 1
    counts = csum[-1]
    padded = ((counts + run - 1) // run) * run
    ends = jnp.cumsum(padded)
    starts = ends - padded
    dest = starts[e_flat] + rank
    tok_of = jnp.zeros((n_rows,), jnp.int32).at[dest].set(
        jnp.arange(n_assign, dtype=jnp.int32) // TOP_K, unique_indices=True)
    tile_start = jnp.arange(n_rows // tile, dtype=jnp.int32) * tile
    tile_expert = jnp.minimum(
        jnp.sum(tile_start[:, None] >= ends[None, :], axis=1, dtype=jnp.int32), n_experts - 1)
    flags = (tile_start < (starts + counts)[tile_expert]).astype(jnp.int32)
    run_info = jnp.concatenate([tile_expert[::run // tile], flags])
    tile_info = jnp.concatenate([tile_expert, flags])
    return tok_of, dest.astype(jnp.int32), flags, run_info, tile_info


def _rope_tables(n_lat, rope, tm_id):
    axis_half = rope // 4
    rows = n_lat // GRID_W
    r = jnp.repeat(jnp.arange(rows, dtype=F32), GRID_W)
    c = jnp.tile(jnp.arange(GRID_W, dtype=F32), rows)
    inv = ROPE_THETA ** (-jnp.arange(axis_half, dtype=F32) / axis_half)
    ar, ac = r[:, None] * inv, c[:, None] * inv
    cos = jnp.concatenate([jnp.cos(ar), jnp.cos(ar), jnp.cos(ac), jnp.cos(ac)], axis=1)
    sin = jnp.concatenate([-jnp.sin(ar), jnp.sin(ar), -jnp.sin(ac), jnp.sin(ac)], axis=1)
    cos = jnp.concatenate([jnp.ones((tm_id, rope), F32), cos], axis=0)
    sin = jnp.concatenate([jnp.zeros((tm_id, rope), F32), sin], axis=0)
    return cos, sin


def kernel(x_prompt, x_sample, cache_ckv, cache_krope, state_gla_fwd, state_gla_bwd, c, c_ctx, w_ada, b_ada, g_pre_mix, g_post_mix, g_pre_ffn, g_post_ffn, w_in, g_q, w_uq, g_kv, w_ukv, w_gate_f, b_gate_f, w_gate_b, b_gate_b, g_gla, w_pa, w_pb, w_o, w_ff_gate, w_ff_up, w_ff_down, w_router, w_ex_gate, w_ex_up, w_ex_down):
    n_ctx_b, t_ctx, d = x_prompt.shape
    n_lat_b, t_lat, _ = x_sample.shape
    depth = w_in.shape[0]
    past = cache_ckv.shape[2]
    q_lora, kv_lora = g_q.shape[1], g_kv.shape[1]
    rope = cache_krope.shape[3]
    heads, dk, dv = state_gla_fwd.shape[2:]
    rank = w_gate_f.shape[1]
    n_experts = w_router.shape[2]
    hq = QK_NOPE + rope
    hw = QK_NOPE + V_HEAD
    n_ctx, n_lat = n_ctx_b * t_ctx, n_lat_b * t_lat
    t = n_ctx + n_lat
    assert 2 * rope == LANE and 2 * rank <= LANE - rope

    tm = _tile(math.gcd(n_ctx, t_lat), 512)
    tm_s = _tile(math.gcd(n_ctx, t_lat), 1024)
    n_ctx_tiles = n_ctx // tm
    lat_tiles = t_lat // tm

    def sample_of_tile(i):
        return jnp.where(i < n_ctx_tiles, 0, 1 + (i - n_ctx_tiles) // lat_tiles)

    def pos_of_tile(i, tile=tm):
        first = n_ctx // tile
        return jnp.where(i < first, 0, tm_s // tile + (i - first) % (t_lat // tile))

    sizes = (q_lora, kv_lora, rope, heads * dk, heads * dk, heads * dv, rank, rank,
             heads * dv, d, d)
    offs = [0]
    for s in sizes:
        offs.append(offs[-1] + s)
    main_groups = (0, 1, 3, 4, 5, 8, 9, 10)
    col = {}
    acc = 0
    for gidx in main_groups:
        col[gidx] = acc
        acc += sizes[gidx]
    n_main = acc
    col_q, col_k, col_v, col_g, col_a, col_b = col[3], col[4], col[5], col[8], col[9], col[10]

    cos_r, sin_r = _rope_tables(t_lat, rope, tm_s)
    n_tab = cos_r.shape[0]
    cos_k = jnp.concatenate([cos_r, jnp.zeros((n_tab, LANE - rope), F32)], axis=1)
    sin_k = jnp.concatenate([sin_r, jnp.zeros((n_tab, LANE - rope), F32)], axis=1)

    x = (x_prompt.reshape(n_ctx, d), x_sample.reshape(n_lat, d))
    c_all = jnp.concatenate([c_ctx[None, :], c, jnp.zeros((8 - 1 - n_lat_b, d), F32)], axis=0)

    def modulation(l):
        def epi(accs, e):
            return accs[0] + e[0]
        tn = _tile(6 * d, 1024)
        m = _matmul([(c_all, d, 0)], [(w_ada, (l,))], n_out=6 * d, tm=8, tn=tn, out_dtype=F32,
                    epilogue=epi, prologue=lambda k, a, e: _silu(a), name=f"ada{l}",
                    extras=[(b_ada.reshape(depth, 1, 6 * d), (None, 1, tn),
                             lambda j, i, l=l: (l, 0, j))])
        return m.reshape(8, 6, d)

    mods = [modulation(l) for l in range(depth)]
    _, h = _resid_norm(x, sample_of_tile, tm, "prenorm0", mod_pre=mods[0], g_pre=g_pre_mix[0],
                       shift_idx=0, scale_idx=1, n_first=n_ctx_tiles)

    new_ckv, new_krope = [], []
    states = None
    for l in range(depth):
        w = w_in[l]
        w_kr = w[:, offs[2]:offs[3]]
        w_all = jnp.concatenate([w[:, offs[gidx]:offs[gidx + 1]] for gidx in main_groups]
                                + [w_kr, w[:, offs[6]:offs[8]],
                                   jnp.zeros((d, LANE - rope - 2 * rank), F32),
                                   _swap_pairs(w_kr, rope), jnp.zeros((d, LANE - rope), F32)],
                                axis=1).astype(BF16)
        n_all = n_main + 2 * LANE
        proj = _matmul([(h, d, 0)], [(w_all, ())], n_out=n_all, tm=tm_s, tn=_tile(n_all, 1280),
                       out_dtype=F32, epilogue=_first, name=f"w_in{l}")

        w_gate = jnp.zeros((LANE, 2 * heads * dk), F32)
        w_gate = w_gate.at[rope:rope + rank, :heads * dk].set(w_gate_f[l])
        w_gate = w_gate.at[rope + rank:rope + 2 * rank, heads * dk:].set(w_gate_b[l])
        b_gate = jnp.concatenate([b_gate_f[l], b_gate_b[l]]).reshape(1, 2 * heads * dk)
        kr_self, la = pl.pallas_call(
            _prep_kernel, grid=(t // tm,),
            in_specs=[pl.BlockSpec((tm, 2 * LANE), lambda i: (i, n_main // (2 * LANE))),
                      pl.BlockSpec((tm, LANE), lambda i: (pos_of_tile(i), 0)),
                      pl.BlockSpec((tm, LANE), lambda i: (pos_of_tile(i), 0)),
                      pl.BlockSpec((LANE, 2 * heads * dk), lambda i: (0, 0)),
                      pl.BlockSpec((1, 2 * heads * dk), lambda i: (0, 0))],
            out_specs=[pl.BlockSpec((tm, LANE), lambda i: (i, 0)),
                       pl.BlockSpec((tm, 2 * heads * dk), lambda i: (i, 0))],
            out_shape=[jax.ShapeDtypeStruct((t, LANE), BF16),
                       jax.ShapeDtypeStruct((t, 2 * heads * dk), F32)],
            compiler_params=_params(1), name=f"prep{l}")(proj, cos_k, sin_k, w_gate.astype(BF16), b_gate)

        wq = w_uq[l].reshape(q_lora, MLA_HEADS, hq)
        wq_r = wq[:, :, QK_NOPE:]
        pad = jnp.zeros((q_lora, MLA_HEADS, LANE - rope), F32)
        as_cols = lambda a: a.reshape(q_lora, -1).astype(BF16)
        wq_n = as_cols(wq[:, :, :QK_NOPE])
        wq_s = as_cols(jnp.concatenate([_swap_pairs(wq_r, rope), pad], axis=2))
        wq_r = as_cols(jnp.concatenate([wq_r, pad], axis=2))
        q_in = [(proj, q_lora, col[0] // q_lora)]
        q_norm = lambda k, a, e: _rms(a, e[0])
        g_q_extra = (g_q[l].reshape(1, q_lora), (1, q_lora), lambda j, i: (0, 0))
        tn_q = _tile(MLA_HEADS * LANE, 1024)

        def rope_epi(accs, e, tn_q=tn_q):
            cos, sin = e[1], e[2]
            return jnp.concatenate([accs[0][:, s:s + LANE] * cos + accs[1][:, s:s + LANE] * sin
                                    for s in range(0, tn_q, LANE)], axis=1)

        q_nope = _matmul(q_in, [(wq_n, ())], n_out=MLA_HEADS * QK_NOPE, tm=tm_s,
                         tn=_tile(MLA_HEADS * QK_NOPE, 1024), out_dtype=BF16, epilogue=_first,
                         name=f"w_uq_nope{l}", prologue=q_norm, extras=[g_q_extra])
        q_rope = _matmul(q_in, [(wq_r, ()), (wq_s, ())], a_of_w=[0, 0], n_out=MLA_HEADS * LANE,
                         tm=tm_s, tn=tn_q, out_dtype=BF16, epilogue=rope_epi,
                         name=f"w_uq_rope{l}", prologue=q_norm,
                         extras=[g_q_extra,
                                 (cos_k, (tm_s, LANE), lambda j, i: (pos_of_tile(i, tm_s), 0)),
                                 (sin_k, (tm_s, LANE), lambda j, i: (pos_of_tile(i, tm_s), 0))])

        c_kv = _norm_cols(proj, kv_lora, col[1] // kv_lora, g_kv[l], tm, f"ckv_norm{l}")
        tn_kv = _tile(MLA_HEADS * hw, 1024)
        kv_self = _matmul([(c_kv, kv_lora, 0)], [(w_ukv, (l,))], n_out=MLA_HEADS * hw, tm=tm_s,
                          tn=tn_kv, out_dtype=BF16, epilogue=_first, name=f"w_ukv{l}")
        ckv_cache = cache_ckv[:, l].reshape(n_lat_b * past, kv_lora)
        kv_cache = _matmul([(ckv_cache, kv_lora, 0)], [(w_ukv, (l,))], n_out=MLA_HEADS * hw,
                           tm=_tile(n_lat_b * past, 512), tn=tn_kv, out_dtype=BF16,
                           epilogue=_first, name=f"w_ukv_cache{l}")
        kr_cache = jnp.pad(cache_krope[:, l].reshape(n_lat_b * past, rope),
                           ((0, 0), (0, LANE - rope))).astype(BF16)

        scale = hq ** -0.5
        attn = _attention(q_nope, q_rope, [(kv_self, t_ctx, 0)], [(kr_self, t_ctx, 0)],
                          n_batch=n_ctx_b, t_q=t_ctx, q_row0=0, tq=_tile(t_ctx, 256),
                          n_heads=math.gcd(MLA_HEADS, 8), group=4, scale=scale, name=f"attn_ctx{l}",
                          o_prev=jnp.zeros((t, MLA_HEADS * V_HEAD), BF16))
        attn = _attention(q_nope, q_rope, [(kv_self, t_lat, n_ctx), (kv_cache, past, 0)],
                          [(kr_self, t_lat, n_ctx), (kr_cache, past, 0)], n_batch=n_lat_b,
                          t_q=t_lat, q_row0=n_ctx, tq=_tile(t_lat, 512),
                          n_heads=math.gcd(MLA_HEADS, 4), group=2, scale=scale, name=f"attn_lat{l}",
                          o_prev=attn)

        gla_kw = dict(heads=heads, dk=dk, dv=dv, col_q=col_q, col_k=col_k, col_v=col_v,
                      col_g=col_g, layer=l)
        gla, *states = _gla(proj, la, g_gla[l], n_batch=n_ctx_b, t_b=t_ctx, row0=0,
                            n_heads=math.gcd(heads, 2), name=f"gla_ctx{l}",
                            o_prev=jnp.zeros((t, heads * dv), BF16), st_prev=states,
                            st_depth=depth, **gla_kw)
        gla, = _gla(proj, la, g_gla[l], n_batch=n_lat_b, t_b=t_lat, row0=n_ctx, n_heads=1,
                    s0=(state_gla_fwd, state_gla_bwd), name=f"gla_lat{l}", o_prev=gla, **gla_kw)

        new_ckv.append(c_kv[:n_ctx].reshape(n_ctx_b, t_ctx, kv_lora))
        new_krope.append(proj[:n_ctx, n_main:n_main + rope].reshape(n_ctx_b, t_ctx, rope))

        tn_m = _tile(math.gcd(col_a, col_b, d), 1024)
        tm_m = _tile(tm, 256, 8)

        def merge_epi(accs, e):
            return jax.nn.sigmoid(e[0]) * accs[0] + jax.nn.sigmoid(e[1]) * accs[1]

        merged = _matmul([(attn, MLA_HEADS * V_HEAD, 0), (gla, heads * dv, 0)],
                         [(w_pa, (l,)), (w_pb, (l,))], n_out=d, tm=tm_m, tn=tn_m, out_dtype=BF16,
                         epilogue=merge_epi, name=f"merge{l}",
                         extras=[(proj, (tm_m, tn_m), lambda j, i: (i, col_a // tn_m + j)),
                                 (proj, (tm_m, tn_m), lambda j, i: (i, col_b // tn_m + j))])
        y = _matmul([(merged, d, 0)], [(w_o, (l,))], n_out=d, tm=tm_s, tn=_tile(d, 1024),
                    out_dtype=F32, epilogue=_first, name=f"w_o{l}")
        moe = l % 2 == 1
        x, h = _resid_norm(x, sample_of_tile, tm, f"mix_resid{l}", y=y, mod_res=mods[l],
                           g_post=g_post_mix[l], gate_idx=2, mod_pre=mods[l], g_pre=g_pre_ffn[l],
                           shift_idx=3, scale_idx=4, n_first=n_ctx_tiles)

        jx = l // 2

        def swiglu_epi(accs, e):
            return _silu(accs[0]) * accs[1]

        ffn_out = {}
        if not moe:
            d_ff = w_ff_gate.shape[2]
            ff = _matmul([(h, d, 0)], [(w_ff_gate, (jx,)), (w_ff_up, (jx,))], a_of_w=[0, 0],
                         n_out=d_ff, tm=tm_s, tn=_tile(d_ff, 512), out_dtype=BF16,
                         epilogue=swiglu_epi, name=f"ffn_up{l}")
            ffn_out["y"] = _matmul([(ff, d_ff, 0)], [(w_ff_down, (jx,))], n_out=d, tm=tm,
                                   tn=_tile(d, 512), out_dtype=F32, epilogue=_first,
                                   name=f"ffn_down{l}")
        else:
            d_ex = w_ex_gate.shape[3]
            w_r = jnp.pad(w_router[jx], ((0, 0), (0, LANE - n_experts))).astype(BF16)
            route = pl.pallas_call(
                functools.partial(_router_kernel, n_experts=n_experts), grid=(t // tm,),
                in_specs=[pl.BlockSpec((tm, d), lambda i: (i, 0)),
                          pl.BlockSpec((d, LANE), lambda i: (0, 0))],
                out_specs=pl.BlockSpec((tm, LANE), lambda i: (i, 0)),
                out_shape=jax.ShapeDtypeStruct((t, LANE), F32),
                compiler_params=_params(1), name=f"router{l}")(h, w_r)
            tile_e = _tile(t * TOP_K, EXPERT_TILE, 8)
            run_e = tile_e * EXPERT_RUN_TILES
            tok_of, dest, flags, run_info, tile_info = _routing_tables(route, n_experts, tile_e,
                                                                       run_e)
            xs = _gather(h.reshape(t, d // LANE, LANE), tok_of, flags, tile_e, f"moe_gather{l}")
            xs = xs.reshape(xs.shape[0], d)
            ff = _moe_up(xs, w_ex_gate, w_ex_up, (jx,), run_info, run=run_e, sub=tile_e,
                         tn=_tile(d_ex, 256), name=f"moe_up{l}")
            ys = _matmul([(ff, d_ex, 0)], [(w_ex_down, (jx, 0))], n_out=d, tm=tile_e,
                         tn=_tile(d, 1024), out_dtype=F32, epilogue=_first, name=f"moe_down{l}",
                         tile_expert=tile_info)
            ffn_out["routed"] = (ys, dest, route)

        if l + 1 < depth:
            x, h = _resid_norm(x, sample_of_tile, tm, f"ffn_resid{l}", mod_res=mods[l],
                               g_post=g_post_ffn[l], gate_idx=5, mod_pre=mods[l + 1],
                               g_pre=g_pre_mix[l + 1], shift_idx=0, scale_idx=1, **ffn_out)
        else:
            x, _ = _resid_norm(x, sample_of_tile, tm, f"ffn_resid{l}", mod_res=mods[l],
                               g_post=g_post_ffn[l], gate_idx=5, split_out=True,
                               n_first=n_ctx_tiles, **ffn_out)

    return (x[0].reshape(n_ctx_b, t_ctx, d), x[1].reshape(n_lat_b, t_lat, d),
            jnp.stack(new_ckv, axis=1), jnp.stack(new_krope, axis=1),
            states[0], states[1])
```

```python
import functools
import math

import jax
import jax.numpy as jnp
from jax import lax
from jax.experimental import pallas as pl
from jax.experimental.pallas import tpu as pltpu

MLA_HEADS = 16
QK_NOPE = 128
V_HEAD = 128
GRID_W = 64
ROPE_THETA = 10000.0
GATE_NORM = 16.0
GLA_CHUNK = 64
TOP_K = 2
EPS = 1e-6
LANE = 128
VMEM_LIMIT = 52 * 1024 * 1024
EXPERT_TILE = 512
EXPERT_RUN_TILES = 4

BF16 = jnp.bfloat16
F32 = jnp.float32


def _params(n_grid, **kw):
    return pltpu.CompilerParams(
        dimension_semantics=("arbitrary",) * n_grid, vmem_limit_bytes=VMEM_LIMIT, **kw)


def _tile(n, target, quantum=LANE):
    if n <= target:
        return n
    t = (target // quantum) * quantum
    while t >= quantum:
        if n % t == 0:
            return t
        t -= quantum
    return n


def _rms(x, g):
    return x * lax.rsqrt(jnp.mean(x * x, axis=-1, keepdims=True) + EPS) * g


def _silu(x):
    return x * jax.nn.sigmoid(x)


def _dot(a, b):
    return jnp.dot(a, b, preferred_element_type=F32)


def _dot_nt(a, b):
    return lax.dot_general(a, b, (((1,), (1,)), ((), ())), preferred_element_type=F32)


def _dot_tn(a, b):
    return lax.dot_general(a, b, (((0,), (0,)), ((), ())), preferred_element_type=F32)


def _mm_kernel(*refs, n_a, n_w, a_of_w, n_e, cast_w, grouped, prologue, epilogue):
    if grouped:
        te_ref, refs = refs[0], refs[1:]
    a_refs = refs[:n_a]
    w_refs = refs[n_a:n_a + n_w]
    e_refs = refs[n_a + n_w:n_a + n_w + n_e]
    o_ref = refs[n_a + n_w + n_e]
    wb_refs = refs[n_a + n_w + n_e + 1:]
    i = pl.program_id(1)

    def compute():
        if cast_w:
            fresh = i == 0
            if grouped:
                fresh = fresh | (te_ref[i] != te_ref[jnp.maximum(i - 1, 0)])

            @pl.when(fresh)
            def _():
                for w_ref, wb_ref in zip(w_refs, wb_refs):
                    wb_ref[...] = w_ref[...].astype(BF16)
            w_use = wb_refs
        else:
            w_use = w_refs
        e = [e_ref[...] for e_ref in e_refs]
        a_vals = []
        for k, a_ref in enumerate(a_refs):
            a = a_ref[...]
            if prologue is not None:
                a = prologue(k, a, e)
            a_vals.append(a.astype(BF16))
        accs = [_dot(a_vals[a_of_w[k]], w_ref[...]) for k, w_ref in enumerate(w_use)]
        o_ref[...] = epilogue(accs, e).astype(o_ref.dtype)

    if grouped:
        has_rows = te_ref[pl.num_programs(1) + i] != 0
        pl.when(has_rows)(compute)

        @pl.when(jnp.logical_not(has_rows))
        def _():
            o_ref[...] = jnp.zeros_like(o_ref)
    else:
        compute()


def _matmul(a_list, w_list, *, n_out, tm, tn, out_dtype, epilogue, name, a_of_w=None,
            prologue=None, extras=(), tile_expert=None):
    m = a_list[0][0].shape[0]
    a_of_w = a_of_w or list(range(len(w_list)))
    grouped = tile_expert is not None
    assert m % tm == 0 and n_out % tn == 0
    in_specs, args, scratch = [], [], []
    for arr, k, cb in a_list:
        if grouped:
            a_map = lambda j, i, te, cb=cb: (te[2 * (m // tm) + i], cb)
        else:
            a_map = lambda j, i, cb=cb: (i, cb)
        in_specs.append(pl.BlockSpec((tm, k), a_map))
        args.append(arr)
    cast_w = w_list[0][0].dtype != BF16
    for arr, lead in w_list:
        k = arr.shape[-2]
        if grouped:
            im = lambda j, i, te, lead=lead: tuple(lead[:-1]) + (te[i], 0, j)
        else:
            im = lambda j, i, lead=lead: tuple(lead) + (0, j)
        in_specs.append(pl.BlockSpec((None,) * len(lead) + (k, tn), im))
        args.append(arr)
        if cast_w:
            scratch.append(pltpu.VMEM((k, tn), BF16))
    for arr, bs, im in extras:
        in_specs.append(pl.BlockSpec(bs, lambda j, i, *_, im=im: im(j, i)))
        args.append(arr)
    kern = functools.partial(_mm_kernel, n_a=len(a_list), n_w=len(w_list), a_of_w=a_of_w,
                             n_e=len(extras), cast_w=cast_w, grouped=grouped,
                             prologue=prologue, epilogue=epilogue)
    grid_spec = pltpu.PrefetchScalarGridSpec(
        num_scalar_prefetch=1 if grouped else 0,
        grid=(n_out // tn, m // tm),
        in_specs=in_specs,
        out_specs=pl.BlockSpec((tm, tn), lambda j, i, *_: (i, j)),
        scratch_shapes=scratch)
    if grouped:
        args = [tile_expert] + args
    return pl.pallas_call(
        kern, grid_spec=grid_spec,
        out_shape=jax.ShapeDtypeStruct((m, n_out), out_dtype),
        compiler_params=_params(2), name=name,
    )(*args)


def _first(accs, e):
    return accs[0]


def _row_copy(src, row, dst, r, sem):
    return pltpu.make_async_copy(src.at[pl.ds(row, 1)], dst.at[pl.ds(r, 1)], sem)


def _gather_rows(idx_ref, base, stride, n, src, dst, dst_base, sem):
    def issue(r, carry):
        _row_copy(src, idx_ref[base + r * stride], dst, dst_base + r, sem).start()
        return carry

    def wait(r, carry):
        _row_copy(src, 0, dst, dst_base + r, sem).wait()
        return carry

    lax.fori_loop(0, n, issue, 0, unroll=8)
    lax.fori_loop(0, n, wait, 0, unroll=8)


def _gather_kernel(idx_ref, flag_ref, src_hbm, o_ref, sem):
    tg = o_ref.shape[0]
    i = pl.program_id(0)

    @pl.when(flag_ref[i] != 0)
    def _():
        _gather_rows(idx_ref, i * tg, 1, tg, src_hbm, o_ref, 0, sem)

    @pl.when(flag_ref[i] == 0)
    def _():
        o_ref[...] = jnp.zeros_like(o_ref)


def _gather(src, idx, flags, tg, name):
    n = idx.shape[0]
    blk = (tg,) + src.shape[1:]
    grid_spec = pltpu.PrefetchScalarGridSpec(
        num_scalar_prefetch=2, grid=(n // tg,),
        in_specs=[pl.BlockSpec(memory_space=pl.ANY)],
        out_specs=pl.BlockSpec(blk, lambda i, *_: (i,) + (0,) * (len(blk) - 1)),
        scratch_shapes=[pltpu.SemaphoreType.DMA(())])
    return pl.pallas_call(
        _gather_kernel, grid_spec=grid_spec,
        out_shape=jax.ShapeDtypeStruct((n,) + src.shape[1:], src.dtype),
        compiler_params=_params(1), name=name)(idx, flags, src)


def _resid_norm_kernel(*refs, mode, want_x, want_h, gate_idx, shift_idx, scale_idx, split_in,
                       split_out, n_first):
    it = iter(refs)
    dest_ref = next(it) if mode == "routed" else None
    x_refs = [next(it) for _ in range(2 if split_in else 1)]
    if mode == "dense":
        y_ref = next(it)
    elif mode == "routed":
        ys_hbm, route_ref = next(it), next(it)
    if mode is not None:
        mod_res_ref, g_post_ref = next(it), next(it)
    if want_h:
        mod_pre_ref, g_pre_ref = next(it), next(it)
    xo_refs = [next(it) for _ in range((2 if split_out else 1) if want_x else 0)]
    h_ref = next(it) if want_h else None
    i = pl.program_id(0)
    x = x_refs[0][...]
    if split_in:
        x = jnp.where(i < n_first, x, x_refs[1][...])
    if mode is not None:
        if mode == "routed":
            buf1, buf2, sem = next(it), next(it), next(it)
            tm = x.shape[0]
            base = i * tm * TOP_K
            _gather_rows(dest_ref, base, TOP_K, tm, ys_hbm, buf1, 0, sem)
            _gather_rows(dest_ref, base + 1, TOP_K, tm, ys_hbm, buf2, 0, sem)
            route = route_ref[...]
            y = route[:, 2:3] * buf1[...] + route[:, 3:4] * buf2[...]
        else:
            y = y_ref[...]
        gate = mod_res_ref[0, gate_idx:gate_idx + 1, :]
        x = x + gate * _rms(y, g_post_ref[...])
    if want_x and split_out:
        @pl.when(i < n_first)
        def _():
            xo_refs[0][...] = x

        @pl.when(i >= n_first)
        def _():
            xo_refs[1][...] = x
    elif want_x:
        xo_refs[0][...] = x
    if want_h:
        scale = mod_pre_ref[0, scale_idx:scale_idx + 1, :]
        shift = mod_pre_ref[0, shift_idx:shift_idx + 1, :]
        h_ref[...] = (_rms(x, g_pre_ref[...]) * (1.0 + scale) + shift).astype(h_ref.dtype)


def _resid_norm(x, sample_of_tile, tm, name, *, y=None, routed=None, mod_res=None, g_post=None,
                gate_idx=0, mod_pre=None, g_pre=None, shift_idx=0, scale_idx=0, split_out=False,
                n_first=0):
    split_in = isinstance(x, (tuple, list))
    xs_in = list(x) if split_in else [x]
    t, d = sum(a.shape[0] for a in xs_in), xs_in[0].shape[1]
    mode = "dense" if y is not None else ("routed" if routed is not None else None)
    want_h = mod_pre is not None
    want_x = mode is not None
    row = pl.BlockSpec((tm, d), lambda i, *_: (i, 0))
    first = pl.BlockSpec((tm, d), lambda i, *_: (jnp.minimum(i, n_first - 1), 0))
    rest = pl.BlockSpec((tm, d), lambda i, *_: (jnp.maximum(i - n_first, 0), 0))
    vec = pl.BlockSpec((1, d), lambda i, *_: (0, 0))
    mod = pl.BlockSpec((1, 6, d), lambda i, *_: (sample_of_tile(i), 0, 0))
    in_specs, args = ([first, rest] if split_in else [row]), xs_in
    out_specs, out_shape, scratch = [], [], []
    if mode == "dense":
        in_specs.append(row)
        args.append(y)
    elif mode == "routed":
        ys, dest, route = routed
        in_specs += [pl.BlockSpec(memory_space=pl.ANY),
                     pl.BlockSpec((tm, LANE), lambda i, *_: (i, 0))]
        args += [ys, route]
        scratch = [pltpu.VMEM((tm, d), F32), pltpu.VMEM((tm, d), F32),
                   pltpu.SemaphoreType.DMA(())]
    if mode is not None:
        in_specs += [mod, vec]
        args += [mod_res, g_post.reshape(1, d)]
    if want_h:
        in_specs += [mod, vec]
        args += [mod_pre, g_pre.reshape(1, d)]
    if want_x and split_out:
        out_specs += [first, rest]
        out_shape += [jax.ShapeDtypeStruct((n_first * tm, d), F32),
                      jax.ShapeDtypeStruct((t - n_first * tm, d), F32)]
    elif want_x:
        out_specs.append(row)
        out_shape.append(jax.ShapeDtypeStruct((t, d), F32))
    if want_h:
        out_specs.append(row)
        out_shape.append(jax.ShapeDtypeStruct((t, d), BF16))
    kern = functools.partial(_resid_norm_kernel, mode=mode, want_x=want_x, want_h=want_h,
                             gate_idx=gate_idx, shift_idx=shift_idx, scale_idx=scale_idx,
                             split_in=split_in, split_out=split_out, n_first=n_first)
    grid_spec = pltpu.PrefetchScalarGridSpec(
        num_scalar_prefetch=1 if mode == "routed" else 0, grid=(t // tm,),
        in_specs=in_specs, out_specs=out_specs, scratch_shapes=scratch)
    if mode == "routed":
        args = [dest] + args
    outs = pl.pallas_call(kern, grid_spec=grid_spec, out_shape=out_shape,
                          compiler_params=_params(1), name=name)(*args)
    n_x = (2 if split_out else 1) if want_x else 0
    x_new = (tuple(outs[:2]) if split_out else outs[0]) if want_x else x
    h = outs[n_x] if want_h else None
    return x_new, h


def _norm_kernel(x_ref, g_ref, o_ref):
    o_ref[...] = _rms(x_ref[...], g_ref[...])


def _norm_cols(arr, width, col_block, g, tm, name):
    t = arr.shape[0]
    return pl.pallas_call(
        _norm_kernel, grid=(t // tm,),
        in_specs=[pl.BlockSpec((tm, width), lambda i: (i, col_block)),
                  pl.BlockSpec((1, width), lambda i: (0, 0))],
        out_specs=pl.BlockSpec((tm, width), lambda i: (i, 0)),
        out_shape=jax.ShapeDtypeStruct((t, width), F32),
        compiler_params=_params(1), name=name)(arr, g.reshape(1, width))


def _swap_pairs(w, rope):
    q = rope // 4
    return jnp.concatenate([w[..., q:2 * q], w[..., :q], w[..., 3 * q:], w[..., 2 * q:3 * q]],
                           axis=-1)


def _prep_kernel(misc_ref, cos_ref, sin_ref, wg_ref, bg_ref, kr_ref, la_ref):
    misc = misc_ref[:, :LANE]
    kr = misc * cos_ref[...] + misc_ref[:, LANE:] * sin_ref[...]
    kr_ref[...] = kr.astype(kr_ref.dtype)
    z = _dot(misc.astype(BF16), wg_ref[...]) + bg_ref[...]
    log_sig = jnp.minimum(z, 0.0) - jnp.log1p(jnp.exp(-jnp.abs(z)))
    la_ref[...] = log_sig * (1.0 / GATE_NORM)


def _attn_kernel(*refs, n_seg, n_heads, group, scale):
    qn_ref, qr_ref = refs[:2]
    kv_refs = refs[2:2 + n_seg]
    kr_refs = refs[2 + n_seg:2 + 2 * n_seg]
    o_ref = refs[-1]
    hw = QK_NOPE + V_HEAD
    krs = [kr_ref[...] for kr_ref in kr_refs]
    add = lambda a, b: a + b
    c = scale * math.log2(math.e)
    for h0 in range(0, n_heads, group):
        hds = range(h0, min(h0 + group, n_heads))
        scores = []
        for hd in hds:
            qn = qn_ref[:, hd * QK_NOPE:(hd + 1) * QK_NOPE]
            qr = qr_ref[:, hd * LANE:(hd + 1) * LANE]
            scores.append([_dot_nt(qn, kv_ref[:, hd * hw:hd * hw + QK_NOPE]) + _dot_nt(qr, kr)
                           for kv_ref, kr in zip(kv_refs, krs)])
        ms = [functools.reduce(jnp.maximum, [jnp.max(s, axis=-1, keepdims=True) for s in ss])
              for ss in scores]
        ps = [[jnp.exp2((s - m) * c) for s in ss] for ss, m in zip(scores, ms)]
        invs = [1.0 / functools.reduce(add, [jnp.sum(p, axis=-1, keepdims=True) for p in pp])
                for pp in ps]
        for hd, pp, inv in zip(hds, ps, invs):
            o = functools.reduce(add, [
                _dot((p * inv).astype(BF16), kv_ref[:, hd * hw + QK_NOPE:(hd + 1) * hw])
                for p, kv_ref in zip(pp, kv_refs)])
            o_ref[:, hd * V_HEAD:(hd + 1) * V_HEAD] = o.astype(o_ref.dtype)


def _attention(q_nope, q_rope, kvs, krs, *, n_batch, t_q, q_row0, tq, n_heads, group, scale, name,
               o_prev):
    hw = QK_NOPE + V_HEAD
    nq = t_q // tq
    q_map = lambda b, g, i: (q_row0 // tq + b * nq + i, g)
    in_specs = [pl.BlockSpec((tq, n_heads * QK_NOPE), q_map),
                pl.BlockSpec((tq, n_heads * LANE), q_map)]
    args = [q_nope, q_rope]
    for arr, tk, row0 in kvs:
        in_specs.append(pl.BlockSpec((tk, n_heads * hw),
                                     lambda b, g, i, tk=tk, row0=row0: (row0 // tk + b, g)))
        args.append(arr)
    for arr, tk, row0 in krs:
        in_specs.append(pl.BlockSpec((tk, LANE),
                                     lambda b, g, i, tk=tk, row0=row0: (row0 // tk + b, 0)))
        args.append(arr)
    aliases = {len(args): 0}
    in_specs.append(pl.BlockSpec(memory_space=pl.ANY))
    args.append(o_prev)
    kern = functools.partial(_attn_kernel, n_seg=len(kvs), n_heads=n_heads, group=group,
                             scale=scale)
    return pl.pallas_call(
        kern, grid=(n_batch, MLA_HEADS // n_heads, nq), in_specs=in_specs,
        out_specs=pl.BlockSpec((tq, n_heads * V_HEAD), q_map),
        out_shape=jax.ShapeDtypeStruct(o_prev.shape, o_prev.dtype),
        input_output_aliases=aliases,
        compiler_params=_params(3), name=name)(*args)


def _split3(x):
    hi = x.astype(BF16)
    r1 = x - hi.astype(F32)
    mid = r1.astype(BF16)
    lo = (r1 - mid.astype(F32)).astype(BF16)
    return hi, mid, lo


def _gla_kernel(*refs, has_init, want_states, n_alias, n_heads, q_scale, layer, all_layers):
    it = iter(refs)
    q_ref, k_ref, v_ref, laf_ref, lab_ref, gout_ref, ggla_ref = (next(it) for _ in range(7))
    s0_refs = (next(it), next(it)) if has_init else None
    for _ in range(n_alias):
        next(it)
    o_ref = next(it)
    s_out_refs = (next(it), next(it)) if want_states else None
    st, vb, qd_s, ks_s, dec_s, oacc = (next(it) for _ in range(6))
    c = GLA_CHUNK
    t_b = q_ref.shape[0]
    dk, dv = q_ref.shape[1] // n_heads, v_ref.shape[1] // n_heads
    n_chunks = t_b // c
    rb = min(t_b, 4 * c)
    row = lax.broadcasted_iota(jnp.int32, (rb, rb), 0)
    col = lax.broadcasted_iota(jnp.int32, (rb, rb), 1)
    same = (row // c) == (col // c)
    masks = (same & (col <= row), same & (col >= row))
    tris = [jnp.where(m, 1.0, 0.0).astype(BF16) for m in masks]
    la_refs = (laf_ref, lab_ref)
    streams = [(g, d) for g in range(n_heads) for d in range(2)]
    kcols = [pl.ds(g * dk, dk) for g in range(n_heads)]
    vcols = [pl.ds(g * dv, dv) for g in range(n_heads)]

    vb[...] = v_ref[...].astype(BF16)
    for s, (g, d) in enumerate(streams):
        st[s] = s0_refs[d][0, g].T if has_init else jnp.zeros(st.shape[1:], F32)

    for r0 in range(0, t_b, rb):
        rows = pl.ds(r0, rb)
        for s, (g, d) in enumerate(streams):
            hi, mid, lo = _split3(la_refs[d][rows, kcols[g]])
            b = (_dot(tris[d], hi) + _dot(tris[d], mid)) + _dot(tris[d], lo)
            end = c - 1 if d == 0 else 0
            b_end = jnp.concatenate(
                [jnp.broadcast_to(b[k * c + end:k * c + end + 1, :], (c, dk))
                 for k in range(rb // c)], axis=0)
            kc = k_ref[rows, kcols[g]]
            qd = (q_ref[rows, kcols[g]] * q_scale * jnp.exp(b)).astype(BF16)
            kd = (kc * jnp.exp(-b)).astype(BF16)
            qd_s[s, rows, :] = qd
            ks_s[s, rows, :] = (kc * jnp.exp(b_end - b)).astype(BF16)
            dec_s[s, rows, :] = jnp.exp(b_end)
            a = jnp.where(masks[d], _dot_nt(qd, kd), 0.0).astype(BF16)
            o_in = _dot(a, vb[rows, vcols[g]])
            if d == 0:
                oacc[rows, vcols[g]] = o_in
            else:
                oacc[rows, vcols[g]] += o_in

    for i in range(n_chunks):
        for s, (g, d) in enumerate(streams):
            c0 = (i if d == 0 else n_chunks - 1 - i) * c
            rows = pl.ds(c0, c)
            u = _dot_tn(vb[rows, vcols[g]], ks_s[s, rows, :])
            s_t = st[s]
            oacc[rows, vcols[g]] += _dot_nt(qd_s[s, rows, :], s_t.astype(BF16))
            st[s] = s_t * dec_s[s, pl.ds(c0, 1), :] + u

    for g in range(n_heads):
        o_ref[:, vcols[g]] = (_rms(oacc[:, vcols[g]], ggla_ref[...])
                              * _silu(gout_ref[:, vcols[g]])).astype(o_ref.dtype)
    if want_states:
        for s, (g, d) in enumerate(streams):
            s_fin = st[s].T
            if all_layers:
                for ll in range(s_out_refs[d].shape[1]):
                    s_out_refs[d][0, ll, g] = s_fin if ll == layer else jnp.zeros_like(s_fin)
            else:
                s_out_refs[d][0, g] = s_fin


def _gla(proj, la, g_gla, *, n_batch, t_b, row0, heads, n_heads, dk, dv, col_q, col_k, col_v,
         col_g, name, o_prev, s0=None, layer=0, st_prev=None, st_depth=0):
    has_init = s0 is not None
    rb = row0 // t_b
    wk, wv = n_heads * dk, n_heads * dv
    assert col_q % wk == 0 and col_k % wk == 0 and col_v % wv == 0 and col_g % wv == 0
    in_specs = [
        pl.BlockSpec((t_b, wk), lambda b, h: (rb + b, col_q // wk + h)),
        pl.BlockSpec((t_b, wk), lambda b, h: (rb + b, col_k // wk + h)),
        pl.BlockSpec((t_b, wv), lambda b, h: (rb + b, col_v // wv + h)),
        pl.BlockSpec((t_b, wk), lambda b, h: (rb + b, h)),
        pl.BlockSpec((t_b, wk), lambda b, h: (rb + b, heads // n_heads + h)),
        pl.BlockSpec((t_b, wv), lambda b, h: (rb + b, col_g // wv + h)),
        pl.BlockSpec((1, dv), lambda b, h: (0, 0)),
    ]
    args = [proj, proj, proj, la, la, proj, g_gla.reshape(1, dv)]
    st_spec = pl.BlockSpec((1, None, n_heads, dk, dv), lambda b, h: (b, layer, h, 0, 0))
    if has_init:
        in_specs += [st_spec, st_spec]
        args += list(s0)
    out_specs = [pl.BlockSpec((t_b, wv), lambda b, h: (rb + b, h))]
    out_shape = [jax.ShapeDtypeStruct(o_prev.shape, o_prev.dtype)]
    aliases = {len(args): 0}
    in_specs.append(pl.BlockSpec(memory_space=pl.ANY))
    args.append(o_prev)
    want_states = st_prev is not None or st_depth > 0
    if st_prev is not None:
        out_specs += [st_spec, st_spec]
        for k, arr in enumerate(st_prev):
            out_shape.append(jax.ShapeDtypeStruct(arr.shape, arr.dtype))
            aliases[len(args)] = 1 + k
            in_specs.append(pl.BlockSpec(memory_space=pl.ANY))
            args.append(arr)
    elif want_states:
        out_specs += [pl.BlockSpec((1, st_depth, n_heads, dk, dv),
                                   lambda b, h: (b, 0, h, 0, 0))] * 2
        out_shape += [jax.ShapeDtypeStruct((n_batch, st_depth, heads, dk, dv), F32)] * 2
    kern = functools.partial(_gla_kernel, has_init=has_init, want_states=want_states,
                             n_alias=len(aliases), n_heads=n_heads, q_scale=dk ** -0.5,
                             layer=layer, all_layers=st_prev is None)
    n_str = 2 * n_heads
    return pl.pallas_call(
        kern, grid=(n_batch, heads // n_heads), in_specs=in_specs, out_specs=out_specs,
        out_shape=out_shape, input_output_aliases=aliases,
        scratch_shapes=[pltpu.VMEM((n_str, dv, dk), F32), pltpu.VMEM((t_b, wv), BF16),
                        pltpu.VMEM((n_str, t_b, dk), BF16), pltpu.VMEM((n_str, t_b, dk), BF16),
                        pltpu.VMEM((n_str, t_b, dk), F32), pltpu.VMEM((t_b, wv), F32)],
        compiler_params=_params(2), name=name)(*args)


def _router_kernel(h_ref, w_ref, o_ref, *, n_experts):
    logits = _dot(h_ref[...].astype(BF16), w_ref[...])
    lane = lax.broadcasted_iota(jnp.int32, logits.shape, 1).astype(F32)
    neg = jnp.float32(-jnp.inf)
    lg = jnp.where(lane < n_experts, logits, neg)
    m1 = jnp.max(lg, axis=-1, keepdims=True)
    i1 = jnp.min(jnp.where(lg == m1, lane, LANE), axis=-1, keepdims=True)
    lg2 = jnp.where(lane == i1, neg, lg)
    m2 = jnp.max(lg2, axis=-1, keepdims=True)
    i2 = jnp.min(jnp.where(lg2 == m2, lane, LANE), axis=-1, keepdims=True)
    e2 = jnp.exp(m2 - m1)
    inv = 1.0 / (1.0 + e2)
    o_ref[...] = (jnp.where(lane == 0, i1, 0.0) + jnp.where(lane == 1, i2, 0.0)
                  + jnp.where(lane == 2, inv, 0.0) + jnp.where(lane == 3, e2 * inv, 0.0))


def _moe_up_kernel(info_ref, x_ref, wg_ref, wu_ref, o_ref, *, n_sub):
    s = pl.program_id(0)
    flag0 = pl.num_programs(0) + s * n_sub
    sub = x_ref.shape[0] // n_sub

    @pl.when(info_ref[flag0] != 0)
    def _():
        wg = wg_ref[...].astype(BF16)
        wu = wu_ref[...].astype(BF16)
        for k in range(n_sub):
            rows = pl.ds(k * sub, sub)

            @pl.when(info_ref[flag0 + k] != 0)
            def _():
                a = x_ref[rows, :]
                o_ref[rows, :] = (_silu(_dot(a, wg)) * _dot(a, wu)).astype(o_ref.dtype)

            @pl.when(info_ref[flag0 + k] == 0)
            def _():
                o_ref[rows, :] = jnp.zeros((sub, o_ref.shape[1]), o_ref.dtype)

    @pl.when(info_ref[flag0] == 0)
    def _():
        o_ref[...] = jnp.zeros_like(o_ref)


def _moe_up(xs, w_gate, w_up, lead, info, *, run, sub, tn, name):
    n_rows, d = xs.shape
    n_out = w_gate.shape[-1]
    n_run = n_rows // run
    n_sub = run // sub

    def w_map(s, j, info):
        return tuple(lead) + (info[s], 0, jnp.where(info[n_run + s * n_sub] != 0, j, 0))

    w_spec = pl.BlockSpec((None,) * (len(lead) + 1) + (d, tn), w_map)
    grid_spec = pltpu.PrefetchScalarGridSpec(
        num_scalar_prefetch=1, grid=(n_run, n_out // tn),
        in_specs=[pl.BlockSpec((run, d), lambda s, j, info: (s, 0)), w_spec, w_spec],
        out_specs=pl.BlockSpec((run, tn), lambda s, j, info: (s, j)))
    return pl.pallas_call(
        functools.partial(_moe_up_kernel, n_sub=n_sub), grid_spec=grid_spec,
        out_shape=jax.ShapeDtypeStruct((n_rows, n_out), BF16),
        compiler_params=_params(2), name=name)(info, xs, w_gate, w_up)


def _routing_tables(route, n_experts, tile, run):
    t = route.shape[0]
    n_assign = t * TOP_K
    n_rows = -(-(n_assign + n_experts * (run - 1)) // run) * run
    e_flat = route[:, :TOP_K].astype(jnp.int32).reshape(n_assign)
    onehot = (e_flat[:, None] == jnp.arange(n_experts, dtype=jnp.int32)[None, :]).astype(jnp.int32)
    csum = jnp.cumsum(onehot, axis=0)
    rank = jnp.take_along_axis(csum, e_flat[:, None], axis=1)[:, 0] - 1
    counts = csum[-1]
    padded = ((counts + run - 1) // run) * run
    ends = jnp.cumsum(padded)
    starts = ends - padded
    dest = starts[e_flat] + rank
    tok_of = jnp.zeros((n_rows,), jnp.int32).at[dest].set(
        jnp.arange(n_assign, dtype=jnp.int32) // TOP_K, unique_indices=True)
    tile_start = jnp.arange(n_rows // tile, dtype=jnp.int32) * tile
    tile_expert = jnp.minimum(
        jnp.sum(tile_start[:, None] >= ends[None, :], axis=1, dtype=jnp.int32), n_experts - 1)
    flags = (tile_start < (starts + counts)[tile_expert]).astype(jnp.int32)
    run_info = jnp.concatenate([tile_expert[::run // tile], flags])
    tile_idx = jnp.arange(n_rows // tile, dtype=jnp.int32)
    tile_info = jnp.concatenate([tile_expert, flags,
                                 lax.cummax(jnp.where(flags != 0, tile_idx, 0))])
    return tok_of, dest.astype(jnp.int32), flags, run_info, tile_info


def _rope_tables(n_lat, rope, tm_id):
    axis_half = rope // 4
    rows = n_lat // GRID_W
    r = jnp.repeat(jnp.arange(rows, dtype=F32), GRID_W)
    c = jnp.tile(jnp.arange(GRID_W, dtype=F32), rows)
    inv = ROPE_THETA ** (-jnp.arange(axis_half, dtype=F32) / axis_half)
    ar, ac = r[:, None] * inv, c[:, None] * inv
    cos = jnp.concatenate([jnp.cos(ar), jnp.cos(ar), jnp.cos(ac), jnp.cos(ac)], axis=1)
    sin = jnp.concatenate([-jnp.sin(ar), jnp.sin(ar), -jnp.sin(ac), jnp.sin(ac)], axis=1)
    cos = jnp.concatenate([jnp.ones((tm_id, rope), F32), cos], axis=0)
    sin = jnp.concatenate([jnp.zeros((tm_id, rope), F32), sin], axis=0)
    return cos, sin


def kernel(x_prompt, x_sample, cache_ckv, cache_krope, state_gla_fwd, state_gla_bwd, c, c_ctx, w_ada, b_ada, g_pre_mix, g_post_mix, g_pre_ffn, g_post_ffn, w_in, g_q, w_uq, g_kv, w_ukv, w_gate_f, b_gate_f, w_gate_b, b_gate_b, g_gla, w_pa, w_pb, w_o, w_ff_gate, w_ff_up, w_ff_down, w_router, w_ex_gate, w_ex_up, w_ex_down):
    n_ctx_b, t_ctx, d = x_prompt.shape
    n_lat_b, t_lat, _ = x_sample.shape
    depth = w_in.shape[0]
    past = cache_ckv.shape[2]
    q_lora, kv_lora = g_q.shape[1], g_kv.shape[1]
    rope = cache_krope.shape[3]
    heads, dk, dv = state_gla_fwd.shape[2:]
    rank = w_gate_f.shape[1]
    n_experts = w_router.shape[2]
    hq = QK_NOPE + rope
    hw = QK_NOPE + V_HEAD
    n_ctx, n_lat = n_ctx_b * t_ctx, n_lat_b * t_lat
    t = n_ctx + n_lat
    assert 2 * rope == LANE and 2 * rank <= LANE - rope

    tm = _tile(math.gcd(n_ctx, t_lat), 512)
    tm_s = _tile(math.gcd(n_ctx, t_lat), 1024)
    n_ctx_tiles = n_ctx // tm
    lat_tiles = t_lat // tm

    def sample_of_tile(i):
        return jnp.where(i < n_ctx_tiles, 0, 1 + (i - n_ctx_tiles) // lat_tiles)

    def pos_of_tile(i, tile=tm):
        first = n_ctx // tile
        return jnp.where(i < first, 0, tm_s // tile + (i - first) % (t_lat // tile))

    sizes = (q_lora, kv_lora, rope, heads * dk, heads * dk, heads * dv, rank, rank,
             heads * dv, d, d)
    offs = [0]
    for s in sizes:
        offs.append(offs[-1] + s)
    main_groups = (0, 1, 3, 4, 5, 8, 9, 10)
    col = {}
    acc = 0
    for gidx in main_groups:
        col[gidx] = acc
        acc += sizes[gidx]
    n_main = acc
    col_q, col_k, col_v, col_g, col_a, col_b = col[3], col[4], col[5], col[8], col[9], col[10]

    cos_r, sin_r = _rope_tables(t_lat, rope, tm_s)
    n_tab = cos_r.shape[0]
    cos_k = jnp.concatenate([cos_r, jnp.zeros((n_tab, LANE - rope), F32)], axis=1)
    sin_k = jnp.concatenate([sin_r, jnp.zeros((n_tab, LANE - rope), F32)], axis=1)

    x = (x_prompt.reshape(n_ctx, d), x_sample.reshape(n_lat, d))
    c_all = jnp.concatenate([c_ctx[None, :], c, jnp.zeros((8 - 1 - n_lat_b, d), F32)], axis=0)

    def modulation(l):
        def epi(accs, e):
            return accs[0] + e[0]
        tn = _tile(6 * d, 1024)
        m = _matmul([(c_all, d, 0)], [(w_ada, (l,))], n_out=6 * d, tm=8, tn=tn, out_dtype=F32,
                    epilogue=epi, prologue=lambda k, a, e: _silu(a), name=f"ada{l}",
                    extras=[(b_ada.reshape(depth, 1, 6 * d), (None, 1, tn),
                             lambda j, i, l=l: (l, 0, j))])
        return m.reshape(8, 6, d)

    mods = [modulation(l) for l in range(depth)]
    _, h = _resid_norm(x, sample_of_tile, tm, "prenorm0", mod_pre=mods[0], g_pre=g_pre_mix[0],
                       shift_idx=0, scale_idx=1, n_first=n_ctx_tiles)

    new_ckv, new_krope = [], []
    states = None
    for l in range(depth):
        n_all = n_main + 2 * LANE

        def repack_kernel(w_ref, o_ref):
            w_kr = w_ref[:, offs[2]:offs[3]]
            rows = w_ref.shape[0]
            pieces = [w_ref[:, offs[gidx]:offs[gidx + 1]] for gidx in main_groups]
            pieces += [w_kr, w_ref[:, offs[6]:offs[8]],
                       jnp.zeros((rows, LANE - rope - 2 * rank), F32),
                       _swap_pairs(w_kr, rope), jnp.zeros((rows, LANE - rope), F32)]
            o_ref[...] = jnp.concatenate(pieces, axis=1).astype(BF16)

        tk_r = _tile(d, 128, 8)
        w_all = pl.pallas_call(
            repack_kernel, grid=(d // tk_r,),
            in_specs=[pl.BlockSpec((None, tk_r, offs[-1]), lambda i, l=l: (l, i, 0))],
            out_specs=pl.BlockSpec((tk_r, n_all), lambda i: (i, 0)),
            out_shape=jax.ShapeDtypeStruct((d, n_all), BF16),
            compiler_params=_params(1), name=f"w_in_repack{l}")(w_in)
        proj = _matmul([(h, d, 0)], [(w_all, ())], n_out=n_all, tm=tm_s, tn=_tile(n_all, 1280),
                       out_dtype=F32, epilogue=_first, name=f"w_in{l}")

        w_gate = jnp.zeros((LANE, 2 * heads * dk), F32)
        w_gate = w_gate.at[rope:rope + rank, :heads * dk].set(w_gate_f[l])
        w_gate = w_gate.at[rope + rank:rope + 2 * rank, heads * dk:].set(w_gate_b[l])
        b_gate = jnp.concatenate([b_gate_f[l], b_gate_b[l]]).reshape(1, 2 * heads * dk)
        kr_self, la = pl.pallas_call(
            _prep_kernel, grid=(t // tm,),
            in_specs=[pl.BlockSpec((tm, 2 * LANE), lambda i: (i, n_main // (2 * LANE))),
                      pl.BlockSpec((tm, LANE), lambda i: (pos_of_tile(i), 0)),
                      pl.BlockSpec((tm, LANE), lambda i: (pos_of_tile(i), 0)),
                      pl.BlockSpec((LANE, 2 * heads * dk), lambda i: (0, 0)),
                      pl.BlockSpec((1, 2 * heads * dk), lambda i: (0, 0))],
            out_specs=[pl.BlockSpec((tm, LANE), lambda i: (i, 0)),
                       pl.BlockSpec((tm, 2 * heads * dk), lambda i: (i, 0))],
            out_shape=[jax.ShapeDtypeStruct((t, LANE), BF16),
                       jax.ShapeDtypeStruct((t, 2 * heads * dk), F32)],
            compiler_params=_params(1), name=f"prep{l}")(proj, cos_k, sin_k, w_gate.astype(BF16), b_gate)

        wq = w_uq[l].reshape(q_lora, MLA_HEADS, hq)
        wq_r = wq[:, :, QK_NOPE:]
        pad = jnp.zeros((q_lora, MLA_HEADS, LANE - rope), F32)
        as_cols = lambda a: a.reshape(q_lora, -1).astype(BF16)
        wq_n = as_cols(wq[:, :, :QK_NOPE])
        wq_s = as_cols(jnp.concatenate([_swap_pairs(wq_r, rope), pad], axis=2))
        wq_r = as_cols(jnp.concatenate([wq_r, pad], axis=2))
        q_in = [(proj, q_lora, col[0] // q_lora)]
        q_norm = lambda k, a, e: _rms(a, e[0])
        g_q_extra = (g_q[l].reshape(1, q_lora), (1, q_lora), lambda j, i: (0, 0))
        tn_q = _tile(MLA_HEADS * LANE, 1024)

        def rope_epi(accs, e, tn_q=tn_q):
            cos, sin = e[1], e[2]
            return jnp.concatenate([accs[0][:, s:s + LANE] * cos + accs[1][:, s:s + LANE] * sin
                                    for s in range(0, tn_q, LANE)], axis=1)

        q_nope = _matmul(q_in, [(wq_n, ())], n_out=MLA_HEADS * QK_NOPE, tm=tm_s,
                         tn=_tile(MLA_HEADS * QK_NOPE, 1024), out_dtype=BF16, epilogue=_first,
                         name=f"w_uq_nope{l}", prologue=q_norm, extras=[g_q_extra])
        q_rope = _matmul(q_in, [(wq_r, ()), (wq_s, ())], a_of_w=[0, 0], n_out=MLA_HEADS * LANE,
                         tm=tm_s, tn=tn_q, out_dtype=BF16, epilogue=rope_epi,
                         name=f"w_uq_rope{l}", prologue=q_norm,
                         extras=[g_q_extra,
                                 (cos_k, (tm_s, LANE), lambda j, i: (pos_of_tile(i, tm_s), 0)),
                                 (sin_k, (tm_s, LANE), lambda j, i: (pos_of_tile(i, tm_s), 0))])

        c_kv = _norm_cols(proj, kv_lora, col[1] // kv_lora, g_kv[l], tm, f"ckv_norm{l}")
        tn_kv = _tile(MLA_HEADS * hw, 1024)
        kv_self = _matmul([(c_kv, kv_lora, 0)], [(w_ukv, (l,))], n_out=MLA_HEADS * hw, tm=tm_s,
                          tn=tn_kv, out_dtype=BF16, epilogue=_first, name=f"w_ukv{l}")
        ckv_cache = cache_ckv[:, l].reshape(n_lat_b * past, kv_lora)
        kv_cache = _matmul([(ckv_cache, kv_lora, 0)], [(w_ukv, (l,))], n_out=MLA_HEADS * hw,
                           tm=_tile(n_lat_b * past, 512), tn=tn_kv, out_dtype=BF16,
                           epilogue=_first, name=f"w_ukv_cache{l}")
        kr_cache = jnp.pad(cache_krope[:, l].reshape(n_lat_b * past, rope),
                           ((0, 0), (0, LANE - rope))).astype(BF16)

        scale = hq ** -0.5
        attn = _attention(q_nope, q_rope, [(kv_self, t_ctx, 0)], [(kr_self, t_ctx, 0)],
                          n_batch=n_ctx_b, t_q=t_ctx, q_row0=0, tq=_tile(t_ctx, 256),
                          n_heads=math.gcd(MLA_HEADS, 8), group=4, scale=scale, name=f"attn_ctx{l}",
                          o_prev=jnp.zeros((t, MLA_HEADS * V_HEAD), BF16))
        attn = _attention(q_nope, q_rope, [(kv_self, t_lat, n_ctx), (kv_cache, past, 0)],
                          [(kr_self, t_lat, n_ctx), (kr_cache, past, 0)], n_batch=n_lat_b,
                          t_q=t_lat, q_row0=n_ctx, tq=_tile(t_lat, 512),
                          n_heads=math.gcd(MLA_HEADS, 4), group=2, scale=scale, name=f"attn_lat{l}",
                          o_prev=attn)

        gla_kw = dict(heads=heads, dk=dk, dv=dv, col_q=col_q, col_k=col_k, col_v=col_v,
                      col_g=col_g, layer=l)
        gla, *states = _gla(proj, la, g_gla[l], n_batch=n_ctx_b, t_b=t_ctx, row0=0,
                            n_heads=math.gcd(heads, 2), name=f"gla_ctx{l}",
                            o_prev=jnp.zeros((t, heads * dv), BF16), st_prev=states,
                            st_depth=depth, **gla_kw)
        gla, = _gla(proj, la, g_gla[l], n_batch=n_lat_b, t_b=t_lat, row0=n_ctx, n_heads=1,
                    s0=(state_gla_fwd, state_gla_bwd), name=f"gla_lat{l}", o_prev=gla, **gla_kw)

        new_ckv.append(c_kv[:n_ctx].reshape(n_ctx_b, t_ctx, kv_lora))
        new_krope.append(proj[:n_ctx, n_main:n_main + rope].reshape(n_ctx_b, t_ctx, rope))

        tn_m = _tile(math.gcd(col_a, col_b, d), 1024)
        tm_m = _tile(tm, 256, 8)

        def merge_epi(accs, e):
            return jax.nn.sigmoid(e[0]) * accs[0] + jax.nn.sigmoid(e[1]) * accs[1]

        merged = _matmul([(attn, MLA_HEADS * V_HEAD, 0), (gla, heads * dv, 0)],
                         [(w_pa, (l,)), (w_pb, (l,))], n_out=d, tm=tm_m, tn=tn_m, out_dtype=BF16,
                         epilogue=merge_epi, name=f"merge{l}",
                         extras=[(proj, (tm_m, tn_m), lambda j, i: (i, col_a // tn_m + j)),
                                 (proj, (tm_m, tn_m), lambda j, i: (i, col_b // tn_m + j))])
        y = _matmul([(merged, d, 0)], [(w_o, (l,))], n_out=d, tm=tm_s, tn=_tile(d, 1024),
                    out_dtype=F32, epilogue=_first, name=f"w_o{l}")
        moe = l % 2 == 1
        x, h = _resid_norm(x, sample_of_tile, tm, f"mix_resid{l}", y=y, mod_res=mods[l],
                           g_post=g_post_mix[l], gate_idx=2, mod_pre=mods[l], g_pre=g_pre_ffn[l],
                           shift_idx=3, scale_idx=4, n_first=n_ctx_tiles)

        jx = l // 2

        def swiglu_epi(accs, e):
            return _silu(accs[0]) * accs[1]

        ffn_out = {}
        if not moe:
            d_ff = w_ff_gate.shape[2]
            ff = _matmul([(h, d, 0)], [(w_ff_gate, (jx,)), (w_ff_up, (jx,))], a_of_w=[0, 0],
                         n_out=d_ff, tm=tm_s, tn=_tile(d_ff, 512), out_dtype=BF16,
                         epilogue=swiglu_epi, name=f"ffn_up{l}")
            ffn_out["y"] = _matmul([(ff, d_ff, 0)], [(w_ff_down, (jx,))], n_out=d, tm=tm,
                                   tn=_tile(d, 512), out_dtype=F32, epilogue=_first,
                                   name=f"ffn_down{l}")
        else:
            d_ex = w_ex_gate.shape[3]
            w_r = jnp.pad(w_router[jx], ((0, 0), (0, LANE - n_experts))).astype(BF16)
            route = pl.pallas_call(
                functools.partial(_router_kernel, n_experts=n_experts), grid=(t // tm,),
                in_specs=[pl.BlockSpec((tm, d), lambda i: (i, 0)),
                          pl.BlockSpec((d, LANE), lambda i: (0, 0))],
                out_specs=pl.BlockSpec((tm, LANE), lambda i: (i, 0)),
                out_shape=jax.ShapeDtypeStruct((t, LANE), F32),
                compiler_params=_params(1), name=f"router{l}")(h, w_r)
            tile_e = _tile(t * TOP_K, EXPERT_TILE, 8)
            run_e = tile_e * EXPERT_RUN_TILES
            tok_of, dest, flags, run_info, tile_info = _routing_tables(route, n_experts, tile_e,
                                                                       run_e)
            xs = _gather(h.reshape(t, d // LANE, LANE), tok_of, flags, tile_e, f"moe_gather{l}")
            xs = xs.reshape(xs.shape[0], d)
            ff = _moe_up(xs, w_ex_gate, w_ex_up, (jx,), run_info, run=run_e, sub=tile_e,
                         tn=_tile(d_ex, 256), name=f"moe_up{l}")
            ys = _matmul([(ff, d_ex, 0)], [(w_ex_down, (jx, 0))], n_out=d, tm=tile_e,
                         tn=_tile(d, 1024), out_dtype=F32, epilogue=_first, name=f"moe_down{l}",
                         tile_expert=tile_info)
            ffn_out["routed"] = (ys, dest, route)

        if l + 1 < depth:
            x, h = _resid_norm(x, sample_of_tile, tm, f"ffn_resid{l}", mod_res=mods[l],
                               g_post=g_post_ffn[l], gate_idx=5, mod_pre=mods[l + 1],
                               g_pre=g_pre_mix[l + 1], shift_idx=0, scale_idx=1, **ffn_out)
        else:
            x, _ = _resid_norm(x, sample_of_tile, tm, f"ffn_resid{l}", mod_res=mods[l],
                               g_post=g_post_ffn[l], gate_idx=5, split_out=True,
                               n_first=n_ctx_tiles, **ffn_out)

    return (x[0].reshape(n_ctx_b, t_ctx, d), x[1].reshape(n_lat_b, t_lat, d),
            jnp.stack(new_ckv, axis=1), jnp.stack(new_krope, axis=1),
            states[0], states[1])
```

```python
import functools
import math

import jax
import jax.numpy as jnp
from jax import lax
from jax.experimental import pallas as pl
from jax.experimental.pallas import tpu as pltpu

MLA_HEADS = 16
QK_NOPE = 128
V_HEAD = 128
GRID_W = 64
ROPE_THETA = 10000.0
GATE_NORM = 16.0
GLA_CHUNK = 64
TOP_K = 2
EPS = 1e-6
LANE = 128
VMEM_LIMIT = 52 * 1024 * 1024
EXPERT_TILE = 512
EXPERT_RUN_TILES = 4

BF16 = jnp.bfloat16
F32 = jnp.float32


def _params(n_grid, **kw):
    return pltpu.CompilerParams(
        dimension_semantics=("arbitrary",) * n_grid, vmem_limit_bytes=VMEM_LIMIT, **kw)


def _tile(n, target, quantum=LANE):
    if n <= target:
        return n
    t = (target // quantum) * quantum
    while t >= quantum:
        if n % t == 0:
            return t
        t -= quantum
    return n


def _rms(x, g):
    return x * lax.rsqrt(jnp.mean(x * x, axis=-1, keepdims=True) + EPS) * g


def _silu(x):
    return x * jax.nn.sigmoid(x)


def _dot(a, b):
    return jnp.dot(a, b, preferred_element_type=F32)


def _dot_nt(a, b):
    return lax.dot_general(a, b, (((1,), (1,)), ((), ())), preferred_element_type=F32)


def _dot_tn(a, b):
    return lax.dot_general(a, b, (((0,), (0,)), ((), ())), preferred_element_type=F32)


def _mm_kernel(*refs, n_a, n_w, a_of_w, n_e, cast_w, grouped, w_rows, prologue, epilogue):
    if grouped:
        te_ref, refs = refs[0], refs[1:]
    a_refs = refs[:n_a]
    w_refs = refs[n_a:n_a + n_w]
    e_refs = refs[n_a + n_w:n_a + n_w + n_e]
    o_ref = refs[n_a + n_w + n_e]
    wb_refs = refs[n_a + n_w + n_e + 1:]
    i = pl.program_id(1)

    def compute():
        if cast_w:
            fresh = i == 0
            if grouped:
                fresh = fresh | (te_ref[i] != te_ref[jnp.maximum(i - 1, 0)])

            @pl.when(fresh)
            def _():
                for w_ref, wb_ref in zip(w_refs, wb_refs):
                    wb_ref[...] = w_ref[...].astype(BF16)
            w_use = wb_refs
        else:
            w_use = w_refs
        e = [e_ref[...] for e_ref in e_refs]
        a_vals = []
        for k, a_ref in enumerate(a_refs):
            a = a_ref[...]
            if prologue is not None:
                a = prologue(k, a, e)
            a_vals.append(a.astype(BF16))
        dot = _dot_nt if w_rows else _dot
        accs = [dot(a_vals[a_of_w[k]], w_ref[...]) for k, w_ref in enumerate(w_use)]
        o_ref[...] = epilogue(accs, e).astype(o_ref.dtype)

    if grouped:
        has_rows = te_ref[pl.num_programs(1) + i] != 0
        pl.when(has_rows)(compute)

        @pl.when(jnp.logical_not(has_rows))
        def _():
            o_ref[...] = jnp.zeros_like(o_ref)
    else:
        compute()


def _matmul(a_list, w_list, *, n_out, tm, tn, out_dtype, epilogue, name, a_of_w=None,
            prologue=None, extras=(), tile_expert=None, w_rows=False):
    m = a_list[0][0].shape[0]
    a_of_w = a_of_w or list(range(len(w_list)))
    grouped = tile_expert is not None
    assert m % tm == 0 and n_out % tn == 0
    in_specs, args, scratch = [], [], []
    for arr, k, cb in a_list:
        if grouped:
            a_map = lambda j, i, te, cb=cb: (te[2 * (m // tm) + i], cb)
        else:
            a_map = lambda j, i, cb=cb: (i, cb)
        in_specs.append(pl.BlockSpec((tm, k), a_map))
        args.append(arr)
    cast_w = w_list[0][0].dtype != BF16
    for arr, lead in w_list:
        k = arr.shape[-1] if w_rows else arr.shape[-2]
        if grouped:
            im = lambda j, i, te, lead=lead: tuple(lead[:-1]) + (te[i], 0, j)
        elif w_rows:
            im = lambda j, i, lead=lead: tuple(lead) + (j, 0)
        else:
            im = lambda j, i, lead=lead: tuple(lead) + (0, j)
        in_specs.append(pl.BlockSpec((None,) * len(lead) + ((tn, k) if w_rows else (k, tn)), im))
        args.append(arr)
        if cast_w:
            scratch.append(pltpu.VMEM((k, tn), BF16))
    for arr, bs, im in extras:
        in_specs.append(pl.BlockSpec(bs, lambda j, i, *_, im=im: im(j, i)))
        args.append(arr)
    kern = functools.partial(_mm_kernel, n_a=len(a_list), n_w=len(w_list), a_of_w=a_of_w,
                             n_e=len(extras), cast_w=cast_w, grouped=grouped, w_rows=w_rows,
                             prologue=prologue, epilogue=epilogue)
    grid_spec = pltpu.PrefetchScalarGridSpec(
        num_scalar_prefetch=1 if grouped else 0,
        grid=(n_out // tn, m // tm),
        in_specs=in_specs,
        out_specs=pl.BlockSpec((tm, tn), lambda j, i, *_: (i, j)),
        scratch_shapes=scratch)
    if grouped:
        args = [tile_expert] + args
    return pl.pallas_call(
        kern, grid_spec=grid_spec,
        out_shape=jax.ShapeDtypeStruct((m, n_out), out_dtype),
        compiler_params=_params(2), name=name,
    )(*args)


def _first(accs, e):
    return accs[0]


def _row_copy(src, row, dst, r, sem):
    return pltpu.make_async_copy(src.at[pl.ds(row, 1)], dst.at[pl.ds(r, 1)], sem)


def _gather_rows(idx_ref, base, stride, n, src, dst, dst_base, sem):
    def issue(r, carry):
        _row_copy(src, idx_ref[base + r * stride], dst, dst_base + r, sem).start()
        return carry

    def wait(r, carry):
        _row_copy(src, 0, dst, dst_base + r, sem).wait()
        return carry

    lax.fori_loop(0, n, issue, 0, unroll=8)
    lax.fori_loop(0, n, wait, 0, unroll=8)


def _gather_kernel(idx_ref, flag_ref, src_hbm, o_ref, sem):
    tg = o_ref.shape[0]
    i = pl.program_id(0)

    @pl.when(flag_ref[i] != 0)
    def _():
        _gather_rows(idx_ref, i * tg, 1, tg, src_hbm, o_ref, 0, sem)

    @pl.when(flag_ref[i] == 0)
    def _():
        o_ref[...] = jnp.zeros_like(o_ref)


def _gather(src, idx, flags, tg, name):
    n = idx.shape[0]
    blk = (tg,) + src.shape[1:]
    grid_spec = pltpu.PrefetchScalarGridSpec(
        num_scalar_prefetch=2, grid=(n // tg,),
        in_specs=[pl.BlockSpec(memory_space=pl.ANY)],
        out_specs=pl.BlockSpec(blk, lambda i, *_: (i,) + (0,) * (len(blk) - 1)),
        scratch_shapes=[pltpu.SemaphoreType.DMA(())])
    return pl.pallas_call(
        _gather_kernel, grid_spec=grid_spec,
        out_shape=jax.ShapeDtypeStruct((n,) + src.shape[1:], src.dtype),
        compiler_params=_params(1), name=name)(idx, flags, src)


def _resid_norm_kernel(*refs, mode, want_x, want_h, gate_idx, shift_idx, scale_idx, split_in,
                       split_out, n_first):
    it = iter(refs)
    dest_ref = next(it) if mode == "routed" else None
    x_refs = [next(it) for _ in range(2 if split_in else 1)]
    if mode == "dense":
        y_ref = next(it)
    elif mode == "routed":
        ys_hbm, route_ref = next(it), next(it)
    if mode is not None:
        mod_res_ref, g_post_ref = next(it), next(it)
    if want_h:
        mod_pre_ref, g_pre_ref = next(it), next(it)
    xo_refs = [next(it) for _ in range((2 if split_out else 1) if want_x else 0)]
    h_ref = next(it) if want_h else None
    i = pl.program_id(0)
    x = x_refs[0][...]
    if split_in:
        x = jnp.where(i < n_first, x, x_refs[1][...])
    if mode is not None:
        if mode == "routed":
            buf1, buf2, sem = next(it), next(it), next(it)
            tm = x.shape[0]
            base = i * tm * TOP_K
            _gather_rows(dest_ref, base, TOP_K, tm, ys_hbm, buf1, 0, sem)
            _gather_rows(dest_ref, base + 1, TOP_K, tm, ys_hbm, buf2, 0, sem)
            route = route_ref[...]
            y = route[:, 2:3] * buf1[...] + route[:, 3:4] * buf2[...]
        else:
            y = y_ref[...]
        gate = mod_res_ref[0, gate_idx:gate_idx + 1, :]
        x = x + gate * _rms(y, g_post_ref[...])
    if want_x and split_out:
        @pl.when(i < n_first)
        def _():
            xo_refs[0][...] = x

        @pl.when(i >= n_first)
        def _():
            xo_refs[1][...] = x
    elif want_x:
        xo_refs[0][...] = x
    if want_h:
        scale = mod_pre_ref[0, scale_idx:scale_idx + 1, :]
        shift = mod_pre_ref[0, shift_idx:shift_idx + 1, :]
        h_ref[...] = (_rms(x, g_pre_ref[...]) * (1.0 + scale) + shift).astype(h_ref.dtype)


def _resid_norm(x, sample_of_tile, tm, name, *, y=None, routed=None, mod_res=None, g_post=None,
                gate_idx=0, mod_pre=None, g_pre=None, shift_idx=0, scale_idx=0, split_out=False,
                n_first=0):
    split_in = isinstance(x, (tuple, list))
    xs_in = list(x) if split_in else [x]
    t, d = sum(a.shape[0] for a in xs_in), xs_in[0].shape[1]
    mode = "dense" if y is not None else ("routed" if routed is not None else None)
    want_h = mod_pre is not None
    want_x = mode is not None
    row = pl.BlockSpec((tm, d), lambda i, *_: (i, 0))
    first = pl.BlockSpec((tm, d), lambda i, *_: (jnp.minimum(i, n_first - 1), 0))
    rest = pl.BlockSpec((tm, d), lambda i, *_: (jnp.maximum(i - n_first, 0), 0))
    vec = pl.BlockSpec((1, d), lambda i, *_: (0, 0))
    mod = pl.BlockSpec((1, 6, d), lambda i, *_: (sample_of_tile(i), 0, 0))
    in_specs, args = ([first, rest] if split_in else [row]), xs_in
    out_specs, out_shape, scratch = [], [], []
    if mode == "dense":
        in_specs.append(row)
        args.append(y)
    elif mode == "routed":
        ys, dest, route = routed
        in_specs += [pl.BlockSpec(memory_space=pl.ANY),
                     pl.BlockSpec((tm, LANE), lambda i, *_: (i, 0))]
        args += [ys, route]
        scratch = [pltpu.VMEM((tm, d), F32), pltpu.VMEM((tm, d), F32),
                   pltpu.SemaphoreType.DMA(())]
    if mode is not None:
        in_specs += [mod, vec]
        args += [mod_res, g_post.reshape(1, d)]
    if want_h:
        in_specs += [mod, vec]
        args += [mod_pre, g_pre.reshape(1, d)]
    if want_x and split_out:
        out_specs += [first, rest]
        out_shape += [jax.ShapeDtypeStruct((n_first * tm, d), F32),
                      jax.ShapeDtypeStruct((t - n_first * tm, d), F32)]
    elif want_x:
        out_specs.append(row)
        out_shape.append(jax.ShapeDtypeStruct((t, d), F32))
    if want_h:
        out_specs.append(row)
        out_shape.append(jax.ShapeDtypeStruct((t, d), BF16))
    kern = functools.partial(_resid_norm_kernel, mode=mode, want_x=want_x, want_h=want_h,
                             gate_idx=gate_idx, shift_idx=shift_idx, scale_idx=scale_idx,
                             split_in=split_in, split_out=split_out, n_first=n_first)
    grid_spec = pltpu.PrefetchScalarGridSpec(
        num_scalar_prefetch=1 if mode == "routed" else 0, grid=(t // tm,),
        in_specs=in_specs, out_specs=out_specs, scratch_shapes=scratch)
    if mode == "routed":
        args = [dest] + args
    outs = pl.pallas_call(kern, grid_spec=grid_spec, out_shape=out_shape,
                          compiler_params=_params(1), name=name)(*args)
    n_x = (2 if split_out else 1) if want_x else 0
    x_new = (tuple(outs[:2]) if split_out else outs[0]) if want_x else x
    h = outs[n_x] if want_h else None
    return x_new, h


def _norm_kernel(x_ref, g_ref, o_ref):
    o_ref[...] = _rms(x_ref[...], g_ref[...])


def _norm_cols(arr, width, col_block, g, tm, name):
    t = arr.shape[0]
    return pl.pallas_call(
        _norm_kernel, grid=(t // tm,),
        in_specs=[pl.BlockSpec((tm, width), lambda i: (i, col_block)),
                  pl.BlockSpec((1, width), lambda i: (0, 0))],
        out_specs=pl.BlockSpec((tm, width), lambda i: (i, 0)),
        out_shape=jax.ShapeDtypeStruct((t, width), F32),
        compiler_params=_params(1), name=name)(arr, g.reshape(1, width))


def _swap_pairs(w, rope, axis=-1):
    q = rope // 4
    part = lambda a, b: lax.slice_in_dim(w, a, b, axis=axis)
    return jnp.concatenate([part(q, 2 * q), part(0, q), part(3 * q, 4 * q), part(2 * q, 3 * q)],
                           axis=axis)


def _prep_kernel(misc_ref, cos_ref, sin_ref, wg_ref, bg_ref, kr_ref, la_ref):
    misc = misc_ref[:, :LANE]
    kr = misc * cos_ref[...] + misc_ref[:, LANE:] * sin_ref[...]
    kr_ref[...] = kr.astype(kr_ref.dtype)
    z = _dot(misc.astype(BF16), wg_ref[...]) + bg_ref[...]
    log_sig = jnp.minimum(z, 0.0) - jnp.log1p(jnp.exp(-jnp.abs(z)))
    la_ref[...] = log_sig * (1.0 / GATE_NORM)


def _attn_kernel(*refs, n_seg, n_heads, group, scale):
    qn_ref, qr_ref = refs[:2]
    kv_refs = refs[2:2 + n_seg]
    kr_refs = refs[2 + n_seg:2 + 2 * n_seg]
    o_ref = refs[-1]
    hw = QK_NOPE + V_HEAD
    krs = [kr_ref[...] for kr_ref in kr_refs]
    add = lambda a, b: a + b
    c = scale * math.log2(math.e)
    for h0 in range(0, n_heads, group):
        hds = range(h0, min(h0 + group, n_heads))
        scores = []
        for hd in hds:
            qn = qn_ref[:, hd * QK_NOPE:(hd + 1) * QK_NOPE]
            qr = qr_ref[:, hd * LANE:(hd + 1) * LANE]
            scores.append([_dot_nt(qn, kv_ref[:, hd * hw:hd * hw + QK_NOPE]) + _dot_nt(qr, kr)
                           for kv_ref, kr in zip(kv_refs, krs)])
        ms = [functools.reduce(jnp.maximum, [jnp.max(s, axis=-1, keepdims=True) for s in ss])
              for ss in scores]
        ps = [[jnp.exp2((s - m) * c) for s in ss] for ss, m in zip(scores, ms)]
        invs = [1.0 / functools.reduce(add, [jnp.sum(p, axis=-1, keepdims=True) for p in pp])
                for pp in ps]
        for hd, pp, inv in zip(hds, ps, invs):
            o = functools.reduce(add, [
                _dot((p * inv).astype(BF16), kv_ref[:, hd * hw + QK_NOPE:(hd + 1) * hw])
                for p, kv_ref in zip(pp, kv_refs)])
            o_ref[:, hd * V_HEAD:(hd + 1) * V_HEAD] = o.astype(o_ref.dtype)


def _attention(q_nope, q_rope, kvs, krs, *, n_batch, t_q, q_row0, tq, n_heads, group, scale, name,
               o_prev):
    hw = QK_NOPE + V_HEAD
    nq = t_q // tq
    q_map = lambda b, g, i: (q_row0 // tq + b * nq + i, g)
    in_specs = [pl.BlockSpec((tq, n_heads * QK_NOPE), q_map),
                pl.BlockSpec((tq, n_heads * LANE), q_map)]
    args = [q_nope, q_rope]
    for arr, tk, row0 in kvs:
        in_specs.append(pl.BlockSpec((tk, n_heads * hw),
                                     lambda b, g, i, tk=tk, row0=row0: (row0 // tk + b, g)))
        args.append(arr)
    for arr, tk, row0 in krs:
        in_specs.append(pl.BlockSpec((tk, LANE),
                                     lambda b, g, i, tk=tk, row0=row0: (row0 // tk + b, 0)))
        args.append(arr)
    aliases = {len(args): 0}
    in_specs.append(pl.BlockSpec(memory_space=pl.ANY))
    args.append(o_prev)
    kern = functools.partial(_attn_kernel, n_seg=len(kvs), n_heads=n_heads, group=group,
                             scale=scale)
    return pl.pallas_call(
        kern, grid=(n_batch, MLA_HEADS // n_heads, nq), in_specs=in_specs,
        out_specs=pl.BlockSpec((tq, n_heads * V_HEAD), q_map),
        out_shape=jax.ShapeDtypeStruct(o_prev.shape, o_prev.dtype),
        input_output_aliases=aliases,
        compiler_params=_params(3), name=name)(*args)


def _split3(x):
    hi = x.astype(BF16)
    r1 = x - hi.astype(F32)
    mid = r1.astype(BF16)
    lo = (r1 - mid.astype(F32)).astype(BF16)
    return hi, mid, lo


def _gla_kernel(*refs, has_init, want_states, n_alias, n_heads, q_scale, layer, all_layers):
    it = iter(refs)
    q_ref, k_ref, v_ref, laf_ref, lab_ref, gout_ref, ggla_ref = (next(it) for _ in range(7))
    s0_refs = (next(it), next(it)) if has_init else None
    for _ in range(n_alias):
        next(it)
    o_ref = next(it)
    s_out_refs = (next(it), next(it)) if want_states else None
    st, vb, qd_s, ks_s, dec_s, oacc = (next(it) for _ in range(6))
    c = GLA_CHUNK
    t_b = q_ref.shape[0]
    dk, dv = q_ref.shape[1] // n_heads, v_ref.shape[1] // n_heads
    n_chunks = t_b // c
    rb = min(t_b, 4 * c)
    row = lax.broadcasted_iota(jnp.int32, (rb, rb), 0)
    col = lax.broadcasted_iota(jnp.int32, (rb, rb), 1)
    same = (row // c) == (col // c)
    masks = (same & (col <= row), same & (col >= row))
    tris = [jnp.where(m, 1.0, 0.0).astype(BF16) for m in masks]
    la_refs = (laf_ref, lab_ref)
    streams = [(g, d) for g in range(n_heads) for d in range(2)]
    kcols = [pl.ds(g * dk, dk) for g in range(n_heads)]
    vcols = [pl.ds(g * dv, dv) for g in range(n_heads)]

    vb[...] = v_ref[...].astype(BF16)
    for s, (g, d) in enumerate(streams):
        st[s] = s0_refs[d][0, g].T if has_init else jnp.zeros(st.shape[1:], F32)

    for r0 in range(0, t_b, rb):
        rows = pl.ds(r0, rb)
        for s, (g, d) in enumerate(streams):
            hi, mid, lo = _split3(la_refs[d][rows, kcols[g]])
            b = (_dot(tris[d], hi) + _dot(tris[d], mid)) + _dot(tris[d], lo)
            end = c - 1 if d == 0 else 0
            b_end = jnp.concatenate(
                [jnp.broadcast_to(b[k * c + end:k * c + end + 1, :], (c, dk))
                 for k in range(rb // c)], axis=0)
            kc = k_ref[rows, kcols[g]]
            qd = (q_ref[rows, kcols[g]] * q_scale * jnp.exp(b)).astype(BF16)
            kd = (kc * jnp.exp(-b)).astype(BF16)
            qd_s[s, rows, :] = qd
            ks_s[s, rows, :] = (kc * jnp.exp(b_end - b)).astype(BF16)
            dec_s[s, rows, :] = jnp.exp(b_end)
            a = jnp.where(masks[d], _dot_nt(qd, kd), 0.0).astype(BF16)
            o_in = _dot(a, vb[rows, vcols[g]])
            if d == 0:
                oacc[rows, vcols[g]] = o_in
            else:
                oacc[rows, vcols[g]] += o_in

    for i in range(n_chunks):
        for s, (g, d) in enumerate(streams):
            c0 = (i if d == 0 else n_chunks - 1 - i) * c
            rows = pl.ds(c0, c)
            u = _dot_tn(vb[rows, vcols[g]], ks_s[s, rows, :])
            s_t = st[s]
            oacc[rows, vcols[g]] += _dot_nt(qd_s[s, rows, :], s_t.astype(BF16))
            st[s] = s_t * dec_s[s, pl.ds(c0, 1), :] + u

    for g in range(n_heads):
        o_ref[:, vcols[g]] = (_rms(oacc[:, vcols[g]], ggla_ref[...])
                              * _silu(gout_ref[:, vcols[g]])).astype(o_ref.dtype)
    if want_states:
        for s, (g, d) in enumerate(streams):
            s_fin = st[s].T
            if all_layers:
                for ll in range(s_out_refs[d].shape[1]):
                    s_out_refs[d][0, ll, g] = s_fin if ll == layer else jnp.zeros_like(s_fin)
            else:
                s_out_refs[d][0, g] = s_fin


def _gla(proj, la, g_gla, *, n_batch, t_b, row0, heads, n_heads, dk, dv, col_q, col_k, col_v,
         col_g, name, o_prev, s0=None, layer=0, st_prev=None, st_depth=0):
    has_init = s0 is not None
    rb = row0 // t_b
    wk, wv = n_heads * dk, n_heads * dv
    assert col_q % wk == 0 and col_k % wk == 0 and col_v % wv == 0 and col_g % wv == 0
    in_specs = [
        pl.BlockSpec((t_b, wk), lambda b, h: (rb + b, col_q // wk + h)),
        pl.BlockSpec((t_b, wk), lambda b, h: (rb + b, col_k // wk + h)),
        pl.BlockSpec((t_b, wv), lambda b, h: (rb + b, col_v // wv + h)),
        pl.BlockSpec((t_b, wk), lambda b, h: (rb + b, h)),
        pl.BlockSpec((t_b, wk), lambda b, h: (rb + b, heads // n_heads + h)),
        pl.BlockSpec((t_b, wv), lambda b, h: (rb + b, col_g // wv + h)),
        pl.BlockSpec((1, dv), lambda b, h: (0, 0)),
    ]
    args = [proj, proj, proj, la, la, proj, g_gla.reshape(1, dv)]
    st_spec = pl.BlockSpec((1, None, n_heads, dk, dv), lambda b, h: (b, layer, h, 0, 0))
    if has_init:
        in_specs += [st_spec, st_spec]
        args += list(s0)
    out_specs = [pl.BlockSpec((t_b, wv), lambda b, h: (rb + b, h))]
    out_shape = [jax.ShapeDtypeStruct(o_prev.shape, o_prev.dtype)]
    aliases = {len(args): 0}
    in_specs.append(pl.BlockSpec(memory_space=pl.ANY))
    args.append(o_prev)
    want_states = st_prev is not None or st_depth > 0
    if st_prev is not None:
        out_specs += [st_spec, st_spec]
        for k, arr in enumerate(st_prev):
            out_shape.append(jax.ShapeDtypeStruct(arr.shape, arr.dtype))
            aliases[len(args)] = 1 + k
            in_specs.append(pl.BlockSpec(memory_space=pl.ANY))
            args.append(arr)
    elif want_states:
        out_specs += [pl.BlockSpec((1, st_depth, n_heads, dk, dv),
                                   lambda b, h: (b, 0, h, 0, 0))] * 2
        out_shape += [jax.ShapeDtypeStruct((n_batch, st_depth, heads, dk, dv), F32)] * 2
    kern = functools.partial(_gla_kernel, has_init=has_init, want_states=want_states,
                             n_alias=len(aliases), n_heads=n_heads, q_scale=dk ** -0.5,
                             layer=layer, all_layers=st_prev is None)
    n_str = 2 * n_heads
    return pl.pallas_call(
        kern, grid=(n_batch, heads // n_heads), in_specs=in_specs, out_specs=out_specs,
        out_shape=out_shape, input_output_aliases=aliases,
        scratch_shapes=[pltpu.VMEM((n_str, dv, dk), F32), pltpu.VMEM((t_b, wv), BF16),
                        pltpu.VMEM((n_str, t_b, dk), BF16), pltpu.VMEM((n_str, t_b, dk), BF16),
                        pltpu.VMEM((n_str, t_b, dk), F32), pltpu.VMEM((t_b, wv), F32)],
        compiler_params=_params(2), name=name)(*args)


def _router_kernel(h_ref, w_ref, o_ref, *, n_experts):
    logits = _dot(h_ref[...].astype(BF16), w_ref[...])
    lane = lax.broadcasted_iota(jnp.int32, logits.shape, 1).astype(F32)
    neg = jnp.float32(-jnp.inf)
    lg = jnp.where(lane < n_experts, logits, neg)
    m1 = jnp.max(lg, axis=-1, keepdims=True)
    i1 = jnp.min(jnp.where(lg == m1, lane, LANE), axis=-1, keepdims=True)
    lg2 = jnp.where(lane == i1, neg, lg)
    m2 = jnp.max(lg2, axis=-1, keepdims=True)
    i2 = jnp.min(jnp.where(lg2 == m2, lane, LANE), axis=-1, keepdims=True)
    e2 = jnp.exp(m2 - m1)
    inv = 1.0 / (1.0 + e2)
    o_ref[...] = (jnp.where(lane == 0, i1, 0.0) + jnp.where(lane == 1, i2, 0.0)
                  + jnp.where(lane == 2, inv, 0.0) + jnp.where(lane == 3, e2 * inv, 0.0))


def _moe_up_kernel(info_ref, x_ref, wg_ref, wu_ref, o_ref, *, n_sub):
    s = pl.program_id(0)
    flag0 = pl.num_programs(0) + s * n_sub
    sub = x_ref.shape[0] // n_sub

    @pl.when(info_ref[flag0] != 0)
    def _():
        wg = wg_ref[...].astype(BF16)
        wu = wu_ref[...].astype(BF16)
        for k in range(n_sub):
            rows = pl.ds(k * sub, sub)

            @pl.when(info_ref[flag0 + k] != 0)
            def _():
                a = x_ref[rows, :]
                o_ref[rows, :] = (_silu(_dot(a, wg)) * _dot(a, wu)).astype(o_ref.dtype)

            @pl.when(info_ref[flag0 + k] == 0)
            def _():
                o_ref[rows, :] = jnp.zeros((sub, o_ref.shape[1]), o_ref.dtype)

    @pl.when(info_ref[flag0] == 0)
    def _():
        o_ref[...] = jnp.zeros_like(o_ref)


def _moe_up(xs, w_gate, w_up, lead, info, *, run, sub, tn, name):
    n_rows, d = xs.shape
    n_out = w_gate.shape[-1]
    n_run = n_rows // run
    n_sub = run // sub

    def w_map(s, j, info):
        return tuple(lead) + (info[s], 0, jnp.where(info[n_run + s * n_sub] != 0, j, 0))

    w_spec = pl.BlockSpec((None,) * (len(lead) + 1) + (d, tn), w_map)
    grid_spec = pltpu.PrefetchScalarGridSpec(
        num_scalar_prefetch=1, grid=(n_run, n_out // tn),
        in_specs=[pl.BlockSpec((run, d), lambda s, j, info: (s, 0)), w_spec, w_spec],
        out_specs=pl.BlockSpec((run, tn), lambda s, j, info: (s, j)))
    return pl.pallas_call(
        functools.partial(_moe_up_kernel, n_sub=n_sub), grid_spec=grid_spec,
        out_shape=jax.ShapeDtypeStruct((n_rows, n_out), BF16),
        compiler_params=_params(2), name=name)(info, xs, w_gate, w_up)


def _routing_tables(route, n_experts, tile, run):
    t = route.shape[0]
    n_assign = t * TOP_K
    n_rows = -(-(n_assign + n_experts * (run - 1)) // run) * run
    e_flat = route[:, :TOP_K].astype(jnp.int32).reshape(n_assign)
    onehot = (e_flat[:, None] == jnp.arange(n_experts, dtype=jnp.int32)[None, :]).astype(jnp.int32)
    csum = jnp.cumsum(onehot, axis=0)
    rank = jnp.take_along_axis(csum, e_flat[:, None], axis=1)[:, 0] - 1
    counts = csum[-1]
    padded = ((counts + run - 1) // run) * run
    ends = jnp.cumsum(padded)
    starts = ends - padded
    dest = starts[e_flat] + rank
    tok_of = jnp.zeros((n_rows,), jnp.int32).at[dest].set(
        jnp.arange(n_assign, dtype=jnp.int32) // TOP_K, unique_indices=True)
    tile_start = jnp.arange(n_rows // tile, dtype=jnp.int32) * tile
    tile_expert = jnp.minimum(
        jnp.sum(tile_start[:, None] >= ends[None, :], axis=1, dtype=jnp.int32), n_experts - 1)
    flags = (tile_start < (starts + counts)[tile_expert]).astype(jnp.int32)
    run_info = jnp.concatenate([tile_expert[::run // tile], flags])
    tile_idx = jnp.arange(n_rows // tile, dtype=jnp.int32)
    tile_info = jnp.concatenate([tile_expert, flags,
                                 lax.cummax(jnp.where(flags != 0, tile_idx, 0))])
    return tok_of, dest.astype(jnp.int32), flags, run_info, tile_info


def _rope_tables(n_lat, rope, tm_id):
    axis_half = rope // 4
    rows = n_lat // GRID_W
    r = jnp.repeat(jnp.arange(rows, dtype=F32), GRID_W)
    c = jnp.tile(jnp.arange(GRID_W, dtype=F32), rows)
    inv = ROPE_THETA ** (-jnp.arange(axis_half, dtype=F32) / axis_half)
    ar, ac = r[:, None] * inv, c[:, None] * inv
    cos = jnp.concatenate([jnp.cos(ar), jnp.cos(ar), jnp.cos(ac), jnp.cos(ac)], axis=1)
    sin = jnp.concatenate([-jnp.sin(ar), jnp.sin(ar), -jnp.sin(ac), jnp.sin(ac)], axis=1)
    cos = jnp.concatenate([jnp.ones((tm_id, rope), F32), cos], axis=0)
    sin = jnp.concatenate([jnp.zeros((tm_id, rope), F32), sin], axis=0)
    return cos, sin


def kernel(x_prompt, x_sample, cache_ckv, cache_krope, state_gla_fwd, state_gla_bwd, c, c_ctx, w_ada, b_ada, g_pre_mix, g_post_mix, g_pre_ffn, g_post_ffn, w_in, g_q, w_uq, g_kv, w_ukv, w_gate_f, b_gate_f, w_gate_b, b_gate_b, g_gla, w_pa, w_pb, w_o, w_ff_gate, w_ff_up, w_ff_down, w_router, w_ex_gate, w_ex_up, w_ex_down):
    n_ctx_b, t_ctx, d = x_prompt.shape
    n_lat_b, t_lat, _ = x_sample.shape
    depth = w_in.shape[0]
    past = cache_ckv.shape[2]
    q_lora, kv_lora = g_q.shape[1], g_kv.shape[1]
    rope = cache_krope.shape[3]
    heads, dk, dv = state_gla_fwd.shape[2:]
    rank = w_gate_f.shape[1]
    n_experts = w_router.shape[2]
    hq = QK_NOPE + rope
    hw = QK_NOPE + V_HEAD
    n_ctx, n_lat = n_ctx_b * t_ctx, n_lat_b * t_lat
    t = n_ctx + n_lat
    assert 2 * rope == LANE and 2 * rank <= LANE - rope

    tm = _tile(math.gcd(n_ctx, t_lat), 512)
    tm_s = _tile(math.gcd(n_ctx, t_lat), 1024)
    n_ctx_tiles = n_ctx // tm
    lat_tiles = t_lat // tm

    def sample_of_tile(i):
        return jnp.where(i < n_ctx_tiles, 0, 1 + (i - n_ctx_tiles) // lat_tiles)

    def pos_of_tile(i, tile=tm):
        first = n_ctx // tile
        return jnp.where(i < first, 0, tm_s // tile + (i - first) % (t_lat // tile))

    sizes = (q_lora, kv_lora, rope, heads * dk, heads * dk, heads * dv, rank, rank,
             heads * dv, d, d)
    offs = [0]
    for s in sizes:
        offs.append(offs[-1] + s)
    main_groups = (0, 1, 3, 4, 5, 8, 9, 10)
    col = {}
    acc = 0
    for gidx in main_groups:
        col[gidx] = acc
        acc += sizes[gidx]
    n_main = acc
    col_q, col_k, col_v, col_g, col_a, col_b = col[3], col[4], col[5], col[8], col[9], col[10]

    cos_r, sin_r = _rope_tables(t_lat, rope, tm_s)
    n_tab = cos_r.shape[0]
    cos_k = jnp.concatenate([cos_r, jnp.zeros((n_tab, LANE - rope), F32)], axis=1)
    sin_k = jnp.concatenate([sin_r, jnp.zeros((n_tab, LANE - rope), F32)], axis=1)

    x = (x_prompt.reshape(n_ctx, d), x_sample.reshape(n_lat, d))
    c_all = jnp.concatenate([c_ctx[None, :], c, jnp.zeros((8 - 1 - n_lat_b, d), F32)], axis=0)

    def modulation(l):
        def epi(accs, e):
            return accs[0] + e[0]
        tn = _tile(6 * d, 1024)
        m = _matmul([(c_all, d, 0)], [(w_ada, (l,))], n_out=6 * d, tm=8, tn=tn, out_dtype=F32,
                    epilogue=epi, prologue=lambda k, a, e: _silu(a), name=f"ada{l}",
                    extras=[(b_ada.reshape(depth, 1, 6 * d), (None, 1, tn),
                             lambda j, i, l=l: (l, 0, j))])
        return m.reshape(8, 6, d)

    mods = [modulation(l) for l in range(depth)]
    _, h = _resid_norm(x, sample_of_tile, tm, "prenorm0", mod_pre=mods[0], g_pre=g_pre_mix[0],
                       shift_idx=0, scale_idx=1, n_first=n_ctx_tiles)

    new_ckv, new_krope = [], []
    states = None
    w_in_t = jnp.swapaxes(w_in, 1, 2)
    for l in range(depth):
        n_all = n_main + 2 * LANE

        def repack_kernel(w_ref, o_ref):
            w_kr = w_ref[offs[2]:offs[3], :]
            lanes = w_ref.shape[1]
            pieces = [w_ref[offs[gidx]:offs[gidx + 1], :] for gidx in main_groups]
            pieces += [w_kr, w_ref[offs[6]:offs[8], :],
                       jnp.zeros((LANE - rope - 2 * rank, lanes), F32),
                       _swap_pairs(w_kr, rope, 0), jnp.zeros((LANE - rope, lanes), F32)]
            o_ref[...] = jnp.concatenate(pieces, axis=0).astype(BF16)

        tk_r = _tile(d, 256)
        w_all = pl.pallas_call(
            repack_kernel, grid=(d // tk_r,),
            in_specs=[pl.BlockSpec((None, offs[-1], tk_r), lambda i, l=l: (l, 0, i))],
            out_specs=pl.BlockSpec((n_all, tk_r), lambda i: (0, i)),
            out_shape=jax.ShapeDtypeStruct((n_all, d), BF16),
            compiler_params=_params(1), name=f"w_in_repack{l}")(w_in_t)
        proj = _matmul([(h, d, 0)], [(w_all, ())], n_out=n_all, tm=tm_s, tn=_tile(n_all, 1280),
                       out_dtype=F32, epilogue=_first, name=f"w_in{l}", w_rows=True)

        w_gate = jnp.zeros((LANE, 2 * heads * dk), F32)
        w_gate = w_gate.at[rope:rope + rank, :heads * dk].set(w_gate_f[l])
        w_gate = w_gate.at[rope + rank:rope + 2 * rank, heads * dk:].set(w_gate_b[l])
        b_gate = jnp.concatenate([b_gate_f[l], b_gate_b[l]]).reshape(1, 2 * heads * dk)
        kr_self, la = pl.pallas_call(
            _prep_kernel, grid=(t // tm,),
            in_specs=[pl.BlockSpec((tm, 2 * LANE), lambda i: (i, n_main // (2 * LANE))),
                      pl.BlockSpec((tm, LANE), lambda i: (pos_of_tile(i), 0)),
                      pl.BlockSpec((tm, LANE), lambda i: (pos_of_tile(i), 0)),
                      pl.BlockSpec((LANE, 2 * heads * dk), lambda i: (0, 0)),
                      pl.BlockSpec((1, 2 * heads * dk), lambda i: (0, 0))],
            out_specs=[pl.BlockSpec((tm, LANE), lambda i: (i, 0)),
                       pl.BlockSpec((tm, 2 * heads * dk), lambda i: (i, 0))],
            out_shape=[jax.ShapeDtypeStruct((t, LANE), BF16),
                       jax.ShapeDtypeStruct((t, 2 * heads * dk), F32)],
            compiler_params=_params(1), name=f"prep{l}")(proj, cos_k, sin_k, w_gate.astype(BF16), b_gate)

        wq = w_uq[l].reshape(q_lora, MLA_HEADS, hq)
        wq_r = wq[:, :, QK_NOPE:]
        pad = jnp.zeros((q_lora, MLA_HEADS, LANE - rope), F32)
        as_cols = lambda a: a.reshape(q_lora, -1).astype(BF16)
        wq_n = as_cols(wq[:, :, :QK_NOPE])
        wq_s = as_cols(jnp.concatenate([_swap_pairs(wq_r, rope), pad], axis=2))
        wq_r = as_cols(jnp.concatenate([wq_r, pad], axis=2))
        q_in = [(proj, q_lora, col[0] // q_lora)]
        q_norm = lambda k, a, e: _rms(a, e[0])
        g_q_extra = (g_q[l].reshape(1, q_lora), (1, q_lora), lambda j, i: (0, 0))
        tn_q = _tile(MLA_HEADS * LANE, 1024)

        def rope_epi(accs, e, tn_q=tn_q):
            cos, sin = e[1], e[2]
            return jnp.concatenate([accs[0][:, s:s + LANE] * cos + accs[1][:, s:s + LANE] * sin
                                    for s in range(0, tn_q, LANE)], axis=1)

        q_nope = _matmul(q_in, [(wq_n, ())], n_out=MLA_HEADS * QK_NOPE, tm=tm_s,
                         tn=_tile(MLA_HEADS * QK_NOPE, 1024), out_dtype=BF16, epilogue=_first,
                         name=f"w_uq_nope{l}", prologue=q_norm, extras=[g_q_extra])
        q_rope = _matmul(q_in, [(wq_r, ()), (wq_s, ())], a_of_w=[0, 0], n_out=MLA_HEADS * LANE,
                         tm=tm_s, tn=tn_q, out_dtype=BF16, epilogue=rope_epi,
                         name=f"w_uq_rope{l}", prologue=q_norm,
                         extras=[g_q_extra,
                                 (cos_k, (tm_s, LANE), lambda j, i: (pos_of_tile(i, tm_s), 0)),
                                 (sin_k, (tm_s, LANE), lambda j, i: (pos_of_tile(i, tm_s), 0))])

        c_kv = _norm_cols(proj, kv_lora, col[1] // kv_lora, g_kv[l], tm, f"ckv_norm{l}")
        tn_kv = _tile(MLA_HEADS * hw, 1024)
        kv_self = _matmul([(c_kv, kv_lora, 0)], [(w_ukv, (l,))], n_out=MLA_HEADS * hw, tm=tm_s,
                          tn=tn_kv, out_dtype=BF16, epilogue=_first, name=f"w_ukv{l}")
        ckv_cache = cache_ckv[:, l].reshape(n_lat_b * past, kv_lora)
        kv_cache = _matmul([(ckv_cache, kv_lora, 0)], [(w_ukv, (l,))], n_out=MLA_HEADS * hw,
                           tm=_tile(n_lat_b * past, 512), tn=tn_kv, out_dtype=BF16,
                           epilogue=_first, name=f"w_ukv_cache{l}")
        kr_cache = jnp.pad(cache_krope[:, l].reshape(n_lat_b * past, rope),
                           ((0, 0), (0, LANE - rope))).astype(BF16)

        scale = hq ** -0.5
        attn = _attention(q_nope, q_rope, [(kv_self, t_ctx, 0)], [(kr_self, t_ctx, 0)],
                          n_batch=n_ctx_b, t_q=t_ctx, q_row0=0, tq=_tile(t_ctx, 256),
                          n_heads=math.gcd(MLA_HEADS, 8), group=4, scale=scale, name=f"attn_ctx{l}",
                          o_prev=jnp.zeros((t, MLA_HEADS * V_HEAD), BF16))
        attn = _attention(q_nope, q_rope, [(kv_self, t_lat, n_ctx), (kv_cache, past, 0)],
                          [(kr_self, t_lat, n_ctx), (kr_cache, past, 0)], n_batch=n_lat_b,
                          t_q=t_lat, q_row0=n_ctx, tq=_tile(t_lat, 512),
                          n_heads=math.gcd(MLA_HEADS, 4), group=2, scale=scale, name=f"attn_lat{l}",
                          o_prev=attn)

        gla_kw = dict(heads=heads, dk=dk, dv=dv, col_q=col_q, col_k=col_k, col_v=col_v,
                      col_g=col_g, layer=l)
        gla, *states = _gla(proj, la, g_gla[l], n_batch=n_ctx_b, t_b=t_ctx, row0=0,
                            n_heads=math.gcd(heads, 2), name=f"gla_ctx{l}",
                            o_prev=jnp.zeros((t, heads * dv), BF16), st_prev=states,
                            st_depth=depth, **gla_kw)
        gla, = _gla(proj, la, g_gla[l], n_batch=n_lat_b, t_b=t_lat, row0=n_ctx, n_heads=1,
                    s0=(state_gla_fwd, state_gla_bwd), name=f"gla_lat{l}", o_prev=gla, **gla_kw)

        new_ckv.append(c_kv[:n_ctx].reshape(n_ctx_b, t_ctx, kv_lora))
        new_krope.append(proj[:n_ctx, n_main:n_main + rope].reshape(n_ctx_b, t_ctx, rope))

        tn_m = _tile(math.gcd(col_a, col_b, d), 1024)
        tm_m = _tile(tm, 256, 8)

        def merge_epi(accs, e):
            return jax.nn.sigmoid(e[0]) * accs[0] + jax.nn.sigmoid(e[1]) * accs[1]

        merged = _matmul([(attn, MLA_HEADS * V_HEAD, 0), (gla, heads * dv, 0)],
                         [(w_pa, (l,)), (w_pb, (l,))], n_out=d, tm=tm_m, tn=tn_m, out_dtype=BF16,
                         epilogue=merge_epi, name=f"merge{l}",
                         extras=[(proj, (tm_m, tn_m), lambda j, i: (i, col_a // tn_m + j)),
                                 (proj, (tm_m, tn_m), lambda j, i: (i, col_b // tn_m + j))])
        y = _matmul([(merged, d, 0)], [(w_o, (l,))], n_out=d, tm=tm_s, tn=_tile(d, 1024),
                    out_dtype=F32, epilogue=_first, name=f"w_o{l}")
        moe = l % 2 == 1
        x, h = _resid_norm(x, sample_of_tile, tm, f"mix_resid{l}", y=y, mod_res=mods[l],
                           g_post=g_post_mix[l], gate_idx=2, mod_pre=mods[l], g_pre=g_pre_ffn[l],
                           shift_idx=3, scale_idx=4, n_first=n_ctx_tiles)

        jx = l // 2

        def swiglu_epi(accs, e):
            return _silu(accs[0]) * accs[1]

        ffn_out = {}
        if not moe:
            d_ff = w_ff_gate.shape[2]
            ff = _matmul([(h, d, 0)], [(w_ff_gate, (jx,)), (w_ff_up, (jx,))], a_of_w=[0, 0],
                         n_out=d_ff, tm=tm_s, tn=_tile(d_ff, 512), out_dtype=BF16,
                         epilogue=swiglu_epi, name=f"ffn_up{l}")
            ffn_out["y"] = _matmul([(ff, d_ff, 0)], [(w_ff_down, (jx,))], n_out=d, tm=tm,
                                   tn=_tile(d, 512), out_dtype=F32, epilogue=_first,
                                   name=f"ffn_down{l}")
        else:
            d_ex = w_ex_gate.shape[3]
            w_r = jnp.pad(w_router[jx], ((0, 0), (0, LANE - n_experts))).astype(BF16)
            route = pl.pallas_call(
                functools.partial(_router_kernel, n_experts=n_experts), grid=(t // tm,),
                in_specs=[pl.BlockSpec((tm, d), lambda i: (i, 0)),
                          pl.BlockSpec((d, LANE), lambda i: (0, 0))],
                out_specs=pl.BlockSpec((tm, LANE), lambda i: (i, 0)),
                out_shape=jax.ShapeDtypeStruct((t, LANE), F32),
                compiler_params=_params(1), name=f"router{l}")(h, w_r)
            tile_e = _tile(t * TOP_K, EXPERT_TILE, 8)
            run_e = tile_e * EXPERT_RUN_TILES
            tok_of, dest, flags, run_info, tile_info = _routing_tables(route, n_experts, tile_e,
                                                                       run_e)
            xs = _gather(h.reshape(t, d // LANE, LANE), tok_of, flags, tile_e, f"moe_gather{l}")
            xs = xs.reshape(xs.shape[0], d)
            ff = _moe_up(xs, w_ex_gate, w_ex_up, (jx,), run_info, run=run_e, sub=tile_e,
                         tn=_tile(d_ex, 256), name=f"moe_up{l}")
            ys = _matmul([(ff, d_ex, 0)], [(w_ex_down, (jx, 0))], n_out=d, tm=tile_e,
                         tn=_tile(d, 1024), out_dtype=F32, epilogue=_first, name=f"moe_down{l}",
                         tile_expert=tile_info)
            ffn_out["routed"] = (ys, dest, route)

        if l + 1 < depth:
            x, h = _resid_norm(x, sample_of_tile, tm, f"ffn_resid{l}", mod_res=mods[l],
                               g_post=g_post_ffn[l], gate_idx=5, mod_pre=mods[l + 1],
                               g_pre=g_pre_mix[l + 1], shift_idx=0, scale_idx=1, **ffn_out)
        else:
            x, _ = _resid_norm(x, sample_of_tile, tm, f"ffn_resid{l}", mod_res=mods[l],
                               g_post=g_post_ffn[l], gate_idx=5, split_out=True,
                               n_first=n_ctx_tiles, **ffn_out)

    return (x[0].reshape(n_ctx_b, t_ctx, d), x[1].reshape(n_lat_b, t_lat, d),
            jnp.stack(new_ckv, axis=1), jnp.stack(new_krope, axis=1),
            states[0], states[1])
```

```python
import functools
import math

import jax
import jax.numpy as jnp
from jax import lax
from jax.experimental import pallas as pl
from jax.experimental.pallas import tpu as pltpu

MLA_HEADS = 16
QK_NOPE = 128
V_HEAD = 128
GRID_W = 64
ROPE_THETA = 10000.0
GATE_NORM = 16.0
GLA_CHUNK = 64
TOP_K = 2
EPS = 1e-6
LANE = 128
VMEM_LIMIT = 52 * 1024 * 1024
EXPERT_TILE = 512
EXPERT_RUN_TILES = 4

BF16 = jnp.bfloat16
F32 = jnp.float32


def _params(n_grid, **kw):
    return pltpu.CompilerParams(
        dimension_semantics=("arbitrary",) * n_grid, vmem_limit_bytes=VMEM_LIMIT, **kw)


def _tile(n, target, quantum=LANE):
    if n <= target:
        return n
    t = (target // quantum) * quantum
    while t >= quantum:
        if n % t == 0:
            return t
        t -= quantum
    return n


def _rms(x, g):
    return x * lax.rsqrt(jnp.mean(x * x, axis=-1, keepdims=True) + EPS) * g


def _silu(x):
    return x * jax.nn.sigmoid(x)


def _dot(a, b):
    return jnp.dot(a, b, preferred_element_type=F32)


def _dot_nt(a, b):
    return lax.dot_general(a, b, (((1,), (1,)), ((), ())), preferred_element_type=F32)


def _dot_tn(a, b):
    return lax.dot_general(a, b, (((0,), (0,)), ((), ())), preferred_element_type=F32)


def _mm_kernel(*refs, n_a, n_w, a_of_w, n_e, cast_w, grouped, w_rows, prologue, epilogue):
    if grouped:
        te_ref, refs = refs[0], refs[1:]
    a_refs = refs[:n_a]
    w_refs = refs[n_a:n_a + n_w]
    e_refs = refs[n_a + n_w:n_a + n_w + n_e]
    o_ref = refs[n_a + n_w + n_e]
    wb_refs = refs[n_a + n_w + n_e + 1:]
    i = pl.program_id(1)

    def compute():
        if cast_w:
            fresh = i == 0
            if grouped:
                fresh = fresh | (te_ref[i] != te_ref[jnp.maximum(i - 1, 0)])

            @pl.when(fresh)
            def _():
                for w_ref, wb_ref in zip(w_refs, wb_refs):
                    wb_ref[...] = w_ref[...].astype(BF16)
            w_use = wb_refs
        else:
            w_use = w_refs
        e = [e_ref[...] for e_ref in e_refs]
        a_vals = []
        for k, a_ref in enumerate(a_refs):
            a = a_ref[...]
            if prologue is not None:
                a = prologue(k, a, e)
            a_vals.append(a.astype(BF16))
        dot = _dot_nt if w_rows else _dot
        accs = [dot(a_vals[a_of_w[k]], w_ref[...]) for k, w_ref in enumerate(w_use)]
        o_ref[...] = epilogue(accs, e).astype(o_ref.dtype)

    if grouped:
        has_rows = te_ref[pl.num_programs(1) + i] != 0
        pl.when(has_rows)(compute)

        @pl.when(jnp.logical_not(has_rows))
        def _():
            o_ref[...] = jnp.zeros_like(o_ref)
    else:
        compute()


def _matmul(a_list, w_list, *, n_out, tm, tn, out_dtype, epilogue, name, a_of_w=None,
            prologue=None, extras=(), tile_expert=None, w_rows=False):
    m = a_list[0][0].shape[0]
    a_of_w = a_of_w or list(range(len(w_list)))
    grouped = tile_expert is not None
    assert m % tm == 0 and n_out % tn == 0
    in_specs, args, scratch = [], [], []
    for arr, k, cb in a_list:
        if grouped:
            a_map = lambda j, i, te, cb=cb: (te[2 * (m // tm) + i], cb)
        else:
            a_map = lambda j, i, cb=cb: (i, cb)
        in_specs.append(pl.BlockSpec((tm, k), a_map))
        args.append(arr)
    cast_w = w_list[0][0].dtype != BF16
    for arr, lead in w_list:
        k = arr.shape[-1] if w_rows else arr.shape[-2]
        if grouped:
            im = lambda j, i, te, lead=lead: tuple(lead[:-1]) + (te[i], 0, j)
        elif w_rows:
            im = lambda j, i, lead=lead: tuple(lead) + (j, 0)
        else:
            im = lambda j, i, lead=lead: tuple(lead) + (0, j)
        in_specs.append(pl.BlockSpec((None,) * len(lead) + ((tn, k) if w_rows else (k, tn)), im))
        args.append(arr)
        if cast_w:
            scratch.append(pltpu.VMEM((k, tn), BF16))
    for arr, bs, im in extras:
        in_specs.append(pl.BlockSpec(bs, lambda j, i, *_, im=im: im(j, i)))
        args.append(arr)
    kern = functools.partial(_mm_kernel, n_a=len(a_list), n_w=len(w_list), a_of_w=a_of_w,
                             n_e=len(extras), cast_w=cast_w, grouped=grouped, w_rows=w_rows,
                             prologue=prologue, epilogue=epilogue)
    grid_spec = pltpu.PrefetchScalarGridSpec(
        num_scalar_prefetch=1 if grouped else 0,
        grid=(n_out // tn, m // tm),
        in_specs=in_specs,
        out_specs=pl.BlockSpec((tm, tn), lambda j, i, *_: (i, j)),
        scratch_shapes=scratch)
    if grouped:
        args = [tile_expert] + args
    return pl.pallas_call(
        kern, grid_spec=grid_spec,
        out_shape=jax.ShapeDtypeStruct((m, n_out), out_dtype),
        compiler_params=_params(2), name=name,
    )(*args)


def _first(accs, e):
    return accs[0]


def _row_copy(src, row, dst, r, sem):
    return pltpu.make_async_copy(src.at[pl.ds(row, 1)], dst.at[pl.ds(r, 1)], sem)


def _gather_rows(idx_ref, jobs, stride, n, src, sem):
    def issue(r, carry):
        for base, dst in jobs:
            _row_copy(src, idx_ref[base + r * stride], dst, r, sem).start()
        return carry

    def wait(r, carry):
        for _, dst in jobs:
            _row_copy(src, 0, dst, r, sem).wait()
        return carry

    lax.fori_loop(0, n, issue, 0, unroll=8)
    lax.fori_loop(0, n, wait, 0, unroll=8)


def _gather_kernel(idx_ref, flag_ref, src_hbm, o_ref, sem):
    tg = o_ref.shape[0]
    i = pl.program_id(0)

    @pl.when(flag_ref[i] != 0)
    def _():
        _gather_rows(idx_ref, [(i * tg, o_ref)], 1, tg, src_hbm, sem)

    @pl.when(flag_ref[i] == 0)
    def _():
        o_ref[...] = jnp.zeros_like(o_ref)


def _gather(src, idx, flags, tg, name):
    n = idx.shape[0]
    blk = (tg,) + src.shape[1:]
    grid_spec = pltpu.PrefetchScalarGridSpec(
        num_scalar_prefetch=2, grid=(n // tg,),
        in_specs=[pl.BlockSpec(memory_space=pl.ANY)],
        out_specs=pl.BlockSpec(blk, lambda i, *_: (i,) + (0,) * (len(blk) - 1)),
        scratch_shapes=[pltpu.SemaphoreType.DMA(())])
    return pl.pallas_call(
        _gather_kernel, grid_spec=grid_spec,
        out_shape=jax.ShapeDtypeStruct((n,) + src.shape[1:], src.dtype),
        compiler_params=_params(1), name=name)(idx, flags, src)


def _resid_norm_kernel(*refs, mode, want_x, want_h, gate_idx, shift_idx, scale_idx, split_in,
                       split_out, n_first):
    it = iter(refs)
    dest_ref = next(it) if mode == "routed" else None
    x_refs = [next(it) for _ in range(2 if split_in else 1)]
    if mode == "dense":
        y_ref = next(it)
    elif mode == "routed":
        ys_hbm, route_ref = next(it), next(it)
    if mode is not None:
        mod_res_ref, g_post_ref = next(it), next(it)
    if want_h:
        mod_pre_ref, g_pre_ref = next(it), next(it)
    xo_refs = [next(it) for _ in range((2 if split_out else 1) if want_x else 0)]
    h_ref = next(it) if want_h else None
    i = pl.program_id(0)
    x = x_refs[0][...]
    if split_in:
        x = jnp.where(i < n_first, x, x_refs[1][...])
    if mode is not None:
        if mode == "routed":
            buf1, buf2, sem = next(it), next(it), next(it)
            tm = x.shape[0]
            base = i * tm * TOP_K
            _gather_rows(dest_ref, [(base, buf1), (base + 1, buf2)], TOP_K, tm, ys_hbm, sem)
            route = route_ref[...]
            y = route[:, 2:3] * buf1[...] + route[:, 3:4] * buf2[...]
        else:
            y = y_ref[...]
        gate = mod_res_ref[0, gate_idx:gate_idx + 1, :]
        x = x + gate * _rms(y, g_post_ref[...])
    if want_x and split_out:
        @pl.when(i < n_first)
        def _():
            xo_refs[0][...] = x

        @pl.when(i >= n_first)
        def _():
            xo_refs[1][...] = x
    elif want_x:
        xo_refs[0][...] = x
    if want_h:
        scale = mod_pre_ref[0, scale_idx:scale_idx + 1, :]
        shift = mod_pre_ref[0, shift_idx:shift_idx + 1, :]
        h_ref[...] = (_rms(x, g_pre_ref[...]) * (1.0 + scale) + shift).astype(h_ref.dtype)


def _resid_norm(x, sample_of_tile, tm, name, *, y=None, routed=None, mod_res=None, g_post=None,
                gate_idx=0, mod_pre=None, g_pre=None, shift_idx=0, scale_idx=0, split_out=False,
                n_first=0):
    split_in = isinstance(x, (tuple, list))
    xs_in = list(x) if split_in else [x]
    t, d = sum(a.shape[0] for a in xs_in), xs_in[0].shape[1]
    mode = "dense" if y is not None else ("routed" if routed is not None else None)
    want_h = mod_pre is not None
    want_x = mode is not None
    row = pl.BlockSpec((tm, d), lambda i, *_: (i, 0))
    first = pl.BlockSpec((tm, d), lambda i, *_: (jnp.minimum(i, n_first - 1), 0))
    rest = pl.BlockSpec((tm, d), lambda i, *_: (jnp.maximum(i - n_first, 0), 0))
    vec = pl.BlockSpec((1, d), lambda i, *_: (0, 0))
    mod = pl.BlockSpec((1, 6, d), lambda i, *_: (sample_of_tile(i), 0, 0))
    in_specs, args = ([first, rest] if split_in else [row]), xs_in
    out_specs, out_shape, scratch = [], [], []
    if mode == "dense":
        in_specs.append(row)
        args.append(y)
    elif mode == "routed":
        ys, dest, route = routed
        in_specs += [pl.BlockSpec(memory_space=pl.ANY),
                     pl.BlockSpec((tm, LANE), lambda i, *_: (i, 0))]
        args += [ys, route]
        scratch = [pltpu.VMEM((tm, d), F32), pltpu.VMEM((tm, d), F32),
                   pltpu.SemaphoreType.DMA(())]
    if mode is not None:
        in_specs += [mod, vec]
        args += [mod_res, g_post.reshape(1, d)]
    if want_h:
        in_specs += [mod, vec]
        args += [mod_pre, g_pre.reshape(1, d)]
    if want_x and split_out:
        out_specs += [first, rest]
        out_shape += [jax.ShapeDtypeStruct((n_first * tm, d), F32),
                      jax.ShapeDtypeStruct((t - n_first * tm, d), F32)]
    elif want_x:
        out_specs.append(row)
        out_shape.append(jax.ShapeDtypeStruct((t, d), F32))
    if want_h:
        out_specs.append(row)
        out_shape.append(jax.ShapeDtypeStruct((t, d), BF16))
    kern = functools.partial(_resid_norm_kernel, mode=mode, want_x=want_x, want_h=want_h,
                             gate_idx=gate_idx, shift_idx=shift_idx, scale_idx=scale_idx,
                             split_in=split_in, split_out=split_out, n_first=n_first)
    grid_spec = pltpu.PrefetchScalarGridSpec(
        num_scalar_prefetch=1 if mode == "routed" else 0, grid=(t // tm,),
        in_specs=in_specs, out_specs=out_specs, scratch_shapes=scratch)
    if mode == "routed":
        args = [dest] + args
    outs = pl.pallas_call(kern, grid_spec=grid_spec, out_shape=out_shape,
                          compiler_params=_params(1), name=name)(*args)
    n_x = (2 if split_out else 1) if want_x else 0
    x_new = (tuple(outs[:2]) if split_out else outs[0]) if want_x else x
    h = outs[n_x] if want_h else None
    return x_new, h


def _norm_kernel(x_ref, g_ref, o_ref):
    o_ref[...] = _rms(x_ref[...], g_ref[...])


def _norm_cols(arr, width, col_block, g, tm, name):
    t = arr.shape[0]
    return pl.pallas_call(
        _norm_kernel, grid=(t // tm,),
        in_specs=[pl.BlockSpec((tm, width), lambda i: (i, col_block)),
                  pl.BlockSpec((1, width), lambda i: (0, 0))],
        out_specs=pl.BlockSpec((tm, width), lambda i: (i, 0)),
        out_shape=jax.ShapeDtypeStruct((t, width), F32),
        compiler_params=_params(1), name=name)(arr, g.reshape(1, width))


def _swap_pairs(w, rope, axis=-1):
    q = rope // 4
    part = lambda a, b: lax.slice_in_dim(w, a, b, axis=axis)
    return jnp.concatenate([part(q, 2 * q), part(0, q), part(3 * q, 4 * q), part(2 * q, 3 * q)],
                           axis=axis)


def _prep_kernel(misc_ref, cos_ref, sin_ref, wg_ref, bg_ref, kr_ref, la_ref):
    misc = misc_ref[:, :LANE]
    kr = misc * cos_ref[...] + misc_ref[:, LANE:] * sin_ref[...]
    kr_ref[...] = kr.astype(kr_ref.dtype)
    z = _dot(misc.astype(BF16), wg_ref[...]) + bg_ref[...]
    log_sig = jnp.minimum(z, 0.0) - jnp.log1p(jnp.exp(-jnp.abs(z)))
    la_ref[...] = log_sig * (1.0 / GATE_NORM)


def _attn_kernel(*refs, n_seg, n_heads, group, scale):
    qn_ref, qr_ref = refs[:2]
    kv_refs = refs[2:2 + n_seg]
    kr_refs = refs[2 + n_seg:2 + 2 * n_seg]
    o_ref = refs[-1]
    hw = QK_NOPE + V_HEAD
    krs = [kr_ref[...] for kr_ref in kr_refs]
    add = lambda a, b: a + b
    c = scale * math.log2(math.e)
    for h0 in range(0, n_heads, group):
        hds = range(h0, min(h0 + group, n_heads))
        scores = []
        for hd in hds:
            q = jnp.concatenate([qn_ref[:, hd * QK_NOPE:(hd + 1) * QK_NOPE],
                                 qr_ref[:, hd * LANE:(hd + 1) * LANE]], axis=1)
            scores.append([
                _dot_nt(q, jnp.concatenate([kv_ref[:, hd * hw:hd * hw + QK_NOPE], kr], axis=1))
                for kv_ref, kr in zip(kv_refs, krs)])
        ms = [functools.reduce(jnp.maximum, [jnp.max(s, axis=-1, keepdims=True) for s in ss])
              for ss in scores]
        ps = [[jnp.exp2((s - m) * c) for s in ss] for ss, m in zip(scores, ms)]
        invs = [1.0 / functools.reduce(add, [jnp.sum(p, axis=-1, keepdims=True) for p in pp])
                for pp in ps]
        for hd, pp, inv in zip(hds, ps, invs):
            o = functools.reduce(add, [
                _dot((p * inv).astype(BF16), kv_ref[:, hd * hw + QK_NOPE:(hd + 1) * hw])
                for p, kv_ref in zip(pp, kv_refs)])
            o_ref[:, hd * V_HEAD:(hd + 1) * V_HEAD] = o.astype(o_ref.dtype)


def _attention(q_nope, q_rope, kvs, krs, *, n_batch, t_q, q_row0, tq, n_heads, group, scale, name,
               o_prev):
    hw = QK_NOPE + V_HEAD
    nq = t_q // tq
    q_map = lambda b, g, i: (q_row0 // tq + b * nq + i, g)
    in_specs = [pl.BlockSpec((tq, n_heads * QK_NOPE), q_map),
                pl.BlockSpec((tq, n_heads * LANE), q_map)]
    args = [q_nope, q_rope]
    for arr, tk, row0 in kvs:
        in_specs.append(pl.BlockSpec((tk, n_heads * hw),
                                     lambda b, g, i, tk=tk, row0=row0: (row0 // tk + b, g)))
        args.append(arr)
    for arr, tk, row0 in krs:
        in_specs.append(pl.BlockSpec((tk, LANE),
                                     lambda b, g, i, tk=tk, row0=row0: (row0 // tk + b, 0)))
        args.append(arr)
    aliases = {len(args): 0}
    in_specs.append(pl.BlockSpec(memory_space=pl.ANY))
    args.append(o_prev)
    kern = functools.partial(_attn_kernel, n_seg=len(kvs), n_heads=n_heads, group=group,
                             scale=scale)
    return pl.pallas_call(
        kern, grid=(n_batch, MLA_HEADS // n_heads, nq), in_specs=in_specs,
        out_specs=pl.BlockSpec((tq, n_heads * V_HEAD), q_map),
        out_shape=jax.ShapeDtypeStruct(o_prev.shape, o_prev.dtype),
        input_output_aliases=aliases,
        compiler_params=_params(3), name=name)(*args)


def _split3(x):
    hi = x.astype(BF16)
    r1 = x - hi.astype(F32)
    mid = r1.astype(BF16)
    lo = (r1 - mid.astype(F32)).astype(BF16)
    return hi, mid, lo


def _gla_kernel(*refs, has_init, want_states, n_alias, n_heads, q_scale, layer, all_layers):
    it = iter(refs)
    q_ref, k_ref, v_ref, laf_ref, lab_ref, gout_ref, ggla_ref = (next(it) for _ in range(7))
    s0_refs = (next(it), next(it)) if has_init else None
    for _ in range(n_alias):
        next(it)
    o_ref = next(it)
    s_out_refs = (next(it), next(it)) if want_states else None
    st, vb, qd_s, ks_s, dec_s, oacc = (next(it) for _ in range(6))
    c = GLA_CHUNK
    t_b = q_ref.shape[0]
    dk, dv = q_ref.shape[1] // n_heads, v_ref.shape[1] // n_heads
    n_chunks = t_b // c
    rb = min(t_b, 4 * c)
    row = lax.broadcasted_iota(jnp.int32, (rb, rb), 0)
    col = lax.broadcasted_iota(jnp.int32, (rb, rb), 1)
    same = (row // c) == (col // c)
    masks = (same & (col <= row), same & (col >= row))
    tris = [jnp.where(m, 1.0, 0.0).astype(BF16) for m in masks]
    la_refs = (laf_ref, lab_ref)
    streams = [(g, d) for g in range(n_heads) for d in range(2)]
    kcols = [pl.ds(g * dk, dk) for g in range(n_heads)]
    vcols = [pl.ds(g * dv, dv) for g in range(n_heads)]

    vb[...] = v_ref[...].astype(BF16)
    for s, (g, d) in enumerate(streams):
        st[s] = s0_refs[d][0, g].T if has_init else jnp.zeros(st.shape[1:], F32)

    for r0 in range(0, t_b, rb):
        rows = pl.ds(r0, rb)
        for s, (g, d) in enumerate(streams):
            hi, mid, lo = _split3(la_refs[d][rows, kcols[g]])
            b = (_dot(tris[d], hi) + _dot(tris[d], mid)) + _dot(tris[d], lo)
            end = c - 1 if d == 0 else 0
            b_end = jnp.concatenate(
                [jnp.broadcast_to(b[k * c + end:k * c + end + 1, :], (c, dk))
                 for k in range(rb // c)], axis=0)
            kc = k_ref[rows, kcols[g]]
            qd = (q_ref[rows, kcols[g]] * q_scale * jnp.exp(b)).astype(BF16)
            kd = (kc * jnp.exp(-b)).astype(BF16)
            qd_s[s, rows, :] = qd
            ks_s[s, rows, :] = (kc * jnp.exp(b_end - b)).astype(BF16)
            dec_s[s, rows, :] = jnp.exp(b_end)
            a = jnp.where(masks[d], _dot_nt(qd, kd), 0.0).astype(BF16)
            o_in = _dot(a, vb[rows, vcols[g]])
            if d == 0:
                oacc[rows, vcols[g]] = o_in
            else:
                oacc[rows, vcols[g]] += o_in

    for i in range(n_chunks):
        for s, (g, d) in enumerate(streams):
            c0 = (i if d == 0 else n_chunks - 1 - i) * c
            rows = pl.ds(c0, c)
            u = _dot_tn(vb[rows, vcols[g]], ks_s[s, rows, :])
            s_t = st[s]
            oacc[rows, vcols[g]] += _dot_nt(qd_s[s, rows, :], s_t.astype(BF16))
            st[s] = s_t * dec_s[s, pl.ds(c0, 1), :] + u

    for g in range(n_heads):
        o_ref[:, vcols[g]] = (_rms(oacc[:, vcols[g]], ggla_ref[...])
                              * _silu(gout_ref[:, vcols[g]])).astype(o_ref.dtype)
    if want_states:
        for s, (g, d) in enumerate(streams):
            s_fin = st[s].T
            if all_layers:
                for ll in range(s_out_refs[d].shape[1]):
                    s_out_refs[d][0, ll, g] = s_fin if ll == layer else jnp.zeros_like(s_fin)
            else:
                s_out_refs[d][0, g] = s_fin


def _gla(proj, la, g_gla, *, n_batch, t_b, row0, heads, n_heads, dk, dv, col_q, col_k, col_v,
         col_g, name, o_prev, s0=None, layer=0, st_prev=None, st_depth=0):
    has_init = s0 is not None
    rb = row0 // t_b
    wk, wv = n_heads * dk, n_heads * dv
    assert col_q % wk == 0 and col_k % wk == 0 and col_v % wv == 0 and col_g % wv == 0
    in_specs = [
        pl.BlockSpec((t_b, wk), lambda b, h: (rb + b, col_q // wk + h)),
        pl.BlockSpec((t_b, wk), lambda b, h: (rb + b, col_k // wk + h)),
        pl.BlockSpec((t_b, wv), lambda b, h: (rb + b, col_v // wv + h)),
        pl.BlockSpec((t_b, wk), lambda b, h: (rb + b, h)),
        pl.BlockSpec((t_b, wk), lambda b, h: (rb + b, heads // n_heads + h)),
        pl.BlockSpec((t_b, wv), lambda b, h: (rb + b, col_g // wv + h)),
        pl.BlockSpec((1, dv), lambda b, h: (0, 0)),
    ]
    args = [proj, proj, proj, la, la, proj, g_gla.reshape(1, dv)]
    st_spec = pl.BlockSpec((1, None, n_heads, dk, dv), lambda b, h: (b, layer, h, 0, 0))
    if has_init:
        in_specs += [st_spec, st_spec]
        args += list(s0)
    out_specs = [pl.BlockSpec((t_b, wv), lambda b, h: (rb + b, h))]
    out_shape = [jax.ShapeDtypeStruct(o_prev.shape, o_prev.dtype)]
    aliases = {len(args): 0}
    in_specs.append(pl.BlockSpec(memory_space=pl.ANY))
    args.append(o_prev)
    want_states = st_prev is not None or st_depth > 0
    if st_prev is not None:
        out_specs += [st_spec, st_spec]
        for k, arr in enumerate(st_prev):
            out_shape.append(jax.ShapeDtypeStruct(arr.shape, arr.dtype))
            aliases[len(args)] = 1 + k
            in_specs.append(pl.BlockSpec(memory_space=pl.ANY))
            args.append(arr)
    elif want_states:
        out_specs += [pl.BlockSpec((1, st_depth, n_heads, dk, dv),
                                   lambda b, h: (b, 0, h, 0, 0))] * 2
        out_shape += [jax.ShapeDtypeStruct((n_batch, st_depth, heads, dk, dv), F32)] * 2
    kern = functools.partial(_gla_kernel, has_init=has_init, want_states=want_states,
                             n_alias=len(aliases), n_heads=n_heads, q_scale=dk ** -0.5,
                             layer=layer, all_layers=st_prev is None)
    n_str = 2 * n_heads
    return pl.pallas_call(
        kern, grid=(n_batch, heads // n_heads), in_specs=in_specs, out_specs=out_specs,
        out_shape=out_shape, input_output_aliases=aliases,
        scratch_shapes=[pltpu.VMEM((n_str, dv, dk), F32), pltpu.VMEM((t_b, wv), BF16),
                        pltpu.VMEM((n_str, t_b, dk), BF16), pltpu.VMEM((n_str, t_b, dk), BF16),
                        pltpu.VMEM((n_str, t_b, dk), F32), pltpu.VMEM((t_b, wv), F32)],
        compiler_params=_params(2), name=name)(*args)


def _router_kernel(h_ref, w_ref, o_ref, *, n_experts):
    logits = _dot(h_ref[...].astype(BF16), w_ref[...])
    lane = lax.broadcasted_iota(jnp.int32, logits.shape, 1).astype(F32)
    neg = jnp.float32(-jnp.inf)
    lg = jnp.where(lane < n_experts, logits, neg)
    m1 = jnp.max(lg, axis=-1, keepdims=True)
    i1 = jnp.min(jnp.where(lg == m1, lane, LANE), axis=-1, keepdims=True)
    lg2 = jnp.where(lane == i1, neg, lg)
    m2 = jnp.max(lg2, axis=-1, keepdims=True)
    i2 = jnp.min(jnp.where(lg2 == m2, lane, LANE), axis=-1, keepdims=True)
    e2 = jnp.exp(m2 - m1)
    inv = 1.0 / (1.0 + e2)
    o_ref[...] = (jnp.where(lane == 0, i1, 0.0) + jnp.where(lane == 1, i2, 0.0)
                  + jnp.where(lane == 2, inv, 0.0) + jnp.where(lane == 3, e2 * inv, 0.0))


def _moe_up_kernel(info_ref, x_ref, wg_ref, wu_ref, o_ref, *, n_sub):
    s = pl.program_id(0)
    flag0 = pl.num_programs(0) + s * n_sub
    sub = x_ref.shape[0] // n_sub

    @pl.when(info_ref[flag0] != 0)
    def _():
        wg = wg_ref[...].astype(BF16)
        wu = wu_ref[...].astype(BF16)
        for k in range(n_sub):
            rows = pl.ds(k * sub, sub)

            @pl.when(info_ref[flag0 + k] != 0)
            def _():
                a = x_ref[rows, :]
                o_ref[rows, :] = (_silu(_dot(a, wg)) * _dot(a, wu)).astype(o_ref.dtype)

            @pl.when(info_ref[flag0 + k] == 0)
            def _():
                o_ref[rows, :] = jnp.zeros((sub, o_ref.shape[1]), o_ref.dtype)

    @pl.when(info_ref[flag0] == 0)
    def _():
        o_ref[...] = jnp.zeros_like(o_ref)


def _moe_up(xs, w_gate, w_up, lead, info, *, run, sub, tn, name):
    n_rows, d = xs.shape
    n_out = w_gate.shape[-1]
    n_run = n_rows // run
    n_sub = run // sub

    def w_map(s, j, info):
        return tuple(lead) + (info[s], 0, jnp.where(info[n_run + s * n_sub] != 0, j, 0))

    w_spec = pl.BlockSpec((None,) * (len(lead) + 1) + (d, tn), w_map)
    grid_spec = pltpu.PrefetchScalarGridSpec(
        num_scalar_prefetch=1, grid=(n_run, n_out // tn),
        in_specs=[pl.BlockSpec((run, d), lambda s, j, info: (s, 0)), w_spec, w_spec],
        out_specs=pl.BlockSpec((run, tn), lambda s, j, info: (s, j)))
    return pl.pallas_call(
        functools.partial(_moe_up_kernel, n_sub=n_sub), grid_spec=grid_spec,
        out_shape=jax.ShapeDtypeStruct((n_rows, n_out), BF16),
        compiler_params=_params(2), name=name)(info, xs, w_gate, w_up)


def _routing_tables(route, n_experts, tile, run):
    t = route.shape[0]
    n_assign = t * TOP_K
    n_rows = -(-(n_assign + n_experts * (run - 1)) // run) * run
    e_flat = route[:, :TOP_K].astype(jnp.int32).reshape(n_assign)
    onehot = (e_flat[:, None] == jnp.arange(n_experts, dtype=jnp.int32)[None, :]).astype(jnp.int32)
    csum = jnp.cumsum(onehot, axis=0)
    rank = jnp.take_along_axis(csum, e_flat[:, None], axis=1)[:, 0] - 1
    counts = csum[-1]
    padded = ((counts + run - 1) // run) * run
    ends = jnp.cumsum(padded)
    starts = ends - padded
    dest = starts[e_flat] + rank
    tok_of = jnp.zeros((n_rows,), jnp.int32).at[dest].set(
        jnp.arange(n_assign, dtype=jnp.int32) // TOP_K, unique_indices=True)
    tile_start = jnp.arange(n_rows // tile, dtype=jnp.int32) * tile
    tile_expert = jnp.minimum(
        jnp.sum(tile_start[:, None] >= ends[None, :], axis=1, dtype=jnp.int32), n_experts - 1)
    flags = (tile_start < (starts + counts)[tile_expert]).astype(jnp.int32)
    run_info = jnp.concatenate([tile_expert[::run // tile], flags])
    tile_idx = jnp.arange(n_rows // tile, dtype=jnp.int32)
    tile_info = jnp.concatenate([tile_expert, flags,
                                 lax.cummax(jnp.where(flags != 0, tile_idx, 0))])
    return tok_of, dest.astype(jnp.int32), flags, run_info, tile_info


def _rope_tables(n_lat, rope, tm_id):
    axis_half = rope // 4
    rows = n_lat // GRID_W
    r = jnp.repeat(jnp.arange(rows, dtype=F32), GRID_W)
    c = jnp.tile(jnp.arange(GRID_W, dtype=F32), rows)
    inv = ROPE_THETA ** (-jnp.arange(axis_half, dtype=F32) / axis_half)
    ar, ac = r[:, None] * inv, c[:, None] * inv
    cos = jnp.concatenate([jnp.cos(ar), jnp.cos(ar), jnp.cos(ac), jnp.cos(ac)], axis=1)
    sin = jnp.concatenate([-jnp.sin(ar), jnp.sin(ar), -jnp.sin(ac), jnp.sin(ac)], axis=1)
    cos = jnp.concatenate([jnp.ones((tm_id, rope), F32), cos], axis=0)
    sin = jnp.concatenate([jnp.zeros((tm_id, rope), F32), sin], axis=0)
    return cos, sin


def kernel(x_prompt, x_sample, cache_ckv, cache_krope, state_gla_fwd, state_gla_bwd, c, c_ctx, w_ada, b_ada, g_pre_mix, g_post_mix, g_pre_ffn, g_post_ffn, w_in, g_q, w_uq, g_kv, w_ukv, w_gate_f, b_gate_f, w_gate_b, b_gate_b, g_gla, w_pa, w_pb, w_o, w_ff_gate, w_ff_up, w_ff_down, w_router, w_ex_gate, w_ex_up, w_ex_down):
    n_ctx_b, t_ctx, d = x_prompt.shape
    n_lat_b, t_lat, _ = x_sample.shape
    depth = w_in.shape[0]
    past = cache_ckv.shape[2]
    q_lora, kv_lora = g_q.shape[1], g_kv.shape[1]
    rope = cache_krope.shape[3]
    heads, dk, dv = state_gla_fwd.shape[2:]
    rank = w_gate_f.shape[1]
    n_experts = w_router.shape[2]
    hq = QK_NOPE + rope
    hw = QK_NOPE + V_HEAD
    n_ctx, n_lat = n_ctx_b * t_ctx, n_lat_b * t_lat
    t = n_ctx + n_lat
    assert 2 * rope == LANE and 2 * rank <= LANE - rope

    tm = _tile(math.gcd(n_ctx, t_lat), 512)
    tm_s = _tile(math.gcd(n_ctx, t_lat), 1024)
    n_ctx_tiles = n_ctx // tm
    lat_tiles = t_lat // tm

    def sample_of_tile(i):
        return jnp.where(i < n_ctx_tiles, 0, 1 + (i - n_ctx_tiles) // lat_tiles)

    def pos_of_tile(i, tile=tm):
        first = n_ctx // tile
        return jnp.where(i < first, 0, tm_s // tile + (i - first) % (t_lat // tile))

    sizes = (q_lora, kv_lora, rope, heads * dk, heads * dk, heads * dv, rank, rank,
             heads * dv, d, d)
    offs = [0]
    for s in sizes:
        offs.append(offs[-1] + s)
    main_groups = (0, 1, 3, 4, 5, 8, 9, 10)
    col = {}
    acc = 0
    for gidx in main_groups:
        col[gidx] = acc
        acc += sizes[gidx]
    n_main = acc
    col_q, col_k, col_v, col_g, col_a, col_b = col[3], col[4], col[5], col[8], col[9], col[10]

    cos_r, sin_r = _rope_tables(t_lat, rope, tm_s)
    n_tab = cos_r.shape[0]
    cos_k = jnp.concatenate([cos_r, jnp.zeros((n_tab, LANE - rope), F32)], axis=1)
    sin_k = jnp.concatenate([sin_r, jnp.zeros((n_tab, LANE - rope), F32)], axis=1)

    x = (x_prompt.reshape(n_ctx, d), x_sample.reshape(n_lat, d))
    c_all = jnp.concatenate([c_ctx[None, :], c, jnp.zeros((8 - 1 - n_lat_b, d), F32)], axis=0)

    def modulation(l):
        def epi(accs, e):
            return accs[0] + e[0]
        tn = _tile(6 * d, 1024)
        m = _matmul([(c_all, d, 0)], [(w_ada, (l,))], n_out=6 * d, tm=8, tn=tn, out_dtype=F32,
                    epilogue=epi, prologue=lambda k, a, e: _silu(a), name=f"ada{l}",
                    extras=[(b_ada.reshape(depth, 1, 6 * d), (None, 1, tn),
                             lambda j, i, l=l: (l, 0, j))])
        return m.reshape(8, 6, d)

    mods = [modulation(l) for l in range(depth)]
    _, h = _resid_norm(x, sample_of_tile, tm, "prenorm0", mod_pre=mods[0], g_pre=g_pre_mix[0],
                       shift_idx=0, scale_idx=1, n_first=n_ctx_tiles)

    new_ckv, new_krope = [], []
    states = None
    w_in_t = jnp.swapaxes(w_in, 1, 2)
    for l in range(depth):
        n_all = n_main + 2 * LANE

        def repack_kernel(w_ref, o_ref):
            w_kr = w_ref[offs[2]:offs[3], :]
            lanes = w_ref.shape[1]
            pieces = [w_ref[offs[gidx]:offs[gidx + 1], :] for gidx in main_groups]
            pieces += [w_kr, w_ref[offs[6]:offs[8], :],
                       jnp.zeros((LANE - rope - 2 * rank, lanes), F32),
                       _swap_pairs(w_kr, rope, 0), jnp.zeros((LANE - rope, lanes), F32)]
            o_ref[...] = jnp.concatenate(pieces, axis=0).astype(BF16)

        tk_r = _tile(d, 256)
        w_all = pl.pallas_call(
            repack_kernel, grid=(d // tk_r,),
            in_specs=[pl.BlockSpec((None, offs[-1], tk_r), lambda i, l=l: (l, 0, i))],
            out_specs=pl.BlockSpec((n_all, tk_r), lambda i: (0, i)),
            out_shape=jax.ShapeDtypeStruct((n_all, d), BF16),
            compiler_params=_params(1), name=f"w_in_repack{l}")(w_in_t)
        proj = _matmul([(h, d, 0)], [(w_all, ())], n_out=n_all, tm=tm_s, tn=_tile(n_all, 1280),
                       out_dtype=F32, epilogue=_first, name=f"w_in{l}", w_rows=True)

        w_gate = jnp.zeros((LANE, 2 * heads * dk), F32)
        w_gate = w_gate.at[rope:rope + rank, :heads * dk].set(w_gate_f[l])
        w_gate = w_gate.at[rope + rank:rope + 2 * rank, heads * dk:].set(w_gate_b[l])
        b_gate = jnp.concatenate([b_gate_f[l], b_gate_b[l]]).reshape(1, 2 * heads * dk)
        kr_self, la = pl.pallas_call(
            _prep_kernel, grid=(t // tm,),
            in_specs=[pl.BlockSpec((tm, 2 * LANE), lambda i: (i, n_main // (2 * LANE))),
                      pl.BlockSpec((tm, LANE), lambda i: (pos_of_tile(i), 0)),
                      pl.BlockSpec((tm, LANE), lambda i: (pos_of_tile(i), 0)),
                      pl.BlockSpec((LANE, 2 * heads * dk), lambda i: (0, 0)),
                      pl.BlockSpec((1, 2 * heads * dk), lambda i: (0, 0))],
            out_specs=[pl.BlockSpec((tm, LANE), lambda i: (i, 0)),
                       pl.BlockSpec((tm, 2 * heads * dk), lambda i: (i, 0))],
            out_shape=[jax.ShapeDtypeStruct((t, LANE), BF16),
                       jax.ShapeDtypeStruct((t, 2 * heads * dk), F32)],
            compiler_params=_params(1), name=f"prep{l}")(proj, cos_k, sin_k, w_gate.astype(BF16), b_gate)

        wq = w_uq[l].reshape(q_lora, MLA_HEADS, hq)
        wq_r = wq[:, :, QK_NOPE:]
        pad = jnp.zeros((q_lora, MLA_HEADS, LANE - rope), F32)
        as_cols = lambda a: a.reshape(q_lora, -1).astype(BF16)
        wq_n = as_cols(wq[:, :, :QK_NOPE])
        wq_s = as_cols(jnp.concatenate([_swap_pairs(wq_r, rope), pad], axis=2))
        wq_r = as_cols(jnp.concatenate([wq_r, pad], axis=2))
        q_in = [(proj, q_lora, col[0] // q_lora)]
        q_norm = lambda k, a, e: _rms(a, e[0])
        g_q_extra = (g_q[l].reshape(1, q_lora), (1, q_lora), lambda j, i: (0, 0))
        tn_q = _tile(MLA_HEADS * LANE, 1024)

        def rope_epi(accs, e, tn_q=tn_q):
            cos, sin = e[1], e[2]
            return jnp.concatenate([accs[0][:, s:s + LANE] * cos + accs[1][:, s:s + LANE] * sin
                                    for s in range(0, tn_q, LANE)], axis=1)

        q_nope = _matmul(q_in, [(wq_n, ())], n_out=MLA_HEADS * QK_NOPE, tm=tm_s,
                         tn=_tile(MLA_HEADS * QK_NOPE, 1024), out_dtype=BF16, epilogue=_first,
                         name=f"w_uq_nope{l}", prologue=q_norm, extras=[g_q_extra])
        q_rope = _matmul(q_in, [(wq_r, ()), (wq_s, ())], a_of_w=[0, 0], n_out=MLA_HEADS * LANE,
                         tm=tm_s, tn=tn_q, out_dtype=BF16, epilogue=rope_epi,
                         name=f"w_uq_rope{l}", prologue=q_norm,
                         extras=[g_q_extra,
                                 (cos_k, (tm_s, LANE), lambda j, i: (pos_of_tile(i, tm_s), 0)),
                                 (sin_k, (tm_s, LANE), lambda j, i: (pos_of_tile(i, tm_s), 0))])

        c_kv = _norm_cols(proj, kv_lora, col[1] // kv_lora, g_kv[l], tm, f"ckv_norm{l}")
        tn_kv = _tile(MLA_HEADS * hw, 1024)
        kv_self = _matmul([(c_kv, kv_lora, 0)], [(w_ukv, (l,))], n_out=MLA_HEADS * hw, tm=tm_s,
                          tn=tn_kv, out_dtype=BF16, epilogue=_first, name=f"w_ukv{l}")
        ckv_cache = cache_ckv[:, l].reshape(n_lat_b * past, kv_lora)
        kv_cache = _matmul([(ckv_cache, kv_lora, 0)], [(w_ukv, (l,))], n_out=MLA_HEADS * hw,
                           tm=_tile(n_lat_b * past, 512), tn=tn_kv, out_dtype=BF16,
                           epilogue=_first, name=f"w_ukv_cache{l}")
        kr_cache = jnp.pad(cache_krope[:, l].reshape(n_lat_b * past, rope),
                           ((0, 0), (0, LANE - rope))).astype(BF16)

        scale = hq ** -0.5
        attn = _attention(q_nope, q_rope, [(kv_self, t_ctx, 0)], [(kr_self, t_ctx, 0)],
                          n_batch=n_ctx_b, t_q=t_ctx, q_row0=0, tq=_tile(t_ctx, 256),
                          n_heads=MLA_HEADS, group=4, scale=scale, name=f"attn_ctx{l}",
                          o_prev=jnp.zeros((t, MLA_HEADS * V_HEAD), BF16))
        attn = _attention(q_nope, q_rope, [(kv_self, t_lat, n_ctx), (kv_cache, past, 0)],
                          [(kr_self, t_lat, n_ctx), (kr_cache, past, 0)], n_batch=n_lat_b,
                          t_q=t_lat, q_row0=n_ctx, tq=_tile(t_lat, 512),
                          n_heads=math.gcd(MLA_HEADS, 4), group=2, scale=scale, name=f"attn_lat{l}",
                          o_prev=attn)

        gla_kw = dict(heads=heads, dk=dk, dv=dv, col_q=col_q, col_k=col_k, col_v=col_v,
                      col_g=col_g, layer=l)
        gla, *states = _gla(proj, la, g_gla[l], n_batch=n_ctx_b, t_b=t_ctx, row0=0,
                            n_heads=math.gcd(heads, 2), name=f"gla_ctx{l}",
                            o_prev=jnp.zeros((t, heads * dv), BF16), st_prev=states,
                            st_depth=depth, **gla_kw)
        gla, = _gla(proj, la, g_gla[l], n_batch=n_lat_b, t_b=t_lat, row0=n_ctx, n_heads=1,
                    s0=(state_gla_fwd, state_gla_bwd), name=f"gla_lat{l}", o_prev=gla, **gla_kw)

        new_ckv.append(c_kv[:n_ctx].reshape(n_ctx_b, t_ctx, kv_lora))
        new_krope.append(proj[:n_ctx, n_main:n_main + rope].reshape(n_ctx_b, t_ctx, rope))

        tn_m = _tile(math.gcd(col_a, col_b, d), 1024)
        tm_m = _tile(tm, 256, 8)

        def merge_epi(accs, e):
            return jax.nn.sigmoid(e[0]) * accs[0] + jax.nn.sigmoid(e[1]) * accs[1]

        merged = _matmul([(attn, MLA_HEADS * V_HEAD, 0), (gla, heads * dv, 0)],
                         [(w_pa, (l,)), (w_pb, (l,))], n_out=d, tm=tm_m, tn=tn_m, out_dtype=BF16,
                         epilogue=merge_epi, name=f"merge{l}",
                         extras=[(proj, (tm_m, tn_m), lambda j, i: (i, col_a // tn_m + j)),
                                 (proj, (tm_m, tn_m), lambda j, i: (i, col_b // tn_m + j))])
        y = _matmul([(merged, d, 0)], [(w_o, (l,))], n_out=d, tm=tm_s, tn=_tile(d, 1024),
                    out_dtype=F32, epilogue=_first, name=f"w_o{l}")
        moe = l % 2 == 1
        x, h = _resid_norm(x, sample_of_tile, tm, f"mix_resid{l}", y=y, mod_res=mods[l],
                           g_post=g_post_mix[l], gate_idx=2, mod_pre=mods[l], g_pre=g_pre_ffn[l],
                           shift_idx=3, scale_idx=4, n_first=n_ctx_tiles)

        jx = l // 2

        def swiglu_epi(accs, e):
            return _silu(accs[0]) * accs[1]

        ffn_out = {}
        if not moe:
            d_ff = w_ff_gate.shape[2]
            ff = _matmul([(h, d, 0)], [(w_ff_gate, (jx,)), (w_ff_up, (jx,))], a_of_w=[0, 0],
                         n_out=d_ff, tm=tm_s, tn=_tile(d_ff, 512), out_dtype=BF16,
                         epilogue=swiglu_epi, name=f"ffn_up{l}")
            ffn_out["y"] = _matmul([(ff, d_ff, 0)], [(w_ff_down, (jx,))], n_out=d, tm=tm,
                                   tn=_tile(d, 512), out_dtype=F32, epilogue=_first,
                                   name=f"ffn_down{l}")
        else:
            d_ex = w_ex_gate.shape[3]
            w_r = jnp.pad(w_router[jx], ((0, 0), (0, LANE - n_experts))).astype(BF16)
            route = pl.pallas_call(
                functools.partial(_router_kernel, n_experts=n_experts), grid=(t // tm,),
                in_specs=[pl.BlockSpec((tm, d), lambda i: (i, 0)),
                          pl.BlockSpec((d, LANE), lambda i: (0, 0))],
                out_specs=pl.BlockSpec((tm, LANE), lambda i: (i, 0)),
                out_shape=jax.ShapeDtypeStruct((t, LANE), F32),
                compiler_params=_params(1), name=f"router{l}")(h, w_r)
            tile_e = _tile(t * TOP_K, EXPERT_TILE, 8)
            run_e = tile_e * EXPERT_RUN_TILES
            tok_of, dest, flags, run_info, tile_info = _routing_tables(route, n_experts, tile_e,
                                                                       run_e)
            xs = _gather(h.reshape(t, d // LANE, LANE), tok_of, flags, tile_e, f"moe_gather{l}")
            xs = xs.reshape(xs.shape[0], d)
            ff = _moe_up(xs, w_ex_gate, w_ex_up, (jx,), run_info, run=run_e, sub=tile_e,
                         tn=_tile(d_ex, 256), name=f"moe_up{l}")
            ys = _matmul([(ff, d_ex, 0)], [(w_ex_down, (jx, 0))], n_out=d, tm=tile_e,
                         tn=_tile(d, 1024), out_dtype=F32, epilogue=_first, name=f"moe_down{l}",
                         tile_expert=tile_info)
            ffn_out["routed"] = (ys, dest, route)

        if l + 1 < depth:
            x, h = _resid_norm(x, sample_of_tile, tm, f"ffn_resid{l}", mod_res=mods[l],
                               g_post=g_post_ffn[l], gate_idx=5, mod_pre=mods[l + 1],
                               g_pre=g_pre_mix[l + 1], shift_idx=0, scale_idx=1, **ffn_out)
        else:
            x, _ = _resid_norm(x, sample_of_tile, tm, f"ffn_resid{l}", mod_res=mods[l],
                               g_post=g_post_ffn[l], gate_idx=5, split_out=True,
                               n_first=n_ctx_tiles, **ffn_out)

    return (x[0].reshape(n_ctx_b, t_ctx, d), x[1].reshape(n_lat_b, t_lat, d),
            jnp.stack(new_ckv, axis=1), jnp.stack(new_krope, axis=1),
            states[0], states[1])
```

```python
import functools
import math

import jax
import jax.numpy as jnp
from jax import lax
from jax.experimental import pallas as pl
from jax.experimental.pallas import tpu as pltpu

MLA_HEADS = 16
QK_NOPE = 128
V_HEAD = 128
GRID_W = 64
ROPE_THETA = 10000.0
GATE_NORM = 16.0
GLA_CHUNK = 64
TOP_K = 2
EPS = 1e-6
LANE = 128
VMEM_LIMIT = 52 * 1024 * 1024
EXPERT_TILE = 512
EXPERT_RUN_TILES = 4

BF16 = jnp.bfloat16
F32 = jnp.float32


def _params(n_grid, **kw):
    return pltpu.CompilerParams(
        dimension_semantics=("arbitrary",) * n_grid, vmem_limit_bytes=VMEM_LIMIT, **kw)


def _tile(n, target, quantum=LANE):
    if n <= target:
        return n
    t = (target // quantum) * quantum
    while t >= quantum:
        if n % t == 0:
            return t
        t -= quantum
    return n


def _rms(x, g):
    return x * lax.rsqrt(jnp.mean(x * x, axis=-1, keepdims=True) + EPS) * g


def _silu(x):
    return x * jax.nn.sigmoid(x)


def _dot(a, b):
    return jnp.dot(a, b, preferred_element_type=F32)


def _dot_nt(a, b):
    return lax.dot_general(a, b, (((1,), (1,)), ((), ())), preferred_element_type=F32)


def _dot_tn(a, b):
    return lax.dot_general(a, b, (((0,), (0,)), ((), ())), preferred_element_type=F32)


def _mm_kernel(*refs, n_a, n_w, a_of_w, n_e, cast_w, grouped, w_rows, prologue, epilogue):
    if grouped:
        te_ref, refs = refs[0], refs[1:]
    a_refs = refs[:n_a]
    w_refs = refs[n_a:n_a + n_w]
    e_refs = refs[n_a + n_w:n_a + n_w + n_e]
    o_ref = refs[n_a + n_w + n_e]
    wb_refs = refs[n_a + n_w + n_e + 1:]
    i = pl.program_id(1)

    def compute():
        if cast_w:
            fresh = i == 0
            if grouped:
                fresh = fresh | (te_ref[i] != te_ref[jnp.maximum(i - 1, 0)])

            @pl.when(fresh)
            def _():
                for w_ref, wb_ref in zip(w_refs, wb_refs):
                    wb_ref[...] = w_ref[...].astype(BF16)
            w_use = wb_refs
        else:
            w_use = w_refs
        e = [e_ref[...] for e_ref in e_refs]
        a_vals = []
        for k, a_ref in enumerate(a_refs):
            a = a_ref[...]
            if prologue is not None:
                a = prologue(k, a, e)
            a_vals.append(a.astype(BF16))
        dot = _dot_nt if w_rows else _dot
        accs = [dot(a_vals[a_of_w[k]], w_ref[...]) for k, w_ref in enumerate(w_use)]
        o_ref[...] = epilogue(accs, e).astype(o_ref.dtype)

    if grouped:
        has_rows = te_ref[pl.num_programs(1) + i] != 0
        pl.when(has_rows)(compute)

        @pl.when(jnp.logical_not(has_rows))
        def _():
            o_ref[...] = jnp.zeros_like(o_ref)
    else:
        compute()


def _matmul(a_list, w_list, *, n_out, tm, tn, out_dtype, epilogue, name, a_of_w=None,
            prologue=None, extras=(), tile_expert=None, w_rows=False):
    m = a_list[0][0].shape[0]
    a_of_w = a_of_w or list(range(len(w_list)))
    grouped = tile_expert is not None
    assert m % tm == 0 and n_out % tn == 0
    in_specs, args, scratch = [], [], []
    for arr, k, cb in a_list:
        if grouped:
            a_map = lambda j, i, te, cb=cb: (te[2 * (m // tm) + i], cb)
        else:
            a_map = lambda j, i, cb=cb: (i, cb)
        in_specs.append(pl.BlockSpec((tm, k), a_map))
        args.append(arr)
    cast_w = w_list[0][0].dtype != BF16
    for arr, lead in w_list:
        k = arr.shape[-1] if w_rows else arr.shape[-2]
        if grouped:
            im = lambda j, i, te, lead=lead: tuple(lead[:-1]) + (te[i], 0, j)
        elif w_rows:
            im = lambda j, i, lead=lead: tuple(lead) + (j, 0)
        else:
            im = lambda j, i, lead=lead: tuple(lead) + (0, j)
        in_specs.append(pl.BlockSpec((None,) * len(lead) + ((tn, k) if w_rows else (k, tn)), im))
        args.append(arr)
        if cast_w:
            scratch.append(pltpu.VMEM((k, tn), BF16))
    for arr, bs, im in extras:
        in_specs.append(pl.BlockSpec(bs, lambda j, i, *_, im=im: im(j, i)))
        args.append(arr)
    kern = functools.partial(_mm_kernel, n_a=len(a_list), n_w=len(w_list), a_of_w=a_of_w,
                             n_e=len(extras), cast_w=cast_w, grouped=grouped, w_rows=w_rows,
                             prologue=prologue, epilogue=epilogue)
    grid_spec = pltpu.PrefetchScalarGridSpec(
        num_scalar_prefetch=1 if grouped else 0,
        grid=(n_out // tn, m // tm),
        in_specs=in_specs,
        out_specs=pl.BlockSpec((tm, tn), lambda j, i, *_: (i, j)),
        scratch_shapes=scratch)
    if grouped:
        args = [tile_expert] + args
    return pl.pallas_call(
        kern, grid_spec=grid_spec,
        out_shape=jax.ShapeDtypeStruct((m, n_out), out_dtype),
        compiler_params=_params(2), name=name,
    )(*args)


def _first(accs, e):
    return accs[0]


def _row_copy(src, row, dst, r, sem):
    return pltpu.make_async_copy(src.at[pl.ds(row, 1)], dst.at[pl.ds(r, 1)], sem)


def _gather_rows(idx_ref, jobs, stride, n, src, sem):
    def issue(r, carry):
        for base, dst in jobs:
            _row_copy(src, idx_ref[base + r * stride], dst, r, sem).start()
        return carry

    def wait(r, carry):
        for _, dst in jobs:
            _row_copy(src, 0, dst, r, sem).wait()
        return carry

    lax.fori_loop(0, n, issue, 0, unroll=8)
    lax.fori_loop(0, n, wait, 0, unroll=8)


def _gather_kernel(idx_ref, flag_ref, src_hbm, o_ref, sem):
    tg = o_ref.shape[0]
    i = pl.program_id(0)

    @pl.when(flag_ref[i] != 0)
    def _():
        _gather_rows(idx_ref, [(i * tg, o_ref)], 1, tg, src_hbm, sem)

    @pl.when(flag_ref[i] == 0)
    def _():
        o_ref[...] = jnp.zeros_like(o_ref)


def _gather(src, idx, flags, tg, name):
    n = idx.shape[0]
    blk = (tg,) + src.shape[1:]
    grid_spec = pltpu.PrefetchScalarGridSpec(
        num_scalar_prefetch=2, grid=(n // tg,),
        in_specs=[pl.BlockSpec(memory_space=pl.ANY)],
        out_specs=pl.BlockSpec(blk, lambda i, *_: (i,) + (0,) * (len(blk) - 1)),
        scratch_shapes=[pltpu.SemaphoreType.DMA(())])
    return pl.pallas_call(
        _gather_kernel, grid_spec=grid_spec,
        out_shape=jax.ShapeDtypeStruct((n,) + src.shape[1:], src.dtype),
        compiler_params=_params(1), name=name)(idx, flags, src)


def _resid_norm_kernel(*refs, mode, want_x, want_h, gate_idx, shift_idx, scale_idx, split_in,
                       split_out, n_first):
    it = iter(refs)
    dest_ref = next(it) if mode == "routed" else None
    x_refs = [next(it) for _ in range(2 if split_in else 1)]
    if mode == "dense":
        y_ref = next(it)
    elif mode == "routed":
        ys_hbm, route_ref = next(it), next(it)
    if mode is not None:
        mod_res_ref, g_post_ref = next(it), next(it)
    if want_h:
        mod_pre_ref, g_pre_ref = next(it), next(it)
    xo_refs = [next(it) for _ in range((2 if split_out else 1) if want_x else 0)]
    h_ref = next(it) if want_h else None
    i = pl.program_id(0)
    x = x_refs[0][...]
    if split_in:
        x = jnp.where(i < n_first, x, x_refs[1][...])
    if mode is not None:
        if mode == "routed":
            buf1, buf2, sem = next(it), next(it), next(it)
            tm = x.shape[0]
            base = i * tm * TOP_K
            _gather_rows(dest_ref, [(base, buf1), (base + 1, buf2)], TOP_K, tm, ys_hbm, sem)
            route = route_ref[...]
            y = route[:, 2:3] * buf1[...] + route[:, 3:4] * buf2[...]
        else:
            y = y_ref[...]
        gate = mod_res_ref[0, gate_idx:gate_idx + 1, :]
        x = x + gate * _rms(y, g_post_ref[...])
    if want_x and split_out:
        @pl.when(i < n_first)
        def _():
            xo_refs[0][...] = x

        @pl.when(i >= n_first)
        def _():
            xo_refs[1][...] = x
    elif want_x:
        xo_refs[0][...] = x
    if want_h:
        scale = mod_pre_ref[0, scale_idx:scale_idx + 1, :]
        shift = mod_pre_ref[0, shift_idx:shift_idx + 1, :]
        h_ref[...] = (_rms(x, g_pre_ref[...]) * (1.0 + scale) + shift).astype(h_ref.dtype)


def _resid_norm(x, sample_of_tile, tm, name, *, y=None, routed=None, mod_res=None, g_post=None,
                gate_idx=0, mod_pre=None, g_pre=None, shift_idx=0, scale_idx=0, split_out=False,
                n_first=0):
    split_in = isinstance(x, (tuple, list))
    xs_in = list(x) if split_in else [x]
    t, d = sum(a.shape[0] for a in xs_in), xs_in[0].shape[1]
    mode = "dense" if y is not None else ("routed" if routed is not None else None)
    want_h = mod_pre is not None
    want_x = mode is not None
    row = pl.BlockSpec((tm, d), lambda i, *_: (i, 0))
    first = pl.BlockSpec((tm, d), lambda i, *_: (jnp.minimum(i, n_first - 1), 0))
    rest = pl.BlockSpec((tm, d), lambda i, *_: (jnp.maximum(i - n_first, 0), 0))
    vec = pl.BlockSpec((1, d), lambda i, *_: (0, 0))
    mod = pl.BlockSpec((1, 6, d), lambda i, *_: (sample_of_tile(i), 0, 0))
    in_specs, args = ([first, rest] if split_in else [row]), xs_in
    out_specs, out_shape, scratch = [], [], []
    if mode == "dense":
        in_specs.append(row)
        args.append(y)
    elif mode == "routed":
        ys, dest, route = routed
        in_specs += [pl.BlockSpec(memory_space=pl.ANY),
                     pl.BlockSpec((tm, LANE), lambda i, *_: (i, 0))]
        args += [ys, route]
        scratch = [pltpu.VMEM((tm, d), F32), pltpu.VMEM((tm, d), F32),
                   pltpu.SemaphoreType.DMA(())]
    if mode is not None:
        in_specs += [mod, vec]
        args += [mod_res, g_post.reshape(1, d)]
    if want_h:
        in_specs += [mod, vec]
        args += [mod_pre, g_pre.reshape(1, d)]
    if want_x and split_out:
        out_specs += [first, rest]
        out_shape += [jax.ShapeDtypeStruct((n_first * tm, d), F32),
                      jax.ShapeDtypeStruct((t - n_first * tm, d), F32)]
    elif want_x:
        out_specs.append(row)
        out_shape.append(jax.ShapeDtypeStruct((t, d), F32))
    if want_h:
        out_specs.append(row)
        out_shape.append(jax.ShapeDtypeStruct((t, d), BF16))
    kern = functools.partial(_resid_norm_kernel, mode=mode, want_x=want_x, want_h=want_h,
                             gate_idx=gate_idx, shift_idx=shift_idx, scale_idx=scale_idx,
                             split_in=split_in, split_out=split_out, n_first=n_first)
    grid_spec = pltpu.PrefetchScalarGridSpec(
        num_scalar_prefetch=1 if mode == "routed" else 0, grid=(t // tm,),
        in_specs=in_specs, out_specs=out_specs, scratch_shapes=scratch)
    if mode == "routed":
        args = [dest] + args
    outs = pl.pallas_call(kern, grid_spec=grid_spec, out_shape=out_shape,
                          compiler_params=_params(1), name=name)(*args)
    n_x = (2 if split_out else 1) if want_x else 0
    x_new = (tuple(outs[:2]) if split_out else outs[0]) if want_x else x
    h = outs[n_x] if want_h else None
    return x_new, h


def _norm_kernel(x_ref, g_ref, o_ref):
    o_ref[...] = _rms(x_ref[...], g_ref[...])


def _norm_cols(arr, width, col_block, g, tm, name):
    t = arr.shape[0]
    return pl.pallas_call(
        _norm_kernel, grid=(t // tm,),
        in_specs=[pl.BlockSpec((tm, width), lambda i: (i, col_block)),
                  pl.BlockSpec((1, width), lambda i: (0, 0))],
        out_specs=pl.BlockSpec((tm, width), lambda i: (i, 0)),
        out_shape=jax.ShapeDtypeStruct((t, width), F32),
        compiler_params=_params(1), name=name)(arr, g.reshape(1, width))


def _swap_pairs(w, rope, axis=-1):
    q = rope // 4
    part = lambda a, b: lax.slice_in_dim(w, a, b, axis=axis)
    return jnp.concatenate([part(q, 2 * q), part(0, q), part(3 * q, 4 * q), part(2 * q, 3 * q)],
                           axis=axis)


def _prep_kernel(misc_ref, cos_ref, sin_ref, wg_ref, bg_ref, kr_ref, la_ref):
    misc = misc_ref[:, :LANE]
    kr = misc * cos_ref[...] + misc_ref[:, LANE:] * sin_ref[...]
    kr_ref[...] = kr.astype(kr_ref.dtype)
    z = _dot(misc.astype(BF16), wg_ref[...]) + bg_ref[...]
    log_sig = jnp.minimum(z, 0.0) - jnp.log1p(jnp.exp(-jnp.abs(z)))
    la_ref[...] = log_sig * (1.0 / GATE_NORM)


def _attn_kernel(*refs, n_seg, n_heads, group, scale):
    qn_ref, qr_ref = refs[:2]
    kv_refs = refs[2:2 + n_seg]
    kr_refs = refs[2 + n_seg:2 + 2 * n_seg]
    o_ref = refs[-1]
    hw = QK_NOPE + V_HEAD
    krs = [kr_ref[...] for kr_ref in kr_refs]
    add = lambda a, b: a + b
    c = scale * math.log2(math.e)
    for h0 in range(0, n_heads, group):
        hds = range(h0, min(h0 + group, n_heads))
        scores = []
        for hd in hds:
            q = jnp.concatenate([qn_ref[:, hd * QK_NOPE:(hd + 1) * QK_NOPE],
                                 qr_ref[:, hd * LANE:(hd + 1) * LANE]], axis=1)
            scores.append([
                _dot_nt(q, jnp.concatenate([kv_ref[:, hd * hw:hd * hw + QK_NOPE], kr], axis=1))
                for kv_ref, kr in zip(kv_refs, krs)])
        ms = [functools.reduce(jnp.maximum, [jnp.max(s, axis=-1, keepdims=True) for s in ss])
              for ss in scores]
        ps = [[jnp.exp2((s - m) * c) for s in ss] for ss, m in zip(scores, ms)]
        invs = [1.0 / functools.reduce(add, [jnp.sum(p, axis=-1, keepdims=True) for p in pp])
                for pp in ps]
        for hd, pp, inv in zip(hds, ps, invs):
            o = functools.reduce(add, [
                _dot((p * inv).astype(BF16), kv_ref[:, hd * hw + QK_NOPE:(hd + 1) * hw])
                for p, kv_ref in zip(pp, kv_refs)])
            o_ref[:, hd * V_HEAD:(hd + 1) * V_HEAD] = o.astype(o_ref.dtype)


def _attention(q_nope, q_rope, kvs, krs, *, n_batch, t_q, q_row0, tq, n_heads, group, scale, name,
               o_prev):
    hw = QK_NOPE + V_HEAD
    nq = t_q // tq
    q_map = lambda b, g, i: (q_row0 // tq + b * nq + i, g)
    in_specs = [pl.BlockSpec((tq, n_heads * QK_NOPE), q_map),
                pl.BlockSpec((tq, n_heads * LANE), q_map)]
    args = [q_nope, q_rope]
    for arr, tk, row0 in kvs:
        in_specs.append(pl.BlockSpec((tk, n_heads * hw),
                                     lambda b, g, i, tk=tk, row0=row0: (row0 // tk + b, g)))
        args.append(arr)
    for arr, tk, row0 in krs:
        in_specs.append(pl.BlockSpec((tk, LANE),
                                     lambda b, g, i, tk=tk, row0=row0: (row0 // tk + b, 0)))
        args.append(arr)
    aliases = {len(args): 0}
    in_specs.append(pl.BlockSpec(memory_space=pl.ANY))
    args.append(o_prev)
    kern = functools.partial(_attn_kernel, n_seg=len(kvs), n_heads=n_heads, group=group,
                             scale=scale)
    return pl.pallas_call(
        kern, grid=(n_batch, MLA_HEADS // n_heads, nq), in_specs=in_specs,
        out_specs=pl.BlockSpec((tq, n_heads * V_HEAD), q_map),
        out_shape=jax.ShapeDtypeStruct(o_prev.shape, o_prev.dtype),
        input_output_aliases=aliases,
        compiler_params=_params(3), name=name)(*args)


def _split3(x):
    hi = x.astype(BF16)
    r1 = x - hi.astype(F32)
    mid = r1.astype(BF16)
    lo = (r1 - mid.astype(F32)).astype(BF16)
    return hi, mid, lo


def _gla_kernel(*refs, has_init, want_states, n_alias, n_heads, q_scale, layer, all_layers):
    it = iter(refs)
    q_ref, k_ref, v_ref, laf_ref, lab_ref, gout_ref, ggla_ref = (next(it) for _ in range(7))
    s0_refs = (next(it), next(it)) if has_init else None
    for _ in range(n_alias):
        next(it)
    o_ref = next(it)
    s_out_refs = (next(it), next(it)) if want_states else None
    st, vb, qd_s, ks_s, dec_s, oacc = (next(it) for _ in range(6))
    c = GLA_CHUNK
    t_b = q_ref.shape[0]
    dk, dv = q_ref.shape[1] // n_heads, v_ref.shape[1] // n_heads
    n_chunks = t_b // c
    rb = min(t_b, 4 * c)
    row = lax.broadcasted_iota(jnp.int32, (rb, rb), 0)
    col = lax.broadcasted_iota(jnp.int32, (rb, rb), 1)
    same = (row // c) == (col // c)
    masks = (same & (col <= row), same & (col >= row))
    tris = [jnp.where(m, 1.0, 0.0).astype(BF16) for m in masks]
    la_refs = (laf_ref, lab_ref)
    streams = [(g, d) for g in range(n_heads) for d in range(2)]
    kcols = [pl.ds(g * dk, dk) for g in range(n_heads)]
    vcols = [pl.ds(g * dv, dv) for g in range(n_heads)]

    vb[...] = v_ref[...].astype(BF16)
    for s, (g, d) in enumerate(streams):
        st[s] = s0_refs[d][0, g].T if has_init else jnp.zeros(st.shape[1:], F32)

    for r0 in range(0, t_b, rb):
        rows = pl.ds(r0, rb)
        for s, (g, d) in enumerate(streams):
            hi, mid, lo = _split3(la_refs[d][rows, kcols[g]])
            b = (_dot(tris[d], hi) + _dot(tris[d], mid)) + _dot(tris[d], lo)
            end = c - 1 if d == 0 else 0
            b_end = jnp.concatenate(
                [jnp.broadcast_to(b[k * c + end:k * c + end + 1, :], (c, dk))
                 for k in range(rb // c)], axis=0)
            kc = k_ref[rows, kcols[g]]
            qd = (q_ref[rows, kcols[g]] * q_scale * jnp.exp(b)).astype(BF16)
            kd = (kc * jnp.exp(-b)).astype(BF16)
            qd_s[s, rows, :] = qd
            ks_s[s, rows, :] = (kc * jnp.exp(b_end - b)).astype(BF16)
            dec_s[s, rows, :] = jnp.exp(b_end)
            a = jnp.where(masks[d], _dot_nt(qd, kd), 0.0).astype(BF16)
            o_in = _dot(a, vb[rows, vcols[g]])
            if d == 0:
                oacc[rows, vcols[g]] = o_in
            else:
                oacc[rows, vcols[g]] += o_in

    for i in range(n_chunks):
        for s, (g, d) in enumerate(streams):
            c0 = (i if d == 0 else n_chunks - 1 - i) * c
            rows = pl.ds(c0, c)
            u = _dot_tn(vb[rows, vcols[g]], ks_s[s, rows, :])
            s_t = st[s]
            oacc[rows, vcols[g]] += _dot_nt(qd_s[s, rows, :], s_t.astype(BF16))
            st[s] = s_t * dec_s[s, pl.ds(c0, 1), :] + u

    for g in range(n_heads):
        o_ref[:, vcols[g]] = (_rms(oacc[:, vcols[g]], ggla_ref[...])
                              * _silu(gout_ref[:, vcols[g]])).astype(o_ref.dtype)
    if want_states:
        for s, (g, d) in enumerate(streams):
            s_fin = st[s].T
            if all_layers:
                for ll in range(s_out_refs[d].shape[1]):
                    s_out_refs[d][0, ll, g] = s_fin if ll == layer else jnp.zeros_like(s_fin)
            else:
                s_out_refs[d][0, g] = s_fin


def _gla(proj, la, g_gla, *, n_batch, t_b, row0, heads, n_heads, dk, dv, col_q, col_k, col_v,
         col_g, name, o_prev, s0=None, layer=0, st_prev=None, st_depth=0):
    has_init = s0 is not None
    rb = row0 // t_b
    wk, wv = n_heads * dk, n_heads * dv
    assert col_q % wk == 0 and col_k % wk == 0 and col_v % wv == 0 and col_g % wv == 0
    in_specs = [
        pl.BlockSpec((t_b, wk), lambda b, h: (rb + b, col_q // wk + h)),
        pl.BlockSpec((t_b, wk), lambda b, h: (rb + b, col_k // wk + h)),
        pl.BlockSpec((t_b, wv), lambda b, h: (rb + b, col_v // wv + h)),
        pl.BlockSpec((t_b, wk), lambda b, h: (rb + b, h)),
        pl.BlockSpec((t_b, wk), lambda b, h: (rb + b, heads // n_heads + h)),
        pl.BlockSpec((t_b, wv), lambda b, h: (rb + b, col_g // wv + h)),
        pl.BlockSpec((1, dv), lambda b, h: (0, 0)),
    ]
    args = [proj, proj, proj, la, la, proj, g_gla.reshape(1, dv)]
    st_spec = pl.BlockSpec((1, None, n_heads, dk, dv), lambda b, h: (b, layer, h, 0, 0))
    if has_init:
        in_specs += [st_spec, st_spec]
        args += list(s0)
    out_specs = [pl.BlockSpec((t_b, wv), lambda b, h: (rb + b, h))]
    out_shape = [jax.ShapeDtypeStruct(o_prev.shape, o_prev.dtype)]
    aliases = {len(args): 0}
    in_specs.append(pl.BlockSpec(memory_space=pl.ANY))
    args.append(o_prev)
    want_states = st_prev is not None or st_depth > 0
    if st_prev is not None:
        out_specs += [st_spec, st_spec]
        for k, arr in enumerate(st_prev):
            out_shape.append(jax.ShapeDtypeStruct(arr.shape, arr.dtype))
            aliases[len(args)] = 1 + k
            in_specs.append(pl.BlockSpec(memory_space=pl.ANY))
            args.append(arr)
    elif want_states:
        out_specs += [pl.BlockSpec((1, st_depth, n_heads, dk, dv),
                                   lambda b, h: (b, 0, h, 0, 0))] * 2
        out_shape += [jax.ShapeDtypeStruct((n_batch, st_depth, heads, dk, dv), F32)] * 2
    kern = functools.partial(_gla_kernel, has_init=has_init, want_states=want_states,
                             n_alias=len(aliases), n_heads=n_heads, q_scale=dk ** -0.5,
                             layer=layer, all_layers=st_prev is None)
    n_str = 2 * n_heads
    return pl.pallas_call(
        kern, grid=(n_batch, heads // n_heads), in_specs=in_specs, out_specs=out_specs,
        out_shape=out_shape, input_output_aliases=aliases,
        scratch_shapes=[pltpu.VMEM((n_str, dv, dk), F32), pltpu.VMEM((t_b, wv), BF16),
                        pltpu.VMEM((n_str, t_b, dk), BF16), pltpu.VMEM((n_str, t_b, dk), BF16),
                        pltpu.VMEM((n_str, t_b, dk), F32), pltpu.VMEM((t_b, wv), F32)],
        compiler_params=_params(2), name=name)(*args)


def _router_kernel(h_ref, w_ref, o_ref, *, n_experts):
    logits = _dot(h_ref[...].astype(BF16), w_ref[...])
    lane = lax.broadcasted_iota(jnp.int32, logits.shape, 1).astype(F32)
    neg = jnp.float32(-jnp.inf)
    lg = jnp.where(lane < n_experts, logits, neg)
    m1 = jnp.max(lg, axis=-1, keepdims=True)
    i1 = jnp.min(jnp.where(lg == m1, lane, LANE), axis=-1, keepdims=True)
    lg2 = jnp.where(lane == i1, neg, lg)
    m2 = jnp.max(lg2, axis=-1, keepdims=True)
    i2 = jnp.min(jnp.where(lg2 == m2, lane, LANE), axis=-1, keepdims=True)
    e2 = jnp.exp(m2 - m1)
    inv = 1.0 / (1.0 + e2)
    o_ref[...] = (jnp.where(lane == 0, i1, 0.0) + jnp.where(lane == 1, i2, 0.0)
                  + jnp.where(lane == 2, inv, 0.0) + jnp.where(lane == 3, e2 * inv, 0.0))


def _moe_proj_kernel(info_ref, x_ref, *refs, n_sub, combine):
    w_refs, o_ref = refs[:-1], refs[-1]
    s = pl.program_id(0)
    flag0 = pl.num_programs(0) + s * n_sub
    sub = x_ref.shape[0] // n_sub

    @pl.when(info_ref[flag0] != 0)
    def _():
        ws = [w_ref[...].astype(BF16) for w_ref in w_refs]
        for k in range(n_sub):
            rows = pl.ds(k * sub, sub)

            @pl.when(info_ref[flag0 + k] != 0)
            def _():
                a = x_ref[rows, :]
                o_ref[rows, :] = combine([_dot(a, w) for w in ws]).astype(o_ref.dtype)

            @pl.when(info_ref[flag0 + k] == 0)
            def _():
                o_ref[rows, :] = jnp.zeros((sub, o_ref.shape[1]), o_ref.dtype)

    @pl.when(info_ref[flag0] == 0)
    def _():
        o_ref[...] = jnp.zeros_like(o_ref)


def _moe_proj(xs, weights, lead, info, *, run, sub, tn, out_dtype, combine, name):
    n_rows, d = xs.shape
    n_out = weights[0].shape[-1]
    tn = _tile(n_out, tn)
    n_run = n_rows // run
    n_sub = run // sub

    def w_map(s, j, info):
        return tuple(lead) + (info[s], 0, jnp.where(info[n_run + s * n_sub] != 0, j, 0))

    w_spec = pl.BlockSpec((None,) * (len(lead) + 1) + (d, tn), w_map)
    grid_spec = pltpu.PrefetchScalarGridSpec(
        num_scalar_prefetch=1, grid=(n_run, n_out // tn),
        in_specs=[pl.BlockSpec((run, d), lambda s, j, info: (s, 0))] + [w_spec] * len(weights),
        out_specs=pl.BlockSpec((run, tn), lambda s, j, info: (s, j)))
    return pl.pallas_call(
        functools.partial(_moe_proj_kernel, n_sub=n_sub, combine=combine), grid_spec=grid_spec,
        out_shape=jax.ShapeDtypeStruct((n_rows, n_out), out_dtype),
        compiler_params=_params(2), name=name)(info, xs, *weights)


def _routing_tables(route, n_experts, tile, run):
    t = route.shape[0]
    n_assign = t * TOP_K
    n_rows = -(-(n_assign + n_experts * (run - 1)) // run) * run
    e_flat = route[:, :TOP_K].astype(jnp.int32).reshape(n_assign)
    onehot = (e_flat[:, None] == jnp.arange(n_experts, dtype=jnp.int32)[None, :]).astype(jnp.int32)
    csum = jnp.cumsum(onehot, axis=0)
    rank = jnp.take_along_axis(csum, e_flat[:, None], axis=1)[:, 0] - 1
    counts = csum[-1]
    padded = ((counts + run - 1) // run) * run
    ends = jnp.cumsum(padded)
    starts = ends - padded
    dest = starts[e_flat] + rank
    tok_of = jnp.zeros((n_rows,), jnp.int32).at[dest].set(
        jnp.arange(n_assign, dtype=jnp.int32) // TOP_K, unique_indices=True)
    tile_start = jnp.arange(n_rows // tile, dtype=jnp.int32) * tile
    tile_expert = jnp.minimum(
        jnp.sum(tile_start[:, None] >= ends[None, :], axis=1, dtype=jnp.int32), n_experts - 1)
    flags = (tile_start < (starts + counts)[tile_expert]).astype(jnp.int32)
    run_info = jnp.concatenate([tile_expert[::run // tile], flags])
    tile_idx = jnp.arange(n_rows // tile, dtype=jnp.int32)
    tile_info = jnp.concatenate([tile_expert, flags,
                                 lax.cummax(jnp.where(flags != 0, tile_idx, 0))])
    return tok_of, dest.astype(jnp.int32), flags, run_info, tile_info


def _rope_tables(n_lat, rope, tm_id):
    axis_half = rope // 4
    rows = n_lat // GRID_W
    r = jnp.repeat(jnp.arange(rows, dtype=F32), GRID_W)
    c = jnp.tile(jnp.arange(GRID_W, dtype=F32), rows)
    inv = ROPE_THETA ** (-jnp.arange(axis_half, dtype=F32) / axis_half)
    ar, ac = r[:, None] * inv, c[:, None] * inv
    cos = jnp.concatenate([jnp.cos(ar), jnp.cos(ar), jnp.cos(ac), jnp.cos(ac)], axis=1)
    sin = jnp.concatenate([-jnp.sin(ar), jnp.sin(ar), -jnp.sin(ac), jnp.sin(ac)], axis=1)
    cos = jnp.concatenate([jnp.ones((tm_id, rope), F32), cos], axis=0)
    sin = jnp.concatenate([jnp.zeros((tm_id, rope), F32), sin], axis=0)
    return cos, sin


def kernel(x_prompt, x_sample, cache_ckv, cache_krope, state_gla_fwd, state_gla_bwd, c, c_ctx, w_ada, b_ada, g_pre_mix, g_post_mix, g_pre_ffn, g_post_ffn, w_in, g_q, w_uq, g_kv, w_ukv, w_gate_f, b_gate_f, w_gate_b, b_gate_b, g_gla, w_pa, w_pb, w_o, w_ff_gate, w_ff_up, w_ff_down, w_router, w_ex_gate, w_ex_up, w_ex_down):
    n_ctx_b, t_ctx, d = x_prompt.shape
    n_lat_b, t_lat, _ = x_sample.shape
    depth = w_in.shape[0]
    past = cache_ckv.shape[2]
    q_lora, kv_lora = g_q.shape[1], g_kv.shape[1]
    rope = cache_krope.shape[3]
    heads, dk, dv = state_gla_fwd.shape[2:]
    rank = w_gate_f.shape[1]
    n_experts = w_router.shape[2]
    hq = QK_NOPE + rope
    hw = QK_NOPE + V_HEAD
    n_ctx, n_lat = n_ctx_b * t_ctx, n_lat_b * t_lat
    t = n_ctx + n_lat
    assert 2 * rope == LANE and 2 * rank <= LANE - rope

    tm = _tile(math.gcd(n_ctx, t_lat), 512)
    tm_s = _tile(math.gcd(n_ctx, t_lat), 1024)
    n_ctx_tiles = n_ctx // tm
    lat_tiles = t_lat // tm

    def sample_of_tile(i):
        return jnp.where(i < n_ctx_tiles, 0, 1 + (i - n_ctx_tiles) // lat_tiles)

    def pos_of_tile(i, tile=tm):
        first = n_ctx // tile
        return jnp.where(i < first, 0, tm_s // tile + (i - first) % (t_lat // tile))

    sizes = (q_lora, kv_lora, rope, heads * dk, heads * dk, heads * dv, rank, rank,
             heads * dv, d, d)
    offs = [0]
    for s in sizes:
        offs.append(offs[-1] + s)
    main_groups = (0, 1, 3, 4, 5, 8, 9, 10)
    col = {}
    acc = 0
    for gidx in main_groups:
        col[gidx] = acc
        acc += sizes[gidx]
    n_main = acc
    col_q, col_k, col_v, col_g, col_a, col_b = col[3], col[4], col[5], col[8], col[9], col[10]

    cos_r, sin_r = _rope_tables(t_lat, rope, tm_s)
    n_tab = cos_r.shape[0]
    cos_k = jnp.concatenate([cos_r, jnp.zeros((n_tab, LANE - rope), F32)], axis=1)
    sin_k = jnp.concatenate([sin_r, jnp.zeros((n_tab, LANE - rope), F32)], axis=1)

    x = (x_prompt.reshape(n_ctx, d), x_sample.reshape(n_lat, d))
    c_all = jnp.concatenate([c_ctx[None, :], c, jnp.zeros((8 - 1 - n_lat_b, d), F32)], axis=0)

    def modulation(l):
        def epi(accs, e):
            return accs[0] + e[0]
        tn = _tile(6 * d, 1024)
        m = _matmul([(c_all, d, 0)], [(w_ada, (l,))], n_out=6 * d, tm=8, tn=tn, out_dtype=F32,
                    epilogue=epi, prologue=lambda k, a, e: _silu(a), name=f"ada{l}",
                    extras=[(b_ada.reshape(depth, 1, 6 * d), (None, 1, tn),
                             lambda j, i, l=l: (l, 0, j))])
        return m.reshape(8, 6, d)

    mods = [modulation(l) for l in range(depth)]
    _, h = _resid_norm(x, sample_of_tile, tm, "prenorm0", mod_pre=mods[0], g_pre=g_pre_mix[0],
                       shift_idx=0, scale_idx=1, n_first=n_ctx_tiles)

    new_ckv, new_krope = [], []
    states = None
    w_in_t = jnp.swapaxes(w_in, 1, 2)
    for l in range(depth):
        n_all = n_main + 2 * LANE

        def repack_kernel(w_ref, o_ref):
            w_kr = w_ref[offs[2]:offs[3], :]
            lanes = w_ref.shape[1]
            pieces = [w_ref[offs[gidx]:offs[gidx + 1], :] for gidx in main_groups]
            pieces += [w_kr, w_ref[offs[6]:offs[8], :],
                       jnp.zeros((LANE - rope - 2 * rank, lanes), F32),
                       _swap_pairs(w_kr, rope, 0), jnp.zeros((LANE - rope, lanes), F32)]
            o_ref[...] = jnp.concatenate(pieces, axis=0).astype(BF16)

        tk_r = _tile(d, 256)
        w_all = pl.pallas_call(
            repack_kernel, grid=(d // tk_r,),
            in_specs=[pl.BlockSpec((None, offs[-1], tk_r), lambda i, l=l: (l, 0, i))],
            out_specs=pl.BlockSpec((n_all, tk_r), lambda i: (0, i)),
            out_shape=jax.ShapeDtypeStruct((n_all, d), BF16),
            compiler_params=_params(1), name=f"w_in_repack{l}")(w_in_t)
        proj = _matmul([(h, d, 0)], [(w_all, ())], n_out=n_all, tm=tm_s, tn=_tile(n_all, 1280),
                       out_dtype=F32, epilogue=_first, name=f"w_in{l}", w_rows=True)

        w_gate = jnp.zeros((LANE, 2 * heads * dk), F32)
        w_gate = w_gate.at[rope:rope + rank, :heads * dk].set(w_gate_f[l])
        w_gate = w_gate.at[rope + rank:rope + 2 * rank, heads * dk:].set(w_gate_b[l])
        b_gate = jnp.concatenate([b_gate_f[l], b_gate_b[l]]).reshape(1, 2 * heads * dk)
        kr_self, la = pl.pallas_call(
            _prep_kernel, grid=(t // tm,),
            in_specs=[pl.BlockSpec((tm, 2 * LANE), lambda i: (i, n_main // (2 * LANE))),
                      pl.BlockSpec((tm, LANE), lambda i: (pos_of_tile(i), 0)),
                      pl.BlockSpec((tm, LANE), lambda i: (pos_of_tile(i), 0)),
                      pl.BlockSpec((LANE, 2 * heads * dk), lambda i: (0, 0)),
                      pl.BlockSpec((1, 2 * heads * dk), lambda i: (0, 0))],
            out_specs=[pl.BlockSpec((tm, LANE), lambda i: (i, 0)),
                       pl.BlockSpec((tm, 2 * heads * dk), lambda i: (i, 0))],
            out_shape=[jax.ShapeDtypeStruct((t, LANE), BF16),
                       jax.ShapeDtypeStruct((t, 2 * heads * dk), F32)],
            compiler_params=_params(1), name=f"prep{l}")(proj, cos_k, sin_k, w_gate.astype(BF16), b_gate)

        wq = w_uq[l].reshape(q_lora, MLA_HEADS, hq)
        wq_r = wq[:, :, QK_NOPE:]
        pad = jnp.zeros((q_lora, MLA_HEADS, LANE - rope), F32)
        as_cols = lambda a: a.reshape(q_lora, -1).astype(BF16)
        wq_n = as_cols(wq[:, :, :QK_NOPE])
        wq_s = as_cols(jnp.concatenate([_swap_pairs(wq_r, rope), pad], axis=2))
        wq_r = as_cols(jnp.concatenate([wq_r, pad], axis=2))
        q_in = [(proj, q_lora, col[0] // q_lora)]
        q_norm = lambda k, a, e: _rms(a, e[0])
        g_q_extra = (g_q[l].reshape(1, q_lora), (1, q_lora), lambda j, i: (0, 0))
        tn_q = _tile(MLA_HEADS * LANE, 1024)

        def rope_epi(accs, e, tn_q=tn_q):
            cos, sin = e[1], e[2]
            return jnp.concatenate([accs[0][:, s:s + LANE] * cos + accs[1][:, s:s + LANE] * sin
                                    for s in range(0, tn_q, LANE)], axis=1)

        q_nope = _matmul(q_in, [(wq_n, ())], n_out=MLA_HEADS * QK_NOPE, tm=tm_s,
                         tn=_tile(MLA_HEADS * QK_NOPE, 2048), out_dtype=BF16, epilogue=_first,
                         name=f"w_uq_nope{l}", prologue=q_norm, extras=[g_q_extra])
        q_rope = _matmul(q_in, [(wq_r, ()), (wq_s, ())], a_of_w=[0, 0], n_out=MLA_HEADS * LANE,
                         tm=tm_s, tn=tn_q, out_dtype=BF16, epilogue=rope_epi,
                         name=f"w_uq_rope{l}", prologue=q_norm,
                         extras=[g_q_extra,
                                 (cos_k, (tm_s, LANE), lambda j, i: (pos_of_tile(i, tm_s), 0)),
                                 (sin_k, (tm_s, LANE), lambda j, i: (pos_of_tile(i, tm_s), 0))])

        c_kv = _norm_cols(proj, kv_lora, col[1] // kv_lora, g_kv[l], tm, f"ckv_norm{l}")
        tn_kv = _tile(MLA_HEADS * hw, 2048)
        kv_self = _matmul([(c_kv, kv_lora, 0)], [(w_ukv, (l,))], n_out=MLA_HEADS * hw, tm=tm_s,
                          tn=tn_kv, out_dtype=BF16, epilogue=_first, name=f"w_ukv{l}")
        ckv_cache = cache_ckv[:, l].reshape(n_lat_b * past, kv_lora)
        kv_cache = _matmul([(ckv_cache, kv_lora, 0)], [(w_ukv, (l,))], n_out=MLA_HEADS * hw,
                           tm=_tile(n_lat_b * past, 512), tn=tn_kv, out_dtype=BF16,
                           epilogue=_first, name=f"w_ukv_cache{l}")
        kr_cache = jnp.pad(cache_krope[:, l].reshape(n_lat_b * past, rope),
                           ((0, 0), (0, LANE - rope))).astype(BF16)

        scale = hq ** -0.5
        attn = _attention(q_nope, q_rope, [(kv_self, t_ctx, 0)], [(kr_self, t_ctx, 0)],
                          n_batch=n_ctx_b, t_q=t_ctx, q_row0=0, tq=_tile(t_ctx, 256),
                          n_heads=MLA_HEADS, group=4, scale=scale, name=f"attn_ctx{l}",
                          o_prev=jnp.zeros((t, MLA_HEADS * V_HEAD), BF16))
        attn = _attention(q_nope, q_rope, [(kv_self, t_lat, n_ctx), (kv_cache, past, 0)],
                          [(kr_self, t_lat, n_ctx), (kr_cache, past, 0)], n_batch=n_lat_b,
                          t_q=t_lat, q_row0=n_ctx, tq=_tile(t_lat, 512),
                          n_heads=math.gcd(MLA_HEADS, 4), group=2, scale=scale, name=f"attn_lat{l}",
                          o_prev=attn)

        gla_kw = dict(heads=heads, dk=dk, dv=dv, col_q=col_q, col_k=col_k, col_v=col_v,
                      col_g=col_g, layer=l)
        gla, *states = _gla(proj, la, g_gla[l], n_batch=n_ctx_b, t_b=t_ctx, row0=0,
                            n_heads=math.gcd(heads, 2), name=f"gla_ctx{l}",
                            o_prev=jnp.zeros((t, heads * dv), BF16), st_prev=states,
                            st_depth=depth, **gla_kw)
        gla, = _gla(proj, la, g_gla[l], n_batch=n_lat_b, t_b=t_lat, row0=n_ctx, n_heads=1,
                    s0=(state_gla_fwd, state_gla_bwd), name=f"gla_lat{l}", o_prev=gla, **gla_kw)

        new_ckv.append(c_kv[:n_ctx].reshape(n_ctx_b, t_ctx, kv_lora))
        new_krope.append(proj[:n_ctx, n_main:n_main + rope].reshape(n_ctx_b, t_ctx, rope))

        tn_m = _tile(math.gcd(col_a, col_b, d), 1024)
        tm_m = _tile(tm, 256, 8)

        def merge_epi(accs, e):
            return jax.nn.sigmoid(e[0]) * accs[0] + jax.nn.sigmoid(e[1]) * accs[1]

        merged = _matmul([(attn, MLA_HEADS * V_HEAD, 0), (gla, heads * dv, 0)],
                         [(w_pa, (l,)), (w_pb, (l,))], n_out=d, tm=tm_m, tn=tn_m, out_dtype=BF16,
                         epilogue=merge_epi, name=f"merge{l}",
                         extras=[(proj, (tm_m, tn_m), lambda j, i: (i, col_a // tn_m + j)),
                                 (proj, (tm_m, tn_m), lambda j, i: (i, col_b // tn_m + j))])
        y = _matmul([(merged, d, 0)], [(w_o, (l,))], n_out=d, tm=tm_s, tn=_tile(d, 1024),
                    out_dtype=F32, epilogue=_first, name=f"w_o{l}")
        moe = l % 2 == 1
        x, h = _resid_norm(x, sample_of_tile, tm, f"mix_resid{l}", y=y, mod_res=mods[l],
                           g_post=g_post_mix[l], gate_idx=2, mod_pre=mods[l], g_pre=g_pre_ffn[l],
                           shift_idx=3, scale_idx=4, n_first=n_ctx_tiles)

        jx = l // 2

        def swiglu_epi(accs, e):
            return _silu(accs[0]) * accs[1]

        ffn_out = {}
        if not moe:
            d_ff = w_ff_gate.shape[2]
            ff = _matmul([(h, d, 0)], [(w_ff_gate, (jx,)), (w_ff_up, (jx,))], a_of_w=[0, 0],
                         n_out=d_ff, tm=tm_s, tn=_tile(d_ff, 512), out_dtype=BF16,
                         epilogue=swiglu_epi, name=f"ffn_up{l}")
            ffn_out["y"] = _matmul([(ff, d_ff, 0)], [(w_ff_down, (jx,))], n_out=d, tm=tm,
                                   tn=_tile(d, 512), out_dtype=F32, epilogue=_first,
                                   name=f"ffn_down{l}")
        else:
            d_ex = w_ex_gate.shape[3]
            w_r = jnp.pad(w_router[jx], ((0, 0), (0, LANE - n_experts))).astype(BF16)
            route = pl.pallas_call(
                functools.partial(_router_kernel, n_experts=n_experts), grid=(t // tm,),
                in_specs=[pl.BlockSpec((tm, d), lambda i: (i, 0)),
                          pl.BlockSpec((d, LANE), lambda i: (0, 0))],
                out_specs=pl.BlockSpec((tm, LANE), lambda i: (i, 0)),
                out_shape=jax.ShapeDtypeStruct((t, LANE), F32),
                compiler_params=_params(1), name=f"router{l}")(h, w_r)
            tile_e = _tile(t * TOP_K, EXPERT_TILE, 8)
            run_e = tile_e * EXPERT_RUN_TILES
            tok_of, dest, flags, run_info, tile_info = _routing_tables(route, n_experts, tile_e,
                                                                       run_e)
            xs = _gather(h.reshape(t, d // LANE, LANE), tok_of, flags, tile_e, f"moe_gather{l}")
            xs = xs.reshape(xs.shape[0], d)
            moe_kw = dict(run=run_e, sub=tile_e, tn=256)
            ff = _moe_proj(xs, [w_ex_gate, w_ex_up], (jx,), run_info, out_dtype=BF16,
                           combine=lambda accs: _silu(accs[0]) * accs[1], name=f"moe_up{l}",
                           **moe_kw)
            ys = _moe_proj(ff, [w_ex_down], (jx,), run_info, out_dtype=F32,
                           combine=lambda accs: accs[0], name=f"moe_down{l}", **moe_kw)
            ffn_out["routed"] = (ys, dest, route)

        if l + 1 < depth:
            x, h = _resid_norm(x, sample_of_tile, tm, f"ffn_resid{l}", mod_res=mods[l],
                               g_post=g_post_ffn[l], gate_idx=5, mod_pre=mods[l + 1],
                               g_pre=g_pre_mix[l + 1], shift_idx=0, scale_idx=1, **ffn_out)
        else:
            x, _ = _resid_norm(x, sample_of_tile, tm, f"ffn_resid{l}", mod_res=mods[l],
                               g_post=g_post_ffn[l], gate_idx=5, split_out=True,
                               n_first=n_ctx_tiles, **ffn_out)

    return (x[0].reshape(n_ctx_b, t_ctx, d), x[1].reshape(n_lat_b, t_lat, d),
            jnp.stack(new_ckv, axis=1), jnp.stack(new_krope, axis=1),
            states[0], states[1])
```

```python
import functools
import math

import jax
import jax.numpy as jnp
from jax import lax
from jax.experimental import pallas as pl
from jax.experimental.pallas import tpu as pltpu

MLA_HEADS = 16
QK_NOPE = 128
V_HEAD = 128
GRID_W = 64
ROPE_THETA = 10000.0
GATE_NORM = 16.0
GLA_CHUNK = 64
TOP_K = 2
EPS = 1e-6
LANE = 128
VMEM_LIMIT = 52 * 1024 * 1024
EXPERT_TILE = 512
EXPERT_RUN_TILES = 4

BF16 = jnp.bfloat16
F32 = jnp.float32


def _params(n_grid, **kw):
    return pltpu.CompilerParams(
        dimension_semantics=("arbitrary",) * n_grid, vmem_limit_bytes=VMEM_LIMIT, **kw)


def _tile(n, target, quantum=LANE):
    if n <= target:
        return n
    t = (target // quantum) * quantum
    while t >= quantum:
        if n % t == 0:
            return t
        t -= quantum
    return n


def _rms(x, g):
    return x * lax.rsqrt(jnp.mean(x * x, axis=-1, keepdims=True) + EPS) * g


def _silu(x):
    return x * jax.nn.sigmoid(x)


def _dot(a, b):
    return jnp.dot(a, b, preferred_element_type=F32)


def _dot_nt(a, b):
    return lax.dot_general(a, b, (((1,), (1,)), ((), ())), preferred_element_type=F32)


def _dot_tn(a, b):
    return lax.dot_general(a, b, (((0,), (0,)), ((), ())), preferred_element_type=F32)


def _mm_kernel(*refs, n_a, n_w, a_of_w, n_e, cast_w, grouped, w_rows, prologue, epilogue):
    if grouped:
        te_ref, refs = refs[0], refs[1:]
    a_refs = refs[:n_a]
    w_refs = refs[n_a:n_a + n_w]
    e_refs = refs[n_a + n_w:n_a + n_w + n_e]
    o_ref = refs[n_a + n_w + n_e]
    wb_refs = refs[n_a + n_w + n_e + 1:]
    i = pl.program_id(1)

    def compute():
        if cast_w:
            fresh = i == 0
            if grouped:
                fresh = fresh | (te_ref[i] != te_ref[jnp.maximum(i - 1, 0)])

            @pl.when(fresh)
            def _():
                for w_ref, wb_ref in zip(w_refs, wb_refs):
                    wb_ref[...] = w_ref[...].astype(BF16)
            w_use = wb_refs
        else:
            w_use = w_refs
        e = [e_ref[...] for e_ref in e_refs]
        a_vals = []
        for k, a_ref in enumerate(a_refs):
            a = a_ref[...]
            if prologue is not None:
                a = prologue(k, a, e)
            a_vals.append(a.astype(BF16))
        dot = _dot_nt if w_rows else _dot
        accs = [dot(a_vals[a_of_w[k]], w_ref[...]) for k, w_ref in enumerate(w_use)]
        o_ref[...] = epilogue(accs, e).astype(o_ref.dtype)

    if grouped:
        has_rows = te_ref[pl.num_programs(1) + i] != 0
        pl.when(has_rows)(compute)

        @pl.when(jnp.logical_not(has_rows))
        def _():
            o_ref[...] = jnp.zeros_like(o_ref)
    else:
        compute()


def _matmul(a_list, w_list, *, n_out, tm, tn, out_dtype, epilogue, name, a_of_w=None,
            prologue=None, extras=(), tile_expert=None, w_rows=False):
    m = a_list[0][0].shape[0]
    a_of_w = a_of_w or list(range(len(w_list)))
    grouped = tile_expert is not None
    assert m % tm == 0 and n_out % tn == 0
    in_specs, args, scratch = [], [], []
    for arr, k, cb in a_list:
        if grouped:
            a_map = lambda j, i, te, cb=cb: (te[2 * (m // tm) + i], cb)
        else:
            a_map = lambda j, i, cb=cb: (i, cb)
        in_specs.append(pl.BlockSpec((tm, k), a_map))
        args.append(arr)
    cast_w = w_list[0][0].dtype != BF16
    for arr, lead in w_list:
        k = arr.shape[-1] if w_rows else arr.shape[-2]
        if grouped:
            im = lambda j, i, te, lead=lead: tuple(lead[:-1]) + (te[i], 0, j)
        elif w_rows:
            im = lambda j, i, lead=lead: tuple(lead) + (j, 0)
        else:
            im = lambda j, i, lead=lead: tuple(lead) + (0, j)
        in_specs.append(pl.BlockSpec((None,) * len(lead) + ((tn, k) if w_rows else (k, tn)), im))
        args.append(arr)
        if cast_w:
            scratch.append(pltpu.VMEM((k, tn), BF16))
    for arr, bs, im in extras:
        in_specs.append(pl.BlockSpec(bs, lambda j, i, *_, im=im: im(j, i)))
        args.append(arr)
    kern = functools.partial(_mm_kernel, n_a=len(a_list), n_w=len(w_list), a_of_w=a_of_w,
                             n_e=len(extras), cast_w=cast_w, grouped=grouped, w_rows=w_rows,
                             prologue=prologue, epilogue=epilogue)
    grid_spec = pltpu.PrefetchScalarGridSpec(
        num_scalar_prefetch=1 if grouped else 0,
        grid=(n_out // tn, m // tm),
        in_specs=in_specs,
        out_specs=pl.BlockSpec((tm, tn), lambda j, i, *_: (i, j)),
        scratch_shapes=scratch)
    if grouped:
        args = [tile_expert] + args
    return pl.pallas_call(
        kern, grid_spec=grid_spec,
        out_shape=jax.ShapeDtypeStruct((m, n_out), out_dtype),
        compiler_params=_params(2), name=name,
    )(*args)


def _first(accs, e):
    return accs[0]


def _row_copy(src, row, dst, r, sem):
    return pltpu.make_async_copy(src.at[pl.ds(row, 1)], dst.at[pl.ds(r, 1)], sem)


def _gather_rows(idx_ref, jobs, stride, n, src, sem):
    def issue(r, carry):
        for base, dst in jobs:
            _row_copy(src, idx_ref[base + r * stride], dst, r, sem).start()
        return carry

    def wait(r, carry):
        for _, dst in jobs:
            _row_copy(src, 0, dst, r, sem).wait()
        return carry

    lax.fori_loop(0, n, issue, 0, unroll=8)
    lax.fori_loop(0, n, wait, 0, unroll=8)


def _gather_kernel(idx_ref, flag_ref, src_hbm, o_ref, sem):
    tg = o_ref.shape[0]
    i = pl.program_id(0)

    @pl.when(flag_ref[i] != 0)
    def _():
        _gather_rows(idx_ref, [(i * tg, o_ref)], 1, tg, src_hbm, sem)

    @pl.when(flag_ref[i] == 0)
    def _():
        o_ref[...] = jnp.zeros_like(o_ref)


def _gather(src, idx, flags, tg, name):
    n = idx.shape[0]
    blk = (tg,) + src.shape[1:]
    grid_spec = pltpu.PrefetchScalarGridSpec(
        num_scalar_prefetch=2, grid=(n // tg,),
        in_specs=[pl.BlockSpec(memory_space=pl.ANY)],
        out_specs=pl.BlockSpec(blk, lambda i, *_: (i,) + (0,) * (len(blk) - 1)),
        scratch_shapes=[pltpu.SemaphoreType.DMA(())])
    return pl.pallas_call(
        _gather_kernel, grid_spec=grid_spec,
        out_shape=jax.ShapeDtypeStruct((n,) + src.shape[1:], src.dtype),
        compiler_params=_params(1), name=name)(idx, flags, src)


def _resid_norm_kernel(*refs, mode, want_x, want_h, gate_idx, shift_idx, scale_idx, split_in,
                       split_out, n_first):
    it = iter(refs)
    dest_ref = next(it) if mode == "routed" else None
    x_refs = [next(it) for _ in range(2 if split_in else 1)]
    if mode == "dense":
        y_ref = next(it)
    elif mode == "routed":
        ys_hbm, route_ref = next(it), next(it)
    if mode is not None:
        mod_res_ref, g_post_ref = next(it), next(it)
    if want_h:
        mod_pre_ref, g_pre_ref = next(it), next(it)
    xo_refs = [next(it) for _ in range((2 if split_out else 1) if want_x else 0)]
    h_ref = next(it) if want_h else None
    i = pl.program_id(0)
    x = x_refs[0][...]
    if split_in:
        x = jnp.where(i < n_first, x, x_refs[1][...])
    if mode is not None:
        if mode == "routed":
            buf1, buf2, sem = next(it), next(it), next(it)
            tm = x.shape[0]
            base = i * tm * TOP_K
            _gather_rows(dest_ref, [(base, buf1), (base + 1, buf2)], TOP_K, tm, ys_hbm, sem)
            route = route_ref[...]
            y = route[:, 2:3] * buf1[...] + route[:, 3:4] * buf2[...]
        else:
            y = y_ref[...]
        gate = mod_res_ref[0, gate_idx:gate_idx + 1, :]
        x = x + gate * _rms(y, g_post_ref[...])
    if want_x and split_out:
        @pl.when(i < n_first)
        def _():
            xo_refs[0][...] = x

        @pl.when(i >= n_first)
        def _():
            xo_refs[1][...] = x
    elif want_x:
        xo_refs[0][...] = x
    if want_h:
        scale = mod_pre_ref[0, scale_idx:scale_idx + 1, :]
        shift = mod_pre_ref[0, shift_idx:shift_idx + 1, :]
        h_ref[...] = (_rms(x, g_pre_ref[...]) * (1.0 + scale) + shift).astype(h_ref.dtype)


def _resid_norm(x, sample_of_tile, tm, name, *, y=None, routed=None, mod_res=None, g_post=None,
                gate_idx=0, mod_pre=None, g_pre=None, shift_idx=0, scale_idx=0, split_out=False,
                n_first=0):
    split_in = isinstance(x, (tuple, list))
    xs_in = list(x) if split_in else [x]
    t, d = sum(a.shape[0] for a in xs_in), xs_in[0].shape[1]
    mode = "dense" if y is not None else ("routed" if routed is not None else None)
    want_h = mod_pre is not None
    want_x = mode is not None
    row = pl.BlockSpec((tm, d), lambda i, *_: (i, 0))
    first = pl.BlockSpec((tm, d), lambda i, *_: (jnp.minimum(i, n_first - 1), 0))
    rest = pl.BlockSpec((tm, d), lambda i, *_: (jnp.maximum(i - n_first, 0), 0))
    vec = pl.BlockSpec((1, d), lambda i, *_: (0, 0))
    mod = pl.BlockSpec((1, 6, d), lambda i, *_: (sample_of_tile(i), 0, 0))
    in_specs, args = ([first, rest] if split_in else [row]), xs_in
    out_specs, out_shape, scratch = [], [], []
    if mode == "dense":
        in_specs.append(row)
        args.append(y)
    elif mode == "routed":
        ys, dest, route = routed
        in_specs += [pl.BlockSpec(memory_space=pl.ANY),
                     pl.BlockSpec((tm, LANE), lambda i, *_: (i, 0))]
        args += [ys, route]
        scratch = [pltpu.VMEM((tm, d), F32), pltpu.VMEM((tm, d), F32),
                   pltpu.SemaphoreType.DMA(())]
    if mode is not None:
        in_specs += [mod, vec]
        args += [mod_res, g_post.reshape(1, d)]
    if want_h:
        in_specs += [mod, vec]
        args += [mod_pre, g_pre.reshape(1, d)]
    if want_x and split_out:
        out_specs += [first, rest]
        out_shape += [jax.ShapeDtypeStruct((n_first * tm, d), F32),
                      jax.ShapeDtypeStruct((t - n_first * tm, d), F32)]
    elif want_x:
        out_specs.append(row)
        out_shape.append(jax.ShapeDtypeStruct((t, d), F32))
    if want_h:
        out_specs.append(row)
        out_shape.append(jax.ShapeDtypeStruct((t, d), BF16))
    kern = functools.partial(_resid_norm_kernel, mode=mode, want_x=want_x, want_h=want_h,
                             gate_idx=gate_idx, shift_idx=shift_idx, scale_idx=scale_idx,
                             split_in=split_in, split_out=split_out, n_first=n_first)
    grid_spec = pltpu.PrefetchScalarGridSpec(
        num_scalar_prefetch=1 if mode == "routed" else 0, grid=(t // tm,),
        in_specs=in_specs, out_specs=out_specs, scratch_shapes=scratch)
    if mode == "routed":
        args = [dest] + args
    outs = pl.pallas_call(kern, grid_spec=grid_spec, out_shape=out_shape,
                          compiler_params=_params(1), name=name)(*args)
    n_x = (2 if split_out else 1) if want_x else 0
    x_new = (tuple(outs[:2]) if split_out else outs[0]) if want_x else x
    h = outs[n_x] if want_h else None
    return x_new, h


def _norm_kernel(x_ref, g_ref, o_ref):
    o_ref[...] = _rms(x_ref[...], g_ref[...])


def _norm_cols(arr, width, col_block, g, tm, name):
    t = arr.shape[0]
    return pl.pallas_call(
        _norm_kernel, grid=(t // tm,),
        in_specs=[pl.BlockSpec((tm, width), lambda i: (i, col_block)),
                  pl.BlockSpec((1, width), lambda i: (0, 0))],
        out_specs=pl.BlockSpec((tm, width), lambda i: (i, 0)),
        out_shape=jax.ShapeDtypeStruct((t, width), F32),
        compiler_params=_params(1), name=name)(arr, g.reshape(1, width))


def _swap_pairs(w, rope, axis=-1):
    q = rope // 4
    part = lambda a, b: lax.slice_in_dim(w, a, b, axis=axis)
    return jnp.concatenate([part(q, 2 * q), part(0, q), part(3 * q, 4 * q), part(2 * q, 3 * q)],
                           axis=axis)


def _prep_kernel(misc_ref, cos_ref, sin_ref, wg_ref, bg_ref, kr_ref, la_ref):
    misc = misc_ref[:, :LANE]
    kr = misc * cos_ref[...] + misc_ref[:, LANE:] * sin_ref[...]
    kr_ref[...] = kr.astype(kr_ref.dtype)
    z = _dot(misc.astype(BF16), wg_ref[...]) + bg_ref[...]
    log_sig = jnp.minimum(z, 0.0) - jnp.log1p(jnp.exp(-jnp.abs(z)))
    la_ref[...] = log_sig * (1.0 / GATE_NORM)


def _attn_kernel(*refs, n_seg, n_heads, group, scale):
    qn_ref, qr_ref = refs[:2]
    kv_refs = refs[2:2 + n_seg]
    kr_refs = refs[2 + n_seg:2 + 2 * n_seg]
    o_ref = refs[-1]
    hw = QK_NOPE + V_HEAD
    krs = [kr_ref[...] for kr_ref in kr_refs]
    add = lambda a, b: a + b
    c = scale * math.log2(math.e)
    for h0 in range(0, n_heads, group):
        hds = range(h0, min(h0 + group, n_heads))
        scores = []
        for hd in hds:
            q = jnp.concatenate([qn_ref[:, hd * QK_NOPE:(hd + 1) * QK_NOPE],
                                 qr_ref[:, hd * LANE:(hd + 1) * LANE]], axis=1)
            scores.append([
                _dot_nt(q, jnp.concatenate([kv_ref[:, hd * hw:hd * hw + QK_NOPE], kr], axis=1))
                for kv_ref, kr in zip(kv_refs, krs)])
        ms = [functools.reduce(jnp.maximum, [jnp.max(s, axis=-1, keepdims=True) for s in ss])
              for ss in scores]
        ps = [[jnp.exp2((s - m) * c) for s in ss] for ss, m in zip(scores, ms)]
        invs = [1.0 / functools.reduce(add, [jnp.sum(p, axis=-1, keepdims=True) for p in pp])
                for pp in ps]
        for hd, pp, inv in zip(hds, ps, invs):
            o = functools.reduce(add, [
                _dot((p * inv).astype(BF16), kv_ref[:, hd * hw + QK_NOPE:(hd + 1) * hw])
                for p, kv_ref in zip(pp, kv_refs)])
            o_ref[:, hd * V_HEAD:(hd + 1) * V_HEAD] = o.astype(o_ref.dtype)


def _attention(q_nope, q_rope, kvs, krs, *, n_batch, t_q, q_row0, tq, n_heads, group, scale, name,
               o_prev):
    hw = QK_NOPE + V_HEAD
    nq = t_q // tq
    q_map = lambda b, g, i: (q_row0 // tq + b * nq + i, g)
    in_specs = [pl.BlockSpec((tq, n_heads * QK_NOPE), q_map),
                pl.BlockSpec((tq, n_heads * LANE), q_map)]
    args = [q_nope, q_rope]
    for arr, tk, row0 in kvs:
        in_specs.append(pl.BlockSpec((tk, n_heads * hw),
                                     lambda b, g, i, tk=tk, row0=row0: (row0 // tk + b, g)))
        args.append(arr)
    for arr, tk, row0 in krs:
        in_specs.append(pl.BlockSpec((tk, LANE),
                                     lambda b, g, i, tk=tk, row0=row0: (row0 // tk + b, 0)))
        args.append(arr)
    aliases = {len(args): 0}
    in_specs.append(pl.BlockSpec(memory_space=pl.ANY))
    args.append(o_prev)
    kern = functools.partial(_attn_kernel, n_seg=len(kvs), n_heads=n_heads, group=group,
                             scale=scale)
    return pl.pallas_call(
        kern, grid=(n_batch, MLA_HEADS // n_heads, nq), in_specs=in_specs,
        out_specs=pl.BlockSpec((tq, n_heads * V_HEAD), q_map),
        out_shape=jax.ShapeDtypeStruct(o_prev.shape, o_prev.dtype),
        input_output_aliases=aliases,
        compiler_params=_params(3), name=name)(*args)


def _split3(x):
    hi = x.astype(BF16)
    r1 = x - hi.astype(F32)
    mid = r1.astype(BF16)
    lo = (r1 - mid.astype(F32)).astype(BF16)
    return hi, mid, lo


def _gla_kernel(*refs, has_init, want_states, n_alias, n_heads, q_scale, layer, all_layers):
    it = iter(refs)
    q_ref, k_ref, v_ref, laf_ref, lab_ref, gout_ref, ggla_ref = (next(it) for _ in range(7))
    s0_refs = (next(it), next(it)) if has_init else None
    for _ in range(n_alias):
        next(it)
    o_ref = next(it)
    s_out_refs = (next(it), next(it)) if want_states else None
    st, vb, qd_s, ks_s, dec_s, oacc = (next(it) for _ in range(6))
    c = GLA_CHUNK
    t_b = q_ref.shape[0]
    dk, dv = q_ref.shape[1] // n_heads, v_ref.shape[1] // n_heads
    n_chunks = t_b // c
    rb = min(t_b, 4 * c)
    row = lax.broadcasted_iota(jnp.int32, (rb, rb), 0)
    col = lax.broadcasted_iota(jnp.int32, (rb, rb), 1)
    same = (row // c) == (col // c)
    masks = (same & (col <= row), same & (col >= row))
    tris = [jnp.where(m, 1.0, 0.0).astype(BF16) for m in masks]
    la_refs = (laf_ref, lab_ref)
    streams = [(g, d) for g in range(n_heads) for d in range(2)]
    kcols = [pl.ds(g * dk, dk) for g in range(n_heads)]
    vcols = [pl.ds(g * dv, dv) for g in range(n_heads)]

    vb[...] = v_ref[...].astype(BF16)
    for s, (g, d) in enumerate(streams):
        st[s] = s0_refs[d][0, g].T if has_init else jnp.zeros(st.shape[1:], F32)

    for r0 in range(0, t_b, rb):
        rows = pl.ds(r0, rb)
        for s, (g, d) in enumerate(streams):
            hi, mid, lo = _split3(la_refs[d][rows, kcols[g]])
            b = (_dot(tris[d], hi) + _dot(tris[d], mid)) + _dot(tris[d], lo)
            end = c - 1 if d == 0 else 0
            b_end = jnp.concatenate(
                [jnp.broadcast_to(b[k * c + end:k * c + end + 1, :], (c, dk))
                 for k in range(rb // c)], axis=0)
            kc = k_ref[rows, kcols[g]]
            qd = (q_ref[rows, kcols[g]] * q_scale * jnp.exp(b)).astype(BF16)
            kd = (kc * jnp.exp(-b)).astype(BF16)
            qd_s[s, rows, :] = qd
            ks_s[s, rows, :] = (kc * jnp.exp(b_end - b)).astype(BF16)
            dec_s[s, rows, :] = jnp.exp(b_end)
            a = jnp.where(masks[d], _dot_nt(qd, kd), 0.0).astype(BF16)
            o_in = _dot(a, vb[rows, vcols[g]])
            if d == 0:
                oacc[rows, vcols[g]] = o_in
            else:
                oacc[rows, vcols[g]] += o_in

    for i in range(n_chunks):
        for s, (g, d) in enumerate(streams):
            c0 = (i if d == 0 else n_chunks - 1 - i) * c
            rows = pl.ds(c0, c)
            u = _dot_tn(vb[rows, vcols[g]], ks_s[s, rows, :])
            s_t = st[s]
            oacc[rows, vcols[g]] += _dot_nt(qd_s[s, rows, :], s_t.astype(BF16))
            st[s] = s_t * dec_s[s, pl.ds(c0, 1), :] + u

    for g in range(n_heads):
        o_ref[:, vcols[g]] = (_rms(oacc[:, vcols[g]], ggla_ref[...])
                              * _silu(gout_ref[:, vcols[g]])).astype(o_ref.dtype)
    if want_states:
        for s, (g, d) in enumerate(streams):
            s_fin = st[s].T
            if all_layers:
                for ll in range(s_out_refs[d].shape[1]):
                    s_out_refs[d][0, ll, g] = s_fin if ll == layer else jnp.zeros_like(s_fin)
            else:
                s_out_refs[d][0, g] = s_fin


def _gla(proj, la, g_gla, *, n_batch, t_b, row0, heads, n_heads, dk, dv, col_q, col_k, col_v,
         col_g, name, o_prev, s0=None, layer=0, st_prev=None, st_depth=0):
    has_init = s0 is not None
    rb = row0 // t_b
    wk, wv = n_heads * dk, n_heads * dv
    assert col_q % wk == 0 and col_k % wk == 0 and col_v % wv == 0 and col_g % wv == 0
    in_specs = [
        pl.BlockSpec((t_b, wk), lambda b, h: (rb + b, col_q // wk + h)),
        pl.BlockSpec((t_b, wk), lambda b, h: (rb + b, col_k // wk + h)),
        pl.BlockSpec((t_b, wv), lambda b, h: (rb + b, col_v // wv + h)),
        pl.BlockSpec((t_b, wk), lambda b, h: (rb + b, h)),
        pl.BlockSpec((t_b, wk), lambda b, h: (rb + b, heads // n_heads + h)),
        pl.BlockSpec((t_b, wv), lambda b, h: (rb + b, col_g // wv + h)),
        pl.BlockSpec((1, dv), lambda b, h: (0, 0)),
    ]
    args = [proj, proj, proj, la, la, proj, g_gla.reshape(1, dv)]
    st_spec = pl.BlockSpec((1, None, n_heads, dk, dv), lambda b, h: (b, layer, h, 0, 0))
    if has_init:
        in_specs += [st_spec, st_spec]
        args += list(s0)
    out_specs = [pl.BlockSpec((t_b, wv), lambda b, h: (rb + b, h))]
    out_shape = [jax.ShapeDtypeStruct(o_prev.shape, o_prev.dtype)]
    aliases = {len(args): 0}
    in_specs.append(pl.BlockSpec(memory_space=pl.ANY))
    args.append(o_prev)
    want_states = st_prev is not None or st_depth > 0
    if st_prev is not None:
        out_specs += [st_spec, st_spec]
        for k, arr in enumerate(st_prev):
            out_shape.append(jax.ShapeDtypeStruct(arr.shape, arr.dtype))
            aliases[len(args)] = 1 + k
            in_specs.append(pl.BlockSpec(memory_space=pl.ANY))
            args.append(arr)
    elif want_states:
        out_specs += [pl.BlockSpec((1, st_depth, n_heads, dk, dv),
                                   lambda b, h: (b, 0, h, 0, 0))] * 2
        out_shape += [jax.ShapeDtypeStruct((n_batch, st_depth, heads, dk, dv), F32)] * 2
    kern = functools.partial(_gla_kernel, has_init=has_init, want_states=want_states,
                             n_alias=len(aliases), n_heads=n_heads, q_scale=dk ** -0.5,
                             layer=layer, all_layers=st_prev is None)
    n_str = 2 * n_heads
    return pl.pallas_call(
        kern, grid=(n_batch, heads // n_heads), in_specs=in_specs, out_specs=out_specs,
        out_shape=out_shape, input_output_aliases=aliases,
        scratch_shapes=[pltpu.VMEM((n_str, dv, dk), F32), pltpu.VMEM((t_b, wv), BF16),
                        pltpu.VMEM((n_str, t_b, dk), BF16), pltpu.VMEM((n_str, t_b, dk), BF16),
                        pltpu.VMEM((n_str, t_b, dk), F32), pltpu.VMEM((t_b, wv), F32)],
        compiler_params=_params(2), name=name)(*args)


def _router_kernel(h_ref, w_ref, o_ref, *, n_experts):
    logits = _dot(h_ref[...].astype(BF16), w_ref[...])
    lane = lax.broadcasted_iota(jnp.int32, logits.shape, 1).astype(F32)
    neg = jnp.float32(-jnp.inf)
    lg = jnp.where(lane < n_experts, logits, neg)
    m1 = jnp.max(lg, axis=-1, keepdims=True)
    i1 = jnp.min(jnp.where(lg == m1, lane, LANE), axis=-1, keepdims=True)
    lg2 = jnp.where(lane == i1, neg, lg)
    m2 = jnp.max(lg2, axis=-1, keepdims=True)
    i2 = jnp.min(jnp.where(lg2 == m2, lane, LANE), axis=-1, keepdims=True)
    e2 = jnp.exp(m2 - m1)
    inv = 1.0 / (1.0 + e2)
    o_ref[...] = (jnp.where(lane == 0, i1, 0.0) + jnp.where(lane == 1, i2, 0.0)
                  + jnp.where(lane == 2, inv, 0.0) + jnp.where(lane == 3, e2 * inv, 0.0))


def _moe_proj_kernel(info_ref, x_ref, *refs, n_sub, combine):
    w_refs, o_ref = refs[:-1], refs[-1]
    s = pl.program_id(0)
    flag0 = pl.num_programs(0) + s * n_sub
    sub = x_ref.shape[0] // n_sub

    @pl.when(info_ref[flag0] != 0)
    def _():
        ws = [w_ref[...].astype(BF16) for w_ref in w_refs]
        for k in range(n_sub):
            rows = pl.ds(k * sub, sub)

            @pl.when(info_ref[flag0 + k] != 0)
            def _():
                a = x_ref[rows, :]
                o_ref[rows, :] = combine([_dot(a, w) for w in ws]).astype(o_ref.dtype)

            @pl.when(info_ref[flag0 + k] == 0)
            def _():
                o_ref[rows, :] = jnp.zeros((sub, o_ref.shape[1]), o_ref.dtype)

    @pl.when(info_ref[flag0] == 0)
    def _():
        o_ref[...] = jnp.zeros_like(o_ref)


def _moe_proj(xs, weights, lead, info, *, run, sub, tn, out_dtype, combine, name):
    n_rows, d = xs.shape
    n_out = weights[0].shape[-1]
    tn = _tile(n_out, tn)
    n_run = n_rows // run
    n_sub = run // sub

    def w_map(s, j, info):
        return tuple(lead) + (info[s], 0, jnp.where(info[n_run + s * n_sub] != 0, j, 0))

    w_spec = pl.BlockSpec((None,) * (len(lead) + 1) + (d, tn), w_map)
    grid_spec = pltpu.PrefetchScalarGridSpec(
        num_scalar_prefetch=1, grid=(n_run, n_out // tn),
        in_specs=[pl.BlockSpec((run, d), lambda s, j, info: (s, 0))] + [w_spec] * len(weights),
        out_specs=pl.BlockSpec((run, tn), lambda s, j, info: (s, j)))
    return pl.pallas_call(
        functools.partial(_moe_proj_kernel, n_sub=n_sub, combine=combine), grid_spec=grid_spec,
        out_shape=jax.ShapeDtypeStruct((n_rows, n_out), out_dtype),
        compiler_params=_params(2), name=name)(info, xs, *weights)


def _routing_tables(route, n_experts, tile, run):
    t = route.shape[0]
    n_assign = t * TOP_K
    n_rows = -(-(n_assign + n_experts * (run - 1)) // run) * run
    e_flat = route[:, :TOP_K].astype(jnp.int32).reshape(n_assign)
    onehot = (e_flat[:, None] == jnp.arange(n_experts, dtype=jnp.int32)[None, :]).astype(jnp.int32)
    csum = jnp.cumsum(onehot, axis=0)
    rank = jnp.take_along_axis(csum, e_flat[:, None], axis=1)[:, 0] - 1
    counts = csum[-1]
    padded = ((counts + run - 1) // run) * run
    ends = jnp.cumsum(padded)
    starts = ends - padded
    dest = starts[e_flat] + rank
    tok_of = jnp.zeros((n_rows,), jnp.int32).at[dest].set(
        jnp.arange(n_assign, dtype=jnp.int32) // TOP_K, unique_indices=True)
    tile_start = jnp.arange(n_rows // tile, dtype=jnp.int32) * tile
    tile_expert = jnp.minimum(
        jnp.sum(tile_start[:, None] >= ends[None, :], axis=1, dtype=jnp.int32), n_experts - 1)
    flags = (tile_start < (starts + counts)[tile_expert]).astype(jnp.int32)
    run_info = jnp.concatenate([tile_expert[::run // tile], flags])
    tile_idx = jnp.arange(n_rows // tile, dtype=jnp.int32)
    tile_info = jnp.concatenate([tile_expert, flags,
                                 lax.cummax(jnp.where(flags != 0, tile_idx, 0))])
    return tok_of, dest.astype(jnp.int32), flags, run_info, tile_info


def _rope_tables(n_lat, rope, tm_id):
    axis_half = rope // 4
    rows = n_lat // GRID_W
    r = jnp.repeat(jnp.arange(rows, dtype=F32), GRID_W)
    c = jnp.tile(jnp.arange(GRID_W, dtype=F32), rows)
    inv = ROPE_THETA ** (-jnp.arange(axis_half, dtype=F32) / axis_half)
    ar, ac = r[:, None] * inv, c[:, None] * inv
    cos = jnp.concatenate([jnp.cos(ar), jnp.cos(ar), jnp.cos(ac), jnp.cos(ac)], axis=1)
    sin = jnp.concatenate([-jnp.sin(ar), jnp.sin(ar), -jnp.sin(ac), jnp.sin(ac)], axis=1)
    cos = jnp.concatenate([jnp.ones((tm_id, rope), F32), cos], axis=0)
    sin = jnp.concatenate([jnp.zeros((tm_id, rope), F32), sin], axis=0)
    return cos, sin


def kernel(x_prompt, x_sample, cache_ckv, cache_krope, state_gla_fwd, state_gla_bwd, c, c_ctx, w_ada, b_ada, g_pre_mix, g_post_mix, g_pre_ffn, g_post_ffn, w_in, g_q, w_uq, g_kv, w_ukv, w_gate_f, b_gate_f, w_gate_b, b_gate_b, g_gla, w_pa, w_pb, w_o, w_ff_gate, w_ff_up, w_ff_down, w_router, w_ex_gate, w_ex_up, w_ex_down):
    n_ctx_b, t_ctx, d = x_prompt.shape
    n_lat_b, t_lat, _ = x_sample.shape
    depth = w_in.shape[0]
    past = cache_ckv.shape[2]
    q_lora, kv_lora = g_q.shape[1], g_kv.shape[1]
    rope = cache_krope.shape[3]
    heads, dk, dv = state_gla_fwd.shape[2:]
    rank = w_gate_f.shape[1]
    n_experts = w_router.shape[2]
    hq = QK_NOPE + rope
    hw = QK_NOPE + V_HEAD
    n_ctx, n_lat = n_ctx_b * t_ctx, n_lat_b * t_lat
    t = n_ctx + n_lat
    assert 2 * rope == LANE and 2 * rank <= LANE - rope

    tm = _tile(math.gcd(n_ctx, t_lat), 512)
    tm_s = _tile(math.gcd(n_ctx, t_lat), 1024)
    n_ctx_tiles = n_ctx // tm
    lat_tiles = t_lat // tm

    def sample_of_tile(i):
        return jnp.where(i < n_ctx_tiles, 0, 1 + (i - n_ctx_tiles) // lat_tiles)

    def pos_of_tile(i, tile=tm):
        first = n_ctx // tile
        return jnp.where(i < first, 0, tm_s // tile + (i - first) % (t_lat // tile))

    sizes = (q_lora, kv_lora, rope, heads * dk, heads * dk, heads * dv, rank, rank,
             heads * dv, d, d)
    offs = [0]
    for s in sizes:
        offs.append(offs[-1] + s)
    main_groups = (0, 1, 3, 4, 5, 8, 9, 10)
    col = {}
    acc = 0
    for gidx in main_groups:
        col[gidx] = acc
        acc += sizes[gidx]
    n_main = acc
    col_q, col_k, col_v, col_g, col_a, col_b = col[3], col[4], col[5], col[8], col[9], col[10]

    cos_r, sin_r = _rope_tables(t_lat, rope, tm_s)
    n_tab = cos_r.shape[0]
    cos_k = jnp.concatenate([cos_r, jnp.zeros((n_tab, LANE - rope), F32)], axis=1)
    sin_k = jnp.concatenate([sin_r, jnp.zeros((n_tab, LANE - rope), F32)], axis=1)

    x = (x_prompt.reshape(n_ctx, d), x_sample.reshape(n_lat, d))
    c_all = jnp.concatenate([c_ctx[None, :], c, jnp.zeros((8 - 1 - n_lat_b, d), F32)], axis=0)

    def modulation(l):
        def epi(accs, e):
            return accs[0] + e[0]
        tn = _tile(6 * d, 1024)
        m = _matmul([(c_all, d, 0)], [(w_ada, (l,))], n_out=6 * d, tm=8, tn=tn, out_dtype=F32,
                    epilogue=epi, prologue=lambda k, a, e: _silu(a), name=f"ada{l}",
                    extras=[(b_ada.reshape(depth, 1, 6 * d), (None, 1, tn),
                             lambda j, i, l=l: (l, 0, j))])
        return m.reshape(8, 6, d)

    mods = [modulation(l) for l in range(depth)]
    _, h = _resid_norm(x, sample_of_tile, tm, "prenorm0", mod_pre=mods[0], g_pre=g_pre_mix[0],
                       shift_idx=0, scale_idx=1, n_first=n_ctx_tiles)

    new_ckv, new_krope = [], []
    states = None
    w_in_t = jnp.swapaxes(w_in, 1, 2)
    for l in range(depth):
        n_all = n_main + 2 * LANE

        def repack_kernel(w_ref, o_ref):
            w_kr = w_ref[offs[2]:offs[3], :]
            lanes = w_ref.shape[1]
            pieces = [w_ref[offs[gidx]:offs[gidx + 1], :] for gidx in main_groups]
            pieces += [w_kr, w_ref[offs[6]:offs[8], :],
                       jnp.zeros((LANE - rope - 2 * rank, lanes), F32),
                       _swap_pairs(w_kr, rope, 0), jnp.zeros((LANE - rope, lanes), F32)]
            o_ref[...] = jnp.concatenate(pieces, axis=0).astype(BF16)

        tk_r = _tile(d, 256)
        w_all = pl.pallas_call(
            repack_kernel, grid=(d // tk_r,),
            in_specs=[pl.BlockSpec((None, offs[-1], tk_r), lambda i, l=l: (l, 0, i))],
            out_specs=pl.BlockSpec((n_all, tk_r), lambda i: (0, i)),
            out_shape=jax.ShapeDtypeStruct((n_all, d), BF16),
            compiler_params=_params(1), name=f"w_in_repack{l}")(w_in_t)
        proj = _matmul([(h, d, 0)], [(w_all, ())], n_out=n_all, tm=tm_s, tn=_tile(n_all, 1280),
                       out_dtype=F32, epilogue=_first, name=f"w_in{l}", w_rows=True)

        w_gate = jnp.zeros((LANE, 2 * heads * dk), F32)
        w_gate = w_gate.at[rope:rope + rank, :heads * dk].set(w_gate_f[l])
        w_gate = w_gate.at[rope + rank:rope + 2 * rank, heads * dk:].set(w_gate_b[l])
        b_gate = jnp.concatenate([b_gate_f[l], b_gate_b[l]]).reshape(1, 2 * heads * dk)
        kr_self, la = pl.pallas_call(
            _prep_kernel, grid=(t // tm,),
            in_specs=[pl.BlockSpec((tm, 2 * LANE), lambda i: (i, n_main // (2 * LANE))),
                      pl.BlockSpec((tm, LANE), lambda i: (pos_of_tile(i), 0)),
                      pl.BlockSpec((tm, LANE), lambda i: (pos_of_tile(i), 0)),
                      pl.BlockSpec((LANE, 2 * heads * dk), lambda i: (0, 0)),
                      pl.BlockSpec((1, 2 * heads * dk), lambda i: (0, 0))],
            out_specs=[pl.BlockSpec((tm, LANE), lambda i: (i, 0)),
                       pl.BlockSpec((tm, 2 * heads * dk), lambda i: (i, 0))],
            out_shape=[jax.ShapeDtypeStruct((t, LANE), BF16),
                       jax.ShapeDtypeStruct((t, 2 * heads * dk), F32)],
            compiler_params=_params(1), name=f"prep{l}")(proj, cos_k, sin_k, w_gate.astype(BF16), b_gate)

        wq = w_uq[l].reshape(q_lora, MLA_HEADS, hq)
        wq_r = wq[:, :, QK_NOPE:]
        pad = jnp.zeros((q_lora, MLA_HEADS, LANE - rope), F32)
        as_cols = lambda a: a.reshape(q_lora, -1).astype(BF16)
        wq_n = as_cols(wq[:, :, :QK_NOPE])
        wq_s = as_cols(jnp.concatenate([_swap_pairs(wq_r, rope), pad], axis=2))
        wq_r = as_cols(jnp.concatenate([wq_r, pad], axis=2))
        q_in = [(proj, q_lora, col[0] // q_lora)]
        q_norm = lambda k, a, e: _rms(a, e[0])
        g_q_extra = (g_q[l].reshape(1, q_lora), (1, q_lora), lambda j, i: (0, 0))
        tn_q = _tile(MLA_HEADS * LANE, 1024)

        def rope_epi(accs, e, tn_q=tn_q):
            cos, sin = e[1], e[2]
            return jnp.concatenate([accs[0][:, s:s + LANE] * cos + accs[1][:, s:s + LANE] * sin
                                    for s in range(0, tn_q, LANE)], axis=1)

        q_nope = _matmul(q_in, [(wq_n, ())], n_out=MLA_HEADS * QK_NOPE, tm=tm_s,
                         tn=_tile(MLA_HEADS * QK_NOPE, 2048), out_dtype=BF16, epilogue=_first,
                         name=f"w_uq_nope{l}", prologue=q_norm, extras=[g_q_extra])
        q_rope = _matmul(q_in, [(wq_r, ()), (wq_s, ())], a_of_w=[0, 0], n_out=MLA_HEADS * LANE,
                         tm=tm_s, tn=tn_q, out_dtype=BF16, epilogue=rope_epi,
                         name=f"w_uq_rope{l}", prologue=q_norm,
                         extras=[g_q_extra,
                                 (cos_k, (tm_s, LANE), lambda j, i: (pos_of_tile(i, tm_s), 0)),
                                 (sin_k, (tm_s, LANE), lambda j, i: (pos_of_tile(i, tm_s), 0))])

        c_kv = _norm_cols(proj, kv_lora, col[1] // kv_lora, g_kv[l], tm, f"ckv_norm{l}")
        tn_kv = _tile(MLA_HEADS * hw, 2048)
        kv_self = _matmul([(c_kv, kv_lora, 0)], [(w_ukv, (l,))], n_out=MLA_HEADS * hw, tm=tm_s,
                          tn=tn_kv, out_dtype=BF16, epilogue=_first, name=f"w_ukv{l}")
        ckv_cache = cache_ckv[:, l].reshape(n_lat_b * past, kv_lora)
        kv_cache = _matmul([(ckv_cache, kv_lora, 0)], [(w_ukv, (l,))], n_out=MLA_HEADS * hw,
                           tm=_tile(n_lat_b * past, 512), tn=tn_kv, out_dtype=BF16,
                           epilogue=_first, name=f"w_ukv_cache{l}")
        kr_cache = jnp.pad(cache_krope[:, l].reshape(n_lat_b * past, rope),
                           ((0, 0), (0, LANE - rope))).astype(BF16)

        scale = hq ** -0.5
        attn = _attention(q_nope, q_rope, [(kv_self, t_ctx, 0)], [(kr_self, t_ctx, 0)],
                          n_batch=n_ctx_b, t_q=t_ctx, q_row0=0, tq=_tile(t_ctx, 256),
                          n_heads=MLA_HEADS, group=4, scale=scale, name=f"attn_ctx{l}",
                          o_prev=jnp.zeros((t, MLA_HEADS * V_HEAD), BF16))
        attn = _attention(q_nope, q_rope, [(kv_self, t_lat, n_ctx), (kv_cache, past, 0)],
                          [(kr_self, t_lat, n_ctx), (kr_cache, past, 0)], n_batch=n_lat_b,
                          t_q=t_lat, q_row0=n_ctx, tq=_tile(t_lat, 1024),
                          n_heads=math.gcd(MLA_HEADS, 4), group=2, scale=scale, name=f"attn_lat{l}",
                          o_prev=attn)

        gla_kw = dict(heads=heads, dk=dk, dv=dv, col_q=col_q, col_k=col_k, col_v=col_v,
                      col_g=col_g, layer=l)
        gla, *states = _gla(proj, la, g_gla[l], n_batch=n_ctx_b, t_b=t_ctx, row0=0,
                            n_heads=math.gcd(heads, 2), name=f"gla_ctx{l}",
                            o_prev=jnp.zeros((t, heads * dv), BF16), st_prev=states,
                            st_depth=depth, **gla_kw)
        gla, = _gla(proj, la, g_gla[l], n_batch=n_lat_b, t_b=t_lat, row0=n_ctx, n_heads=1,
                    s0=(state_gla_fwd, state_gla_bwd), name=f"gla_lat{l}", o_prev=gla, **gla_kw)

        new_ckv.append(c_kv[:n_ctx].reshape(n_ctx_b, t_ctx, kv_lora))
        new_krope.append(proj[:n_ctx, n_main:n_main + rope].reshape(n_ctx_b, t_ctx, rope))

        tn_m = _tile(math.gcd(col_a, col_b, d), 1024)
        tm_m = _tile(tm, 256, 8)

        def merge_epi(accs, e):
            return jax.nn.sigmoid(e[0]) * accs[0] + jax.nn.sigmoid(e[1]) * accs[1]

        merged = _matmul([(attn, MLA_HEADS * V_HEAD, 0), (gla, heads * dv, 0)],
                         [(w_pa, (l,)), (w_pb, (l,))], n_out=d, tm=tm_m, tn=tn_m, out_dtype=BF16,
                         epilogue=merge_epi, name=f"merge{l}",
                         extras=[(proj, (tm_m, tn_m), lambda j, i: (i, col_a // tn_m + j)),
                                 (proj, (tm_m, tn_m), lambda j, i: (i, col_b // tn_m + j))])
        y = _matmul([(merged, d, 0)], [(w_o, (l,))], n_out=d, tm=tm_s, tn=_tile(d, 1024),
                    out_dtype=F32, epilogue=_first, name=f"w_o{l}")
        moe = l % 2 == 1
        x, h = _resid_norm(x, sample_of_tile, tm, f"mix_resid{l}", y=y, mod_res=mods[l],
                           g_post=g_post_mix[l], gate_idx=2, mod_pre=mods[l], g_pre=g_pre_ffn[l],
                           shift_idx=3, scale_idx=4, n_first=n_ctx_tiles)

        jx = l // 2

        def swiglu_epi(accs, e):
            return _silu(accs[0]) * accs[1]

        ffn_out = {}
        if not moe:
            d_ff = w_ff_gate.shape[2]
            ff = _matmul([(h, d, 0)], [(w_ff_gate, (jx,)), (w_ff_up, (jx,))], a_of_w=[0, 0],
                         n_out=d_ff, tm=tm_s, tn=_tile(d_ff, 512), out_dtype=BF16,
                         epilogue=swiglu_epi, name=f"ffn_up{l}")
            ffn_out["y"] = _matmul([(ff, d_ff, 0)], [(w_ff_down, (jx,))], n_out=d, tm=tm,
                                   tn=_tile(d, 512), out_dtype=F32, epilogue=_first,
                                   name=f"ffn_down{l}")
        else:
            d_ex = w_ex_gate.shape[3]
            w_r = jnp.pad(w_router[jx], ((0, 0), (0, LANE - n_experts))).astype(BF16)
            route = pl.pallas_call(
                functools.partial(_router_kernel, n_experts=n_experts), grid=(t // tm,),
                in_specs=[pl.BlockSpec((tm, d), lambda i: (i, 0)),
                          pl.BlockSpec((d, LANE), lambda i: (0, 0))],
                out_specs=pl.BlockSpec((tm, LANE), lambda i: (i, 0)),
                out_shape=jax.ShapeDtypeStruct((t, LANE), F32),
                compiler_params=_params(1), name=f"router{l}")(h, w_r)
            tile_e = _tile(t * TOP_K, EXPERT_TILE, 8)
            run_e = tile_e * EXPERT_RUN_TILES
            tok_of, dest, flags, run_info, tile_info = _routing_tables(route, n_experts, tile_e,
                                                                       run_e)
            xs = _gather(h.reshape(t, d // LANE, LANE), tok_of, flags, tile_e, f"moe_gather{l}")
            xs = xs.reshape(xs.shape[0], d)
            ff = _moe_proj(xs, [w_ex_gate, w_ex_up], (jx,), run_info, run=run_e, sub=tile_e,
                           tn=256, out_dtype=BF16,
                           combine=lambda accs: _silu(accs[0]) * accs[1], name=f"moe_up{l}")
            ys = _matmul([(ff, d_ex, 0)], [(w_ex_down, (jx, 0))], n_out=d, tm=tile_e,
                         tn=_tile(d, 1024), out_dtype=F32, epilogue=_first, name=f"moe_down{l}",
                         tile_expert=tile_info)
            ffn_out["routed"] = (ys, dest, route)

        if l + 1 < depth:
            x, h = _resid_norm(x, sample_of_tile, tm, f"ffn_resid{l}", mod_res=mods[l],
                               g_post=g_post_ffn[l], gate_idx=5, mod_pre=mods[l + 1],
                               g_pre=g_pre_mix[l + 1], shift_idx=0, scale_idx=1, **ffn_out)
        else:
            x, _ = _resid_norm(x, sample_of_tile, tm, f"ffn_resid{l}", mod_res=mods[l],
                               g_post=g_post_ffn[l], gate_idx=5, split_out=True,
                               n_first=n_ctx_tiles, **ffn_out)

    return (x[0].reshape(n_ctx_b, t_ctx, d), x[1].reshape(n_lat_b, t_lat, d),
            jnp.stack(new_ckv, axis=1), jnp.stack(new_krope, axis=1),
            states[0], states[1])
```

```python
import functools
import math

import jax
import jax.numpy as jnp
from jax import lax
from jax.experimental import pallas as pl
from jax.experimental.pallas import tpu as pltpu

MLA_HEADS = 16
QK_NOPE = 128
V_HEAD = 128
GRID_W = 64
ROPE_THETA = 10000.0
GATE_NORM = 16.0
GLA_CHUNK = 64
TOP_K = 2
EPS = 1e-6
LANE = 128
VMEM_LIMIT = 52 * 1024 * 1024
EXPERT_TILE = 512
EXPERT_RUN_TILES = 4

BF16 = jnp.bfloat16
F32 = jnp.float32


def _params(n_grid, **kw):
    return pltpu.CompilerParams(
        dimension_semantics=("arbitrary",) * n_grid, vmem_limit_bytes=VMEM_LIMIT, **kw)


def _tile(n, target, quantum=LANE):
    if n <= target:
        return n
    t = (target // quantum) * quantum
    while t >= quantum:
        if n % t == 0:
            return t
        t -= quantum
    return n


def _rms(x, g):
    return x * lax.rsqrt(jnp.mean(x * x, axis=-1, keepdims=True) + EPS) * g


def _silu(x):
    return x * jax.nn.sigmoid(x)


def _dot(a, b):
    return jnp.dot(a, b, preferred_element_type=F32)


def _dot_nt(a, b):
    return lax.dot_general(a, b, (((1,), (1,)), ((), ())), preferred_element_type=F32)


def _dot_tn(a, b):
    return lax.dot_general(a, b, (((0,), (0,)), ((), ())), preferred_element_type=F32)


def _mm_kernel(*refs, n_a, n_w, a_of_w, n_e, cast_w, grouped, w_rows, prologue, epilogue):
    if grouped:
        te_ref, refs = refs[0], refs[1:]
    a_refs = refs[:n_a]
    w_refs = refs[n_a:n_a + n_w]
    e_refs = refs[n_a + n_w:n_a + n_w + n_e]
    o_ref = refs[n_a + n_w + n_e]
    wb_refs = refs[n_a + n_w + n_e + 1:]
    i = pl.program_id(1)

    def compute():
        if cast_w:
            fresh = i == 0
            if grouped:
                fresh = fresh | (te_ref[i] != te_ref[jnp.maximum(i - 1, 0)])

            @pl.when(fresh)
            def _():
                for w_ref, wb_ref in zip(w_refs, wb_refs):
                    wb_ref[...] = w_ref[...].astype(BF16)
            w_use = wb_refs
        else:
            w_use = w_refs
        e = [e_ref[...] for e_ref in e_refs]
        a_vals = []
        for k, a_ref in enumerate(a_refs):
            a = a_ref[...]
            if prologue is not None:
                a = prologue(k, a, e)
            a_vals.append(a.astype(BF16))
        dot = _dot_nt if w_rows else _dot
        accs = [dot(a_vals[a_of_w[k]], w_ref[...]) for k, w_ref in enumerate(w_use)]
        o_ref[...] = epilogue(accs, e).astype(o_ref.dtype)

    if grouped:
        has_rows = te_ref[pl.num_programs(1) + i] != 0
        pl.when(has_rows)(compute)

        @pl.when(jnp.logical_not(has_rows))
        def _():
            o_ref[...] = jnp.zeros_like(o_ref)
    else:
        compute()


def _matmul(a_list, w_list, *, n_out, tm, tn, out_dtype, epilogue, name, a_of_w=None,
            prologue=None, extras=(), tile_expert=None, w_rows=False):
    m = a_list[0][0].shape[0]
    a_of_w = a_of_w or list(range(len(w_list)))
    grouped = tile_expert is not None
    assert m % tm == 0 and n_out % tn == 0
    in_specs, args, scratch = [], [], []
    for arr, k, cb in a_list:
        if grouped:
            a_map = lambda j, i, te, cb=cb: (te[2 * (m // tm) + i], cb)
        else:
            a_map = lambda j, i, cb=cb: (i, cb)
        in_specs.append(pl.BlockSpec((tm, k), a_map))
        args.append(arr)
    cast_w = w_list[0][0].dtype != BF16
    for arr, lead in w_list:
        k = arr.shape[-1] if w_rows else arr.shape[-2]
        if grouped:
            im = lambda j, i, te, lead=lead: tuple(lead[:-1]) + (te[i], 0, j)
        elif w_rows:
            im = lambda j, i, lead=lead: tuple(lead) + (j, 0)
        else:
            im = lambda j, i, lead=lead: tuple(lead) + (0, j)
        in_specs.append(pl.BlockSpec((None,) * len(lead) + ((tn, k) if w_rows else (k, tn)), im))
        args.append(arr)
        if cast_w:
            scratch.append(pltpu.VMEM((k, tn), BF16))
    for arr, bs, im in extras:
        in_specs.append(pl.BlockSpec(bs, lambda j, i, *_, im=im: im(j, i)))
        args.append(arr)
    kern = functools.partial(_mm_kernel, n_a=len(a_list), n_w=len(w_list), a_of_w=a_of_w,
                             n_e=len(extras), cast_w=cast_w, grouped=grouped, w_rows=w_rows,
                             prologue=prologue, epilogue=epilogue)
    grid_spec = pltpu.PrefetchScalarGridSpec(
        num_scalar_prefetch=1 if grouped else 0,
        grid=(n_out // tn, m // tm),
        in_specs=in_specs,
        out_specs=pl.BlockSpec((tm, tn), lambda j, i, *_: (i, j)),
        scratch_shapes=scratch)
    if grouped:
        args = [tile_expert] + args
    return pl.pallas_call(
        kern, grid_spec=grid_spec,
        out_shape=jax.ShapeDtypeStruct((m, n_out), out_dtype),
        compiler_params=_params(2), name=name,
    )(*args)


def _first(accs, e):
    return accs[0]


def _row_copy(src, row, dst, r, sem):
    return pltpu.make_async_copy(src.at[pl.ds(row, 1)], dst.at[pl.ds(r, 1)], sem)


def _gather_rows(idx_ref, jobs, stride, n, src, sem):
    def issue(r, carry):
        for base, dst in jobs:
            _row_copy(src, idx_ref[base + r * stride], dst, r, sem).start()
        return carry

    def wait(r, carry):
        for _, dst in jobs:
            _row_copy(src, 0, dst, r, sem).wait()
        return carry

    lax.fori_loop(0, n, issue, 0, unroll=8)
    lax.fori_loop(0, n, wait, 0, unroll=8)


def _gather_kernel(idx_ref, flag_ref, src_hbm, o_ref, sem):
    tg = o_ref.shape[0]
    i = pl.program_id(0)

    @pl.when(flag_ref[i] != 0)
    def _():
        _gather_rows(idx_ref, [(i * tg, o_ref)], 1, tg, src_hbm, sem)

    @pl.when(flag_ref[i] == 0)
    def _():
        o_ref[...] = jnp.zeros_like(o_ref)


def _gather(src, idx, flags, tg, name):
    n = idx.shape[0]
    blk = (tg,) + src.shape[1:]
    grid_spec = pltpu.PrefetchScalarGridSpec(
        num_scalar_prefetch=2, grid=(n // tg,),
        in_specs=[pl.BlockSpec(memory_space=pl.ANY)],
        out_specs=pl.BlockSpec(blk, lambda i, *_: (i,) + (0,) * (len(blk) - 1)),
        scratch_shapes=[pltpu.SemaphoreType.DMA(())])
    return pl.pallas_call(
        _gather_kernel, grid_spec=grid_spec,
        out_shape=jax.ShapeDtypeStruct((n,) + src.shape[1:], src.dtype),
        compiler_params=_params(1), name=name)(idx, flags, src)


def _resid_norm_kernel(*refs, mode, want_x, want_h, gate_idx, shift_idx, scale_idx, split_in,
                       split_out, n_first):
    it = iter(refs)
    dest_ref = next(it) if mode == "routed" else None
    x_refs = [next(it) for _ in range(2 if split_in else 1)]
    if mode == "dense":
        y_ref = next(it)
    elif mode == "routed":
        ys_hbm, route_ref = next(it), next(it)
    if mode is not None:
        mod_res_ref, g_post_ref = next(it), next(it)
    if want_h:
        mod_pre_ref, g_pre_ref = next(it), next(it)
    xo_refs = [next(it) for _ in range((2 if split_out else 1) if want_x else 0)]
    h_ref = next(it) if want_h else None
    i = pl.program_id(0)
    x = x_refs[0][...]
    if split_in:
        x = jnp.where(i < n_first, x, x_refs[1][...])
    if mode is not None:
        if mode == "routed":
            buf1, buf2, sem = next(it), next(it), next(it)
            tm = x.shape[0]
            base = i * tm * TOP_K
            _gather_rows(dest_ref, [(base, buf1), (base + 1, buf2)], TOP_K, tm, ys_hbm, sem)
            route = route_ref[...]
            y = route[:, 2:3] * buf1[...] + route[:, 3:4] * buf2[...]
        else:
            y = y_ref[...]
        gate = mod_res_ref[0, gate_idx:gate_idx + 1, :]
        x = x + gate * _rms(y, g_post_ref[...])
    if want_x and split_out:
        @pl.when(i < n_first)
        def _():
            xo_refs[0][...] = x

        @pl.when(i >= n_first)
        def _():
            xo_refs[1][...] = x
    elif want_x:
        xo_refs[0][...] = x
    if want_h:
        scale = mod_pre_ref[0, scale_idx:scale_idx + 1, :]
        shift = mod_pre_ref[0, shift_idx:shift_idx + 1, :]
        h_ref[...] = (_rms(x, g_pre_ref[...]) * (1.0 + scale) + shift).astype(h_ref.dtype)


def _resid_norm(x, sample_of_tile, tm, name, *, y=None, routed=None, mod_res=None, g_post=None,
                gate_idx=0, mod_pre=None, g_pre=None, shift_idx=0, scale_idx=0, split_out=False,
                n_first=0):
    split_in = isinstance(x, (tuple, list))
    xs_in = list(x) if split_in else [x]
    t, d = sum(a.shape[0] for a in xs_in), xs_in[0].shape[1]
    mode = "dense" if y is not None else ("routed" if routed is not None else None)
    want_h = mod_pre is not None
    want_x = mode is not None
    row = pl.BlockSpec((tm, d), lambda i, *_: (i, 0))
    first = pl.BlockSpec((tm, d), lambda i, *_: (jnp.minimum(i, n_first - 1), 0))
    rest = pl.BlockSpec((tm, d), lambda i, *_: (jnp.maximum(i - n_first, 0), 0))
    vec = pl.BlockSpec((1, d), lambda i, *_: (0, 0))
    mod = pl.BlockSpec((1, 6, d), lambda i, *_: (sample_of_tile(i), 0, 0))
    in_specs, args = ([first, rest] if split_in else [row]), xs_in
    out_specs, out_shape, scratch = [], [], []
    if mode == "dense":
        in_specs.append(row)
        args.append(y)
    elif mode == "routed":
        ys, dest, route = routed
        in_specs += [pl.BlockSpec(memory_space=pl.ANY),
                     pl.BlockSpec((tm, LANE), lambda i, *_: (i, 0))]
        args += [ys, route]
        scratch = [pltpu.VMEM((tm, d), F32), pltpu.VMEM((tm, d), F32),
                   pltpu.SemaphoreType.DMA(())]
    if mode is not None:
        in_specs += [mod, vec]
        args += [mod_res, g_post.reshape(1, d)]
    if want_h:
        in_specs += [mod, vec]
        args += [mod_pre, g_pre.reshape(1, d)]
    if want_x and split_out:
        out_specs += [first, rest]
        out_shape += [jax.ShapeDtypeStruct((n_first * tm, d), F32),
                      jax.ShapeDtypeStruct((t - n_first * tm, d), F32)]
    elif want_x:
        out_specs.append(row)
        out_shape.append(jax.ShapeDtypeStruct((t, d), F32))
    if want_h:
        out_specs.append(row)
        out_shape.append(jax.ShapeDtypeStruct((t, d), BF16))
    kern = functools.partial(_resid_norm_kernel, mode=mode, want_x=want_x, want_h=want_h,
                             gate_idx=gate_idx, shift_idx=shift_idx, scale_idx=scale_idx,
                             split_in=split_in, split_out=split_out, n_first=n_first)
    grid_spec = pltpu.PrefetchScalarGridSpec(
        num_scalar_prefetch=1 if mode == "routed" else 0, grid=(t // tm,),
        in_specs=in_specs, out_specs=out_specs, scratch_shapes=scratch)
    if mode == "routed":
        args = [dest] + args
    outs = pl.pallas_call(kern, grid_spec=grid_spec, out_shape=out_shape,
                          compiler_params=_params(1), name=name)(*args)
    n_x = (2 if split_out else 1) if want_x else 0
    x_new = (tuple(outs[:2]) if split_out else outs[0]) if want_x else x
    h = outs[n_x] if want_h else None
    return x_new, h


def _norm_kernel(x_ref, g_ref, o_ref):
    o_ref[...] = _rms(x_ref[...], g_ref[...])


def _norm_cols(arr, width, col_block, g, tm, name):
    t = arr.shape[0]
    return pl.pallas_call(
        _norm_kernel, grid=(t // tm,),
        in_specs=[pl.BlockSpec((tm, width), lambda i: (i, col_block)),
                  pl.BlockSpec((1, width), lambda i: (0, 0))],
        out_specs=pl.BlockSpec((tm, width), lambda i: (i, 0)),
        out_shape=jax.ShapeDtypeStruct((t, width), F32),
        compiler_params=_params(1), name=name)(arr, g.reshape(1, width))


def _swap_pairs(w, rope, axis=-1):
    q = rope // 4
    part = lambda a, b: lax.slice_in_dim(w, a, b, axis=axis)
    return jnp.concatenate([part(q, 2 * q), part(0, q), part(3 * q, 4 * q), part(2 * q, 3 * q)],
                           axis=axis)


def _prep_kernel(misc_ref, cos_ref, sin_ref, wg_ref, bg_ref, kr_ref, la_ref):
    misc = misc_ref[:, :LANE]
    kr = misc * cos_ref[...] + misc_ref[:, LANE:] * sin_ref[...]
    kr_ref[...] = kr.astype(kr_ref.dtype)
    z = _dot(misc.astype(BF16), wg_ref[...]) + bg_ref[...]
    log_sig = jnp.minimum(z, 0.0) - jnp.log(1.0 + jnp.exp(-jnp.abs(z)))
    la_ref[...] = log_sig * (1.0 / GATE_NORM)


def _attn_kernel(*refs, n_seg, n_heads, group, scale):
    qn_ref, qr_ref = refs[:2]
    kv_refs = refs[2:2 + n_seg]
    kr_refs = refs[2 + n_seg:2 + 2 * n_seg]
    o_ref = refs[-1]
    hw = QK_NOPE + V_HEAD
    krs = [kr_ref[...] for kr_ref in kr_refs]
    add = lambda a, b: a + b
    c = scale * math.log2(math.e)
    for h0 in range(0, n_heads, group):
        hds = range(h0, min(h0 + group, n_heads))
        scores = []
        for hd in hds:
            q = jnp.concatenate([qn_ref[:, hd * QK_NOPE:(hd + 1) * QK_NOPE],
                                 qr_ref[:, hd * LANE:(hd + 1) * LANE]], axis=1)
            scores.append([
                _dot_nt(q, jnp.concatenate([kv_ref[:, hd * hw:hd * hw + QK_NOPE], kr], axis=1))
                for kv_ref, kr in zip(kv_refs, krs)])
        ms = [functools.reduce(jnp.maximum, [jnp.max(s, axis=-1, keepdims=True) for s in ss])
              for ss in scores]
        ps = [[jnp.exp2((s - m) * c) for s in ss] for ss, m in zip(scores, ms)]
        invs = [1.0 / functools.reduce(add, [jnp.sum(p, axis=-1, keepdims=True) for p in pp])
                for pp in ps]
        for hd, pp, inv in zip(hds, ps, invs):
            o = functools.reduce(add, [
                _dot((p * inv).astype(BF16), kv_ref[:, hd * hw + QK_NOPE:(hd + 1) * hw])
                for p, kv_ref in zip(pp, kv_refs)])
            o_ref[:, hd * V_HEAD:(hd + 1) * V_HEAD] = o.astype(o_ref.dtype)


def _attention(q_nope, q_rope, kvs, krs, *, n_batch, t_q, q_row0, tq, n_heads, group, scale, name,
               o_prev):
    hw = QK_NOPE + V_HEAD
    nq = t_q // tq
    q_map = lambda b, g, i: (q_row0 // tq + b * nq + i, g)
    in_specs = [pl.BlockSpec((tq, n_heads * QK_NOPE), q_map),
                pl.BlockSpec((tq, n_heads * LANE), q_map)]
    args = [q_nope, q_rope]
    for arr, tk, row0 in kvs:
        in_specs.append(pl.BlockSpec((tk, n_heads * hw),
                                     lambda b, g, i, tk=tk, row0=row0: (row0 // tk + b, g)))
        args.append(arr)
    for arr, tk, row0 in krs:
        in_specs.append(pl.BlockSpec((tk, LANE),
                                     lambda b, g, i, tk=tk, row0=row0: (row0 // tk + b, 0)))
        args.append(arr)
    aliases = {len(args): 0}
    in_specs.append(pl.BlockSpec(memory_space=pl.ANY))
    args.append(o_prev)
    kern = functools.partial(_attn_kernel, n_seg=len(kvs), n_heads=n_heads, group=group,
                             scale=scale)
    return pl.pallas_call(
        kern, grid=(n_batch, MLA_HEADS // n_heads, nq), in_specs=in_specs,
        out_specs=pl.BlockSpec((tq, n_heads * V_HEAD), q_map),
        out_shape=jax.ShapeDtypeStruct(o_prev.shape, o_prev.dtype),
        input_output_aliases=aliases,
        compiler_params=_params(3), name=name)(*args)


def _split3(x):
    hi = x.astype(BF16)
    r1 = x - hi.astype(F32)
    mid = r1.astype(BF16)
    lo = (r1 - mid.astype(F32)).astype(BF16)
    return hi, mid, lo


def _gla_kernel(*refs, has_init, want_states, n_alias, n_heads, q_scale, layer, all_layers):
    it = iter(refs)
    q_ref, k_ref, v_ref, laf_ref, lab_ref, gout_ref, ggla_ref = (next(it) for _ in range(7))
    s0_refs = (next(it), next(it)) if has_init else None
    for _ in range(n_alias):
        next(it)
    o_ref = next(it)
    s_out_refs = (next(it), next(it)) if want_states else None
    st, vb, qd_s, ks_s, dec_s, oacc = (next(it) for _ in range(6))
    c = GLA_CHUNK
    t_b = q_ref.shape[0]
    dk, dv = q_ref.shape[1] // n_heads, v_ref.shape[1] // n_heads
    n_chunks = t_b // c
    rb = min(t_b, 4 * c)
    row = lax.broadcasted_iota(jnp.int32, (rb, rb), 0)
    col = lax.broadcasted_iota(jnp.int32, (rb, rb), 1)
    same = (row // c) == (col // c)
    masks = (same & (col <= row), same & (col >= row))
    tris = [jnp.where(m, 1.0, 0.0).astype(BF16) for m in masks]
    la_refs = (laf_ref, lab_ref)
    streams = [(g, d) for g in range(n_heads) for d in range(2)]
    kcols = [pl.ds(g * dk, dk) for g in range(n_heads)]
    vcols = [pl.ds(g * dv, dv) for g in range(n_heads)]

    vb[...] = v_ref[...].astype(BF16)
    for s, (g, d) in enumerate(streams):
        st[s] = s0_refs[d][0, g].T if has_init else jnp.zeros(st.shape[1:], F32)

    for r0 in range(0, t_b, rb):
        rows = pl.ds(r0, rb)
        for s, (g, d) in enumerate(streams):
            hi, mid, lo = _split3(la_refs[d][rows, kcols[g]])
            b = (_dot(tris[d], hi) + _dot(tris[d], mid)) + _dot(tris[d], lo)
            end = c - 1 if d == 0 else 0
            b_end = jnp.concatenate(
                [jnp.broadcast_to(b[k * c + end:k * c + end + 1, :], (c, dk))
                 for k in range(rb // c)], axis=0)
            kc = k_ref[rows, kcols[g]]
            qd = (q_ref[rows, kcols[g]] * q_scale * jnp.exp(b)).astype(BF16)
            kd = (kc * jnp.exp(-b)).astype(BF16)
            qd_s[s, rows, :] = qd
            ks_s[s, rows, :] = (kc * jnp.exp(b_end - b)).astype(BF16)
            dec_s[s, rows, :] = jnp.exp(b_end)
            a = jnp.where(masks[d], _dot_nt(qd, kd), 0.0).astype(BF16)
            o_in = _dot(a, vb[rows, vcols[g]])
            if d == 0:
                oacc[rows, vcols[g]] = o_in
            else:
                oacc[rows, vcols[g]] += o_in

    for i in range(n_chunks):
        for s, (g, d) in enumerate(streams):
            c0 = (i if d == 0 else n_chunks - 1 - i) * c
            rows = pl.ds(c0, c)
            u = _dot_tn(vb[rows, vcols[g]], ks_s[s, rows, :])
            s_t = st[s]
            oacc[rows, vcols[g]] += _dot_nt(qd_s[s, rows, :], s_t.astype(BF16))
            st[s] = s_t * dec_s[s, pl.ds(c0, 1), :] + u

    for g in range(n_heads):
        o_ref[:, vcols[g]] = (_rms(oacc[:, vcols[g]], ggla_ref[...])
                              * _silu(gout_ref[:, vcols[g]])).astype(o_ref.dtype)
    if want_states:
        for s, (g, d) in enumerate(streams):
            s_fin = st[s].T
            if all_layers:
                for ll in range(s_out_refs[d].shape[1]):
                    s_out_refs[d][0, ll, g] = s_fin if ll == layer else jnp.zeros_like(s_fin)
            else:
                s_out_refs[d][0, g] = s_fin


def _gla(proj, la, g_gla, *, n_batch, t_b, row0, heads, n_heads, dk, dv, col_q, col_k, col_v,
         col_g, name, o_prev, s0=None, layer=0, st_prev=None, st_depth=0):
    has_init = s0 is not None
    rb = row0 // t_b
    wk, wv = n_heads * dk, n_heads * dv
    assert col_q % wk == 0 and col_k % wk == 0 and col_v % wv == 0 and col_g % wv == 0
    in_specs = [
        pl.BlockSpec((t_b, wk), lambda b, h: (rb + b, col_q // wk + h)),
        pl.BlockSpec((t_b, wk), lambda b, h: (rb + b, col_k // wk + h)),
        pl.BlockSpec((t_b, wv), lambda b, h: (rb + b, col_v // wv + h)),
        pl.BlockSpec((t_b, wk), lambda b, h: (rb + b, h)),
        pl.BlockSpec((t_b, wk), lambda b, h: (rb + b, heads // n_heads + h)),
        pl.BlockSpec((t_b, wv), lambda b, h: (rb + b, col_g // wv + h)),
        pl.BlockSpec((1, dv), lambda b, h: (0, 0)),
    ]
    args = [proj, proj, proj, la, la, proj, g_gla.reshape(1, dv)]
    st_spec = pl.BlockSpec((1, None, n_heads, dk, dv), lambda b, h: (b, layer, h, 0, 0))
    if has_init:
        in_specs += [st_spec, st_spec]
        args += list(s0)
    out_specs = [pl.BlockSpec((t_b, wv), lambda b, h: (rb + b, h))]
    out_shape = [jax.ShapeDtypeStruct(o_prev.shape, o_prev.dtype)]
    aliases = {len(args): 0}
    in_specs.append(pl.BlockSpec(memory_space=pl.ANY))
    args.append(o_prev)
    want_states = st_prev is not None or st_depth > 0
    if st_prev is not None:
        out_specs += [st_spec, st_spec]
        for k, arr in enumerate(st_prev):
            out_shape.append(jax.ShapeDtypeStruct(arr.shape, arr.dtype))
            aliases[len(args)] = 1 + k
            in_specs.append(pl.BlockSpec(memory_space=pl.ANY))
            args.append(arr)
    elif want_states:
        out_specs += [pl.BlockSpec((1, st_depth, n_heads, dk, dv),
                                   lambda b, h: (b, 0, h, 0, 0))] * 2
        out_shape += [jax.ShapeDtypeStruct((n_batch, st_depth, heads, dk, dv), F32)] * 2
    kern = functools.partial(_gla_kernel, has_init=has_init, want_states=want_states,
                             n_alias=len(aliases), n_heads=n_heads, q_scale=dk ** -0.5,
                             layer=layer, all_layers=st_prev is None)
    n_str = 2 * n_heads
    return pl.pallas_call(
        kern, grid=(n_batch, heads // n_heads), in_specs=in_specs, out_specs=out_specs,
        out_shape=out_shape, input_output_aliases=aliases,
        scratch_shapes=[pltpu.VMEM((n_str, dv, dk), F32), pltpu.VMEM((t_b, wv), BF16),
                        pltpu.VMEM((n_str, t_b, dk), BF16), pltpu.VMEM((n_str, t_b, dk), BF16),
                        pltpu.VMEM((n_str, t_b, dk), F32), pltpu.VMEM((t_b, wv), F32)],
        compiler_params=_params(2), name=name)(*args)


def _router_kernel(h_ref, w_ref, o_ref, *, n_experts):
    logits = _dot(h_ref[...].astype(BF16), w_ref[...])
    lane = lax.broadcasted_iota(jnp.int32, logits.shape, 1).astype(F32)
    neg = jnp.float32(-jnp.inf)
    lg = jnp.where(lane < n_experts, logits, neg)
    m1 = jnp.max(lg, axis=-1, keepdims=True)
    i1 = jnp.min(jnp.where(lg == m1, lane, LANE), axis=-1, keepdims=True)
    lg2 = jnp.where(lane == i1, neg, lg)
    m2 = jnp.max(lg2, axis=-1, keepdims=True)
    i2 = jnp.min(jnp.where(lg2 == m2, lane, LANE), axis=-1, keepdims=True)
    e2 = jnp.exp(m2 - m1)
    inv = 1.0 / (1.0 + e2)
    o_ref[...] = (jnp.where(lane == 0, i1, 0.0) + jnp.where(lane == 1, i2, 0.0)
                  + jnp.where(lane == 2, inv, 0.0) + jnp.where(lane == 3, e2 * inv, 0.0))


def _moe_proj_kernel(info_ref, x_ref, *refs, n_sub, combine):
    w_refs, o_ref = refs[:-1], refs[-1]
    s = pl.program_id(0)
    flag0 = pl.num_programs(0) + s * n_sub
    sub = x_ref.shape[0] // n_sub

    @pl.when(info_ref[flag0] != 0)
    def _():
        ws = [w_ref[...].astype(BF16) for w_ref in w_refs]
        for k in range(n_sub):
            rows = pl.ds(k * sub, sub)

            @pl.when(info_ref[flag0 + k] != 0)
            def _():
                a = x_ref[rows, :]
                o_ref[rows, :] = combine([_dot(a, w) for w in ws]).astype(o_ref.dtype)

            @pl.when(info_ref[flag0 + k] == 0)
            def _():
                o_ref[rows, :] = jnp.zeros((sub, o_ref.shape[1]), o_ref.dtype)

    @pl.when(info_ref[flag0] == 0)
    def _():
        o_ref[...] = jnp.zeros_like(o_ref)


def _moe_proj(xs, weights, lead, info, *, run, sub, tn, out_dtype, combine, name):
    n_rows, d = xs.shape
    n_out = weights[0].shape[-1]
    tn = _tile(n_out, tn)
    n_run = n_rows // run
    n_sub = run // sub

    def w_map(s, j, info):
        return tuple(lead) + (info[s], 0, jnp.where(info[n_run + s * n_sub] != 0, j, 0))

    w_spec = pl.BlockSpec((None,) * (len(lead) + 1) + (d, tn), w_map)
    grid_spec = pltpu.PrefetchScalarGridSpec(
        num_scalar_prefetch=1, grid=(n_run, n_out // tn),
        in_specs=[pl.BlockSpec((run, d), lambda s, j, info: (s, 0))] + [w_spec] * len(weights),
        out_specs=pl.BlockSpec((run, tn), lambda s, j, info: (s, j)))
    return pl.pallas_call(
        functools.partial(_moe_proj_kernel, n_sub=n_sub, combine=combine), grid_spec=grid_spec,
        out_shape=jax.ShapeDtypeStruct((n_rows, n_out), out_dtype),
        compiler_params=_params(2), name=name)(info, xs, *weights)


def _routing_tables(route, n_experts, tile, run):
    t = route.shape[0]
    n_assign = t * TOP_K
    n_rows = -(-(n_assign + n_experts * (run - 1)) // run) * run
    e_flat = route[:, :TOP_K].astype(jnp.int32).reshape(n_assign)
    onehot = (e_flat[:, None] == jnp.arange(n_experts, dtype=jnp.int32)[None, :]).astype(jnp.int32)
    csum = jnp.cumsum(onehot, axis=0)
    rank = jnp.take_along_axis(csum, e_flat[:, None], axis=1)[:, 0] - 1
    counts = csum[-1]
    padded = ((counts + run - 1) // run) * run
    ends = jnp.cumsum(padded)
    starts = ends - padded
    dest = starts[e_flat] + rank
    tok_of = jnp.zeros((n_rows,), jnp.int32).at[dest].set(
        jnp.arange(n_assign, dtype=jnp.int32) // TOP_K, unique_indices=True)
    tile_start = jnp.arange(n_rows // tile, dtype=jnp.int32) * tile
    tile_expert = jnp.minimum(
        jnp.sum(tile_start[:, None] >= ends[None, :], axis=1, dtype=jnp.int32), n_experts - 1)
    flags = (tile_start < (starts + counts)[tile_expert]).astype(jnp.int32)
    run_info = jnp.concatenate([tile_expert[::run // tile], flags])
    tile_idx = jnp.arange(n_rows // tile, dtype=jnp.int32)
    tile_info = jnp.concatenate([tile_expert, flags,
                                 lax.cummax(jnp.where(flags != 0, tile_idx, 0))])
    return tok_of, dest.astype(jnp.int32), flags, run_info, tile_info


def _rope_tables(n_lat, rope, tm_id):
    axis_half = rope // 4
    rows = n_lat // GRID_W
    r = jnp.repeat(jnp.arange(rows, dtype=F32), GRID_W)
    c = jnp.tile(jnp.arange(GRID_W, dtype=F32), rows)
    inv = ROPE_THETA ** (-jnp.arange(axis_half, dtype=F32) / axis_half)
    ar, ac = r[:, None] * inv, c[:, None] * inv
    cos = jnp.concatenate([jnp.cos(ar), jnp.cos(ar), jnp.cos(ac), jnp.cos(ac)], axis=1)
    sin = jnp.concatenate([-jnp.sin(ar), jnp.sin(ar), -jnp.sin(ac), jnp.sin(ac)], axis=1)
    cos = jnp.concatenate([jnp.ones((tm_id, rope), F32), cos], axis=0)
    sin = jnp.concatenate([jnp.zeros((tm_id, rope), F32), sin], axis=0)
    return cos, sin


def kernel(x_prompt, x_sample, cache_ckv, cache_krope, state_gla_fwd, state_gla_bwd, c, c_ctx, w_ada, b_ada, g_pre_mix, g_post_mix, g_pre_ffn, g_post_ffn, w_in, g_q, w_uq, g_kv, w_ukv, w_gate_f, b_gate_f, w_gate_b, b_gate_b, g_gla, w_pa, w_pb, w_o, w_ff_gate, w_ff_up, w_ff_down, w_router, w_ex_gate, w_ex_up, w_ex_down):
    n_ctx_b, t_ctx, d = x_prompt.shape
    n_lat_b, t_lat, _ = x_sample.shape
    depth = w_in.shape[0]
    past = cache_ckv.shape[2]
    q_lora, kv_lora = g_q.shape[1], g_kv.shape[1]
    rope = cache_krope.shape[3]
    heads, dk, dv = state_gla_fwd.shape[2:]
    rank = w_gate_f.shape[1]
    n_experts = w_router.shape[2]
    hq = QK_NOPE + rope
    hw = QK_NOPE + V_HEAD
    n_ctx, n_lat = n_ctx_b * t_ctx, n_lat_b * t_lat
    t = n_ctx + n_lat
    assert 2 * rope == LANE and 2 * rank <= LANE - rope

    tm = _tile(math.gcd(n_ctx, t_lat), 512)
    tm_s = _tile(math.gcd(n_ctx, t_lat), 1024)
    n_ctx_tiles = n_ctx // tm
    lat_tiles = t_lat // tm

    def sample_of_tile(i):
        return jnp.where(i < n_ctx_tiles, 0, 1 + (i - n_ctx_tiles) // lat_tiles)

    def pos_of_tile(i, tile=tm):
        first = n_ctx // tile
        return jnp.where(i < first, 0, tm_s // tile + (i - first) % (t_lat // tile))

    sizes = (q_lora, kv_lora, rope, heads * dk, heads * dk, heads * dv, rank, rank,
             heads * dv, d, d)
    offs = [0]
    for s in sizes:
        offs.append(offs[-1] + s)
    main_groups = (0, 1, 3, 4, 5, 8, 9, 10)
    col = {}
    acc = 0
    for gidx in main_groups:
        col[gidx] = acc
        acc += sizes[gidx]
    n_main = acc
    col_q, col_k, col_v, col_g, col_a, col_b = col[3], col[4], col[5], col[8], col[9], col[10]

    cos_r, sin_r = _rope_tables(t_lat, rope, tm_s)
    n_tab = cos_r.shape[0]
    cos_k = jnp.concatenate([cos_r, jnp.zeros((n_tab, LANE - rope), F32)], axis=1)
    sin_k = jnp.concatenate([sin_r, jnp.zeros((n_tab, LANE - rope), F32)], axis=1)

    x = (x_prompt.reshape(n_ctx, d), x_sample.reshape(n_lat, d))
    c_all = jnp.concatenate([c_ctx[None, :], c, jnp.zeros((8 - 1 - n_lat_b, d), F32)], axis=0)

    def modulation(l):
        def epi(accs, e):
            return accs[0] + e[0]
        tn = _tile(6 * d, 1024)
        m = _matmul([(c_all, d, 0)], [(w_ada, (l,))], n_out=6 * d, tm=8, tn=tn, out_dtype=F32,
                    epilogue=epi, prologue=lambda k, a, e: _silu(a), name=f"ada{l}",
                    extras=[(b_ada.reshape(depth, 1, 6 * d), (None, 1, tn),
                             lambda j, i, l=l: (l, 0, j))])
        return m.reshape(8, 6, d)

    mods = [modulation(l) for l in range(depth)]
    _, h = _resid_norm(x, sample_of_tile, tm, "prenorm0", mod_pre=mods[0], g_pre=g_pre_mix[0],
                       shift_idx=0, scale_idx=1, n_first=n_ctx_tiles)

    new_ckv, new_krope = [], []
    states = None
    w_in_t = jnp.swapaxes(w_in, 1, 2)
    for l in range(depth):
        n_all = n_main + 2 * LANE

        def repack_kernel(w_ref, o_ref):
            w_kr = w_ref[offs[2]:offs[3], :]
            lanes = w_ref.shape[1]
            pieces = [w_ref[offs[gidx]:offs[gidx + 1], :] for gidx in main_groups]
            pieces += [w_kr, w_ref[offs[6]:offs[8], :],
                       jnp.zeros((LANE - rope - 2 * rank, lanes), F32),
                       _swap_pairs(w_kr, rope, 0), jnp.zeros((LANE - rope, lanes), F32)]
            o_ref[...] = jnp.concatenate(pieces, axis=0).astype(BF16)

        tk_r = _tile(d, 256)
        w_all = pl.pallas_call(
            repack_kernel, grid=(d // tk_r,),
            in_specs=[pl.BlockSpec((None, offs[-1], tk_r), lambda i, l=l: (l, 0, i))],
            out_specs=pl.BlockSpec((n_all, tk_r), lambda i: (0, i)),
            out_shape=jax.ShapeDtypeStruct((n_all, d), BF16),
            compiler_params=_params(1), name=f"w_in_repack{l}")(w_in_t)
        proj = _matmul([(h, d, 0)], [(w_all, ())], n_out=n_all, tm=tm_s, tn=_tile(n_all, 1280),
                       out_dtype=F32, epilogue=_first, name=f"w_in{l}", w_rows=True)

        w_gate = jnp.zeros((LANE, 2 * heads * dk), F32)
        w_gate = w_gate.at[rope:rope + rank, :heads * dk].set(w_gate_f[l])
        w_gate = w_gate.at[rope + rank:rope + 2 * rank, heads * dk:].set(w_gate_b[l])
        b_gate = jnp.concatenate([b_gate_f[l], b_gate_b[l]]).reshape(1, 2 * heads * dk)
        kr_self, la = pl.pallas_call(
            _prep_kernel, grid=(t // tm,),
            in_specs=[pl.BlockSpec((tm, 2 * LANE), lambda i: (i, n_main // (2 * LANE))),
                      pl.BlockSpec((tm, LANE), lambda i: (pos_of_tile(i), 0)),
                      pl.BlockSpec((tm, LANE), lambda i: (pos_of_tile(i), 0)),
                      pl.BlockSpec((LANE, 2 * heads * dk), lambda i: (0, 0)),
                      pl.BlockSpec((1, 2 * heads * dk), lambda i: (0, 0))],
            out_specs=[pl.BlockSpec((tm, LANE), lambda i: (i, 0)),
                       pl.BlockSpec((tm, 2 * heads * dk), lambda i: (i, 0))],
            out_shape=[jax.ShapeDtypeStruct((t, LANE), BF16),
                       jax.ShapeDtypeStruct((t, 2 * heads * dk), F32)],
            compiler_params=_params(1), name=f"prep{l}")(proj, cos_k, sin_k, w_gate.astype(BF16), b_gate)

        wq = w_uq[l].reshape(q_lora, MLA_HEADS, hq)
        wq_r = wq[:, :, QK_NOPE:]
        pad = jnp.zeros((q_lora, MLA_HEADS, LANE - rope), F32)
        as_cols = lambda a: a.reshape(q_lora, -1).astype(BF16)
        wq_n = as_cols(wq[:, :, :QK_NOPE])
        wq_s = as_cols(jnp.concatenate([_swap_pairs(wq_r, rope), pad], axis=2))
        wq_r = as_cols(jnp.concatenate([wq_r, pad], axis=2))
        q_in = [(proj, q_lora, col[0] // q_lora)]
        q_norm = lambda k, a, e: _rms(a, e[0])
        g_q_extra = (g_q[l].reshape(1, q_lora), (1, q_lora), lambda j, i: (0, 0))
        tn_q = _tile(MLA_HEADS * LANE, 1024)

        def rope_epi(accs, e, tn_q=tn_q):
            cos, sin = e[1], e[2]
            return jnp.concatenate([accs[0][:, s:s + LANE] * cos + accs[1][:, s:s + LANE] * sin
                                    for s in range(0, tn_q, LANE)], axis=1)

        q_nope = _matmul(q_in, [(wq_n, ())], n_out=MLA_HEADS * QK_NOPE, tm=tm_s,
                         tn=_tile(MLA_HEADS * QK_NOPE, 2048), out_dtype=BF16, epilogue=_first,
                         name=f"w_uq_nope{l}", prologue=q_norm, extras=[g_q_extra])
        q_rope = _matmul(q_in, [(wq_r, ()), (wq_s, ())], a_of_w=[0, 0], n_out=MLA_HEADS * LANE,
                         tm=tm_s, tn=tn_q, out_dtype=BF16, epilogue=rope_epi,
                         name=f"w_uq_rope{l}", prologue=q_norm,
                         extras=[g_q_extra,
                                 (cos_k, (tm_s, LANE), lambda j, i: (pos_of_tile(i, tm_s), 0)),
                                 (sin_k, (tm_s, LANE), lambda j, i: (pos_of_tile(i, tm_s), 0))])

        c_kv = _norm_cols(proj, kv_lora, col[1] // kv_lora, g_kv[l], tm, f"ckv_norm{l}")
        tn_kv = _tile(MLA_HEADS * hw, 2048)
        kv_self = _matmul([(c_kv, kv_lora, 0)], [(w_ukv, (l,))], n_out=MLA_HEADS * hw, tm=tm_s,
                          tn=tn_kv, out_dtype=BF16, epilogue=_first, name=f"w_ukv{l}")
        ckv_cache = cache_ckv[:, l].reshape(n_lat_b * past, kv_lora)
        kv_cache = _matmul([(ckv_cache, kv_lora, 0)], [(w_ukv, (l,))], n_out=MLA_HEADS * hw,
                           tm=_tile(n_lat_b * past, 512), tn=tn_kv, out_dtype=BF16,
                           epilogue=_first, name=f"w_ukv_cache{l}")
        kr_cache = jnp.pad(cache_krope[:, l].reshape(n_lat_b * past, rope),
                           ((0, 0), (0, LANE - rope))).astype(BF16)

        scale = hq ** -0.5
        attn = _attention(q_nope, q_rope, [(kv_self, t_ctx, 0)], [(kr_self, t_ctx, 0)],
                          n_batch=n_ctx_b, t_q=t_ctx, q_row0=0, tq=_tile(t_ctx, 256),
                          n_heads=MLA_HEADS, group=8, scale=scale, name=f"attn_ctx{l}",
                          o_prev=jnp.zeros((t, MLA_HEADS * V_HEAD), BF16))
        attn = _attention(q_nope, q_rope, [(kv_self, t_lat, n_ctx), (kv_cache, past, 0)],
                          [(kr_self, t_lat, n_ctx), (kr_cache, past, 0)], n_batch=n_lat_b,
                          t_q=t_lat, q_row0=n_ctx, tq=_tile(t_lat, 1024),
                          n_heads=math.gcd(MLA_HEADS, 4), group=4, scale=scale, name=f"attn_lat{l}",
                          o_prev=attn)

        gla_kw = dict(heads=heads, dk=dk, dv=dv, col_q=col_q, col_k=col_k, col_v=col_v,
                      col_g=col_g, layer=l)
        gla, *states = _gla(proj, la, g_gla[l], n_batch=n_ctx_b, t_b=t_ctx, row0=0,
                            n_heads=math.gcd(heads, 2), name=f"gla_ctx{l}",
                            o_prev=jnp.zeros((t, heads * dv), BF16), st_prev=states,
                            st_depth=depth, **gla_kw)
        gla, = _gla(proj, la, g_gla[l], n_batch=n_lat_b, t_b=t_lat, row0=n_ctx, n_heads=1,
                    s0=(state_gla_fwd, state_gla_bwd), name=f"gla_lat{l}", o_prev=gla, **gla_kw)

        new_ckv.append(c_kv[:n_ctx].reshape(n_ctx_b, t_ctx, kv_lora))
        new_krope.append(proj[:n_ctx, n_main:n_main + rope].reshape(n_ctx_b, t_ctx, rope))

        tn_m = _tile(math.gcd(col_a, col_b, d), 1024)
        tm_m = _tile(tm, 256, 8)

        def merge_epi(accs, e):
            return jax.nn.sigmoid(e[0]) * accs[0] + jax.nn.sigmoid(e[1]) * accs[1]

        merged = _matmul([(attn, MLA_HEADS * V_HEAD, 0), (gla, heads * dv, 0)],
                         [(w_pa, (l,)), (w_pb, (l,))], n_out=d, tm=tm_m, tn=tn_m, out_dtype=BF16,
                         epilogue=merge_epi, name=f"merge{l}",
                         extras=[(proj, (tm_m, tn_m), lambda j, i: (i, col_a // tn_m + j)),
                                 (proj, (tm_m, tn_m), lambda j, i: (i, col_b // tn_m + j))])
        y = _matmul([(merged, d, 0)], [(w_o, (l,))], n_out=d, tm=tm_s, tn=_tile(d, 1024),
                    out_dtype=F32, epilogue=_first, name=f"w_o{l}")
        moe = l % 2 == 1
        x, h = _resid_norm(x, sample_of_tile, tm, f"mix_resid{l}", y=y, mod_res=mods[l],
                           g_post=g_post_mix[l], gate_idx=2, mod_pre=mods[l], g_pre=g_pre_ffn[l],
                           shift_idx=3, scale_idx=4, n_first=n_ctx_tiles)

        jx = l // 2

        def swiglu_epi(accs, e):
            return _silu(accs[0]) * accs[1]

        ffn_out = {}
        if not moe:
            d_ff = w_ff_gate.shape[2]
            ff = _matmul([(h, d, 0)], [(w_ff_gate, (jx,)), (w_ff_up, (jx,))], a_of_w=[0, 0],
                         n_out=d_ff, tm=tm_s, tn=_tile(d_ff, 512), out_dtype=BF16,
                         epilogue=swiglu_epi, name=f"ffn_up{l}")
            ffn_out["y"] = _matmul([(ff, d_ff, 0)], [(w_ff_down, (jx,))], n_out=d, tm=tm,
                                   tn=_tile(d, 512), out_dtype=F32, epilogue=_first,
                                   name=f"ffn_down{l}")
        else:
            d_ex = w_ex_gate.shape[3]
            w_r = jnp.pad(w_router[jx], ((0, 0), (0, LANE - n_experts))).astype(BF16)
            route = pl.pallas_call(
                functools.partial(_router_kernel, n_experts=n_experts), grid=(t // tm,),
                in_specs=[pl.BlockSpec((tm, d), lambda i: (i, 0)),
                          pl.BlockSpec((d, LANE), lambda i: (0, 0))],
                out_specs=pl.BlockSpec((tm, LANE), lambda i: (i, 0)),
                out_shape=jax.ShapeDtypeStruct((t, LANE), F32),
                compiler_params=_params(1), name=f"router{l}")(h, w_r)
            tile_e = _tile(t * TOP_K, EXPERT_TILE, 8)
            run_e = tile_e * EXPERT_RUN_TILES
            tok_of, dest, flags, run_info, tile_info = _routing_tables(route, n_experts, tile_e,
                                                                       run_e)
            xs = _gather(h.reshape(t, d // LANE, LANE), tok_of, flags, tile_e, f"moe_gather{l}")
            xs = xs.reshape(xs.shape[0], d)
            ff = _moe_proj(xs, [w_ex_gate, w_ex_up], (jx,), run_info, run=run_e, sub=tile_e,
                           tn=256, out_dtype=BF16,
                           combine=lambda accs: _silu(accs[0]) * accs[1], name=f"moe_up{l}")
            ys = _matmul([(ff, d_ex, 0)], [(w_ex_down, (jx, 0))], n_out=d, tm=tile_e,
                         tn=_tile(d, 1024), out_dtype=F32, epilogue=_first, name=f"moe_down{l}",
                         tile_expert=tile_info)
            ffn_out["routed"] = (ys, dest, route)

        if l + 1 < depth:
            x, h = _resid_norm(x, sample_of_tile, tm, f"ffn_resid{l}", mod_res=mods[l],
                               g_post=g_post_ffn[l], gate_idx=5, mod_pre=mods[l + 1],
                               g_pre=g_pre_mix[l + 1], shift_idx=0, scale_idx=1, **ffn_out)
        else:
            x, _ = _resid_norm(x, sample_of_tile, tm, f"ffn_resid{l}", mod_res=mods[l],
                               g_post=g_post_ffn[l], gate_idx=5, split_out=True,
                               n_first=n_ctx_tiles, **ffn_out)

    return (x[0].reshape(n_ctx_b, t_ctx, d), x[1].reshape(n_lat_b, t_lat, d),
            jnp.stack(new_ckv, axis=1), jnp.stack(new_krope, axis=1),
            states[0], states[1])
```

```python
import functools
import math

import jax
import jax.numpy as jnp
from jax import lax
from jax.experimental import pallas as pl
from jax.experimental.pallas import tpu as pltpu

MLA_HEADS = 16
QK_NOPE = 128
V_HEAD = 128
GRID_W = 64
ROPE_THETA = 10000.0
GATE_NORM = 16.0
GLA_CHUNK = 64
TOP_K = 2
EPS = 1e-6
LANE = 128
VMEM_LIMIT = 52 * 1024 * 1024
EXPERT_TILE = 512
EXPERT_RUN_TILES = 4

BF16 = jnp.bfloat16
F32 = jnp.float32


def _params(n_grid, **kw):
    return pltpu.CompilerParams(
        dimension_semantics=("arbitrary",) * n_grid, vmem_limit_bytes=VMEM_LIMIT, **kw)


def _tile(n, target, quantum=LANE):
    if n <= target:
        return n
    t = (target // quantum) * quantum
    while t >= quantum:
        if n % t == 0:
            return t
        t -= quantum
    return n


def _rms(x, g):
    return x * lax.rsqrt(jnp.mean(x * x, axis=-1, keepdims=True) + EPS) * g


def _silu(x):
    return x * jax.nn.sigmoid(x)


def _dot(a, b):
    return jnp.dot(a, b, preferred_element_type=F32)


def _dot_nt(a, b):
    return lax.dot_general(a, b, (((1,), (1,)), ((), ())), preferred_element_type=F32)


def _dot_tn(a, b):
    return lax.dot_general(a, b, (((0,), (0,)), ((), ())), preferred_element_type=F32)


def _mm_kernel(*refs, n_a, n_w, a_of_w, n_e, cast_w, grouped, w_rows, prologue, epilogue):
    if grouped:
        te_ref, refs = refs[0], refs[1:]
    a_refs = refs[:n_a]
    w_refs = refs[n_a:n_a + n_w]
    e_refs = refs[n_a + n_w:n_a + n_w + n_e]
    o_ref = refs[n_a + n_w + n_e]
    wb_refs = refs[n_a + n_w + n_e + 1:]
    i = pl.program_id(1)

    def compute():
        if cast_w:
            fresh = i == 0
            if grouped:
                fresh = fresh | (te_ref[i] != te_ref[jnp.maximum(i - 1, 0)])

            @pl.when(fresh)
            def _():
                for w_ref, wb_ref in zip(w_refs, wb_refs):
                    wb_ref[...] = w_ref[...].astype(BF16)
            w_use = wb_refs
        else:
            w_use = w_refs
        e = [e_ref[...] for e_ref in e_refs]
        a_vals = []
        for k, a_ref in enumerate(a_refs):
            a = a_ref[...]
            if prologue is not None:
                a = prologue(k, a, e)
            a_vals.append(a.astype(BF16))
        dot = _dot_nt if w_rows else _dot
        accs = [dot(a_vals[a_of_w[k]], w_ref[...]) for k, w_ref in enumerate(w_use)]
        o_ref[...] = epilogue(accs, e).astype(o_ref.dtype)

    if grouped:
        has_rows = te_ref[pl.num_programs(1) + i] != 0
        pl.when(has_rows)(compute)

        @pl.when(jnp.logical_not(has_rows))
        def _():
            o_ref[...] = jnp.zeros_like(o_ref)
    else:
        compute()


def _matmul(a_list, w_list, *, n_out, tm, tn, out_dtype, epilogue, name, a_of_w=None,
            prologue=None, extras=(), tile_expert=None, w_rows=False):
    m = a_list[0][0].shape[0]
    a_of_w = a_of_w or list(range(len(w_list)))
    grouped = tile_expert is not None
    assert m % tm == 0 and n_out % tn == 0
    in_specs, args, scratch = [], [], []
    for arr, k, cb in a_list:
        if grouped:
            a_map = lambda j, i, te, cb=cb: (te[2 * (m // tm) + i], cb)
        else:
            a_map = lambda j, i, cb=cb: (i, cb)
        in_specs.append(pl.BlockSpec((tm, k), a_map))
        args.append(arr)
    cast_w = w_list[0][0].dtype != BF16
    for arr, lead in w_list:
        k = arr.shape[-1] if w_rows else arr.shape[-2]
        if grouped:
            im = lambda j, i, te, lead=lead: tuple(lead[:-1]) + (te[i], 0, j)
        elif w_rows:
            im = lambda j, i, lead=lead: tuple(lead) + (j, 0)
        else:
            im = lambda j, i, lead=lead: tuple(lead) + (0, j)
        in_specs.append(pl.BlockSpec((None,) * len(lead) + ((tn, k) if w_rows else (k, tn)), im))
        args.append(arr)
        if cast_w:
            scratch.append(pltpu.VMEM((k, tn), BF16))
    for arr, bs, im in extras:
        in_specs.append(pl.BlockSpec(bs, lambda j, i, *_, im=im: im(j, i)))
        args.append(arr)
    kern = functools.partial(_mm_kernel, n_a=len(a_list), n_w=len(w_list), a_of_w=a_of_w,
                             n_e=len(extras), cast_w=cast_w, grouped=grouped, w_rows=w_rows,
                             prologue=prologue, epilogue=epilogue)
    grid_spec = pltpu.PrefetchScalarGridSpec(
        num_scalar_prefetch=1 if grouped else 0,
        grid=(n_out // tn, m // tm),
        in_specs=in_specs,
        out_specs=pl.BlockSpec((tm, tn), lambda j, i, *_: (i, j)),
        scratch_shapes=scratch)
    if grouped:
        args = [tile_expert] + args
    return pl.pallas_call(
        kern, grid_spec=grid_spec,
        out_shape=jax.ShapeDtypeStruct((m, n_out), out_dtype),
        compiler_params=_params(2), name=name,
    )(*args)


def _first(accs, e):
    return accs[0]


def _row_copy(src, row, dst, r, sem):
    return pltpu.make_async_copy(src.at[pl.ds(row, 1)], dst.at[pl.ds(r, 1)], sem)


def _start_rows(idx_ref, jobs, stride, n, src, sem):
    def issue(r, carry):
        for base, dst in jobs:
            _row_copy(src, idx_ref[base + r * stride], dst, r, sem).start()
        return carry

    lax.fori_loop(0, n, issue, 0, unroll=8)


def _wait_rows(dsts, n, src, sem):
    def wait(r, carry):
        for dst in dsts:
            _row_copy(src, 0, dst, r, sem).wait()
        return carry

    lax.fori_loop(0, n, wait, 0, unroll=8)


def _gather_rows(idx_ref, jobs, stride, n, src, sem):
    _start_rows(idx_ref, jobs, stride, n, src, sem)
    _wait_rows([dst for _, dst in jobs], n, src, sem)


def _gather_kernel(idx_ref, flag_ref, src_hbm, o_ref, sem):
    tg = o_ref.shape[0]
    i = pl.program_id(0)

    @pl.when(flag_ref[i] != 0)
    def _():
        _gather_rows(idx_ref, [(i * tg, o_ref)], 1, tg, src_hbm, sem)

    @pl.when(flag_ref[i] == 0)
    def _():
        o_ref[...] = jnp.zeros_like(o_ref)


def _gather(src, idx, flags, tg, name):
    n = idx.shape[0]
    blk = (tg,) + src.shape[1:]
    grid_spec = pltpu.PrefetchScalarGridSpec(
        num_scalar_prefetch=2, grid=(n // tg,),
        in_specs=[pl.BlockSpec(memory_space=pl.ANY)],
        out_specs=pl.BlockSpec(blk, lambda i, *_: (i,) + (0,) * (len(blk) - 1)),
        scratch_shapes=[pltpu.SemaphoreType.DMA(())])
    return pl.pallas_call(
        _gather_kernel, grid_spec=grid_spec,
        out_shape=jax.ShapeDtypeStruct((n,) + src.shape[1:], src.dtype),
        compiler_params=_params(1), name=name)(idx, flags, src)


def _resid_norm_kernel(*refs, mode, want_x, want_h, gate_idx, shift_idx, scale_idx, split_in,
                       split_out, n_first):
    it = iter(refs)
    dest_ref = next(it) if mode == "routed" else None
    x_refs = [next(it) for _ in range(2 if split_in else 1)]
    if mode == "dense":
        y_ref = next(it)
    elif mode == "routed":
        ys_hbm, route_ref = next(it), next(it)
    if mode is not None:
        mod_res_ref, g_post_ref = next(it), next(it)
    if want_h:
        mod_pre_ref, g_pre_ref = next(it), next(it)
    xo_refs = [next(it) for _ in range((2 if split_out else 1) if want_x else 0)]
    h_ref = next(it) if want_h else None
    i = pl.program_id(0)
    x = x_refs[0][...]
    if split_in:
        x = jnp.where(i < n_first, x, x_refs[1][...])
    if mode is not None:
        if mode == "routed":
            buf, sem = next(it), next(it)
            tm = x.shape[0]
            slot = i % 2

            def start(step, sl):
                base = step * tm * TOP_K
                _start_rows(dest_ref, [(base + k, buf.at[sl, k]) for k in range(TOP_K)], TOP_K,
                            tm, ys_hbm, sem.at[sl])

            @pl.when(i == 0)
            def _():
                start(0, 0)

            @pl.when(i + 1 < pl.num_programs(0))
            def _():
                start(i + 1, 1 - slot)

            _wait_rows([buf.at[slot, k] for k in range(TOP_K)], tm, ys_hbm, sem.at[slot])
            route = route_ref[...]
            y = route[:, 2:3] * buf[slot, 0] + route[:, 3:4] * buf[slot, 1]
        else:
            y = y_ref[...]
        gate = mod_res_ref[0, gate_idx:gate_idx + 1, :]
        x = x + gate * _rms(y, g_post_ref[...])
    if want_x and split_out:
        @pl.when(i < n_first)
        def _():
            xo_refs[0][...] = x

        @pl.when(i >= n_first)
        def _():
            xo_refs[1][...] = x
    elif want_x:
        xo_refs[0][...] = x
    if want_h:
        scale = mod_pre_ref[0, scale_idx:scale_idx + 1, :]
        shift = mod_pre_ref[0, shift_idx:shift_idx + 1, :]
        h_ref[...] = (_rms(x, g_pre_ref[...]) * (1.0 + scale) + shift).astype(h_ref.dtype)


def _resid_norm(x, sample_of_tile, tm, name, *, y=None, routed=None, mod_res=None, g_post=None,
                gate_idx=0, mod_pre=None, g_pre=None, shift_idx=0, scale_idx=0, split_out=False,
                n_first=0):
    split_in = isinstance(x, (tuple, list))
    xs_in = list(x) if split_in else [x]
    t, d = sum(a.shape[0] for a in xs_in), xs_in[0].shape[1]
    mode = "dense" if y is not None else ("routed" if routed is not None else None)
    want_h = mod_pre is not None
    want_x = mode is not None
    row = pl.BlockSpec((tm, d), lambda i, *_: (i, 0))
    first = pl.BlockSpec((tm, d), lambda i, *_: (jnp.minimum(i, n_first - 1), 0))
    rest = pl.BlockSpec((tm, d), lambda i, *_: (jnp.maximum(i - n_first, 0), 0))
    vec = pl.BlockSpec((1, d), lambda i, *_: (0, 0))
    mod = pl.BlockSpec((1, 6, d), lambda i, *_: (sample_of_tile(i), 0, 0))
    in_specs, args = ([first, rest] if split_in else [row]), xs_in
    out_specs, out_shape, scratch = [], [], []
    if mode == "dense":
        in_specs.append(row)
        args.append(y)
    elif mode == "routed":
        ys, dest, route = routed
        in_specs += [pl.BlockSpec(memory_space=pl.ANY),
                     pl.BlockSpec((tm, LANE), lambda i, *_: (i, 0))]
        args += [ys, route]
        scratch = [pltpu.VMEM((2, TOP_K, tm, d), F32), pltpu.SemaphoreType.DMA((2,))]
    if mode is not None:
        in_specs += [mod, vec]
        args += [mod_res, g_post.reshape(1, d)]
    if want_h:
        in_specs += [mod, vec]
        args += [mod_pre, g_pre.reshape(1, d)]
    if want_x and split_out:
        out_specs += [first, rest]
        out_shape += [jax.ShapeDtypeStruct((n_first * tm, d), F32),
                      jax.ShapeDtypeStruct((t - n_first * tm, d), F32)]
    elif want_x:
        out_specs.append(row)
        out_shape.append(jax.ShapeDtypeStruct((t, d), F32))
    if want_h:
        out_specs.append(row)
        out_shape.append(jax.ShapeDtypeStruct((t, d), BF16))
    kern = functools.partial(_resid_norm_kernel, mode=mode, want_x=want_x, want_h=want_h,
                             gate_idx=gate_idx, shift_idx=shift_idx, scale_idx=scale_idx,
                             split_in=split_in, split_out=split_out, n_first=n_first)
    grid_spec = pltpu.PrefetchScalarGridSpec(
        num_scalar_prefetch=1 if mode == "routed" else 0, grid=(t // tm,),
        in_specs=in_specs, out_specs=out_specs, scratch_shapes=scratch)
    if mode == "routed":
        args = [dest] + args
    outs = pl.pallas_call(kern, grid_spec=grid_spec, out_shape=out_shape,
                          compiler_params=_params(1), name=name)(*args)
    n_x = (2 if split_out else 1) if want_x else 0
    x_new = (tuple(outs[:2]) if split_out else outs[0]) if want_x else x
    h = outs[n_x] if want_h else None
    return x_new, h


def _norm_kernel(x_ref, g_ref, o_ref):
    o_ref[...] = _rms(x_ref[...], g_ref[...])


def _norm_cols(arr, width, col_block, g, tm, name):
    t = arr.shape[0]
    return pl.pallas_call(
        _norm_kernel, grid=(t // tm,),
        in_specs=[pl.BlockSpec((tm, width), lambda i: (i, col_block)),
                  pl.BlockSpec((1, width), lambda i: (0, 0))],
        out_specs=pl.BlockSpec((tm, width), lambda i: (i, 0)),
        out_shape=jax.ShapeDtypeStruct((t, width), F32),
        compiler_params=_params(1), name=name)(arr, g.reshape(1, width))


def _swap_pairs(w, rope, axis=-1):
    q = rope // 4
    part = lambda a, b: lax.slice_in_dim(w, a, b, axis=axis)
    return jnp.concatenate([part(q, 2 * q), part(0, q), part(3 * q, 4 * q), part(2 * q, 3 * q)],
                           axis=axis)


def _prep_kernel(misc_ref, cos_ref, sin_ref, wg_ref, bg_ref, kr_ref, la_ref):
    misc = misc_ref[:, :LANE]
    kr = misc * cos_ref[...] + misc_ref[:, LANE:] * sin_ref[...]
    kr_ref[...] = kr.astype(kr_ref.dtype)
    z = _dot(misc.astype(BF16), wg_ref[...]) + bg_ref[...]
    log_sig = jnp.minimum(z, 0.0) - jnp.log(1.0 + jnp.exp(-jnp.abs(z)))
    la_ref[...] = log_sig * (1.0 / GATE_NORM)


def _attn_kernel(*refs, n_seg, n_heads, group, scale):
    qn_ref, qr_ref = refs[:2]
    kv_refs = refs[2:2 + n_seg]
    kr_refs = refs[2 + n_seg:2 + 2 * n_seg]
    o_ref = refs[-1]
    hw = QK_NOPE + V_HEAD
    krs = [kr_ref[...] for kr_ref in kr_refs]
    add = lambda a, b: a + b
    c = scale * math.log2(math.e)
    for h0 in range(0, n_heads, group):
        hds = range(h0, min(h0 + group, n_heads))
        scores = []
        for hd in hds:
            q = jnp.concatenate([qn_ref[:, hd * QK_NOPE:(hd + 1) * QK_NOPE],
                                 qr_ref[:, hd * LANE:(hd + 1) * LANE]], axis=1)
            scores.append([
                _dot_nt(q, jnp.concatenate([kv_ref[:, hd * hw:hd * hw + QK_NOPE], kr], axis=1))
                for kv_ref, kr in zip(kv_refs, krs)])
        ms = [functools.reduce(jnp.maximum, [jnp.max(s, axis=-1, keepdims=True) for s in ss])
              for ss in scores]
        ps = [[jnp.exp2((s - m) * c) for s in ss] for ss, m in zip(scores, ms)]
        invs = [1.0 / functools.reduce(add, [jnp.sum(p, axis=-1, keepdims=True) for p in pp])
                for pp in ps]
        for hd, pp, inv in zip(hds, ps, invs):
            o = functools.reduce(add, [
                _dot((p * inv).astype(BF16), kv_ref[:, hd * hw + QK_NOPE:(hd + 1) * hw])
                for p, kv_ref in zip(pp, kv_refs)])
            o_ref[:, hd * V_HEAD:(hd + 1) * V_HEAD] = o.astype(o_ref.dtype)


def _attention(q_nope, q_rope, kvs, krs, *, n_batch, t_q, q_row0, tq, n_heads, group, scale, name,
               o_prev):
    hw = QK_NOPE + V_HEAD
    nq = t_q // tq
    q_map = lambda b, g, i: (q_row0 // tq + b * nq + i, g)
    in_specs = [pl.BlockSpec((tq, n_heads * QK_NOPE), q_map),
                pl.BlockSpec((tq, n_heads * LANE), q_map)]
    args = [q_nope, q_rope]
    for arr, tk, row0 in kvs:
        in_specs.append(pl.BlockSpec((tk, n_heads * hw),
                                     lambda b, g, i, tk=tk, row0=row0: (row0 // tk + b, g)))
        args.append(arr)
    for arr, tk, row0 in krs:
        in_specs.append(pl.BlockSpec((tk, LANE),
                                     lambda b, g, i, tk=tk, row0=row0: (row0 // tk + b, 0)))
        args.append(arr)
    aliases = {len(args): 0}
    in_specs.append(pl.BlockSpec(memory_space=pl.ANY))
    args.append(o_prev)
    kern = functools.partial(_attn_kernel, n_seg=len(kvs), n_heads=n_heads, group=group,
                             scale=scale)
    return pl.pallas_call(
        kern, grid=(n_batch, MLA_HEADS // n_heads, nq), in_specs=in_specs,
        out_specs=pl.BlockSpec((tq, n_heads * V_HEAD), q_map),
        out_shape=jax.ShapeDtypeStruct(o_prev.shape, o_prev.dtype),
        input_output_aliases=aliases,
        compiler_params=_params(3), name=name)(*args)


def _split3(x):
    hi = x.astype(BF16)
    r1 = x - hi.astype(F32)
    mid = r1.astype(BF16)
    lo = (r1 - mid.astype(F32)).astype(BF16)
    return hi, mid, lo


def _gla_kernel(*refs, has_init, want_states, n_alias, n_heads, q_scale, layer, all_layers):
    it = iter(refs)
    q_ref, k_ref, v_ref, laf_ref, lab_ref, gout_ref, ggla_ref = (next(it) for _ in range(7))
    s0_refs = (next(it), next(it)) if has_init else None
    for _ in range(n_alias):
        next(it)
    o_ref = next(it)
    s_out_refs = (next(it), next(it)) if want_states else None
    st, vb, qd_s, ks_s, dec_s, oacc = (next(it) for _ in range(6))
    c = GLA_CHUNK
    t_b = q_ref.shape[0]
    dk, dv = q_ref.shape[1] // n_heads, v_ref.shape[1] // n_heads
    n_chunks = t_b // c
    rb = min(t_b, 4 * c)
    row = lax.broadcasted_iota(jnp.int32, (rb, rb), 0)
    col = lax.broadcasted_iota(jnp.int32, (rb, rb), 1)
    same = (row // c) == (col // c)
    masks = (same & (col <= row), same & (col >= row))
    tris = [jnp.where(m, 1.0, 0.0).astype(BF16) for m in masks]
    la_refs = (laf_ref, lab_ref)
    streams = [(g, d) for g in range(n_heads) for d in range(2)]
    kcols = [pl.ds(g * dk, dk) for g in range(n_heads)]
    vcols = [pl.ds(g * dv, dv) for g in range(n_heads)]

    vb[...] = v_ref[...].astype(BF16)
    for s, (g, d) in enumerate(streams):
        st[s] = s0_refs[d][0, g].T if has_init else jnp.zeros(st.shape[1:], F32)

    for r0 in range(0, t_b, rb):
        rows = pl.ds(r0, rb)
        for s, (g, d) in enumerate(streams):
            hi, mid, lo = _split3(la_refs[d][rows, kcols[g]])
            b = (_dot(tris[d], hi) + _dot(tris[d], mid)) + _dot(tris[d], lo)
            end = c - 1 if d == 0 else 0
            b_end = jnp.concatenate(
                [jnp.broadcast_to(b[k * c + end:k * c + end + 1, :], (c, dk))
                 for k in range(rb // c)], axis=0)
            kc = k_ref[rows, kcols[g]]
            qd = (q_ref[rows, kcols[g]] * q_scale * jnp.exp(b)).astype(BF16)
            kd = (kc * jnp.exp(-b)).astype(BF16)
            qd_s[s, rows, :] = qd
            ks_s[s, rows, :] = (kc * jnp.exp(b_end - b)).astype(BF16)
            dec_s[s, rows, :] = jnp.exp(b_end)
            a = jnp.where(masks[d], _dot_nt(qd, kd), 0.0).astype(BF16)
            o_in = _dot(a, vb[rows, vcols[g]])
            if d == 0:
                oacc[rows, vcols[g]] = o_in
            else:
                oacc[rows, vcols[g]] += o_in

    for i in range(n_chunks):
        for s, (g, d) in enumerate(streams):
            c0 = (i if d == 0 else n_chunks - 1 - i) * c
            rows = pl.ds(c0, c)
            u = _dot_tn(vb[rows, vcols[g]], ks_s[s, rows, :])
            s_t = st[s]
            oacc[rows, vcols[g]] += _dot_nt(qd_s[s, rows, :], s_t.astype(BF16))
            st[s] = s_t * dec_s[s, pl.ds(c0, 1), :] + u

    for g in range(n_heads):
        o_ref[:, vcols[g]] = (_rms(oacc[:, vcols[g]], ggla_ref[...])
                              * _silu(gout_ref[:, vcols[g]])).astype(o_ref.dtype)
    if want_states:
        for s, (g, d) in enumerate(streams):
            s_fin = st[s].T
            if all_layers:
                for ll in range(s_out_refs[d].shape[1]):
                    s_out_refs[d][0, ll, g] = s_fin if ll == layer else jnp.zeros_like(s_fin)
            else:
                s_out_refs[d][0, g] = s_fin


def _gla(proj, la, g_gla, *, n_batch, t_b, row0, heads, n_heads, dk, dv, col_q, col_k, col_v,
         col_g, name, o_prev, s0=None, layer=0, st_prev=None, st_depth=0):
    has_init = s0 is not None
    rb = row0 // t_b
    wk, wv = n_heads * dk, n_heads * dv
    assert col_q % wk == 0 and col_k % wk == 0 and col_v % wv == 0 and col_g % wv == 0
    in_specs = [
        pl.BlockSpec((t_b, wk), lambda b, h: (rb + b, col_q // wk + h)),
        pl.BlockSpec((t_b, wk), lambda b, h: (rb + b, col_k // wk + h)),
        pl.BlockSpec((t_b, wv), lambda b, h: (rb + b, col_v // wv + h)),
        pl.BlockSpec((t_b, wk), lambda b, h: (rb + b, h)),
        pl.BlockSpec((t_b, wk), lambda b, h: (rb + b, heads // n_heads + h)),
        pl.BlockSpec((t_b, wv), lambda b, h: (rb + b, col_g // wv + h)),
        pl.BlockSpec((1, dv), lambda b, h: (0, 0)),
    ]
    args = [proj, proj, proj, la, la, proj, g_gla.reshape(1, dv)]
    st_spec = pl.BlockSpec((1, None, n_heads, dk, dv), lambda b, h: (b, layer, h, 0, 0))
    if has_init:
        in_specs += [st_spec, st_spec]
        args += list(s0)
    out_specs = [pl.BlockSpec((t_b, wv), lambda b, h: (rb + b, h))]
    out_shape = [jax.ShapeDtypeStruct(o_prev.shape, o_prev.dtype)]
    aliases = {len(args): 0}
    in_specs.append(pl.BlockSpec(memory_space=pl.ANY))
    args.append(o_prev)
    want_states = st_prev is not None or st_depth > 0
    if st_prev is not None:
        out_specs += [st_spec, st_spec]
        for k, arr in enumerate(st_prev):
            out_shape.append(jax.ShapeDtypeStruct(arr.shape, arr.dtype))
            aliases[len(args)] = 1 + k
            in_specs.append(pl.BlockSpec(memory_space=pl.ANY))
            args.append(arr)
    elif want_states:
        out_specs += [pl.BlockSpec((1, st_depth, n_heads, dk, dv),
                                   lambda b, h: (b, 0, h, 0, 0))] * 2
        out_shape += [jax.ShapeDtypeStruct((n_batch, st_depth, heads, dk, dv), F32)] * 2
    kern = functools.partial(_gla_kernel, has_init=has_init, want_states=want_states,
                             n_alias=len(aliases), n_heads=n_heads, q_scale=dk ** -0.5,
                             layer=layer, all_layers=st_prev is None)
    n_str = 2 * n_heads
    return pl.pallas_call(
        kern, grid=(n_batch, heads // n_heads), in_specs=in_specs, out_specs=out_specs,
        out_shape=out_shape, input_output_aliases=aliases,
        scratch_shapes=[pltpu.VMEM((n_str, dv, dk), F32), pltpu.VMEM((t_b, wv), BF16),
                        pltpu.VMEM((n_str, t_b, dk), BF16), pltpu.VMEM((n_str, t_b, dk), BF16),
                        pltpu.VMEM((n_str, t_b, dk), F32), pltpu.VMEM((t_b, wv), F32)],
        compiler_params=_params(2), name=name)(*args)


def _router_kernel(h_ref, w_ref, o_ref, *, n_experts):
    logits = _dot(h_ref[...].astype(BF16), w_ref[...])
    lane = lax.broadcasted_iota(jnp.int32, logits.shape, 1).astype(F32)
    neg = jnp.float32(-jnp.inf)
    lg = jnp.where(lane < n_experts, logits, neg)
    m1 = jnp.max(lg, axis=-1, keepdims=True)
    i1 = jnp.min(jnp.where(lg == m1, lane, LANE), axis=-1, keepdims=True)
    lg2 = jnp.where(lane == i1, neg, lg)
    m2 = jnp.max(lg2, axis=-1, keepdims=True)
    i2 = jnp.min(jnp.where(lg2 == m2, lane, LANE), axis=-1, keepdims=True)
    e2 = jnp.exp(m2 - m1)
    inv = 1.0 / (1.0 + e2)
    o_ref[...] = (jnp.where(lane == 0, i1, 0.0) + jnp.where(lane == 1, i2, 0.0)
                  + jnp.where(lane == 2, inv, 0.0) + jnp.where(lane == 3, e2 * inv, 0.0))


def _moe_proj_kernel(info_ref, x_ref, *refs, n_sub, combine):
    w_refs, o_ref = refs[:-1], refs[-1]
    s = pl.program_id(0)
    flag0 = pl.num_programs(0) + s * n_sub
    sub = x_ref.shape[0] // n_sub

    @pl.when(info_ref[flag0] != 0)
    def _():
        ws = [w_ref[...].astype(BF16) for w_ref in w_refs]
        for k in range(n_sub):
            rows = pl.ds(k * sub, sub)

            @pl.when(info_ref[flag0 + k] != 0)
            def _():
                a = x_ref[rows, :]
                o_ref[rows, :] = combine([_dot(a, w) for w in ws]).astype(o_ref.dtype)

            @pl.when(info_ref[flag0 + k] == 0)
            def _():
                o_ref[rows, :] = jnp.zeros((sub, o_ref.shape[1]), o_ref.dtype)

    @pl.when(info_ref[flag0] == 0)
    def _():
        o_ref[...] = jnp.zeros_like(o_ref)


def _moe_proj(xs, weights, lead, info, *, run, sub, tn, out_dtype, combine, name):
    n_rows, d = xs.shape
    n_out = weights[0].shape[-1]
    tn = _tile(n_out, tn)
    n_run = n_rows // run
    n_sub = run // sub

    def w_map(s, j, info):
        return tuple(lead) + (info[s], 0, jnp.where(info[n_run + s * n_sub] != 0, j, 0))

    w_spec = pl.BlockSpec((None,) * (len(lead) + 1) + (d, tn), w_map)
    grid_spec = pltpu.PrefetchScalarGridSpec(
        num_scalar_prefetch=1, grid=(n_run, n_out // tn),
        in_specs=[pl.BlockSpec((run, d), lambda s, j, info: (s, 0))] + [w_spec] * len(weights),
        out_specs=pl.BlockSpec((run, tn), lambda s, j, info: (s, j)))
    return pl.pallas_call(
        functools.partial(_moe_proj_kernel, n_sub=n_sub, combine=combine), grid_spec=grid_spec,
        out_shape=jax.ShapeDtypeStruct((n_rows, n_out), out_dtype),
        compiler_params=_params(2), name=name)(info, xs, *weights)


def _routing_tables(route, n_experts, tile, run):
    t = route.shape[0]
    n_assign = t * TOP_K
    n_rows = -(-(n_assign + n_experts * (run - 1)) // run) * run
    e_flat = route[:, :TOP_K].astype(jnp.int32).reshape(n_assign)
    onehot = (e_flat[:, None] == jnp.arange(n_experts, dtype=jnp.int32)[None, :]).astype(jnp.int32)
    csum = jnp.cumsum(onehot, axis=0)
    rank = jnp.take_along_axis(csum, e_flat[:, None], axis=1)[:, 0] - 1
    counts = csum[-1]
    padded = ((counts + run - 1) // run) * run
    ends = jnp.cumsum(padded)
    starts = ends - padded
    dest = starts[e_flat] + rank
    tok_of = jnp.zeros((n_rows,), jnp.int32).at[dest].set(
        jnp.arange(n_assign, dtype=jnp.int32) // TOP_K, unique_indices=True)
    tile_start = jnp.arange(n_rows // tile, dtype=jnp.int32) * tile
    tile_expert = jnp.minimum(
        jnp.sum(tile_start[:, None] >= ends[None, :], axis=1, dtype=jnp.int32), n_experts - 1)
    flags = (tile_start < (starts + counts)[tile_expert]).astype(jnp.int32)
    run_info = jnp.concatenate([tile_expert[::run // tile], flags])
    tile_idx = jnp.arange(n_rows // tile, dtype=jnp.int32)
    tile_info = jnp.concatenate([tile_expert, flags,
                                 lax.cummax(jnp.where(flags != 0, tile_idx, 0))])
    return tok_of, dest.astype(jnp.int32), flags, run_info, tile_info


def _rope_tables(n_lat, rope, tm_id):
    axis_half = rope // 4
    rows = n_lat // GRID_W
    r = jnp.repeat(jnp.arange(rows, dtype=F32), GRID_W)
    c = jnp.tile(jnp.arange(GRID_W, dtype=F32), rows)
    inv = ROPE_THETA ** (-jnp.arange(axis_half, dtype=F32) / axis_half)
    ar, ac = r[:, None] * inv, c[:, None] * inv
    cos = jnp.concatenate([jnp.cos(ar), jnp.cos(ar), jnp.cos(ac), jnp.cos(ac)], axis=1)
    sin = jnp.concatenate([-jnp.sin(ar), jnp.sin(ar), -jnp.sin(ac), jnp.sin(ac)], axis=1)
    cos = jnp.concatenate([jnp.ones((tm_id, rope), F32), cos], axis=0)
    sin = jnp.concatenate([jnp.zeros((tm_id, rope), F32), sin], axis=0)
    return cos, sin


def kernel(x_prompt, x_sample, cache_ckv, cache_krope, state_gla_fwd, state_gla_bwd, c, c_ctx, w_ada, b_ada, g_pre_mix, g_post_mix, g_pre_ffn, g_post_ffn, w_in, g_q, w_uq, g_kv, w_ukv, w_gate_f, b_gate_f, w_gate_b, b_gate_b, g_gla, w_pa, w_pb, w_o, w_ff_gate, w_ff_up, w_ff_down, w_router, w_ex_gate, w_ex_up, w_ex_down):
    n_ctx_b, t_ctx, d = x_prompt.shape
    n_lat_b, t_lat, _ = x_sample.shape
    depth = w_in.shape[0]
    past = cache_ckv.shape[2]
    q_lora, kv_lora = g_q.shape[1], g_kv.shape[1]
    rope = cache_krope.shape[3]
    heads, dk, dv = state_gla_fwd.shape[2:]
    rank = w_gate_f.shape[1]
    n_experts = w_router.shape[2]
    hq = QK_NOPE + rope
    hw = QK_NOPE + V_HEAD
    n_ctx, n_lat = n_ctx_b * t_ctx, n_lat_b * t_lat
    t = n_ctx + n_lat
    assert 2 * rope == LANE and 2 * rank <= LANE - rope

    tm = _tile(math.gcd(n_ctx, t_lat), 512)
    tm_s = _tile(math.gcd(n_ctx, t_lat), 1024)
    n_ctx_tiles = n_ctx // tm
    lat_tiles = t_lat // tm

    def sample_of_tile(i):
        return jnp.where(i < n_ctx_tiles, 0, 1 + (i - n_ctx_tiles) // lat_tiles)

    def pos_of_tile(i, tile=tm):
        first = n_ctx // tile
        return jnp.where(i < first, 0, tm_s // tile + (i - first) % (t_lat // tile))

    sizes = (q_lora, kv_lora, rope, heads * dk, heads * dk, heads * dv, rank, rank,
             heads * dv, d, d)
    offs = [0]
    for s in sizes:
        offs.append(offs[-1] + s)
    main_groups = (0, 1, 3, 4, 5, 8, 9, 10)
    col = {}
    acc = 0
    for gidx in main_groups:
        col[gidx] = acc
        acc += sizes[gidx]
    n_main = acc
    col_q, col_k, col_v, col_g, col_a, col_b = col[3], col[4], col[5], col[8], col[9], col[10]

    cos_r, sin_r = _rope_tables(t_lat, rope, tm_s)
    n_tab = cos_r.shape[0]
    cos_k = jnp.concatenate([cos_r, jnp.zeros((n_tab, LANE - rope), F32)], axis=1)
    sin_k = jnp.concatenate([sin_r, jnp.zeros((n_tab, LANE - rope), F32)], axis=1)

    x = (x_prompt.reshape(n_ctx, d), x_sample.reshape(n_lat, d))
    c_all = jnp.concatenate([c_ctx[None, :], c, jnp.zeros((8 - 1 - n_lat_b, d), F32)], axis=0)

    def modulation(l):
        def epi(accs, e):
            return accs[0] + e[0]
        tn = _tile(6 * d, 1024)
        m = _matmul([(c_all, d, 0)], [(w_ada, (l,))], n_out=6 * d, tm=8, tn=tn, out_dtype=F32,
                    epilogue=epi, prologue=lambda k, a, e: _silu(a), name=f"ada{l}",
                    extras=[(b_ada.reshape(depth, 1, 6 * d), (None, 1, tn),
                             lambda j, i, l=l: (l, 0, j))])
        return m.reshape(8, 6, d)

    mods = [modulation(l) for l in range(depth)]
    _, h = _resid_norm(x, sample_of_tile, tm, "prenorm0", mod_pre=mods[0], g_pre=g_pre_mix[0],
                       shift_idx=0, scale_idx=1, n_first=n_ctx_tiles)

    new_ckv, new_krope = [], []
    states = None
    w_in_t = jnp.swapaxes(w_in, 1, 2)
    for l in range(depth):
        n_all = n_main + 2 * LANE

        def repack_kernel(w_ref, o_ref):
            w_kr = w_ref[offs[2]:offs[3], :]
            lanes = w_ref.shape[1]
            pieces = [w_ref[offs[gidx]:offs[gidx + 1], :] for gidx in main_groups]
            pieces += [w_kr, w_ref[offs[6]:offs[8], :],
                       jnp.zeros((LANE - rope - 2 * rank, lanes), F32),
                       _swap_pairs(w_kr, rope, 0), jnp.zeros((LANE - rope, lanes), F32)]
            o_ref[...] = jnp.concatenate(pieces, axis=0).astype(BF16)

        tk_r = _tile(d, 256)
        w_all = pl.pallas_call(
            repack_kernel, grid=(d // tk_r,),
            in_specs=[pl.BlockSpec((None, offs[-1], tk_r), lambda i, l=l: (l, 0, i))],
            out_specs=pl.BlockSpec((n_all, tk_r), lambda i: (0, i)),
            out_shape=jax.ShapeDtypeStruct((n_all, d), BF16),
            compiler_params=_params(1), name=f"w_in_repack{l}")(w_in_t)
        proj = _matmul([(h, d, 0)], [(w_all, ())], n_out=n_all, tm=tm_s, tn=_tile(n_all, 1280),
                       out_dtype=F32, epilogue=_first, name=f"w_in{l}", w_rows=True)

        w_gate = jnp.zeros((LANE, 2 * heads * dk), F32)
        w_gate = w_gate.at[rope:rope + rank, :heads * dk].set(w_gate_f[l])
        w_gate = w_gate.at[rope + rank:rope + 2 * rank, heads * dk:].set(w_gate_b[l])
        b_gate = jnp.concatenate([b_gate_f[l], b_gate_b[l]]).reshape(1, 2 * heads * dk)
        kr_self, la = pl.pallas_call(
            _prep_kernel, grid=(t // tm,),
            in_specs=[pl.BlockSpec((tm, 2 * LANE), lambda i: (i, n_main // (2 * LANE))),
                      pl.BlockSpec((tm, LANE), lambda i: (pos_of_tile(i), 0)),
                      pl.BlockSpec((tm, LANE), lambda i: (pos_of_tile(i), 0)),
                      pl.BlockSpec((LANE, 2 * heads * dk), lambda i: (0, 0)),
                      pl.BlockSpec((1, 2 * heads * dk), lambda i: (0, 0))],
            out_specs=[pl.BlockSpec((tm, LANE), lambda i: (i, 0)),
                       pl.BlockSpec((tm, 2 * heads * dk), lambda i: (i, 0))],
            out_shape=[jax.ShapeDtypeStruct((t, LANE), BF16),
                       jax.ShapeDtypeStruct((t, 2 * heads * dk), F32)],
            compiler_params=_params(1), name=f"prep{l}")(proj, cos_k, sin_k, w_gate.astype(BF16), b_gate)

        wq = w_uq[l].reshape(q_lora, MLA_HEADS, hq)
        wq_r = wq[:, :, QK_NOPE:]
        pad = jnp.zeros((q_lora, MLA_HEADS, LANE - rope), F32)
        as_cols = lambda a: a.reshape(q_lora, -1).astype(BF16)
        wq_n = as_cols(wq[:, :, :QK_NOPE])
        wq_s = as_cols(jnp.concatenate([_swap_pairs(wq_r, rope), pad], axis=2))
        wq_r = as_cols(jnp.concatenate([wq_r, pad], axis=2))
        q_in = [(proj, q_lora, col[0] // q_lora)]
        q_norm = lambda k, a, e: _rms(a, e[0])
        g_q_extra = (g_q[l].reshape(1, q_lora), (1, q_lora), lambda j, i: (0, 0))
        tn_q = _tile(MLA_HEADS * LANE, 1024)

        def rope_epi(accs, e, tn_q=tn_q):
            cos, sin = e[1], e[2]
            return jnp.concatenate([accs[0][:, s:s + LANE] * cos + accs[1][:, s:s + LANE] * sin
                                    for s in range(0, tn_q, LANE)], axis=1)

        q_nope = _matmul(q_in, [(wq_n, ())], n_out=MLA_HEADS * QK_NOPE, tm=tm_s,
                         tn=_tile(MLA_HEADS * QK_NOPE, 2048), out_dtype=BF16, epilogue=_first,
                         name=f"w_uq_nope{l}", prologue=q_norm, extras=[g_q_extra])
        q_rope = _matmul(q_in, [(wq_r, ()), (wq_s, ())], a_of_w=[0, 0], n_out=MLA_HEADS * LANE,
                         tm=tm_s, tn=tn_q, out_dtype=BF16, epilogue=rope_epi,
                         name=f"w_uq_rope{l}", prologue=q_norm,
                         extras=[g_q_extra,
                                 (cos_k, (tm_s, LANE), lambda j, i: (pos_of_tile(i, tm_s), 0)),
                                 (sin_k, (tm_s, LANE), lambda j, i: (pos_of_tile(i, tm_s), 0))])

        c_kv = _norm_cols(proj, kv_lora, col[1] // kv_lora, g_kv[l], tm, f"ckv_norm{l}")
        tn_kv = _tile(MLA_HEADS * hw, 2048)
        kv_self = _matmul([(c_kv, kv_lora, 0)], [(w_ukv, (l,))], n_out=MLA_HEADS * hw, tm=tm_s,
                          tn=tn_kv, out_dtype=BF16, epilogue=_first, name=f"w_ukv{l}")
        ckv_cache = cache_ckv[:, l].reshape(n_lat_b * past, kv_lora)
        kv_cache = _matmul([(ckv_cache, kv_lora, 0)], [(w_ukv, (l,))], n_out=MLA_HEADS * hw,
                           tm=_tile(n_lat_b * past, 512), tn=tn_kv, out_dtype=BF16,
                           epilogue=_first, name=f"w_ukv_cache{l}")
        kr_cache = jnp.pad(cache_krope[:, l].reshape(n_lat_b * past, rope),
                           ((0, 0), (0, LANE - rope))).astype(BF16)

        scale = hq ** -0.5
        attn = _attention(q_nope, q_rope, [(kv_self, t_ctx, 0)], [(kr_self, t_ctx, 0)],
                          n_batch=n_ctx_b, t_q=t_ctx, q_row0=0, tq=_tile(t_ctx, 256),
                          n_heads=MLA_HEADS, group=8, scale=scale, name=f"attn_ctx{l}",
                          o_prev=jnp.zeros((t, MLA_HEADS * V_HEAD), BF16))
        attn = _attention(q_nope, q_rope, [(kv_self, t_lat, n_ctx), (kv_cache, past, 0)],
                          [(kr_self, t_lat, n_ctx), (kr_cache, past, 0)], n_batch=n_lat_b,
                          t_q=t_lat, q_row0=n_ctx, tq=_tile(t_lat, 1024),
                          n_heads=math.gcd(MLA_HEADS, 4), group=4, scale=scale, name=f"attn_lat{l}",
                          o_prev=attn)

        gla_kw = dict(heads=heads, dk=dk, dv=dv, col_q=col_q, col_k=col_k, col_v=col_v,
                      col_g=col_g, layer=l)
        gla, *states = _gla(proj, la, g_gla[l], n_batch=n_ctx_b, t_b=t_ctx, row0=0,
                            n_heads=math.gcd(heads, 2), name=f"gla_ctx{l}",
                            o_prev=jnp.zeros((t, heads * dv), BF16), st_prev=states,
                            st_depth=depth, **gla_kw)
        gla, = _gla(proj, la, g_gla[l], n_batch=n_lat_b, t_b=t_lat, row0=n_ctx, n_heads=1,
                    s0=(state_gla_fwd, state_gla_bwd), name=f"gla_lat{l}", o_prev=gla, **gla_kw)

        new_ckv.append(c_kv[:n_ctx].reshape(n_ctx_b, t_ctx, kv_lora))
        new_krope.append(proj[:n_ctx, n_main:n_main + rope].reshape(n_ctx_b, t_ctx, rope))

        tn_m = _tile(math.gcd(col_a, col_b, d), 1024)
        tm_m = _tile(tm, 256, 8)

        def merge_epi(accs, e):
            return jax.nn.sigmoid(e[0]) * accs[0] + jax.nn.sigmoid(e[1]) * accs[1]

        merged = _matmul([(attn, MLA_HEADS * V_HEAD, 0), (gla, heads * dv, 0)],
                         [(w_pa, (l,)), (w_pb, (l,))], n_out=d, tm=tm_m, tn=tn_m, out_dtype=BF16,
                         epilogue=merge_epi, name=f"merge{l}",
                         extras=[(proj, (tm_m, tn_m), lambda j, i: (i, col_a // tn_m + j)),
                                 (proj, (tm_m, tn_m), lambda j, i: (i, col_b // tn_m + j))])
        y = _matmul([(merged, d, 0)], [(w_o, (l,))], n_out=d, tm=tm_s, tn=_tile(d, 1024),
                    out_dtype=F32, epilogue=_first, name=f"w_o{l}")
        moe = l % 2 == 1
        x, h = _resid_norm(x, sample_of_tile, tm, f"mix_resid{l}", y=y, mod_res=mods[l],
                           g_post=g_post_mix[l], gate_idx=2, mod_pre=mods[l], g_pre=g_pre_ffn[l],
                           shift_idx=3, scale_idx=4, n_first=n_ctx_tiles)

        jx = l // 2

        def swiglu_epi(accs, e):
            return _silu(accs[0]) * accs[1]

        ffn_out = {}
        if not moe:
            d_ff = w_ff_gate.shape[2]
            ff = _matmul([(h, d, 0)], [(w_ff_gate, (jx,)), (w_ff_up, (jx,))], a_of_w=[0, 0],
                         n_out=d_ff, tm=tm_s, tn=_tile(d_ff, 512), out_dtype=BF16,
                         epilogue=swiglu_epi, name=f"ffn_up{l}")
            ffn_out["y"] = _matmul([(ff, d_ff, 0)], [(w_ff_down, (jx,))], n_out=d, tm=tm,
                                   tn=_tile(d, 512), out_dtype=F32, epilogue=_first,
                                   name=f"ffn_down{l}")
        else:
            d_ex = w_ex_gate.shape[3]
            w_r = jnp.pad(w_router[jx], ((0, 0), (0, LANE - n_experts))).astype(BF16)
            route = pl.pallas_call(
                functools.partial(_router_kernel, n_experts=n_experts), grid=(t // tm,),
                in_specs=[pl.BlockSpec((tm, d), lambda i: (i, 0)),
                          pl.BlockSpec((d, LANE), lambda i: (0, 0))],
                out_specs=pl.BlockSpec((tm, LANE), lambda i: (i, 0)),
                out_shape=jax.ShapeDtypeStruct((t, LANE), F32),
                compiler_params=_params(1), name=f"router{l}")(h, w_r)
            tile_e = _tile(t * TOP_K, EXPERT_TILE, 8)
            run_e = tile_e * EXPERT_RUN_TILES
            tok_of, dest, flags, run_info, tile_info = _routing_tables(route, n_experts, tile_e,
                                                                       run_e)
            xs = _gather(h.reshape(t, d // LANE, LANE), tok_of, flags, tile_e, f"moe_gather{l}")
            xs = xs.reshape(xs.shape[0], d)
            ff = _moe_proj(xs, [w_ex_gate, w_ex_up], (jx,), run_info, run=run_e, sub=tile_e,
                           tn=256, out_dtype=BF16,
                           combine=lambda accs: _silu(accs[0]) * accs[1], name=f"moe_up{l}")
            ys = _matmul([(ff, d_ex, 0)], [(w_ex_down, (jx, 0))], n_out=d, tm=tile_e,
                         tn=_tile(d, 1024), out_dtype=F32, epilogue=_first, name=f"moe_down{l}",
                         tile_expert=tile_info)
            ffn_out["routed"] = (ys, dest, route)

        if l + 1 < depth:
            x, h = _resid_norm(x, sample_of_tile, tm, f"ffn_resid{l}", mod_res=mods[l],
                               g_post=g_post_ffn[l], gate_idx=5, mod_pre=mods[l + 1],
                               g_pre=g_pre_mix[l + 1], shift_idx=0, scale_idx=1, **ffn_out)
        else:
            x, _ = _resid_norm(x, sample_of_tile, tm, f"ffn_resid{l}", mod_res=mods[l],
                               g_post=g_post_ffn[l], gate_idx=5, split_out=True,
                               n_first=n_ctx_tiles, **ffn_out)

    return (x[0].reshape(n_ctx_b, t_ctx, d), x[1].reshape(n_lat_b, t_lat, d),
            jnp.stack(new_ckv, axis=1), jnp.stack(new_krope, axis=1),
            states[0], states[1])
```

```python
import functools
import math

import jax
import jax.numpy as jnp
from jax import lax
from jax.experimental import pallas as pl
from jax.experimental.pallas import tpu as pltpu

MLA_HEADS = 16
QK_NOPE = 128
V_HEAD = 128
GRID_W = 64
ROPE_THETA = 10000.0
GATE_NORM = 16.0
GLA_CHUNK = 64
TOP_K = 2
EPS = 1e-6
LANE = 128
VMEM_LIMIT = 52 * 1024 * 1024
EXPERT_TILE = 512
EXPERT_RUN_TILES = 4

BF16 = jnp.bfloat16
F32 = jnp.float32


def _params(n_grid, **kw):
    return pltpu.CompilerParams(
        dimension_semantics=("arbitrary",) * n_grid, vmem_limit_bytes=VMEM_LIMIT, **kw)


def _tile(n, target, quantum=LANE):
    if n <= target:
        return n
    t = (target // quantum) * quantum
    while t >= quantum:
        if n % t == 0:
            return t
        t -= quantum
    return n


def _rms(x, g):
    return x * lax.rsqrt(jnp.mean(x * x, axis=-1, keepdims=True) + EPS) * g


def _silu(x):
    return x * jax.nn.sigmoid(x)


def _dot(a, b):
    return jnp.dot(a, b, preferred_element_type=F32)


def _dot_nt(a, b):
    return lax.dot_general(a, b, (((1,), (1,)), ((), ())), preferred_element_type=F32)


def _dot_tn(a, b):
    return lax.dot_general(a, b, (((0,), (0,)), ((), ())), preferred_element_type=F32)


def _mm_kernel(*refs, n_a, n_w, a_of_w, n_e, cast_w, grouped, w_rows, prologue, epilogue):
    if grouped:
        te_ref, refs = refs[0], refs[1:]
    a_refs = refs[:n_a]
    w_refs = refs[n_a:n_a + n_w]
    e_refs = refs[n_a + n_w:n_a + n_w + n_e]
    o_ref = refs[n_a + n_w + n_e]
    wb_refs = refs[n_a + n_w + n_e + 1:]
    i = pl.program_id(1)

    def compute():
        if cast_w:
            fresh = i == 0
            if grouped:
                fresh = fresh | (te_ref[i] != te_ref[jnp.maximum(i - 1, 0)])

            @pl.when(fresh)
            def _():
                for w_ref, wb_ref in zip(w_refs, wb_refs):
                    wb_ref[...] = w_ref[...].astype(BF16)
            w_use = wb_refs
        else:
            w_use = w_refs
        e = [e_ref[...] for e_ref in e_refs]
        a_vals = []
        for k, a_ref in enumerate(a_refs):
            a = a_ref[...]
            if prologue is not None:
                a = prologue(k, a, e)
            a_vals.append(a.astype(BF16))
        dot = _dot_nt if w_rows else _dot
        accs = [dot(a_vals[a_of_w[k]], w_ref[...]) for k, w_ref in enumerate(w_use)]
        o_ref[...] = epilogue(accs, e).astype(o_ref.dtype)

    if grouped:
        has_rows = te_ref[pl.num_programs(1) + i] != 0
        pl.when(has_rows)(compute)

        @pl.when(jnp.logical_not(has_rows))
        def _():
            o_ref[...] = jnp.zeros_like(o_ref)
    else:
        compute()


def _matmul(a_list, w_list, *, n_out, tm, tn, out_dtype, epilogue, name, a_of_w=None,
            prologue=None, extras=(), tile_expert=None, w_rows=False):
    m = a_list[0][0].shape[0]
    a_of_w = a_of_w or list(range(len(w_list)))
    grouped = tile_expert is not None
    assert m % tm == 0 and n_out % tn == 0
    in_specs, args, scratch = [], [], []
    for arr, k, cb in a_list:
        if grouped:
            a_map = lambda j, i, te, cb=cb: (te[2 * (m // tm) + i], cb)
        else:
            a_map = lambda j, i, cb=cb: (i, cb)
        in_specs.append(pl.BlockSpec((tm, k), a_map))
        args.append(arr)
    cast_w = w_list[0][0].dtype != BF16
    for arr, lead in w_list:
        k = arr.shape[-1] if w_rows else arr.shape[-2]
        if grouped:
            im = lambda j, i, te, lead=lead: tuple(lead[:-1]) + (te[i], 0, j)
        elif w_rows:
            im = lambda j, i, lead=lead: tuple(lead) + (j, 0)
        else:
            im = lambda j, i, lead=lead: tuple(lead) + (0, j)
        in_specs.append(pl.BlockSpec((None,) * len(lead) + ((tn, k) if w_rows else (k, tn)), im))
        args.append(arr)
        if cast_w:
            scratch.append(pltpu.VMEM((k, tn), BF16))
    for arr, bs, im in extras:
        in_specs.append(pl.BlockSpec(bs, lambda j, i, *_, im=im: im(j, i)))
        args.append(arr)
    kern = functools.partial(_mm_kernel, n_a=len(a_list), n_w=len(w_list), a_of_w=a_of_w,
                             n_e=len(extras), cast_w=cast_w, grouped=grouped, w_rows=w_rows,
                             prologue=prologue, epilogue=epilogue)
    grid_spec = pltpu.PrefetchScalarGridSpec(
        num_scalar_prefetch=1 if grouped else 0,
        grid=(n_out // tn, m // tm),
        in_specs=in_specs,
        out_specs=pl.BlockSpec((tm, tn), lambda j, i, *_: (i, j)),
        scratch_shapes=scratch)
    if grouped:
        args = [tile_expert] + args
    return pl.pallas_call(
        kern, grid_spec=grid_spec,
        out_shape=jax.ShapeDtypeStruct((m, n_out), out_dtype),
        compiler_params=_params(2), name=name,
    )(*args)


def _first(accs, e):
    return accs[0]


def _row_copy(src, row, dst, r, sem):
    return pltpu.make_async_copy(src.at[pl.ds(row, 1)], dst.at[pl.ds(r, 1)], sem)


def _start_rows(idx_ref, jobs, stride, n, src, sem):
    def issue(r, carry):
        for base, dst in jobs:
            _row_copy(src, idx_ref[base + r * stride], dst, r, sem).start()
        return carry

    lax.fori_loop(0, n, issue, 0, unroll=16)


def _wait_rows(dsts, n, src, sem):
    def wait(r, carry):
        for dst in dsts:
            _row_copy(src, 0, dst, r, sem).wait()
        return carry

    lax.fori_loop(0, n, wait, 0, unroll=16)


def _gather_rows(idx_ref, jobs, stride, n, src, sem):
    _start_rows(idx_ref, jobs, stride, n, src, sem)
    _wait_rows([dst for _, dst in jobs], n, src, sem)


def _gather_kernel(idx_ref, flag_ref, src_hbm, o_ref, sem):
    tg = o_ref.shape[0]
    i = pl.program_id(0)

    @pl.when(flag_ref[i] != 0)
    def _():
        _gather_rows(idx_ref, [(i * tg, o_ref)], 1, tg, src_hbm, sem)

    @pl.when(flag_ref[i] == 0)
    def _():
        o_ref[...] = jnp.zeros_like(o_ref)


def _gather(src, idx, flags, tg, name):
    n = idx.shape[0]
    blk = (tg,) + src.shape[1:]
    grid_spec = pltpu.PrefetchScalarGridSpec(
        num_scalar_prefetch=2, grid=(n // tg,),
        in_specs=[pl.BlockSpec(memory_space=pl.ANY)],
        out_specs=pl.BlockSpec(blk, lambda i, *_: (i,) + (0,) * (len(blk) - 1)),
        scratch_shapes=[pltpu.SemaphoreType.DMA(())])
    return pl.pallas_call(
        _gather_kernel, grid_spec=grid_spec,
        out_shape=jax.ShapeDtypeStruct((n,) + src.shape[1:], src.dtype),
        compiler_params=_params(1), name=name)(idx, flags, src)


def _resid_norm_kernel(*refs, mode, want_x, want_h, gate_idx, shift_idx, scale_idx, split_in,
                       split_out, n_first):
    it = iter(refs)
    dest_ref = next(it) if mode == "routed" else None
    x_refs = [next(it) for _ in range(2 if split_in else 1)]
    if mode == "dense":
        y_ref = next(it)
    elif mode == "routed":
        ys_hbm, route_ref = next(it), next(it)
    if mode is not None:
        mod_res_ref, g_post_ref = next(it), next(it)
    if want_h:
        mod_pre_ref, g_pre_ref = next(it), next(it)
    xo_refs = [next(it) for _ in range((2 if split_out else 1) if want_x else 0)]
    h_ref = next(it) if want_h else None
    i = pl.program_id(0)
    x = x_refs[0][...]
    if split_in:
        x = jnp.where(i < n_first, x, x_refs[1][...])
    if mode is not None:
        if mode == "routed":
            buf, sem = next(it), next(it)
            tm = x.shape[0]
            slot = i % 2

            def start(step, sl):
                base = step * tm * TOP_K
                _start_rows(dest_ref, [(base + k, buf.at[sl, k]) for k in range(TOP_K)], TOP_K,
                            tm, ys_hbm, sem.at[sl])

            @pl.when(i == 0)
            def _():
                start(0, 0)

            @pl.when(i + 1 < pl.num_programs(0))
            def _():
                start(i + 1, 1 - slot)

            _wait_rows([buf.at[slot, k] for k in range(TOP_K)], tm, ys_hbm, sem.at[slot])
            route = route_ref[...]
            y = route[:, 2:3] * buf[slot, 0] + route[:, 3:4] * buf[slot, 1]
        else:
            y = y_ref[...]
        gate = mod_res_ref[0, gate_idx:gate_idx + 1, :]
        x = x + gate * _rms(y, g_post_ref[...])
    if want_x and split_out:
        @pl.when(i < n_first)
        def _():
            xo_refs[0][...] = x

        @pl.when(i >= n_first)
        def _():
            xo_refs[1][...] = x
    elif want_x:
        xo_refs[0][...] = x
    if want_h:
        scale = mod_pre_ref[0, scale_idx:scale_idx + 1, :]
        shift = mod_pre_ref[0, shift_idx:shift_idx + 1, :]
        h_ref[...] = (_rms(x, g_pre_ref[...]) * (1.0 + scale) + shift).astype(h_ref.dtype)


def _resid_norm(x, sample_of_tile, tm, name, *, y=None, routed=None, mod_res=None, g_post=None,
                gate_idx=0, mod_pre=None, g_pre=None, shift_idx=0, scale_idx=0, split_out=False,
                n_first=0):
    split_in = isinstance(x, (tuple, list))
    xs_in = list(x) if split_in else [x]
    t, d = sum(a.shape[0] for a in xs_in), xs_in[0].shape[1]
    mode = "dense" if y is not None else ("routed" if routed is not None else None)
    want_h = mod_pre is not None
    want_x = mode is not None
    row = pl.BlockSpec((tm, d), lambda i, *_: (i, 0))
    first = pl.BlockSpec((tm, d), lambda i, *_: (jnp.minimum(i, n_first - 1), 0))
    rest = pl.BlockSpec((tm, d), lambda i, *_: (jnp.maximum(i - n_first, 0), 0))
    vec = pl.BlockSpec((1, d), lambda i, *_: (0, 0))
    mod = pl.BlockSpec((1, 6, d), lambda i, *_: (sample_of_tile(i), 0, 0))
    in_specs, args = ([first, rest] if split_in else [row]), xs_in
    out_specs, out_shape, scratch = [], [], []
    if mode == "dense":
        in_specs.append(row)
        args.append(y)
    elif mode == "routed":
        ys, dest, route = routed
        in_specs += [pl.BlockSpec(memory_space=pl.ANY),
                     pl.BlockSpec((tm, LANE), lambda i, *_: (i, 0))]
        args += [ys, route]
        scratch = [pltpu.VMEM((2, TOP_K, tm, d), F32), pltpu.SemaphoreType.DMA((2,))]
    if mode is not None:
        in_specs += [mod, vec]
        args += [mod_res, g_post.reshape(1, d)]
    if want_h:
        in_specs += [mod, vec]
        args += [mod_pre, g_pre.reshape(1, d)]
    if want_x and split_out:
        out_specs += [first, rest]
        out_shape += [jax.ShapeDtypeStruct((n_first * tm, d), F32),
                      jax.ShapeDtypeStruct((t - n_first * tm, d), F32)]
    elif want_x:
        out_specs.append(row)
        out_shape.append(jax.ShapeDtypeStruct((t, d), F32))
    if want_h:
        out_specs.append(row)
        out_shape.append(jax.ShapeDtypeStruct((t, d), BF16))
    kern = functools.partial(_resid_norm_kernel, mode=mode, want_x=want_x, want_h=want_h,
                             gate_idx=gate_idx, shift_idx=shift_idx, scale_idx=scale_idx,
                             split_in=split_in, split_out=split_out, n_first=n_first)
    grid_spec = pltpu.PrefetchScalarGridSpec(
        num_scalar_prefetch=1 if mode == "routed" else 0, grid=(t // tm,),
        in_specs=in_specs, out_specs=out_specs, scratch_shapes=scratch)
    if mode == "routed":
        args = [dest] + args
    outs = pl.pallas_call(kern, grid_spec=grid_spec, out_shape=out_shape,
                          compiler_params=_params(1), name=name)(*args)
    n_x = (2 if split_out else 1) if want_x else 0
    x_new = (tuple(outs[:2]) if split_out else outs[0]) if want_x else x
    h = outs[n_x] if want_h else None
    return x_new, h


def _norm_kernel(x_ref, g_ref, o_ref):
    o_ref[...] = _rms(x_ref[...], g_ref[...])


def _norm_cols(arr, width, col_block, g, tm, name):
    t = arr.shape[0]
    return pl.pallas_call(
        _norm_kernel, grid=(t // tm,),
        in_specs=[pl.BlockSpec((tm, width), lambda i: (i, col_block)),
                  pl.BlockSpec((1, width), lambda i: (0, 0))],
        out_specs=pl.BlockSpec((tm, width), lambda i: (i, 0)),
        out_shape=jax.ShapeDtypeStruct((t, width), F32),
        compiler_params=_params(1), name=name)(arr, g.reshape(1, width))


def _swap_pairs(w, rope, axis=-1):
    q = rope // 4
    part = lambda a, b: lax.slice_in_dim(w, a, b, axis=axis)
    return jnp.concatenate([part(q, 2 * q), part(0, q), part(3 * q, 4 * q), part(2 * q, 3 * q)],
                           axis=axis)


def _prep_kernel(misc_ref, cos_ref, sin_ref, wg_ref, bg_ref, kr_ref, la_ref):
    misc = misc_ref[:, :LANE]
    kr = misc * cos_ref[...] + misc_ref[:, LANE:] * sin_ref[...]
    kr_ref[...] = kr.astype(kr_ref.dtype)
    z = _dot(misc.astype(BF16), wg_ref[...]) + bg_ref[...]
    log_sig = jnp.minimum(z, 0.0) - jnp.log(1.0 + jnp.exp(-jnp.abs(z)))
    la_ref[...] = log_sig * (1.0 / GATE_NORM)


def _attn_kernel(*refs, n_seg, n_heads, group, scale):
    qn_ref, qr_ref = refs[:2]
    kv_refs = refs[2:2 + n_seg]
    kr_refs = refs[2 + n_seg:2 + 2 * n_seg]
    o_ref = refs[-1]
    hw = QK_NOPE + V_HEAD
    krs = [kr_ref[...] for kr_ref in kr_refs]
    add = lambda a, b: a + b
    c = scale * math.log2(math.e)
    for h0 in range(0, n_heads, group):
        hds = range(h0, min(h0 + group, n_heads))
        scores = []
        for hd in hds:
            q = jnp.concatenate([qn_ref[:, hd * QK_NOPE:(hd + 1) * QK_NOPE],
                                 qr_ref[:, hd * LANE:(hd + 1) * LANE]], axis=1)
            scores.append([
                _dot_nt(q, jnp.concatenate([kv_ref[:, hd * hw:hd * hw + QK_NOPE], kr], axis=1))
                for kv_ref, kr in zip(kv_refs, krs)])
        ms = [functools.reduce(jnp.maximum, [jnp.max(s, axis=-1, keepdims=True) for s in ss])
              for ss in scores]
        ps = [[jnp.exp2((s - m) * c) for s in ss] for ss, m in zip(scores, ms)]
        invs = [1.0 / functools.reduce(add, [jnp.sum(p, axis=-1, keepdims=True) for p in pp])
                for pp in ps]
        for hd, pp, inv in zip(hds, ps, invs):
            o = functools.reduce(add, [
                _dot((p * inv).astype(BF16), kv_ref[:, hd * hw + QK_NOPE:(hd + 1) * hw])
                for p, kv_ref in zip(pp, kv_refs)])
            o_ref[:, hd * V_HEAD:(hd + 1) * V_HEAD] = o.astype(o_ref.dtype)


def _attention(q_nope, q_rope, kvs, krs, *, n_batch, t_q, q_row0, tq, n_heads, group, scale, name,
               o_prev):
    hw = QK_NOPE + V_HEAD
    nq = t_q // tq
    q_map = lambda b, g, i: (q_row0 // tq + b * nq + i, g)
    in_specs = [pl.BlockSpec((tq, n_heads * QK_NOPE), q_map),
                pl.BlockSpec((tq, n_heads * LANE), q_map)]
    args = [q_nope, q_rope]
    for arr, tk, row0 in kvs:
        in_specs.append(pl.BlockSpec((tk, n_heads * hw),
                                     lambda b, g, i, tk=tk, row0=row0: (row0 // tk + b, g)))
        args.append(arr)
    for arr, tk, row0 in krs:
        in_specs.append(pl.BlockSpec((tk, LANE),
                                     lambda b, g, i, tk=tk, row0=row0: (row0 // tk + b, 0)))
        args.append(arr)
    aliases = {len(args): 0}
    in_specs.append(pl.BlockSpec(memory_space=pl.ANY))
    args.append(o_prev)
    kern = functools.partial(_attn_kernel, n_seg=len(kvs), n_heads=n_heads, group=group,
                             scale=scale)
    return pl.pallas_call(
        kern, grid=(n_batch, MLA_HEADS // n_heads, nq), in_specs=in_specs,
        out_specs=pl.BlockSpec((tq, n_heads * V_HEAD), q_map),
        out_shape=jax.ShapeDtypeStruct(o_prev.shape, o_prev.dtype),
        input_output_aliases=aliases,
        compiler_params=_params(3), name=name)(*args)


def _split3(x):
    hi = x.astype(BF16)
    r1 = x - hi.astype(F32)
    mid = r1.astype(BF16)
    lo = (r1 - mid.astype(F32)).astype(BF16)
    return hi, mid, lo


def _gla_kernel(*refs, has_init, want_states, n_alias, n_heads, q_scale, layer, all_layers):
    it = iter(refs)
    q_ref, k_ref, v_ref, laf_ref, lab_ref, gout_ref, ggla_ref = (next(it) for _ in range(7))
    s0_refs = (next(it), next(it)) if has_init else None
    for _ in range(n_alias):
        next(it)
    o_ref = next(it)
    s_out_refs = (next(it), next(it)) if want_states else None
    st, vb, qd_s, ks_s, dec_s, oacc = (next(it) for _ in range(6))
    c = GLA_CHUNK
    t_b = q_ref.shape[0]
    dk, dv = q_ref.shape[1] // n_heads, v_ref.shape[1] // n_heads
    n_chunks = t_b // c
    rb = min(t_b, 4 * c)
    row = lax.broadcasted_iota(jnp.int32, (rb, rb), 0)
    col = lax.broadcasted_iota(jnp.int32, (rb, rb), 1)
    same = (row // c) == (col // c)
    masks = (same & (col <= row), same & (col >= row))
    tris = [jnp.where(m, 1.0, 0.0).astype(BF16) for m in masks]
    la_refs = (laf_ref, lab_ref)
    streams = [(g, d) for g in range(n_heads) for d in range(2)]
    kcols = [pl.ds(g * dk, dk) for g in range(n_heads)]
    vcols = [pl.ds(g * dv, dv) for g in range(n_heads)]

    vb[...] = v_ref[...].astype(BF16)
    for s, (g, d) in enumerate(streams):
        st[s] = s0_refs[d][0, g].T if has_init else jnp.zeros(st.shape[1:], F32)

    for r0 in range(0, t_b, rb):
        rows = pl.ds(r0, rb)
        for s, (g, d) in enumerate(streams):
            hi, mid, lo = _split3(la_refs[d][rows, kcols[g]])
            b = (_dot(tris[d], hi) + _dot(tris[d], mid)) + _dot(tris[d], lo)
            end = c - 1 if d == 0 else 0
            b_end = jnp.concatenate(
                [jnp.broadcast_to(b[k * c + end:k * c + end + 1, :], (c, dk))
                 for k in range(rb // c)], axis=0)
            kc = k_ref[rows, kcols[g]]
            qd = (q_ref[rows, kcols[g]] * q_scale * jnp.exp(b)).astype(BF16)
            kd = (kc * jnp.exp(-b)).astype(BF16)
            qd_s[s, rows, :] = qd
            ks_s[s, rows, :] = (kc * jnp.exp(b_end - b)).astype(BF16)
            dec_s[s, rows, :] = jnp.exp(b_end)
            a = jnp.where(masks[d], _dot_nt(qd, kd), 0.0).astype(BF16)
            o_in = _dot(a, vb[rows, vcols[g]])
            if d == 0:
                oacc[rows, vcols[g]] = o_in
            else:
                oacc[rows, vcols[g]] += o_in

    for i in range(n_chunks):
        for s, (g, d) in enumerate(streams):
            c0 = (i if d == 0 else n_chunks - 1 - i) * c
            rows = pl.ds(c0, c)
            u = _dot_tn(vb[rows, vcols[g]], ks_s[s, rows, :])
            s_t = st[s]
            oacc[rows, vcols[g]] += _dot_nt(qd_s[s, rows, :], s_t.astype(BF16))
            st[s] = s_t * dec_s[s, pl.ds(c0, 1), :] + u

    for g in range(n_heads):
        o_ref[:, vcols[g]] = (_rms(oacc[:, vcols[g]], ggla_ref[...])
                              * _silu(gout_ref[:, vcols[g]])).astype(o_ref.dtype)
    if want_states:
        for s, (g, d) in enumerate(streams):
            s_fin = st[s].T
            if all_layers:
                for ll in range(s_out_refs[d].shape[1]):
                    s_out_refs[d][0, ll, g] = s_fin if ll == layer else jnp.zeros_like(s_fin)
            else:
                s_out_refs[d][0, g] = s_fin


def _gla(proj, la, g_gla, *, n_batch, t_b, row0, heads, n_heads, dk, dv, col_q, col_k, col_v,
         col_g, name, o_prev, s0=None, layer=0, st_prev=None, st_depth=0):
    has_init = s0 is not None
    rb = row0 // t_b
    wk, wv = n_heads * dk, n_heads * dv
    assert col_q % wk == 0 and col_k % wk == 0 and col_v % wv == 0 and col_g % wv == 0
    in_specs = [
        pl.BlockSpec((t_b, wk), lambda b, h: (rb + b, col_q // wk + h)),
        pl.BlockSpec((t_b, wk), lambda b, h: (rb + b, col_k // wk + h)),
        pl.BlockSpec((t_b, wv), lambda b, h: (rb + b, col_v // wv + h)),
        pl.BlockSpec((t_b, wk), lambda b, h: (rb + b, h)),
        pl.BlockSpec((t_b, wk), lambda b, h: (rb + b, heads // n_heads + h)),
        pl.BlockSpec((t_b, wv), lambda b, h: (rb + b, col_g // wv + h)),
        pl.BlockSpec((1, dv), lambda b, h: (0, 0)),
    ]
    args = [proj, proj, proj, la, la, proj, g_gla.reshape(1, dv)]
    st_spec = pl.BlockSpec((1, None, n_heads, dk, dv), lambda b, h: (b, layer, h, 0, 0))
    if has_init:
        in_specs += [st_spec, st_spec]
        args += list(s0)
    out_specs = [pl.BlockSpec((t_b, wv), lambda b, h: (rb + b, h))]
    out_shape = [jax.ShapeDtypeStruct(o_prev.shape, o_prev.dtype)]
    aliases = {len(args): 0}
    in_specs.append(pl.BlockSpec(memory_space=pl.ANY))
    args.append(o_prev)
    want_states = st_prev is not None or st_depth > 0
    if st_prev is not None:
        out_specs += [st_spec, st_spec]
        for k, arr in enumerate(st_prev):
            out_shape.append(jax.ShapeDtypeStruct(arr.shape, arr.dtype))
            aliases[len(args)] = 1 + k
            in_specs.append(pl.BlockSpec(memory_space=pl.ANY))
            args.append(arr)
    elif want_states:
        out_specs += [pl.BlockSpec((1, st_depth, n_heads, dk, dv),
                                   lambda b, h: (b, 0, h, 0, 0))] * 2
        out_shape += [jax.ShapeDtypeStruct((n_batch, st_depth, heads, dk, dv), F32)] * 2
    kern = functools.partial(_gla_kernel, has_init=has_init, want_states=want_states,
                             n_alias=len(aliases), n_heads=n_heads, q_scale=dk ** -0.5,
                             layer=layer, all_layers=st_prev is None)
    n_str = 2 * n_heads
    return pl.pallas_call(
        kern, grid=(n_batch, heads // n_heads), in_specs=in_specs, out_specs=out_specs,
        out_shape=out_shape, input_output_aliases=aliases,
        scratch_shapes=[pltpu.VMEM((n_str, dv, dk), F32), pltpu.VMEM((t_b, wv), BF16),
                        pltpu.VMEM((n_str, t_b, dk), BF16), pltpu.VMEM((n_str, t_b, dk), BF16),
                        pltpu.VMEM((n_str, t_b, dk), F32), pltpu.VMEM((t_b, wv), F32)],
        compiler_params=_params(2), name=name)(*args)


def _router_kernel(h_ref, w_ref, o_ref, *, n_experts):
    logits = _dot(h_ref[...].astype(BF16), w_ref[...])
    lane = lax.broadcasted_iota(jnp.int32, logits.shape, 1).astype(F32)
    neg = jnp.float32(-jnp.inf)
    lg = jnp.where(lane < n_experts, logits, neg)
    m1 = jnp.max(lg, axis=-1, keepdims=True)
    i1 = jnp.min(jnp.where(lg == m1, lane, LANE), axis=-1, keepdims=True)
    lg2 = jnp.where(lane == i1, neg, lg)
    m2 = jnp.max(lg2, axis=-1, keepdims=True)
    i2 = jnp.min(jnp.where(lg2 == m2, lane, LANE), axis=-1, keepdims=True)
    e2 = jnp.exp(m2 - m1)
    inv = 1.0 / (1.0 + e2)
    o_ref[...] = (jnp.where(lane == 0, i1, 0.0) + jnp.where(lane == 1, i2, 0.0)
                  + jnp.where(lane == 2, inv, 0.0) + jnp.where(lane == 3, e2 * inv, 0.0))


def _moe_proj_kernel(info_ref, x_ref, *refs, n_sub, combine):
    w_refs, o_ref = refs[:-1], refs[-1]
    s = pl.program_id(0)
    flag0 = pl.num_programs(0) + s * n_sub
    sub = x_ref.shape[0] // n_sub

    @pl.when(info_ref[flag0] != 0)
    def _():
        ws = [w_ref[...].astype(BF16) for w_ref in w_refs]
        for k in range(n_sub):
            rows = pl.ds(k * sub, sub)

            @pl.when(info_ref[flag0 + k] != 0)
            def _():
                a = x_ref[rows, :]
                o_ref[rows, :] = combine([_dot(a, w) for w in ws]).astype(o_ref.dtype)

            @pl.when(info_ref[flag0 + k] == 0)
            def _():
                o_ref[rows, :] = jnp.zeros((sub, o_ref.shape[1]), o_ref.dtype)

    @pl.when(info_ref[flag0] == 0)
    def _():
        o_ref[...] = jnp.zeros_like(o_ref)


def _moe_proj(xs, weights, lead, info, *, run, sub, tn, out_dtype, combine, name):
    n_rows, d = xs.shape
    n_out = weights[0].shape[-1]
    tn = _tile(n_out, tn)
    n_run = n_rows // run
    n_sub = run // sub

    def w_map(s, j, info):
        return tuple(lead) + (info[s], 0, jnp.where(info[n_run + s * n_sub] != 0, j, 0))

    w_spec = pl.BlockSpec((None,) * (len(lead) + 1) + (d, tn), w_map)
    grid_spec = pltpu.PrefetchScalarGridSpec(
        num_scalar_prefetch=1, grid=(n_run, n_out // tn),
        in_specs=[pl.BlockSpec((run, d), lambda s, j, info: (s, 0))] + [w_spec] * len(weights),
        out_specs=pl.BlockSpec((run, tn), lambda s, j, info: (s, j)))
    return pl.pallas_call(
        functools.partial(_moe_proj_kernel, n_sub=n_sub, combine=combine), grid_spec=grid_spec,
        out_shape=jax.ShapeDtypeStruct((n_rows, n_out), out_dtype),
        compiler_params=_params(2), name=name)(info, xs, *weights)


def _routing_tables(route, n_experts, tile, run):
    t = route.shape[0]
    n_assign = t * TOP_K
    n_rows = -(-(n_assign + n_experts * (run - 1)) // run) * run
    e_flat = route[:, :TOP_K].astype(jnp.int32).reshape(n_assign)
    onehot = (e_flat[:, None] == jnp.arange(n_experts, dtype=jnp.int32)[None, :]).astype(jnp.int32)
    csum = jnp.cumsum(onehot, axis=0)
    rank = jnp.take_along_axis(csum, e_flat[:, None], axis=1)[:, 0] - 1
    counts = csum[-1]
    padded = ((counts + run - 1) // run) * run
    ends = jnp.cumsum(padded)
    starts = ends - padded
    dest = starts[e_flat] + rank
    tok_of = jnp.zeros((n_rows,), jnp.int32).at[dest].set(
        jnp.arange(n_assign, dtype=jnp.int32) // TOP_K, unique_indices=True)
    tile_start = jnp.arange(n_rows // tile, dtype=jnp.int32) * tile
    tile_expert = jnp.minimum(
        jnp.sum(tile_start[:, None] >= ends[None, :], axis=1, dtype=jnp.int32), n_experts - 1)
    flags = (tile_start < (starts + counts)[tile_expert]).astype(jnp.int32)
    run_info = jnp.concatenate([tile_expert[::run // tile], flags])
    tile_idx = jnp.arange(n_rows // tile, dtype=jnp.int32)
    tile_info = jnp.concatenate([tile_expert, flags,
                                 lax.cummax(jnp.where(flags != 0, tile_idx, 0))])
    return tok_of, dest.astype(jnp.int32), flags, run_info, tile_info


def _rope_tables(n_lat, rope, tm_id):
    axis_half = rope // 4
    rows = n_lat // GRID_W
    r = jnp.repeat(jnp.arange(rows, dtype=F32), GRID_W)
    c = jnp.tile(jnp.arange(GRID_W, dtype=F32), rows)
    inv = ROPE_THETA ** (-jnp.arange(axis_half, dtype=F32) / axis_half)
    ar, ac = r[:, None] * inv, c[:, None] * inv
    cos = jnp.concatenate([jnp.cos(ar), jnp.cos(ar), jnp.cos(ac), jnp.cos(ac)], axis=1)
    sin = jnp.concatenate([-jnp.sin(ar), jnp.sin(ar), -jnp.sin(ac), jnp.sin(ac)], axis=1)
    cos = jnp.concatenate([jnp.ones((tm_id, rope), F32), cos], axis=0)
    sin = jnp.concatenate([jnp.zeros((tm_id, rope), F32), sin], axis=0)
    return cos, sin


def kernel(x_prompt, x_sample, cache_ckv, cache_krope, state_gla_fwd, state_gla_bwd, c, c_ctx, w_ada, b_ada, g_pre_mix, g_post_mix, g_pre_ffn, g_post_ffn, w_in, g_q, w_uq, g_kv, w_ukv, w_gate_f, b_gate_f, w_gate_b, b_gate_b, g_gla, w_pa, w_pb, w_o, w_ff_gate, w_ff_up, w_ff_down, w_router, w_ex_gate, w_ex_up, w_ex_down):
    n_ctx_b, t_ctx, d = x_prompt.shape
    n_lat_b, t_lat, _ = x_sample.shape
    depth = w_in.shape[0]
    past = cache_ckv.shape[2]
    q_lora, kv_lora = g_q.shape[1], g_kv.shape[1]
    rope = cache_krope.shape[3]
    heads, dk, dv = state_gla_fwd.shape[2:]
    rank = w_gate_f.shape[1]
    n_experts = w_router.shape[2]
    hq = QK_NOPE + rope
    hw = QK_NOPE + V_HEAD
    n_ctx, n_lat = n_ctx_b * t_ctx, n_lat_b * t_lat
    t = n_ctx + n_lat
    assert 2 * rope == LANE and 2 * rank <= LANE - rope

    tm = _tile(math.gcd(n_ctx, t_lat), 512)
    tm_s = _tile(math.gcd(n_ctx, t_lat), 1024)
    n_ctx_tiles = n_ctx // tm
    lat_tiles = t_lat // tm

    def sample_of_tile(i):
        return jnp.where(i < n_ctx_tiles, 0, 1 + (i - n_ctx_tiles) // lat_tiles)

    def pos_of_tile(i, tile=tm):
        first = n_ctx // tile
        return jnp.where(i < first, 0, tm_s // tile + (i - first) % (t_lat // tile))

    sizes = (q_lora, kv_lora, rope, heads * dk, heads * dk, heads * dv, rank, rank,
             heads * dv, d, d)
    offs = [0]
    for s in sizes:
        offs.append(offs[-1] + s)
    main_groups = (0, 1, 3, 4, 5, 8, 9, 10)
    col = {}
    acc = 0
    for gidx in main_groups:
        col[gidx] = acc
        acc += sizes[gidx]
    n_main = acc
    col_q, col_k, col_v, col_g, col_a, col_b = col[3], col[4], col[5], col[8], col[9], col[10]

    cos_r, sin_r = _rope_tables(t_lat, rope, tm_s)
    n_tab = cos_r.shape[0]
    cos_k = jnp.concatenate([cos_r, jnp.zeros((n_tab, LANE - rope), F32)], axis=1)
    sin_k = jnp.concatenate([sin_r, jnp.zeros((n_tab, LANE - rope), F32)], axis=1)

    x = (x_prompt.reshape(n_ctx, d), x_sample.reshape(n_lat, d))
    c_all = jnp.concatenate([c_ctx[None, :], c, jnp.zeros((8 - 1 - n_lat_b, d), F32)], axis=0)

    def modulation(l):
        def epi(accs, e):
            return accs[0] + e[0]
        tn = _tile(6 * d, 1024)
        m = _matmul([(c_all, d, 0)], [(w_ada, (l,))], n_out=6 * d, tm=8, tn=tn, out_dtype=F32,
                    epilogue=epi, prologue=lambda k, a, e: _silu(a), name=f"ada{l}",
                    extras=[(b_ada.reshape(depth, 1, 6 * d), (None, 1, tn),
                             lambda j, i, l=l: (l, 0, j))])
        return m.reshape(8, 6, d)

    mods = [modulation(l) for l in range(depth)]
    _, h = _resid_norm(x, sample_of_tile, tm, "prenorm0", mod_pre=mods[0], g_pre=g_pre_mix[0],
                       shift_idx=0, scale_idx=1, n_first=n_ctx_tiles)

    new_ckv, new_krope = [], []
    states = None
    w_in_t = jnp.swapaxes(w_in, 1, 2)
    for l in range(depth):
        n_all = n_main + 2 * LANE

        def repack_kernel(w_ref, o_ref):
            w_kr = w_ref[offs[2]:offs[3], :]
            lanes = w_ref.shape[1]
            pieces = [w_ref[offs[gidx]:offs[gidx + 1], :] for gidx in main_groups]
            pieces += [w_kr, w_ref[offs[6]:offs[8], :],
                       jnp.zeros((LANE - rope - 2 * rank, lanes), F32),
                       _swap_pairs(w_kr, rope, 0), jnp.zeros((LANE - rope, lanes), F32)]
            o_ref[...] = jnp.concatenate(pieces, axis=0).astype(BF16)

        tk_r = _tile(d, 256)
        w_all = pl.pallas_call(
            repack_kernel, grid=(d // tk_r,),
            in_specs=[pl.BlockSpec((None, offs[-1], tk_r), lambda i, l=l: (l, 0, i))],
            out_specs=pl.BlockSpec((n_all, tk_r), lambda i: (0, i)),
            out_shape=jax.ShapeDtypeStruct((n_all, d), BF16),
            compiler_params=_params(1), name=f"w_in_repack{l}")(w_in_t)
        proj = _matmul([(h, d, 0)], [(w_all, ())], n_out=n_all, tm=tm_s, tn=_tile(n_all, 1280),
                       out_dtype=F32, epilogue=_first, name=f"w_in{l}", w_rows=True)

        w_gate = jnp.zeros((LANE, 2 * heads * dk), F32)
        w_gate = w_gate.at[rope:rope + rank, :heads * dk].set(w_gate_f[l])
        w_gate = w_gate.at[rope + rank:rope + 2 * rank, heads * dk:].set(w_gate_b[l])
        b_gate = jnp.concatenate([b_gate_f[l], b_gate_b[l]]).reshape(1, 2 * heads * dk)
        kr_self, la = pl.pallas_call(
            _prep_kernel, grid=(t // tm,),
            in_specs=[pl.BlockSpec((tm, 2 * LANE), lambda i: (i, n_main // (2 * LANE))),
                      pl.BlockSpec((tm, LANE), lambda i: (pos_of_tile(i), 0)),
                      pl.BlockSpec((tm, LANE), lambda i: (pos_of_tile(i), 0)),
                      pl.BlockSpec((LANE, 2 * heads * dk), lambda i: (0, 0)),
                      pl.BlockSpec((1, 2 * heads * dk), lambda i: (0, 0))],
            out_specs=[pl.BlockSpec((tm, LANE), lambda i: (i, 0)),
                       pl.BlockSpec((tm, 2 * heads * dk), lambda i: (i, 0))],
            out_shape=[jax.ShapeDtypeStruct((t, LANE), BF16),
                       jax.ShapeDtypeStruct((t, 2 * heads * dk), F32)],
            compiler_params=_params(1), name=f"prep{l}")(proj, cos_k, sin_k, w_gate.astype(BF16), b_gate)

        wq = w_uq[l].reshape(q_lora, MLA_HEADS, hq)
        wq_r = wq[:, :, QK_NOPE:]
        pad = jnp.zeros((q_lora, MLA_HEADS, LANE - rope), F32)
        as_cols = lambda a: a.reshape(q_lora, -1).astype(BF16)
        wq_n = as_cols(wq[:, :, :QK_NOPE])
        wq_s = as_cols(jnp.concatenate([_swap_pairs(wq_r, rope), pad], axis=2))
        wq_r = as_cols(jnp.concatenate([wq_r, pad], axis=2))
        q_in = [(proj, q_lora, col[0] // q_lora)]
        q_norm = lambda k, a, e: _rms(a, e[0])
        g_q_extra = (g_q[l].reshape(1, q_lora), (1, q_lora), lambda j, i: (0, 0))
        tn_q = _tile(MLA_HEADS * LANE, 1024)

        def rope_epi(accs, e, tn_q=tn_q):
            cos, sin = e[1], e[2]
            return jnp.concatenate([accs[0][:, s:s + LANE] * cos + accs[1][:, s:s + LANE] * sin
                                    for s in range(0, tn_q, LANE)], axis=1)

        q_nope = _matmul(q_in, [(wq_n, ())], n_out=MLA_HEADS * QK_NOPE, tm=tm_s,
                         tn=_tile(MLA_HEADS * QK_NOPE, 2048), out_dtype=BF16, epilogue=_first,
                         name=f"w_uq_nope{l}", prologue=q_norm, extras=[g_q_extra])
        q_rope = _matmul(q_in, [(wq_r, ()), (wq_s, ())], a_of_w=[0, 0], n_out=MLA_HEADS * LANE,
                         tm=tm_s, tn=tn_q, out_dtype=BF16, epilogue=rope_epi,
                         name=f"w_uq_rope{l}", prologue=q_norm,
                         extras=[g_q_extra,
                                 (cos_k, (tm_s, LANE), lambda j, i: (pos_of_tile(i, tm_s), 0)),
                                 (sin_k, (tm_s, LANE), lambda j, i: (pos_of_tile(i, tm_s), 0))])

        c_kv = _norm_cols(proj, kv_lora, col[1] // kv_lora, g_kv[l], tm, f"ckv_norm{l}")
        tn_kv = _tile(MLA_HEADS * hw, 2048)
        kv_self = _matmul([(c_kv, kv_lora, 0)], [(w_ukv, (l,))], n_out=MLA_HEADS * hw, tm=tm_s,
                          tn=tn_kv, out_dtype=BF16, epilogue=_first, name=f"w_ukv{l}")
        ckv_cache = cache_ckv[:, l].reshape(n_lat_b * past, kv_lora)
        kv_cache = _matmul([(ckv_cache, kv_lora, 0)], [(w_ukv, (l,))], n_out=MLA_HEADS * hw,
                           tm=_tile(n_lat_b * past, 512), tn=tn_kv, out_dtype=BF16,
                           epilogue=_first, name=f"w_ukv_cache{l}")
        kr_cache = jnp.pad(cache_krope[:, l].reshape(n_lat_b * past, rope),
                           ((0, 0), (0, LANE - rope))).astype(BF16)

        scale = hq ** -0.5
        attn = _attention(q_nope, q_rope, [(kv_self, t_ctx, 0)], [(kr_self, t_ctx, 0)],
                          n_batch=n_ctx_b, t_q=t_ctx, q_row0=0, tq=_tile(t_ctx, 256),
                          n_heads=MLA_HEADS, group=8, scale=scale, name=f"attn_ctx{l}",
                          o_prev=jnp.zeros((t, MLA_HEADS * V_HEAD), BF16))
        attn = _attention(q_nope, q_rope, [(kv_self, t_lat, n_ctx), (kv_cache, past, 0)],
                          [(kr_self, t_lat, n_ctx), (kr_cache, past, 0)], n_batch=n_lat_b,
                          t_q=t_lat, q_row0=n_ctx, tq=_tile(t_lat, 1024),
                          n_heads=math.gcd(MLA_HEADS, 4), group=4, scale=scale, name=f"attn_lat{l}",
                          o_prev=attn)

        gla_kw = dict(heads=heads, dk=dk, dv=dv, col_q=col_q, col_k=col_k, col_v=col_v,
                      col_g=col_g, layer=l)
        gla, *states = _gla(proj, la, g_gla[l], n_batch=n_ctx_b, t_b=t_ctx, row0=0,
                            n_heads=math.gcd(heads, 2), name=f"gla_ctx{l}",
                            o_prev=jnp.zeros((t, heads * dv), BF16), st_prev=states,
                            st_depth=depth, **gla_kw)
        gla, = _gla(proj, la, g_gla[l], n_batch=n_lat_b, t_b=t_lat, row0=n_ctx, n_heads=1,
                    s0=(state_gla_fwd, state_gla_bwd), name=f"gla_lat{l}", o_prev=gla, **gla_kw)

        new_ckv.append(c_kv[:n_ctx].reshape(n_ctx_b, t_ctx, kv_lora))
        new_krope.append(proj[:n_ctx, n_main:n_main + rope].reshape(n_ctx_b, t_ctx, rope))

        tn_m = _tile(math.gcd(col_a, col_b, d), 1024)
        tm_m = _tile(tm, 256, 8)

        def merge_epi(accs, e):
            return jax.nn.sigmoid(e[0]) * accs[0] + jax.nn.sigmoid(e[1]) * accs[1]

        merged = _matmul([(attn, MLA_HEADS * V_HEAD, 0), (gla, heads * dv, 0)],
                         [(w_pa, (l,)), (w_pb, (l,))], n_out=d, tm=tm_m, tn=tn_m, out_dtype=BF16,
                         epilogue=merge_epi, name=f"merge{l}",
                         extras=[(proj, (tm_m, tn_m), lambda j, i: (i, col_a // tn_m + j)),
                                 (proj, (tm_m, tn_m), lambda j, i: (i, col_b // tn_m + j))])
        y = _matmul([(merged, d, 0)], [(w_o, (l,))], n_out=d, tm=tm_s, tn=_tile(d, 1024),
                    out_dtype=F32, epilogue=_first, name=f"w_o{l}")
        moe = l % 2 == 1
        x, h = _resid_norm(x, sample_of_tile, tm, f"mix_resid{l}", y=y, mod_res=mods[l],
                           g_post=g_post_mix[l], gate_idx=2, mod_pre=mods[l], g_pre=g_pre_ffn[l],
                           shift_idx=3, scale_idx=4, n_first=n_ctx_tiles)

        jx = l // 2

        def swiglu_epi(accs, e):
            return _silu(accs[0]) * accs[1]

        ffn_out = {}
        if not moe:
            d_ff = w_ff_gate.shape[2]
            ff = _matmul([(h, d, 0)], [(w_ff_gate, (jx,)), (w_ff_up, (jx,))], a_of_w=[0, 0],
                         n_out=d_ff, tm=tm_s, tn=_tile(d_ff, 512), out_dtype=BF16,
                         epilogue=swiglu_epi, name=f"ffn_up{l}")
            ffn_out["y"] = _matmul([(ff, d_ff, 0)], [(w_ff_down, (jx,))], n_out=d, tm=tm,
                                   tn=_tile(d, 512), out_dtype=F32, epilogue=_first,
                                   name=f"ffn_down{l}")
        else:
            d_ex = w_ex_gate.shape[3]
            w_r = jnp.pad(w_router[jx], ((0, 0), (0, LANE - n_experts))).astype(BF16)
            route = pl.pallas_call(
                functools.partial(_router_kernel, n_experts=n_experts), grid=(t // tm,),
                in_specs=[pl.BlockSpec((tm, d), lambda i: (i, 0)),
                          pl.BlockSpec((d, LANE), lambda i: (0, 0))],
                out_specs=pl.BlockSpec((tm, LANE), lambda i: (i, 0)),
                out_shape=jax.ShapeDtypeStruct((t, LANE), F32),
                compiler_params=_params(1), name=f"router{l}")(h, w_r)
            tile_e = _tile(t * TOP_K, EXPERT_TILE, 8)
            run_e = tile_e * EXPERT_RUN_TILES
            tok_of, dest, flags, run_info, tile_info = _routing_tables(route, n_experts, tile_e,
                                                                       run_e)
            xs = _gather(h.reshape(t, d // LANE, LANE), tok_of, flags, tile_e, f"moe_gather{l}")
            xs = xs.reshape(xs.shape[0], d)
            ff = _moe_proj(xs, [w_ex_gate, w_ex_up], (jx,), run_info, run=run_e, sub=tile_e,
                           tn=256, out_dtype=BF16,
                           combine=lambda accs: _silu(accs[0]) * accs[1], name=f"moe_up{l}")
            ys = _matmul([(ff, d_ex, 0)], [(w_ex_down, (jx, 0))], n_out=d, tm=tile_e,
                         tn=_tile(d, 1024), out_dtype=F32, epilogue=_first, name=f"moe_down{l}",
                         tile_expert=tile_info)
            ffn_out["routed"] = (ys, dest, route)

        if l + 1 < depth:
            x, h = _resid_norm(x, sample_of_tile, tm, f"ffn_resid{l}", mod_res=mods[l],
                               g_post=g_post_ffn[l], gate_idx=5, mod_pre=mods[l + 1],
                               g_pre=g_pre_mix[l + 1], shift_idx=0, scale_idx=1, **ffn_out)
        else:
            x, _ = _resid_norm(x, sample_of_tile, tm, f"ffn_resid{l}", mod_res=mods[l],
                               g_post=g_post_ffn[l], gate_idx=5, split_out=True,
                               n_first=n_ctx_tiles, **ffn_out)

    return (x[0].reshape(n_ctx_b, t_ctx, d), x[1].reshape(n_lat_b, t_lat, d),
            jnp.stack(new_ckv, axis=1), jnp.stack(new_krope, axis=1),
            states[0], states[1])
```
